```python
import jax, jax.numpy as jnp
from jax import lax
import numpy as np

D_MODEL = 1024
BATCH = 4
SEQ = 8192
DEPTH = 2

HEAD_DIM = 64
CONV_CH = 256
CONV_WIDTH = 3
HGRN_HEADS = 4
HGRN_DK = 64
HGRN_DV = 64
HGRN_QK = HGRN_HEADS * HGRN_DK
HGRN_WIDTH = HGRN_HEADS * HGRN_DV
CHUNK = 64
FOX_HEADS = 8
FOX_WIDTH = FOX_HEADS * HEAD_DIM
Q_BLOCK = 128
MIX_WIDTH = CONV_CH + HGRN_WIDTH + FOX_WIDTH
MIX_IN_SIZES = (CONV_CH, CONV_CH, CONV_CH,
                HGRN_QK, HGRN_QK, HGRN_WIDTH, HGRN_WIDTH,
                FOX_WIDTH, FOX_WIDTH, FOX_WIDTH, FOX_HEADS)
MIX_IN = sum(MIX_IN_SIZES)
D_FF = 2816
EPS = 1e-6
MASK_VALUE = -1e30

kernel_name = "hybrid_parallel_conv_hgrn2_fox_macaron"


def rms_norm(x, gain):
    xf = x.astype(jnp.float32)
    inv = lax.rsqrt(jnp.mean(xf * xf, axis=-1, keepdims=True) + EPS)
    return (xf * inv).astype(x.dtype) * gain


def swiglu(h, w_in, w_out):
    gate, up = jnp.split(h @ w_in, 2, axis=-1)
    return (jax.nn.silu(gate) * up) @ w_out


def short_conv_mixer(x_in, b_gate, c_gate, conv_w):
    u = c_gate * x_in
    taps = conv_w[:, None, :].astype(u.dtype)
    y = lax.conv_general_dilated(u, taps, window_strides=(1,),
                                 padding=((CONV_WIDTH - 1, 0),),
                                 dimension_numbers=('NWC', 'WIO', 'NWC'),
                                 feature_group_count=CONV_CH)
    return b_gate * y


def hgrn2_mixer(q, f_logit, v, g, lb, out_gain):
    f32 = jnp.float32
    bsz, seqlen, _ = q.shape
    n_chunks = seqlen // CHUNK
    z = f_logit.astype(f32)
    lb = lb.astype(f32)
    log_f = jax.nn.log_sigmoid(z) + jnp.log1p(lb * jnp.exp(-z))
    k = (1.0 - lb) * jax.nn.sigmoid(-z)

    def to_chunks(t, d):
        return t.astype(f32).reshape(bsz, n_chunks, CHUNK, HGRN_HEADS, d).transpose(1, 0, 3, 2, 4)

    qc, kc, lfc = to_chunks(q, HGRN_DK), to_chunks(k, HGRN_DK), to_chunks(log_f, HGRN_DK)
    vc = to_chunks(v, HGRN_DV)
    causal = jnp.tril(jnp.ones((CHUNK, CHUNK), dtype=bool))[:, :, None]

    def step(state, inp):
        qb, kb, vb, lfb = inp
        bcum = jnp.cumsum(lfb, axis=-2)
        rel = bcum[..., :, None, :] - bcum[..., None, :, :]
        decay = jnp.where(causal, jnp.exp(jnp.where(causal, rel, 0.0)), 0.0)
        scores = jnp.einsum('bhtd,bhsd,bhtsd->bhts', qb, kb, decay)
        o = (jnp.einsum('bhts,bhsv->bhtv', scores, vb)
             + jnp.einsum('bhtd,bhdv->bhtv', qb * jnp.exp(bcum), state))
        b_last = bcum[..., -1:, :]
        state = (jnp.exp(b_last[..., 0, :])[..., None] * state
                 + jnp.einsum('bhsd,bhsv->bhdv', kb * jnp.exp(b_last - bcum), vb))
        return state, o

    s0 = jnp.zeros((bsz, HGRN_HEADS, HGRN_DK, HGRN_DV), f32)
    _, o = lax.scan(step, s0, (qc, kc, vc, lfc))
    o = o.transpose(1, 0, 3, 2, 4).reshape(bsz, seqlen, HGRN_HEADS, HGRN_DV)
    gate = g.astype(f32).reshape(bsz, seqlen, HGRN_HEADS, HGRN_DV)
    o = rms_norm(o, out_gain) * jax.nn.silu(gate)
    return o.reshape(bsz, seqlen, HGRN_WIDTH).astype(q.dtype)


def fox_mixer(q, k, v, f_logit, f_bias, q_gain, k_gain):
    f32 = jnp.float32
    bsz, seqlen, _ = q.shape

    def heads(t):
        return t.reshape(bsz, seqlen, FOX_HEADS, HEAD_DIM).transpose(0, 2, 1, 3)

    qh = rms_norm(heads(q), q_gain) * (HEAD_DIM ** -0.5)
    kh = rms_norm(heads(k), k_gain)
    vh = heads(v)
    log_f = jax.nn.log_sigmoid((f_logit + f_bias).astype(f32))
    cum = jnp.cumsum(log_f, axis=1).transpose(0, 2, 1)
    kpos = jnp.arange(seqlen)

    def block(i):
        start = i * Q_BLOCK
        qb = lax.dynamic_slice_in_dim(qh, start, Q_BLOCK, axis=2)
        cb = lax.dynamic_slice_in_dim(cum, start, Q_BLOCK, axis=2)
        qpos = start + jnp.arange(Q_BLOCK)
        s = (jnp.einsum('bhqd,bhkd->bhqk', qb, kh).astype(f32)
             + (cb[..., :, None] - cum[..., None, :]))
        s = jnp.where(qpos[:, None] >= kpos[None, :], s, MASK_VALUE)
        p = jax.nn.softmax(s, axis=-1)
        return jnp.einsum('bhqk,bhkd->bhqd', p.astype(vh.dtype), vh)

    o = lax.map(block, jnp.arange(seqlen // Q_BLOCK))
    return o.transpose(1, 0, 3, 2, 4).reshape(bsz, seqlen, FOX_WIDTH)


def setup_inputs(seed: int = 0) -> dict:
    key = jax.random.key(seed)
    ks = jax.random.split(key, 17)
    nrm = jax.random.normal
    f32 = jnp.float32
    return {
        "x": nrm(ks[0], (BATCH, SEQ, D_MODEL), f32),
        "ffn1_norm": 1.0 + 0.02 * nrm(ks[1], (DEPTH, D_MODEL), f32),
        "ffn1_w_in": nrm(ks[2], (DEPTH, D_MODEL, 2 * D_FF), f32) * D_MODEL ** -0.5,
        "ffn1_w_out": nrm(ks[3], (DEPTH, D_FF, D_MODEL), f32) * D_FF ** -0.5,
        "mix_norm": 1.0 + 0.02 * nrm(ks[4], (DEPTH, D_MODEL), f32),
        "w_mix_in": nrm(ks[5], (DEPTH, D_MODEL, MIX_IN), f32) * D_MODEL ** -0.5,
        "conv_w": nrm(ks[6], (DEPTH, CONV_WIDTH, CONV_CH), f32) * CONV_WIDTH ** -0.5,
        "hgrn_lb_logits": nrm(ks[7], (DEPTH, HGRN_QK), f32),
        "hgrn_out_gain": 1.0 + 0.02 * nrm(ks[8], (DEPTH, HGRN_DV), f32),
        "fox_q_gain": 1.0 + 0.02 * nrm(ks[9], (DEPTH, HEAD_DIM), f32),
        "fox_k_gain": 1.0 + 0.02 * nrm(ks[10], (DEPTH, HEAD_DIM), f32),
        "fox_f_bias": 2.0 + 0.5 * nrm(ks[11], (DEPTH, FOX_HEADS), f32),
        "w_mix_out": nrm(ks[12], (DEPTH, MIX_WIDTH, D_MODEL), f32) * MIX_WIDTH ** -0.5,
        "ffn2_norm": 1.0 + 0.02 * nrm(ks[13], (DEPTH, D_MODEL), f32),
        "ffn2_w_in": nrm(ks[14], (DEPTH, D_MODEL, 2 * D_FF), f32) * D_MODEL ** -0.5,
        "ffn2_w_out": nrm(ks[15], (DEPTH, D_FF, D_MODEL), f32) * D_FF ** -0.5,
    }


def reference(x, ffn1_norm, ffn1_w_in, ffn1_w_out, mix_norm, w_mix_in, conv_w,
              hgrn_lb_logits, hgrn_out_gain, fox_q_gain, fox_k_gain, fox_f_bias,
              w_mix_out, ffn2_norm, ffn2_w_in, ffn2_w_out):
    lb_soft = jax.nn.softmax(hgrn_lb_logits.astype(jnp.float32), axis=0)
    lower_bounds = jnp.clip(jnp.cumsum(lb_soft, axis=0) - lb_soft[0:1], 0.0, 1.0)
    split_at = [int(s) for s in np.cumsum(MIX_IN_SIZES)[:-1]]
    for l in range(DEPTH):
        x = x + 0.5 * swiglu(rms_norm(x, ffn1_norm[l]), ffn1_w_in[l], ffn1_w_out[l])
        h = rms_norm(x, mix_norm[l]) @ w_mix_in[l]
        (c_x, c_b, c_c, h_q, h_f, h_i, h_g, f_q, f_k, f_v, f_f) = jnp.split(h, split_at, axis=-1)
        y = jnp.concatenate([
            short_conv_mixer(c_x, c_b, c_c, conv_w[l]),
            hgrn2_mixer(h_q, h_f, h_i, h_g, lower_bounds[l], hgrn_out_gain[l]),
            fox_mixer(f_q, f_k, f_v, f_f, fox_f_bias[l], fox_q_gain[l], fox_k_gain[l]),
        ], axis=-1)
        x = x + y @ w_mix_out[l]
        x = x + 0.5 * swiglu(rms_norm(x, ffn2_norm[l]), ffn2_w_in[l], ffn2_w_out[l])
    return x
```

```python
import functools

import jax
import jax.numpy as jnp
import numpy as np
from jax import lax
from jax.experimental import pallas as pl
from jax.experimental.pallas import tpu as pltpu

F32 = jnp.float32
BF16 = jnp.bfloat16

D_MODEL = 1024
D_FF = 2816
HEAD_DIM = 64
CONV_CH = 256
CONV_WIDTH = 3
HGRN_HEADS = 4
HGRN_W = 256
FOX_HEADS = 8
FOX_W = 512
CHUNK = 64
EPS = 1e-6
MASK_VALUE = -1e30

LANES = 128
FF_TILE = 256
N_FF_TILES = D_FF // FF_TILE
ROW_TILE = 512
ATT_TILE = 512
HGRN_ROWS = 256
S_GROUP = 8
VMEM_LIMIT = 56 * 1024 * 1024

Q_C_LANE = HEAD_DIM
K_ONE_LANE = HEAD_DIM
N_PIECES = 3


def _dot(a, b):
    return jnp.dot(a, b, preferred_element_type=F32)


def _dot_nt(a, b):
    return lax.dot_general(a, b, (((1,), (1,)), ((), ())), preferred_element_type=F32)


def _dot_tn(a, b):
    return lax.dot_general(a, b, (((0,), (0,)), ((), ())), preferred_element_type=F32)


def _rms_norm(x, gain):
    inv = lax.rsqrt(jnp.mean(x * x, axis=-1, keepdims=True) + EPS)
    return x * inv * gain


def _sigmoid(x):
    return 1.0 / (1.0 + jnp.exp(-x))


def _log_sigmoid(x):
    return jnp.minimum(x, 0.0) - jnp.log1p(jnp.exp(-jnp.abs(x)))


def _split3(x):
    hi = x.astype(BF16)
    r = x - hi.astype(F32)
    mid = r.astype(BF16)
    lo = (r - mid.astype(F32)).astype(BF16)
    return hi, mid, lo


def _const_spec(shape):
    nd = len(shape)
    return pl.BlockSpec(shape, lambda *_: (0,) * nd, pipeline_mode=pl.Buffered(1))


def _params(sem):
    return pltpu.CompilerParams(dimension_semantics=sem, vmem_limit_bytes=VMEM_LIMIT)


def _ffn_body(x_ref, g_ref, wg_ref, wu_ref, wo_ref, o_ref, acc_ref):
    x = x_ref[...]
    xn = _rms_norm(x, g_ref[...]).astype(BF16)
    for c in range(N_FF_TILES):
        gate = _dot(xn, wg_ref[c])
        up = _dot(xn, wu_ref[c])
        act = (gate * _sigmoid(gate) * up).astype(BF16)
        part = _dot(act, wo_ref[c])
        if c == 0:
            acc_ref[...] = part
        else:
            acc_ref[...] += part
    o_ref[...] = x + 0.5 * acc_ref[...]


def _ffn(x2d, gain, wg, wu, wo):
    n = x2d.shape[0]
    tm = min(ROW_TILE, n)
    row = pl.BlockSpec((tm, D_MODEL), lambda i: (i, 0))
    return pl.pallas_call(
        _ffn_body,
        grid=(n // tm,),
        in_specs=[row, _const_spec(gain.shape), _const_spec(wg.shape),
                  _const_spec(wu.shape), _const_spec(wo.shape)],
        out_specs=row,
        out_shape=jax.ShapeDtypeStruct((n, D_MODEL), F32),
        scratch_shapes=[pltpu.VMEM((tm, D_MODEL), F32)],
        compiler_params=_params(("arbitrary",)),
        name="ffn",
    )(x2d, gain, wg, wu, wo)


def _mix_in_body(x_ref, g_ref, wc_ref, cw_ref, wh_ref, wq_ref, wk_ref, wvt_ref, wf_ref,
                 fb_ref, gq_ref, gk_ref, tri_ref, plq_ref, plk_ref, oneq_ref, onek_ref,
                 yc_ref, hh_ref, qa_ref, ka_ref, vat_ref, ubuf_ref, carry_ref, *, tm):
    @pl.when(pl.program_id(1) == 0)
    def _():
        ubuf_ref[0:8, :] = jnp.zeros((8, CONV_CH), F32)
        carry_ref[...] = jnp.zeros_like(carry_ref)

    xn = _rms_norm(x_ref[0], g_ref[...]).astype(BF16)

    hc = _dot(xn, wc_ref[...])
    u = hc[:, 2 * CONV_CH:3 * CONV_CH] * hc[:, 0:CONV_CH]
    ubuf_ref[8:8 + tm, :] = u
    u1 = ubuf_ref[7:7 + tm, :]
    u2 = ubuf_ref[6:6 + tm, :]
    cw = cw_ref[...]
    conv = cw[0:1, :] * u2 + cw[1:2, :] * u1 + cw[2:3, :] * u
    yc_ref[0] = (hc[:, CONV_CH:2 * CONV_CH] * conv).astype(BF16)
    ubuf_ref[0:8, :] = u[tm - 8:tm, :]

    hh_ref[0] = _dot(xn, wh_ref[...])

    lf = _log_sigmoid(_dot(xn, wf_ref[...]) + fb_ref[...])
    c = carry_ref[...] + _dot(tri_ref[...], jnp.concatenate(_split3(lf), axis=0))
    carry_ref[...] = c[tm - 1:tm, :]
    cp = jnp.concatenate(_split3(c), axis=1)
    cq = _dot(cp, plq_ref[...]) + oneq_ref[...]
    ck = _dot(cp, plk_ref[...]) + onek_ref[...]

    hq = _dot(xn, wq_ref[...])
    hk = _dot(xn, wk_ref[...])
    for h in range(FOX_HEADS):
        sl = slice(h * LANES, (h + 1) * LANES)
        xq = hq[:, sl]
        xk = hk[:, sl]
        iq = lax.rsqrt(jnp.sum(xq * xq, axis=-1, keepdims=True) * (1.0 / HEAD_DIM) + EPS)
        ik = lax.rsqrt(jnp.sum(xk * xk, axis=-1, keepdims=True) * (1.0 / HEAD_DIM) + EPS)
        qa_ref[0, h] = (xq * iq * gq_ref[...] + cq[:, sl]).astype(BF16)
        ka_ref[0, h] = (xk * ik * gk_ref[...] + ck[:, sl]).astype(BF16)

    vt = _dot_nt(wvt_ref[...], xn)
    one_row = (lax.broadcasted_iota(jnp.int32, (LANES, tm), 0) == HEAD_DIM).astype(F32)
    for h in range(FOX_HEADS):
        vat_ref[0, h, 0] = (vt[h * LANES:(h + 1) * LANES, :] + one_row).astype(BF16)


def _mix_in(x, p):
    b, l, _ = x.shape
    tm = min(ROW_TILE, l)
    nt = l // tm
    consts = [p["mix_norm"], p["wc"], p["conv_w"], p["wh"], p["wq"], p["wk"], p["wvt"], p["wf"],
              p["fbias"], p["gq"], p["gk"], p["tri"], p["plq"], p["plk"], p["oneq"], p["onek"]]
    out_shape = [
        jax.ShapeDtypeStruct((b, l, CONV_CH), BF16),
        jax.ShapeDtypeStruct((b, l, 4 * HGRN_W), F32),
        jax.ShapeDtypeStruct((b, FOX_HEADS, l, LANES), BF16),
        jax.ShapeDtypeStruct((b, FOX_HEADS, l, LANES), BF16),
        jax.ShapeDtypeStruct((b, FOX_HEADS, nt, LANES, tm), BF16),
    ]
    out_specs = [
        pl.BlockSpec((1, tm, CONV_CH), lambda bi, i: (bi, i, 0)),
        pl.BlockSpec((1, tm, 4 * HGRN_W), lambda bi, i: (bi, i, 0)),
        pl.BlockSpec((1, FOX_HEADS, tm, LANES), lambda bi, i: (bi, 0, i, 0)),
        pl.BlockSpec((1, FOX_HEADS, tm, LANES), lambda bi, i: (bi, 0, i, 0)),
        pl.BlockSpec((1, FOX_HEADS, 1, LANES, tm), lambda bi, i: (bi, 0, i, 0, 0)),
    ]
    return pl.pallas_call(
        functools.partial(_mix_in_body, tm=tm),
        grid=(b, nt),
        in_specs=[pl.BlockSpec((1, tm, D_MODEL), lambda bi, i: (bi, i, 0))]
        + [_const_spec(c.shape) for c in consts],
        out_specs=out_specs,
        out_shape=out_shape,
        scratch_shapes=[pltpu.VMEM((tm + 8, CONV_CH), F32), pltpu.VMEM((1, LANES), F32)],
        compiler_params=_params(("arbitrary", "arbitrary")),
        name="mix_in",
    )(x, *consts)


def _hgrn_body(q_ref, z_ref, v_ref, g_ref, lbl_ref, gain_ref, tri_ref, bd_ref, bmask_ref,
               o_ref, st_ref, bc_ref, k_ref, vv_ref, a_ref, *, layer, rows):
    @pl.when(pl.program_id(1) == 0)
    def _():
        st_ref[...] = jnp.zeros_like(st_ref)

    lbl = lbl_ref[...]
    e = jnp.exp(lbl - jnp.max(lbl, axis=0, keepdims=True))
    soft = e / jnp.sum(e, axis=0, keepdims=True)
    lb = jnp.clip(jnp.sum(soft[0:layer + 1, :], axis=0, keepdims=True) - soft[0:1, :], 0.0, 1.0)

    bd = bd_ref[...]
    t_idx = lax.broadcasted_iota(jnp.int32, (CHUNK, HGRN_W), 0)

    for ch in range(rows // CHUNK):
        rs = slice(ch * CHUNK, (ch + 1) * CHUNK)
        q = q_ref[0, rs, :]
        z = z_ref[0, rs, :]
        v = v_ref[0, rs, :]
        lf = _log_sigmoid(z) + jnp.log1p(lb * jnp.exp(-z))
        k = (1.0 - lb) * _sigmoid(-z)
        bc = _dot(tri_ref[...], jnp.concatenate(_split3(lf), axis=0))
        bc_ref[...] = bc
        k_ref[...] = k
        vv_ref[...] = v

        def group(sg, o):
            for i in range(S_GROUP):
                s = sg * S_GROUP + i
                decay = jnp.exp(jnp.minimum(bc - bc_ref[pl.ds(s, 1), :], 0.0))
                a = jnp.where(t_idx >= s, q * decay * k_ref[pl.ds(s, 1), :], 0.0)
                a_ref[i * CHUNK:(i + 1) * CHUNK, :] = a.astype(BF16)
            r = _dot(a_ref[...], bd)
            for i in range(S_GROUP):
                o = o + r[i * CHUNK:(i + 1) * CHUNK, :] * vv_ref[pl.ds(sg * S_GROUP + i, 1), :]
            return o

        o = lax.fori_loop(0, CHUNK // S_GROUP, group, jnp.zeros((CHUNK, HGRN_W), F32))

        st = st_ref[...]
        o = o + _dot_nt((q * jnp.exp(bc)).astype(BF16), st.astype(BF16))
        b_last = bc[CHUNK - 1:CHUNK, :]
        k_hat = (k * jnp.exp(b_last - bc)).astype(BF16)
        st_ref[...] = st * jnp.exp(b_last) + bmask_ref[...] * _dot_tn(v.astype(BF16), k_hat)

        oo = o * o
        hi = oo.astype(BF16)
        lo = (oo - hi.astype(F32)).astype(BF16)
        ms = (_dot(hi, bd) + _dot(lo, bd)) * (1.0 / HEAD_DIM)
        gate = g_ref[0, rs, :]
        y = o * lax.rsqrt(ms + EPS) * gain_ref[...] * (gate * _sigmoid(gate))
        o_ref[0, rs, :] = y.astype(BF16)


def _hgrn(hh, p, layer):
    b, l, _ = hh.shape
    rows = min(HGRN_ROWS, l)
    consts = [p["lb_logits"], p["hgrn_gain"], p["tri64"], p["bd"], p["bmask"]]

    def section(k):
        return pl.BlockSpec((1, rows, HGRN_W), lambda bi, i: (bi, i, k))

    return pl.pallas_call(
        functools.partial(_hgrn_body, layer=layer, rows=rows),
        grid=(b, l // rows),
        in_specs=[section(0), section(1), section(2), section(3)]
        + [_const_spec(c.shape) for c in consts],
        out_specs=pl.BlockSpec((1, rows, HGRN_W), lambda bi, i: (bi, i, 0)),
        out_shape=jax.ShapeDtypeStruct((b, l, HGRN_W), BF16),
        scratch_shapes=[pltpu.VMEM((HGRN_W, HGRN_W), F32),
                        pltpu.VMEM((CHUNK, HGRN_W), F32),
                        pltpu.VMEM((CHUNK, HGRN_W), F32),
                        pltpu.VMEM((CHUNK, HGRN_W), F32),
                        pltpu.VMEM((S_GROUP * CHUNK, HGRN_W), BF16)],
        compiler_params=_params(("arbitrary", "arbitrary")),
        name="hgrn",
    )(hh, hh, hh, hh, *consts)


def _fox_body(qa_ref, ka_ref, vat_ref, o_ref, m_ref, acc_ref, *, t):
    qi = pl.program_id(2)
    kv_le_q = (lax.broadcasted_iota(jnp.int32, (t, t), 0)
               <= lax.broadcasted_iota(jnp.int32, (t, t), 1))

    for hh in range(2):
        qa = qa_ref[0, hh]

        def block(j, diagonal):
            s = _dot_nt(ka_ref[0, hh, pl.ds(pl.multiple_of(j * t, t), t), :], qa)
            if diagonal:
                s = jnp.where(kv_le_q, s, MASK_VALUE)
            m_old = m_ref[hh]
            m_new = jnp.maximum(m_old, jnp.max(s, axis=0, keepdims=True))
            p = jnp.exp(s - m_new).astype(BF16)
            acc_ref[hh] = acc_ref[hh] * jnp.exp(m_old - m_new) + _dot(vat_ref[0, hh, j], p)
            m_ref[hh] = m_new

        m_ref[hh] = jnp.full((1, t), MASK_VALUE, F32)
        acc_ref[hh] = jnp.zeros((LANES, t), F32)
        block(qi, True)

        def body(n, carry):
            block(qi - 1 - n, False)
            return carry

        lax.fori_loop(0, qi, body, 0)

    outs = []
    for hh in range(2):
        acc = acc_ref[hh]
        outs.append(acc[0:HEAD_DIM, :] / acc[HEAD_DIM:HEAD_DIM + 1, :])
    o_ref[0] = jnp.concatenate(outs, axis=0).T.astype(BF16)


def _fox(qa, ka, vat):
    b, _, l, _ = qa.shape
    t = vat.shape[-1]
    nt = l // t
    return pl.pallas_call(
        functools.partial(_fox_body, t=t),
        grid=(b, FOX_HEADS // 2, nt),
        in_specs=[
            pl.BlockSpec((1, 2, t, LANES), lambda bi, hp, i: (bi, hp, i, 0)),
            pl.BlockSpec((1, 2, l, LANES), lambda bi, hp, i: (bi, hp, 0, 0)),
            pl.BlockSpec((1, 2, nt, LANES, t), lambda bi, hp, i: (bi, hp, 0, 0, 0)),
        ],
        out_specs=pl.BlockSpec((1, t, LANES), lambda bi, hp, i: (bi, i, hp)),
        out_shape=jax.ShapeDtypeStruct((b, l, FOX_W), BF16),
        scratch_shapes=[pltpu.VMEM((2, 1, t), F32), pltpu.VMEM((2, LANES, t), F32)],
        compiler_params=_params(("arbitrary", "arbitrary", "arbitrary")),
        name="fox",
    )(qa, ka, vat)


def _mix_out_body(x_ref, yc_ref, yh_ref, yf_ref, wc_ref, wh_ref, wf_ref, o_ref):
    o_ref[...] = (x_ref[...] + _dot(yc_ref[...], wc_ref[...]) + _dot(yh_ref[...], wh_ref[...])
                  + _dot(yf_ref[...], wf_ref[...]))


def _mix_out(x2d, yc, yh, yf, wc, wh, wf):
    n = x2d.shape[0]
    tm = min(ROW_TILE, n)

    def row(w):
        return pl.BlockSpec((tm, w), lambda i: (i, 0))

    return pl.pallas_call(
        _mix_out_body,
        grid=(n // tm,),
        in_specs=[row(D_MODEL), row(CONV_CH), row(HGRN_W), row(FOX_W),
                  _const_spec(wc.shape), _const_spec(wh.shape), _const_spec(wf.shape)],
        out_specs=row(D_MODEL),
        out_shape=jax.ShapeDtypeStruct((n, D_MODEL), F32),
        compiler_params=_params(("arbitrary",)),
        name="mix_out",
    )(x2d, yc, yh, yf, wc, wh, wf)


def _pad_heads(w):
    r = w.shape[0]
    w = w.reshape(r, FOX_HEADS, HEAD_DIM)
    return jnp.pad(w, ((0, 0), (0, 0), (0, LANES - HEAD_DIM))).reshape(r, FOX_HEADS * LANES)


def _pad_lanes(v, width=LANES):
    return jnp.pad(v, (0, width - v.shape[0])).reshape(1, width)


def _placement(first_lane, sign):
    m = np.zeros((N_PIECES * LANES, FOX_HEADS * LANES), np.float32)
    for p in range(N_PIECES):
        for h in range(FOX_HEADS):
            m[p * LANES + h, h * LANES + first_lane + p] = sign
    return jnp.asarray(m, BF16)


def _ones_lanes(first_lane):
    m = np.zeros((1, FOX_HEADS * LANES), np.float32)
    for h in range(FOX_HEADS):
        m[0, h * LANES + first_lane:h * LANES + first_lane + N_PIECES] = 1.0
    return jnp.asarray(m)


def _tri3(n):
    return jnp.asarray(np.tile(np.tril(np.ones((n, n), np.float32)), (1, N_PIECES)), BF16)


def _ffn_weights(w_in, w_out):
    def tiles(w):
        return w.reshape(D_MODEL, N_FF_TILES, FF_TILE).transpose(1, 0, 2).astype(BF16)

    return (tiles(w_in[:, :D_FF]), tiles(w_in[:, D_FF:]),
            w_out.reshape(N_FF_TILES, FF_TILE, D_MODEL).astype(BF16))


def _layer_operands(l, tm, mix_norm, w_mix_in, conv_w, hgrn_lb_logits, hgrn_out_gain,
                    fox_q_gain, fox_k_gain, fox_f_bias, w_mix_out):
    w = w_mix_in[l]
    o_h = 3 * CONV_CH
    o_q = o_h + 4 * HGRN_W
    o_k, o_v, o_f = o_q + FOX_W, o_q + 2 * FOX_W, o_q + 3 * FOX_W
    head_blocks = np.kron(np.eye(HGRN_HEADS, dtype=np.float32),
                          np.ones((HEAD_DIM, HEAD_DIM), np.float32))
    wo = w_mix_out[l].astype(BF16)
    return {
        "mix_norm": mix_norm[l].reshape(1, D_MODEL),
        "wc": w[:, :o_h].astype(BF16),
        "conv_w": jnp.pad(conv_w[l], ((0, 8 - CONV_WIDTH), (0, 0))),
        "wh": w[:, o_h:o_q].astype(BF16),
        "wq": _pad_heads(w[:, o_q:o_k]).astype(BF16),
        "wk": _pad_heads(w[:, o_k:o_v]).astype(BF16),
        "wvt": _pad_heads(w[:, o_v:o_f]).T.astype(BF16),
        "wf": jnp.pad(w[:, o_f:], ((0, 0), (0, LANES - FOX_HEADS))).astype(BF16),
        "fbias": _pad_lanes(fox_f_bias[l]),
        "gq": _pad_lanes(fox_q_gain[l] * (HEAD_DIM ** -0.5)),
        "gk": _pad_lanes(fox_k_gain[l]),
        "tri": _tri3(tm),
        "plq": _placement(Q_C_LANE, 1.0),
        "plk": _placement(K_ONE_LANE + N_PIECES, -1.0),
        "oneq": _ones_lanes(Q_C_LANE + N_PIECES),
        "onek": _ones_lanes(K_ONE_LANE),
        "lb_logits": hgrn_lb_logits,
        "hgrn_gain": jnp.tile(hgrn_out_gain[l], HGRN_HEADS).reshape(1, HGRN_W),
        "tri64": _tri3(CHUNK),
        "bd": jnp.asarray(head_blocks, BF16),
        "bmask": jnp.asarray(head_blocks),
        "wo_c": wo[:CONV_CH],
        "wo_h": wo[CONV_CH:CONV_CH + HGRN_W],
        "wo_f": wo[CONV_CH + HGRN_W:],
    }


def kernel(x, ffn1_norm, ffn1_w_in, ffn1_w_out, mix_norm, w_mix_in, conv_w, hgrn_lb_logits,
           hgrn_out_gain, fox_q_gain, fox_k_gain, fox_f_bias, w_mix_out, ffn2_norm, ffn2_w_in,
           ffn2_w_out):
    b, l, d = x.shape
    assert d == D_MODEL and l % min(ROW_TILE, l) == 0 and ROW_TILE == ATT_TILE
    depth = ffn1_norm.shape[0]
    tm = min(ROW_TILE, l)
    n = b * l
    for layer in range(depth):
        p = _layer_operands(layer, tm, mix_norm, w_mix_in, conv_w, hgrn_lb_logits, hgrn_out_gain,
                            fox_q_gain, fox_k_gain, fox_f_bias, w_mix_out)
        x2 = _ffn(x.reshape(n, d), ffn1_norm[layer].reshape(1, d),
                  *_ffn_weights(ffn1_w_in[layer], ffn1_w_out[layer]))
        yc, hh, qa, ka, vat = _mix_in(x2.reshape(b, l, d), p)
        yh = _hgrn(hh, p, layer)
        yf = _fox(qa, ka, vat)
        x2 = _mix_out(x2, yc.reshape(n, CONV_CH), yh.reshape(n, HGRN_W), yf.reshape(n, FOX_W),
                      p["wo_c"], p["wo_h"], p["wo_f"])
        x2 = _ffn(x2, ffn2_norm[layer].reshape(1, d),
                  *_ffn_weights(ffn2_w_in[layer], ffn2_w_out[layer]))
        x = x2.reshape(b, l, d)
    return x
```

```python
import functools

import jax
import jax.numpy as jnp
import numpy as np
from jax import lax
from jax.experimental import pallas as pl
from jax.experimental.pallas import tpu as pltpu

F32 = jnp.float32
BF16 = jnp.bfloat16

D_MODEL = 1024
D_FF = 2816
HEAD_DIM = 64
CONV_CH = 256
CONV_WIDTH = 3
HGRN_HEADS = 4
HGRN_W = 256
FOX_HEADS = 8
FOX_W = 512
CHUNK = 64
EPS = 1e-6
MASK_VALUE = -1e30

LANES = 128
FF_TILE = 256
N_FF_TILES = D_FF // FF_TILE
ROW_TILE = 512
Q_TILE = 512
KV_TILE = 256
HGRN_ROWS = 256
S_GROUP = 8
VMEM_LIMIT = 56 * 1024 * 1024

Q_C_LANE = HEAD_DIM
K_ONE_LANE = HEAD_DIM
N_PIECES = 3
LOG2E = 1.4426950408889634
SKIP_BELOW = -153.0


def _dot(a, b):
    return jnp.dot(a, b, preferred_element_type=F32)


def _dot_nt(a, b):
    return lax.dot_general(a, b, (((1,), (1,)), ((), ())), preferred_element_type=F32)


def _dot_tn(a, b):
    return lax.dot_general(a, b, (((0,), (0,)), ((), ())), preferred_element_type=F32)


def _rms_norm(x, gain):
    inv = lax.rsqrt(jnp.mean(x * x, axis=-1, keepdims=True) + EPS)
    return x * inv * gain


def _sigmoid(x):
    return 1.0 / (1.0 + jnp.exp(-x))


def _log_sigmoid(x):
    return jnp.minimum(x, 0.0) - jnp.log1p(jnp.exp(-jnp.abs(x)))


def _split3(x):
    hi = x.astype(BF16)
    r = x - hi.astype(F32)
    mid = r.astype(BF16)
    lo = (r - mid.astype(F32)).astype(BF16)
    return hi, mid, lo


def _const_spec(shape):
    nd = len(shape)
    return pl.BlockSpec(shape, lambda *_: (0,) * nd, pipeline_mode=pl.Buffered(1))


def _params(sem):
    return pltpu.CompilerParams(dimension_semantics=sem, vmem_limit_bytes=VMEM_LIMIT)


def _ffn_body(x_ref, g_ref, wg_ref, wu_ref, wo_ref, o_ref, acc_ref):
    x = x_ref[...]
    xn = _rms_norm(x, g_ref[...]).astype(BF16)
    for c in range(N_FF_TILES):
        gate = _dot(xn, wg_ref[c])
        up = _dot(xn, wu_ref[c])
        act = (gate * _sigmoid(gate) * up).astype(BF16)
        part = _dot(act, wo_ref[c])
        if c == 0:
            acc_ref[...] = part
        else:
            acc_ref[...] += part
    o_ref[...] = x + 0.5 * acc_ref[...]


def _ffn(x2d, gain, wg, wu, wo):
    n = x2d.shape[0]
    tm = min(ROW_TILE, n)
    row = pl.BlockSpec((tm, D_MODEL), lambda i: (i, 0))
    return pl.pallas_call(
        _ffn_body,
        grid=(n // tm,),
        in_specs=[row, _const_spec(gain.shape), _const_spec(wg.shape),
                  _const_spec(wu.shape), _const_spec(wo.shape)],
        out_specs=row,
        out_shape=jax.ShapeDtypeStruct((n, D_MODEL), F32),
        scratch_shapes=[pltpu.VMEM((tm, D_MODEL), F32)],
        compiler_params=_params(("arbitrary",)),
        name="ffn",
    )(x2d, gain, wg, wu, wo)


def _mix_in_body(x_ref, g_ref, wc_ref, cw_ref, wh_ref, wq_ref, wk_ref, wvt_ref, wf_ref,
                 fb_ref, gq_ref, gk_ref, tri_ref, plq_ref, plk_ref, oneq_ref, onek_ref,
                 yc_ref, hh_ref, qa_ref, ka_ref, vat_ref, cend_ref, ubuf_ref, carry_ref, *, tm):
    @pl.when(pl.program_id(1) == 0)
    def _():
        ubuf_ref[0:8, :] = jnp.zeros((8, CONV_CH), F32)
        carry_ref[...] = jnp.zeros_like(carry_ref)

    xn = _rms_norm(x_ref[0], g_ref[...]).astype(BF16)

    hc = _dot(xn, wc_ref[...])
    u = hc[:, 2 * CONV_CH:3 * CONV_CH] * hc[:, 0:CONV_CH]
    ubuf_ref[8:8 + tm, :] = u
    u1 = ubuf_ref[7:7 + tm, :]
    u2 = ubuf_ref[6:6 + tm, :]
    cw = cw_ref[...]
    conv = cw[0:1, :] * u2 + cw[1:2, :] * u1 + cw[2:3, :] * u
    yc_ref[0] = (hc[:, CONV_CH:2 * CONV_CH] * conv).astype(BF16)
    ubuf_ref[0:8, :] = u[tm - 8:tm, :]

    hh_ref[0] = _dot(xn, wh_ref[...])

    lf = _log_sigmoid(_dot(xn, wf_ref[...]) + fb_ref[...])
    c = carry_ref[...] + _dot(tri_ref[...], jnp.concatenate(_split3(lf), axis=0))
    carry_ref[...] = c[tm - 1:tm, :]
    ends = [c[(n + 1) * KV_TILE - 1:(n + 1) * KV_TILE, :] for n in range(tm // KV_TILE)]
    cend_ref[0, 0] = jnp.concatenate(ends + [jnp.zeros((8 - len(ends), LANES), F32)], axis=0)
    cp = jnp.concatenate(_split3(c * LOG2E), axis=1)
    cq = _dot(cp, plq_ref[...]) + oneq_ref[...]
    ck = _dot(cp, plk_ref[...]) + onek_ref[...]

    hq = _dot(xn, wq_ref[...])
    hk = _dot(xn, wk_ref[...])
    for h in range(FOX_HEADS):
        sl = slice(h * LANES, (h + 1) * LANES)
        xq = hq[:, sl]
        xk = hk[:, sl]
        iq = lax.rsqrt(jnp.sum(xq * xq, axis=-1, keepdims=True) * (1.0 / HEAD_DIM) + EPS)
        ik = lax.rsqrt(jnp.sum(xk * xk, axis=-1, keepdims=True) * (1.0 / HEAD_DIM) + EPS)
        qa_ref[0, h] = (xq * iq * gq_ref[...] + cq[:, sl]).astype(BF16)
        ka_ref[0, h] = (xk * ik * gk_ref[...] + ck[:, sl]).astype(BF16)

    vt = _dot_nt(wvt_ref[...], xn)
    one_row = (lax.broadcasted_iota(jnp.int32, (LANES, tm), 0) == HEAD_DIM).astype(F32)
    for h in range(FOX_HEADS):
        va = (vt[h * LANES:(h + 1) * LANES, :] + one_row).astype(BF16)
        for n in range(tm // KV_TILE):
            vat_ref[0, h, n] = va[:, n * KV_TILE:(n + 1) * KV_TILE]


def _mix_in(x, p):
    b, l, _ = x.shape
    tm = min(ROW_TILE, l)
    nt = l // tm
    consts = [p["mix_norm"], p["wc"], p["conv_w"], p["wh"], p["wq"], p["wk"], p["wvt"], p["wf"],
              p["fbias"], p["gq"], p["gk"], p["tri"], p["plq"], p["plk"], p["oneq"], p["onek"]]
    out_shape = [
        jax.ShapeDtypeStruct((b, l, CONV_CH), BF16),
        jax.ShapeDtypeStruct((b, l, 4 * HGRN_W), F32),
        jax.ShapeDtypeStruct((b, FOX_HEADS, l, LANES), BF16),
        jax.ShapeDtypeStruct((b, FOX_HEADS, l, LANES), BF16),
        jax.ShapeDtypeStruct((b, FOX_HEADS, l // KV_TILE, LANES, KV_TILE), BF16),
        jax.ShapeDtypeStruct((b, nt, 8, LANES), F32),
    ]
    out_specs = [
        pl.BlockSpec((1, tm, CONV_CH), lambda bi, i: (bi, i, 0)),
        pl.BlockSpec((1, tm, 4 * HGRN_W), lambda bi, i: (bi, i, 0)),
        pl.BlockSpec((1, FOX_HEADS, tm, LANES), lambda bi, i: (bi, 0, i, 0)),
        pl.BlockSpec((1, FOX_HEADS, tm, LANES), lambda bi, i: (bi, 0, i, 0)),
        pl.BlockSpec((1, FOX_HEADS, tm // KV_TILE, LANES, KV_TILE), lambda bi, i: (bi, 0, i, 0, 0)),
        pl.BlockSpec((1, 1, 8, LANES), lambda bi, i: (bi, i, 0, 0)),
    ]
    return pl.pallas_call(
        functools.partial(_mix_in_body, tm=tm),
        grid=(b, nt),
        in_specs=[pl.BlockSpec((1, tm, D_MODEL), lambda bi, i: (bi, i, 0))]
        + [_const_spec(c.shape) for c in consts],
        out_specs=out_specs,
        out_shape=out_shape,
        scratch_shapes=[pltpu.VMEM((tm + 8, CONV_CH), F32), pltpu.VMEM((1, LANES), F32)],
        compiler_params=_params(("arbitrary", "arbitrary")),
        name="mix_in",
    )(x, *consts)


def _hgrn_body(q_ref, z_ref, v_ref, g_ref, lbl_ref, gain_ref, tri_ref, bd_ref, bmask_ref,
               o_ref, st_ref, bc_ref, k_ref, vv_ref, a_ref, *, layer, rows):
    @pl.when(pl.program_id(1) == 0)
    def _():
        st_ref[...] = jnp.zeros_like(st_ref)

    lbl = lbl_ref[...]
    e = jnp.exp(lbl - jnp.max(lbl, axis=0, keepdims=True))
    soft = e / jnp.sum(e, axis=0, keepdims=True)
    lb = jnp.clip(jnp.sum(soft[0:layer + 1, :], axis=0, keepdims=True) - soft[0:1, :], 0.0, 1.0)

    bd = bd_ref[...]
    t_idx = lax.broadcasted_iota(jnp.int32, (CHUNK, HGRN_W), 0)

    for ch in range(rows // CHUNK):
        rs = slice(ch * CHUNK, (ch + 1) * CHUNK)
        q = q_ref[0, rs, :]
        z = z_ref[0, rs, :]
        v = v_ref[0, rs, :]
        lf = _log_sigmoid(z) + jnp.log1p(lb * jnp.exp(-z))
        k = (1.0 - lb) * _sigmoid(-z)
        bc = _dot(tri_ref[...], jnp.concatenate(_split3(lf), axis=0))
        bc_ref[...] = bc
        k_ref[...] = k
        vv_ref[...] = v

        def group(sg, o):
            for i in range(S_GROUP):
                s = sg * S_GROUP + i
                decay = jnp.exp(jnp.minimum(bc - bc_ref[pl.ds(s, 1), :], 0.0))
                a = jnp.where(t_idx >= s, q * decay * k_ref[pl.ds(s, 1), :], 0.0)
                a_ref[i * CHUNK:(i + 1) * CHUNK, :] = a.astype(BF16)
            r = _dot(a_ref[...], bd)
            for i in range(S_GROUP):
                o = o + r[i * CHUNK:(i + 1) * CHUNK, :] * vv_ref[pl.ds(sg * S_GROUP + i, 1), :]
            return o

        o = lax.fori_loop(0, CHUNK // S_GROUP, group, jnp.zeros((CHUNK, HGRN_W), F32))

        st = st_ref[...]
        o = o + _dot_nt((q * jnp.exp(bc)).astype(BF16), st.astype(BF16))
        b_last = bc[CHUNK - 1:CHUNK, :]
        k_hat = (k * jnp.exp(b_last - bc)).astype(BF16)
        st_ref[...] = st * jnp.exp(b_last) + bmask_ref[...] * _dot_tn(v.astype(BF16), k_hat)

        oo = o * o
        hi = oo.astype(BF16)
        lo = (oo - hi.astype(F32)).astype(BF16)
        ms = (_dot(hi, bd) + _dot(lo, bd)) * (1.0 / HEAD_DIM)
        gate = g_ref[0, rs, :]
        y = o * lax.rsqrt(ms + EPS) * gain_ref[...] * (gate * _sigmoid(gate))
        o_ref[0, rs, :] = y.astype(BF16)


def _hgrn(hh, p, layer):
    b, l, _ = hh.shape
    rows = min(HGRN_ROWS, l)
    consts = [p["lb_logits"], p["hgrn_gain"], p["tri64"], p["bd"], p["bmask"]]

    def section(k):
        return pl.BlockSpec((1, rows, HGRN_W), lambda bi, i: (bi, i, k))

    return pl.pallas_call(
        functools.partial(_hgrn_body, layer=layer, rows=rows),
        grid=(b, l // rows),
        in_specs=[section(0), section(1), section(2), section(3)]
        + [_const_spec(c.shape) for c in consts],
        out_specs=pl.BlockSpec((1, rows, HGRN_W), lambda bi, i: (bi, i, 0)),
        out_shape=jax.ShapeDtypeStruct((b, l, HGRN_W), BF16),
        scratch_shapes=[pltpu.VMEM((HGRN_W, HGRN_W), F32),
                        pltpu.VMEM((CHUNK, HGRN_W), F32),
                        pltpu.VMEM((CHUNK, HGRN_W), F32),
                        pltpu.VMEM((CHUNK, HGRN_W), F32),
                        pltpu.VMEM((S_GROUP * CHUNK, HGRN_W), BF16)],
        compiler_params=_params(("arbitrary", "arbitrary")),
        name="hgrn",
    )(hh, hh, hh, hh, *consts)


def _fox_body(qkb_ref, cpre_ref, qa_ref, ka_ref, vat_ref, o_ref, m_ref, acc_ref, *, tq, tk, n_pre):
    bi, hp, qi = pl.program_id(0), pl.program_id(1), pl.program_id(2)
    per_q = tq // tk
    row = lax.broadcasted_iota(jnp.int32, (tk, tq), 0)
    col = lax.broadcasted_iota(jnp.int32, (tk, tq), 1)

    def c_before(hh, n):
        return cpre_ref[(bi * FOX_HEADS + 2 * hp + hh) * n_pre + n]

    def scores(hh, j):
        return _dot_nt(ka_ref[0, hh, pl.ds(pl.multiple_of(j * tk, tk), tk), :], qa_ref[0, hh])

    def update(hh, j, s, mask):
        if mask is not None:
            s = jnp.where(mask, s, MASK_VALUE)
        m_old = m_ref[hh]
        m_new = jnp.maximum(m_old, jnp.max(s, axis=0, keepdims=True))
        p = jnp.exp2(s - m_new).astype(BF16)
        acc_ref[hh] = acc_ref[hh] * jnp.exp2(m_old - m_new) + _dot(vat_ref[0, hh, j], p)
        m_ref[hh] = m_new

    for hh in range(2):
        m_ref[hh] = jnp.full((1, tq), MASK_VALUE, F32)
        acc_ref[hh] = jnp.zeros((LANES, tq), F32)

    diag = [(hh, dj) for dj in range(per_q) for hh in range(2)]
    s_diag = [scores(hh, qi * per_q + dj) for hh, dj in diag]
    for (hh, dj), s in zip(diag, s_diag):
        update(hh, qi * per_q + dj, s, row + dj * tk <= col)

    def needed(j):
        out = None
        for hh in range(2):
            gap = (qkb_ref[0] + c_before(hh, qi * per_q) - c_before(hh, j + 1)
                   - jnp.min(m_ref[hh]))
            out = gap >= SKIP_BELOW if out is None else jnp.logical_or(out, gap >= SKIP_BELOW)
        return out.astype(jnp.int32)

    def cond(carry):
        g, go = carry
        return jnp.logical_and(g >= 0, go > 0)

    def body(carry):
        g, _ = carry
        group = [(hh, g * per_q + per_q - 1 - d) for d in range(per_q) for hh in range(2)]
        s_group = [scores(hh, j) for hh, j in group]
        for (hh, j), s in zip(group, s_group):
            update(hh, j, s, None)
        return g - 1, needed(jnp.maximum(g * per_q - 1, 0))

    lax.while_loop(cond, body, (qi - 1, needed(jnp.maximum(qi * per_q - 1, 0))))

    outs = []
    for hh in range(2):
        acc = acc_ref[hh]
        outs.append(acc[0:HEAD_DIM, :] / acc[HEAD_DIM:HEAD_DIM + 1, :])
    o_ref[0] = jnp.concatenate(outs, axis=0).T.astype(BF16)


def _fox(qkb, cpre, qa, ka, vat):
    b, _, l, _ = qa.shape
    tq, tk = min(Q_TILE, l), vat.shape[-1]
    smem = pl.BlockSpec(memory_space=pltpu.SMEM)
    return pl.pallas_call(
        functools.partial(_fox_body, tq=tq, tk=tk, n_pre=l // tk + 1),
        grid=(b, FOX_HEADS // 2, l // tq),
        in_specs=[
            smem, smem,
            pl.BlockSpec((1, 2, tq, LANES), lambda bi, hp, i: (bi, hp, i, 0)),
            pl.BlockSpec((1, 2, l, LANES), lambda bi, hp, i: (bi, hp, 0, 0)),
            pl.BlockSpec((1, 2, l // tk, LANES, tk), lambda bi, hp, i: (bi, hp, 0, 0, 0)),
        ],
        out_specs=pl.BlockSpec((1, tq, LANES), lambda bi, hp, i: (bi, i, hp)),
        out_shape=jax.ShapeDtypeStruct((b, l, FOX_W), BF16),
        scratch_shapes=[pltpu.VMEM((2, 1, tq), F32), pltpu.VMEM((2, LANES, tq), F32)],
        compiler_params=_params(("arbitrary", "arbitrary", "arbitrary")),
        name="fox",
    )(qkb, cpre, qa, ka, vat)


def _mix_out_body(x_ref, yc_ref, yh_ref, yf_ref, wc_ref, wh_ref, wf_ref, o_ref):
    o_ref[...] = (x_ref[...] + _dot(yc_ref[...], wc_ref[...]) + _dot(yh_ref[...], wh_ref[...])
                  + _dot(yf_ref[...], wf_ref[...]))


def _mix_out(x2d, yc, yh, yf, wc, wh, wf):
    n = x2d.shape[0]
    tm = min(ROW_TILE, n)

    def row(w):
        return pl.BlockSpec((tm, w), lambda i: (i, 0))

    return pl.pallas_call(
        _mix_out_body,
        grid=(n // tm,),
        in_specs=[row(D_MODEL), row(CONV_CH), row(HGRN_W), row(FOX_W),
                  _const_spec(wc.shape), _const_spec(wh.shape), _const_spec(wf.shape)],
        out_specs=row(D_MODEL),
        out_shape=jax.ShapeDtypeStruct((n, D_MODEL), F32),
        compiler_params=_params(("arbitrary",)),
        name="mix_out",
    )(x2d, yc, yh, yf, wc, wh, wf)


def _pad_heads(w):
    r = w.shape[0]
    w = w.reshape(r, FOX_HEADS, HEAD_DIM)
    return jnp.pad(w, ((0, 0), (0, 0), (0, LANES - HEAD_DIM))).reshape(r, FOX_HEADS * LANES)


def _pad_lanes(v, width=LANES):
    return jnp.pad(v, (0, width - v.shape[0])).reshape(1, width)


def _placement(first_lane, sign):
    m = np.zeros((N_PIECES * LANES, FOX_HEADS * LANES), np.float32)
    for p in range(N_PIECES):
        for h in range(FOX_HEADS):
            m[p * LANES + h, h * LANES + first_lane + p] = sign
    return jnp.asarray(m, BF16)


def _ones_lanes(first_lane):
    m = np.zeros((1, FOX_HEADS * LANES), np.float32)
    for h in range(FOX_HEADS):
        m[0, h * LANES + first_lane:h * LANES + first_lane + N_PIECES] = 1.0
    return jnp.asarray(m)


def _tri3(n):
    return jnp.asarray(np.tile(np.tril(np.ones((n, n), np.float32)), (1, N_PIECES)), BF16)


def _ffn_weights(w_in, w_out):
    def tiles(w):
        return w.reshape(D_MODEL, N_FF_TILES, FF_TILE).transpose(1, 0, 2).astype(BF16)

    return (tiles(w_in[:, :D_FF]), tiles(w_in[:, D_FF:]),
            w_out.reshape(N_FF_TILES, FF_TILE, D_MODEL).astype(BF16))


def _layer_operands(l, tm, mix_norm, w_mix_in, conv_w, hgrn_lb_logits, hgrn_out_gain,
                    fox_q_gain, fox_k_gain, fox_f_bias, w_mix_out):
    w = w_mix_in[l]
    o_h = 3 * CONV_CH
    o_q = o_h + 4 * HGRN_W
    o_k, o_v, o_f = o_q + FOX_W, o_q + 2 * FOX_W, o_q + 3 * FOX_W
    head_blocks = np.kron(np.eye(HGRN_HEADS, dtype=np.float32),
                          np.ones((HEAD_DIM, HEAD_DIM), np.float32))
    wo = w_mix_out[l].astype(BF16)
    q_scale = fox_q_gain[l] * (HEAD_DIM ** -0.5 * LOG2E)
    return {
        "mix_norm": mix_norm[l].reshape(1, D_MODEL),
        "wc": w[:, :o_h].astype(BF16),
        "conv_w": jnp.pad(conv_w[l], ((0, 8 - CONV_WIDTH), (0, 0))),
        "wh": w[:, o_h:o_q].astype(BF16),
        "wq": _pad_heads(w[:, o_q:o_k]).astype(BF16),
        "wk": _pad_heads(w[:, o_k:o_v]).astype(BF16),
        "wvt": _pad_heads(w[:, o_v:o_f]).T.astype(BF16),
        "wf": jnp.pad(w[:, o_f:], ((0, 0), (0, LANES - FOX_HEADS))).astype(BF16),
        "fbias": _pad_lanes(fox_f_bias[l]),
        "gq": _pad_lanes(q_scale),
        "gk": _pad_lanes(fox_k_gain[l]),
        "qkb": (1.01 * HEAD_DIM * jnp.max(jnp.abs(q_scale))
                * jnp.max(jnp.abs(fox_k_gain[l]))).reshape(1),
        "tri": _tri3(tm),
        "plq": _placement(Q_C_LANE, 1.0),
        "plk": _placement(K_ONE_LANE + N_PIECES, -1.0),
        "oneq": _ones_lanes(Q_C_LANE + N_PIECES),
        "onek": _ones_lanes(K_ONE_LANE),
        "lb_logits": hgrn_lb_logits,
        "hgrn_gain": jnp.tile(hgrn_out_gain[l], HGRN_HEADS).reshape(1, HGRN_W),
        "tri64": _tri3(CHUNK),
        "bd": jnp.asarray(head_blocks, BF16),
        "bmask": jnp.asarray(head_blocks),
        "wo_c": wo[:CONV_CH],
        "wo_h": wo[CONV_CH:CONV_CH + HGRN_W],
        "wo_f": wo[CONV_CH + HGRN_W:],
    }


def kernel(x, ffn1_norm, ffn1_w_in, ffn1_w_out, mix_norm, w_mix_in, conv_w, hgrn_lb_logits,
           hgrn_out_gain, fox_q_gain, fox_k_gain, fox_f_bias, w_mix_out, ffn2_norm, ffn2_w_in,
           ffn2_w_out):
    b, l, d = x.shape
    assert d == D_MODEL and l % ROW_TILE == 0 and ROW_TILE == Q_TILE and Q_TILE % KV_TILE == 0
    depth = ffn1_norm.shape[0]
    tm = min(ROW_TILE, l)
    n = b * l
    for layer in range(depth):
        p = _layer_operands(layer, tm, mix_norm, w_mix_in, conv_w, hgrn_lb_logits, hgrn_out_gain,
                            fox_q_gain, fox_k_gain, fox_f_bias, w_mix_out)
        x2 = _ffn(x.reshape(n, d), ffn1_norm[layer].reshape(1, d),
                  *_ffn_weights(ffn1_w_in[layer], ffn1_w_out[layer]))
        yc, hh, qa, ka, vat, cend = _mix_in(x2.reshape(b, l, d), p)
        yh = _hgrn(hh, p, layer)
        cend = cend[:, :, :tm // KV_TILE, :FOX_HEADS].reshape(b, l // KV_TILE, FOX_HEADS)
        cpre = jnp.pad(cend.transpose(0, 2, 1), ((0, 0), (0, 0), (1, 0))) * LOG2E
        yf = _fox(p["qkb"], cpre.reshape(-1), qa, ka, vat)
        x2 = _mix_out(x2, yc.reshape(n, CONV_CH), yh.reshape(n, HGRN_W), yf.reshape(n, FOX_W),
                      p["wo_c"], p["wo_h"], p["wo_f"])
        x2 = _ffn(x2, ffn2_norm[layer].reshape(1, d),
                  *_ffn_weights(ffn2_w_in[layer], ffn2_w_out[layer]))
        x = x2.reshape(b, l, d)
    return x
```

```python
import functools

import jax
import jax.numpy as jnp
import numpy as np
from jax import lax
from jax.experimental import pallas as pl
from jax.experimental.pallas import tpu as pltpu

F32 = jnp.float32
BF16 = jnp.bfloat16

D_MODEL = 1024
D_FF = 2816
HEAD_DIM = 64
CONV_CH = 256
CONV_WIDTH = 3
HGRN_HEADS = 4
HGRN_W = 256
FOX_HEADS = 8
FOX_W = 512
CHUNK = 64
EPS = 1e-6
MASK_VALUE = -1e30

LANES = 128
FF_TILE = 256
N_FF_TILES = D_FF // FF_TILE
ROW_TILE = 512
Q_TILE = 512
KV_TILE = 256
HGRN_ROWS = 256
SUB = 16
VMEM_LIMIT = 56 * 1024 * 1024

Q_C_LANE = HEAD_DIM
K_ONE_LANE = HEAD_DIM
N_PIECES = 3
LOG2E = 1.4426950408889634
SKIP_BELOW = -153.0


def _dot(a, b):
    return jnp.dot(a, b, preferred_element_type=F32)


def _dot_nt(a, b):
    return lax.dot_general(a, b, (((1,), (1,)), ((), ())), preferred_element_type=F32)


def _dot_tn(a, b):
    return lax.dot_general(a, b, (((0,), (0,)), ((), ())), preferred_element_type=F32)


def _rms_norm(x, gain):
    inv = lax.rsqrt(jnp.mean(x * x, axis=-1, keepdims=True) + EPS)
    return x * inv * gain


def _sigmoid(x):
    return 1.0 / (1.0 + jnp.exp(-x))


def _log_sigmoid(x):
    return jnp.minimum(x, 0.0) - jnp.log(1.0 + jnp.exp(-jnp.abs(x)))


def _split3(x):
    hi = x.astype(BF16)
    r = x - hi.astype(F32)
    mid = r.astype(BF16)
    lo = (r - mid.astype(F32)).astype(BF16)
    return hi, mid, lo


def _const_spec(shape):
    nd = len(shape)
    return pl.BlockSpec(shape, lambda *_: (0,) * nd, pipeline_mode=pl.Buffered(1))


def _params(sem):
    return pltpu.CompilerParams(dimension_semantics=sem, vmem_limit_bytes=VMEM_LIMIT)


def _ffn_body(x_ref, g_ref, wg_ref, wu_ref, wo_ref, o_ref, acc_ref):
    x = x_ref[...]
    xn = _rms_norm(x, g_ref[...]).astype(BF16)
    for c in range(N_FF_TILES):
        gate = _dot(xn, wg_ref[c])
        up = _dot(xn, wu_ref[c])
        act = (gate * _sigmoid(gate) * up).astype(BF16)
        part = _dot(act, wo_ref[c])
        if c == 0:
            acc_ref[...] = part
        else:
            acc_ref[...] += part
    o_ref[...] = x + 0.5 * acc_ref[...]


def _ffn(x2d, gain, wg, wu, wo):
    n = x2d.shape[0]
    tm = min(ROW_TILE, n)
    row = pl.BlockSpec((tm, D_MODEL), lambda i: (i, 0))
    return pl.pallas_call(
        _ffn_body,
        grid=(n // tm,),
        in_specs=[row, _const_spec(gain.shape), _const_spec(wg.shape),
                  _const_spec(wu.shape), _const_spec(wo.shape)],
        out_specs=row,
        out_shape=jax.ShapeDtypeStruct((n, D_MODEL), F32),
        scratch_shapes=[pltpu.VMEM((tm, D_MODEL), F32)],
        compiler_params=_params(("arbitrary",)),
        name="ffn",
    )(x2d, gain, wg, wu, wo)


def _mix_in_body(x_ref, g_ref, wc_ref, cw_ref, wh_ref, wq_ref, wk_ref, wvt_ref, wf_ref,
                 fb_ref, gq_ref, gk_ref, tri_ref, plq_ref, plk_ref, oneq_ref, onek_ref,
                 yc_ref, hh_ref, qa_ref, ka_ref, vat_ref, cend_ref, ubuf_ref, carry_ref, *, tm):
    @pl.when(pl.program_id(1) == 0)
    def _():
        ubuf_ref[0:8, :] = jnp.zeros((8, CONV_CH), F32)
        carry_ref[...] = jnp.zeros_like(carry_ref)

    xn = _rms_norm(x_ref[0], g_ref[...]).astype(BF16)

    hc = _dot(xn, wc_ref[...])
    u = hc[:, 2 * CONV_CH:3 * CONV_CH] * hc[:, 0:CONV_CH]
    ubuf_ref[8:8 + tm, :] = u
    u1 = ubuf_ref[7:7 + tm, :]
    u2 = ubuf_ref[6:6 + tm, :]
    cw = cw_ref[...]
    conv = cw[0:1, :] * u2 + cw[1:2, :] * u1 + cw[2:3, :] * u
    yc_ref[0] = (hc[:, CONV_CH:2 * CONV_CH] * conv).astype(BF16)
    ubuf_ref[0:8, :] = u[tm - 8:tm, :]

    hh_ref[0] = _dot(xn, wh_ref[...])

    lf = _log_sigmoid(_dot(xn, wf_ref[...]) + fb_ref[...])
    c = carry_ref[...] + _dot(tri_ref[...], jnp.concatenate(_split3(lf), axis=0))
    carry_ref[...] = c[tm - 1:tm, :]
    ends = [c[(n + 1) * KV_TILE - 1:(n + 1) * KV_TILE, :] for n in range(tm // KV_TILE)]
    cend_ref[0, 0] = jnp.concatenate(ends + [jnp.zeros((8 - len(ends), LANES), F32)], axis=0)
    cp = jnp.concatenate(_split3(c * LOG2E), axis=1)
    cq = _dot(cp, plq_ref[...]) + oneq_ref[...]
    ck = _dot(cp, plk_ref[...]) + onek_ref[...]

    hq = _dot(xn, wq_ref[...])
    hk = _dot(xn, wk_ref[...])
    for h in range(FOX_HEADS):
        sl = slice(h * LANES, (h + 1) * LANES)
        xq = hq[:, sl]
        xk = hk[:, sl]
        iq = lax.rsqrt(jnp.sum(xq * xq, axis=-1, keepdims=True) * (1.0 / HEAD_DIM) + EPS)
        ik = lax.rsqrt(jnp.sum(xk * xk, axis=-1, keepdims=True) * (1.0 / HEAD_DIM) + EPS)
        qa_ref[0, h] = (xq * iq * gq_ref[...] + cq[:, sl]).astype(BF16)
        ka_ref[0, h] = (xk * ik * gk_ref[...] + ck[:, sl]).astype(BF16)

    vt = _dot_nt(wvt_ref[...], xn)
    one_row = (lax.broadcasted_iota(jnp.int32, (LANES, tm), 0) == HEAD_DIM).astype(F32)
    for h in range(FOX_HEADS):
        va = (vt[h * LANES:(h + 1) * LANES, :] + one_row).astype(BF16)
        for n in range(tm // KV_TILE):
            vat_ref[0, h, n] = va[:, n * KV_TILE:(n + 1) * KV_TILE]


def _mix_in(x, p):
    b, l, _ = x.shape
    tm = min(ROW_TILE, l)
    nt = l // tm
    consts = [p["mix_norm"], p["wc"], p["conv_w"], p["wh"], p["wq"], p["wk"], p["wvt"], p["wf"],
              p["fbias"], p["gq"], p["gk"], p["tri"], p["plq"], p["plk"], p["oneq"], p["onek"]]
    out_shape = [
        jax.ShapeDtypeStruct((b, l, CONV_CH), BF16),
        jax.ShapeDtypeStruct((b, l, 4 * HGRN_W), F32),
        jax.ShapeDtypeStruct((b, FOX_HEADS, l, LANES), BF16),
        jax.ShapeDtypeStruct((b, FOX_HEADS, l, LANES), BF16),
        jax.ShapeDtypeStruct((b, FOX_HEADS, l // KV_TILE, LANES, KV_TILE), BF16),
        jax.ShapeDtypeStruct((b, nt, 8, LANES), F32),
    ]
    out_specs = [
        pl.BlockSpec((1, tm, CONV_CH), lambda bi, i: (bi, i, 0)),
        pl.BlockSpec((1, tm, 4 * HGRN_W), lambda bi, i: (bi, i, 0)),
        pl.BlockSpec((1, FOX_HEADS, tm, LANES), lambda bi, i: (bi, 0, i, 0)),
        pl.BlockSpec((1, FOX_HEADS, tm, LANES), lambda bi, i: (bi, 0, i, 0)),
        pl.BlockSpec((1, FOX_HEADS, tm // KV_TILE, LANES, KV_TILE), lambda bi, i: (bi, 0, i, 0, 0)),
        pl.BlockSpec((1, 1, 8, LANES), lambda bi, i: (bi, i, 0, 0)),
    ]
    return pl.pallas_call(
        functools.partial(_mix_in_body, tm=tm),
        grid=(b, nt),
        in_specs=[pl.BlockSpec((1, tm, D_MODEL), lambda bi, i: (bi, i, 0))]
        + [_const_spec(c.shape) for c in consts],
        out_specs=out_specs,
        out_shape=out_shape,
        scratch_shapes=[pltpu.VMEM((tm + 8, CONV_CH), F32), pltpu.VMEM((1, LANES), F32)],
        compiler_params=_params(("arbitrary", "arbitrary")),
        name="mix_in",
    )(x, *consts)


def _hgrn_body(q_ref, z_ref, v_ref, g_ref, lbl_ref, gain_ref, cum_ref, rep_ref, bd_ref, bmask_ref,
               o_ref, st_ref, vv_ref, *, layer, rows):
    @pl.when(pl.program_id(1) == 0)
    def _():
        st_ref[...] = jnp.zeros_like(st_ref)

    lbl = lbl_ref[...]
    e = jnp.exp(lbl - jnp.max(lbl, axis=0, keepdims=True))
    soft = e / jnp.sum(e, axis=0, keepdims=True)
    lb = jnp.clip(jnp.sum(soft[0:layer + 1, :], axis=0, keepdims=True) - soft[0:1, :], 0.0, 1.0)

    bd = bd_ref[...]
    n_sub = CHUNK // SUB
    lane = lax.broadcasted_iota(jnp.int32, (1, LANES), 1)
    sub_t = lax.broadcasted_iota(jnp.int32, (CHUNK, CHUNK), 0) // SUB
    sub_s = lax.broadcasted_iota(jnp.int32, (CHUNK, CHUNK), 1) // SUB

    def within_chunk(ch, slot):
        rs = pl.ds(pl.multiple_of(ch * CHUNK, CHUNK), CHUNK)
        q = q_ref[0, rs, :]
        z = z_ref[0, rs, :]
        v = v_ref[0, rs, :]
        lf = _log_sigmoid(z) + jnp.log(1.0 + lb * jnp.exp(-z))
        k = (1.0 - lb) * _sigmoid(-z)
        sums = _dot(cum_ref[...], jnp.concatenate(_split3(lf), axis=0))
        bc = sums[0:CHUNK]
        bs = sums[CHUNK:2 * CHUNK]
        be = sums[2 * CHUNK:3 * CHUNK]
        pair = sums[3 * CHUNK:]
        vv_ref[slot] = v

        k_rep = _dot(rep_ref[...], k.astype(BF16))
        a = []
        for i in range(n_sub):
            rows_i = slice(i * SUB * SUB, (i + 1) * SUB * SUB)
            q_rep = jnp.concatenate([q[i * SUB:(i + 1) * SUB]] * SUB, axis=0)
            a.append((q_rep * jnp.exp(pair[rows_i]) * k_rep[rows_i]).astype(BF16))
        r = _dot(jnp.concatenate(a, axis=0), bd)
        o_sub = []
        for i in range(n_sub):
            acc = None
            for sl in range(SUB):
                s = i * SUB + sl
                term = r[s * SUB:(s + 1) * SUB] * vv_ref[slot, s:s + 1, :]
                acc = term if acc is None else acc + term
            o_sub.append(acc)
        o = jnp.concatenate(o_sub, axis=0)

        q_t = q * jnp.exp(bc - bs)
        k_t = (k * jnp.exp(be - bc)).astype(BF16)
        lags = [q_t]
        for lag in range(2, n_sub):
            be_shift = jnp.concatenate([jnp.zeros((lag * SUB, HGRN_W), F32),
                                        be[0:CHUNK - lag * SUB]], axis=0)
            lags.append(q_t * jnp.exp(jnp.minimum(bs - be_shift, 0.0)))
        q_lags = jnp.concatenate(lags, axis=0)
        v16 = v.astype(BF16)
        o_pairs = []
        for pair in range(HGRN_HEADS // 2):
            cols = slice(pair * LANES, (pair + 1) * LANES)
            o_pair = None
            for half in range(2):
                own = (lane < HEAD_DIM) if half == 0 else (lane >= HEAD_DIM)
                sc = _dot_nt(jnp.where(own, q_lags[:, cols], 0.0).astype(BF16), k_t[:, cols])
                a_h = jnp.zeros((CHUNK, CHUNK), F32)
                for n, lag in enumerate(range(1, n_sub)):
                    a_h = jnp.where(sub_t - sub_s == lag, sc[n * CHUNK:(n + 1) * CHUNK], a_h)
                o_h = jnp.where(own, _dot(a_h.astype(BF16), v16[:, cols]), 0.0)
                o_pair = o_h if o_pair is None else o_pair + o_h
            o_pairs.append(o_pair)
        o = o + jnp.concatenate(o_pairs, axis=1)

        b_last = bc[CHUNK - 1:CHUNK, :]
        q_hat = (q * jnp.exp(bc)).astype(BF16)
        k_hat = (k * jnp.exp(b_last - bc)).astype(BF16)
        return rs, o, q_hat, k_hat, v16, jnp.exp(b_last)

    def finish(rs, o, q_hat, k_hat, v16, decay_last):
        st = st_ref[...]
        o = o + _dot_nt(q_hat, st.astype(BF16))
        st_ref[...] = st * decay_last + bmask_ref[...] * _dot_tn(v16, k_hat)

        oo = o * o
        hi = oo.astype(BF16)
        lo = (oo - hi.astype(F32)).astype(BF16)
        ms = (_dot(hi, bd) + _dot(lo, bd)) * (1.0 / HEAD_DIM)
        gate = g_ref[0, rs, :]
        y = o * lax.rsqrt(ms + EPS) * gain_ref[...] * (gate * _sigmoid(gate))
        o_ref[0, rs, :] = y.astype(BF16)

    def chunk_pair(cc, carry):
        parts = [within_chunk(2 * cc + slot, slot) for slot in range(2)]
        for part in parts:
            finish(*part)
        return carry

    lax.fori_loop(0, rows // (2 * CHUNK), chunk_pair, 0)


def _hgrn(hh, p, layer):
    b, l, _ = hh.shape
    rows = min(HGRN_ROWS, l)
    consts = [p["lb_logits"], p["hgrn_gain"], p["cum64"], p["rep64"], p["bd"], p["bmask"]]

    def section(k):
        return pl.BlockSpec((1, rows, HGRN_W), lambda bi, i: (bi, i, k))

    return pl.pallas_call(
        functools.partial(_hgrn_body, layer=layer, rows=rows),
        grid=(b, l // rows),
        in_specs=[section(0), section(1), section(2), section(3)]
        + [_const_spec(c.shape) for c in consts],
        out_specs=pl.BlockSpec((1, rows, HGRN_W), lambda bi, i: (bi, i, 0)),
        out_shape=jax.ShapeDtypeStruct((b, l, HGRN_W), BF16),
        scratch_shapes=[pltpu.VMEM((HGRN_W, HGRN_W), F32), pltpu.VMEM((2, CHUNK, HGRN_W), F32)],
        compiler_params=_params(("arbitrary", "arbitrary")),
        name="hgrn",
    )(hh, hh, hh, hh, *consts)


def _fox_body(qkb_ref, cpre_ref, qa_ref, ka_ref, vat_ref, o_ref, m_ref, acc_ref, *, tq, tk, n_pre):
    bi, hp, qi = pl.program_id(0), pl.program_id(1), pl.program_id(2)
    per_q = tq // tk
    row = lax.broadcasted_iota(jnp.int32, (tk, tq), 0)
    col = lax.broadcasted_iota(jnp.int32, (tk, tq), 1)

    def c_before(hh, n):
        return cpre_ref[(bi * FOX_HEADS + 2 * hp + hh) * n_pre + n]

    def scores(hh, j):
        return _dot_nt(ka_ref[0, hh, pl.ds(pl.multiple_of(j * tk, tk), tk), :], qa_ref[0, hh])

    def update(hh, j, s, mask):
        if mask is not None:
            s = jnp.where(mask, s, MASK_VALUE)
        m_old = m_ref[hh]
        m_new = jnp.maximum(m_old, jnp.max(s, axis=0, keepdims=True))
        p = jnp.exp2(s - m_new).astype(BF16)
        acc_ref[hh] = acc_ref[hh] * jnp.exp2(m_old - m_new) + _dot(vat_ref[0, hh, j], p)
        m_ref[hh] = m_new

    for hh in range(2):
        m_ref[hh] = jnp.full((1, tq), MASK_VALUE, F32)
        acc_ref[hh] = jnp.zeros((LANES, tq), F32)

    diag = [(hh, dj) for dj in range(per_q) for hh in range(2)]
    s_diag = [scores(hh, qi * per_q + dj) for hh, dj in diag]
    for (hh, dj), s in zip(diag, s_diag):
        update(hh, qi * per_q + dj, s, row + dj * tk <= col)

    def needed(j):
        out = None
        for hh in range(2):
            gap = (qkb_ref[0] + c_before(hh, qi * per_q) - c_before(hh, j + 1)
                   - jnp.min(m_ref[hh]))
            out = gap >= SKIP_BELOW if out is None else jnp.logical_or(out, gap >= SKIP_BELOW)
        return out.astype(jnp.int32)

    def cond(carry):
        g, go = carry
        return jnp.logical_and(g >= 0, go > 0)

    def body(carry):
        g, _ = carry
        group = [(hh, g * per_q + per_q - 1 - d) for d in range(per_q) for hh in range(2)]
        s_group = [scores(hh, j) for hh, j in group]
        for (hh, j), s in zip(group, s_group):
            update(hh, j, s, None)
        return g - 1, needed(jnp.maximum(g * per_q - 1, 0))

    lax.while_loop(cond, body, (qi - 1, needed(jnp.maximum(qi * per_q - 1, 0))))

    outs = []
    for hh in range(2):
        acc = acc_ref[hh]
        outs.append(acc[0:HEAD_DIM, :] / acc[HEAD_DIM:HEAD_DIM + 1, :])
    o_ref[0] = jnp.concatenate(outs, axis=0).T.astype(BF16)


def _fox(qkb, cpre, qa, ka, vat):
    b, _, l, _ = qa.shape
    tq, tk = min(Q_TILE, l), vat.shape[-1]
    smem = pl.BlockSpec(memory_space=pltpu.SMEM)
    return pl.pallas_call(
        functools.partial(_fox_body, tq=tq, tk=tk, n_pre=l // tk + 1),
        grid=(b, FOX_HEADS // 2, l // tq),
        in_specs=[
            smem, smem,
            pl.BlockSpec((1, 2, tq, LANES), lambda bi, hp, i: (bi, hp, i, 0)),
            pl.BlockSpec((1, 2, l, LANES), lambda bi, hp, i: (bi, hp, 0, 0)),
            pl.BlockSpec((1, 2, l // tk, LANES, tk), lambda bi, hp, i: (bi, hp, 0, 0, 0)),
        ],
        out_specs=pl.BlockSpec((1, tq, LANES), lambda bi, hp, i: (bi, i, hp)),
        out_shape=jax.ShapeDtypeStruct((b, l, FOX_W), BF16),
        scratch_shapes=[pltpu.VMEM((2, 1, tq), F32), pltpu.VMEM((2, LANES, tq), F32)],
        compiler_params=_params(("arbitrary", "arbitrary", "arbitrary")),
        name="fox",
    )(qkb, cpre, qa, ka, vat)


def _mix_out_body(x_ref, yc_ref, yh_ref, yf_ref, wc_ref, wh_ref, wf_ref, o_ref):
    o_ref[...] = (x_ref[...] + _dot(yc_ref[...], wc_ref[...]) + _dot(yh_ref[...], wh_ref[...])
                  + _dot(yf_ref[...], wf_ref[...]))


def _mix_out(x2d, yc, yh, yf, wc, wh, wf):
    n = x2d.shape[0]
    tm = min(ROW_TILE, n)

    def row(w):
        return pl.BlockSpec((tm, w), lambda i: (i, 0))

    return pl.pallas_call(
        _mix_out_body,
        grid=(n // tm,),
        in_specs=[row(D_MODEL), row(CONV_CH), row(HGRN_W), row(FOX_W),
                  _const_spec(wc.shape), _const_spec(wh.shape), _const_spec(wf.shape)],
        out_specs=row(D_MODEL),
        out_shape=jax.ShapeDtypeStruct((n, D_MODEL), F32),
        compiler_params=_params(("arbitrary",)),
        name="mix_out",
    )(x2d, yc, yh, yf, wc, wh, wf)


def _pad_heads(w):
    r = w.shape[0]
    w = w.reshape(r, FOX_HEADS, HEAD_DIM)
    return jnp.pad(w, ((0, 0), (0, 0), (0, LANES - HEAD_DIM))).reshape(r, FOX_HEADS * LANES)


def _pad_lanes(v, width=LANES):
    return jnp.pad(v, (0, width - v.shape[0])).reshape(1, width)


def _placement(first_lane, sign):
    m = np.zeros((N_PIECES * LANES, FOX_HEADS * LANES), np.float32)
    for p in range(N_PIECES):
        for h in range(FOX_HEADS):
            m[p * LANES + h, h * LANES + first_lane + p] = sign
    return jnp.asarray(m, BF16)


def _ones_lanes(first_lane):
    m = np.zeros((1, FOX_HEADS * LANES), np.float32)
    for h in range(FOX_HEADS):
        m[0, h * LANES + first_lane:h * LANES + first_lane + N_PIECES] = 1.0
    return jnp.asarray(m)


def _tri3(n):
    return jnp.asarray(np.tile(np.tril(np.ones((n, n), np.float32)), (1, N_PIECES)), BF16)


def _chunk_sums():
    t = np.arange(CHUNK)[:, None]
    r = np.arange(CHUNK)[None, :]
    first = (t // SUB) * SUB
    s_pair, t_pair = _sub_chunk_pairs()
    rows = [r <= t, r < first, r < first + SUB,
            (r > s_pair[:, None]) & (r <= t_pair[:, None])]
    m = np.concatenate(rows, axis=0).astype(np.float32)
    return jnp.asarray(np.tile(m, (1, N_PIECES)), BF16)


def _sub_chunk_pairs():
    s = np.repeat(np.arange(CHUNK), SUB)
    t = (s // SUB) * SUB + np.tile(np.arange(SUB), CHUNK)
    return s, t


def _causal_replicate():
    s, t = _sub_chunk_pairs()
    m = (np.arange(CHUNK)[None, :] == s[:, None]) & (t >= s)[:, None]
    return jnp.asarray(m.astype(np.float32), BF16)


def _ffn_weights(w_in, w_out):
    def tiles(w):
        return w.reshape(D_MODEL, N_FF_TILES, FF_TILE).transpose(1, 0, 2).astype(BF16)

    return (tiles(w_in[:, :D_FF]), tiles(w_in[:, D_FF:]),
            w_out.reshape(N_FF_TILES, FF_TILE, D_MODEL).astype(BF16))


def _layer_operands(l, tm, mix_norm, w_mix_in, conv_w, hgrn_lb_logits, hgrn_out_gain,
                    fox_q_gain, fox_k_gain, fox_f_bias, w_mix_out):
    w = w_mix_in[l]
    o_h = 3 * CONV_CH
    o_q = o_h + 4 * HGRN_W
    o_k, o_v, o_f = o_q + FOX_W, o_q + 2 * FOX_W, o_q + 3 * FOX_W
    head_blocks = np.kron(np.eye(HGRN_HEADS, dtype=np.float32),
                          np.ones((HEAD_DIM, HEAD_DIM), np.float32))
    wo = w_mix_out[l].astype(BF16)
    q_scale = fox_q_gain[l] * (HEAD_DIM ** -0.5 * LOG2E)
    return {
        "mix_norm": mix_norm[l].reshape(1, D_MODEL),
        "wc": w[:, :o_h].astype(BF16),
        "conv_w": jnp.pad(conv_w[l], ((0, 8 - CONV_WIDTH), (0, 0))),
        "wh": w[:, o_h:o_q].astype(BF16),
        "wq": _pad_heads(w[:, o_q:o_k]).astype(BF16),
        "wk": _pad_heads(w[:, o_k:o_v]).astype(BF16),
        "wvt": _pad_heads(w[:, o_v:o_f]).T.astype(BF16),
        "wf": jnp.pad(w[:, o_f:], ((0, 0), (0, LANES - FOX_HEADS))).astype(BF16),
        "fbias": _pad_lanes(fox_f_bias[l]),
        "gq": _pad_lanes(q_scale),
        "gk": _pad_lanes(fox_k_gain[l]),
        "qkb": (1.01 * HEAD_DIM * jnp.max(jnp.abs(q_scale))
                * jnp.max(jnp.abs(fox_k_gain[l]))).reshape(1),
        "tri": _tri3(tm),
        "plq": _placement(Q_C_LANE, 1.0),
        "plk": _placement(K_ONE_LANE + N_PIECES, -1.0),
        "oneq": _ones_lanes(Q_C_LANE + N_PIECES),
        "onek": _ones_lanes(K_ONE_LANE),
        "lb_logits": hgrn_lb_logits,
        "hgrn_gain": jnp.tile(hgrn_out_gain[l], HGRN_HEADS).reshape(1, HGRN_W),
        "cum64": _chunk_sums(),
        "rep64": _causal_replicate(),
        "bd": jnp.asarray(head_blocks, BF16),
        "bmask": jnp.asarray(head_blocks),
        "wo_c": wo[:CONV_CH],
        "wo_h": wo[CONV_CH:CONV_CH + HGRN_W],
        "wo_f": wo[CONV_CH + HGRN_W:],
    }


def kernel(x, ffn1_norm, ffn1_w_in, ffn1_w_out, mix_norm, w_mix_in, conv_w, hgrn_lb_logits,
           hgrn_out_gain, fox_q_gain, fox_k_gain, fox_f_bias, w_mix_out, ffn2_norm, ffn2_w_in,
           ffn2_w_out):
    b, l, d = x.shape
    assert d == D_MODEL and l % ROW_TILE == 0 and ROW_TILE == Q_TILE and Q_TILE % KV_TILE == 0
    depth = ffn1_norm.shape[0]
    tm = min(ROW_TILE, l)
    n = b * l
    for layer in range(depth):
        p = _layer_operands(layer, tm, mix_norm, w_mix_in, conv_w, hgrn_lb_logits, hgrn_out_gain,
                            fox_q_gain, fox_k_gain, fox_f_bias, w_mix_out)
        x2 = _ffn(x.reshape(n, d), ffn1_norm[layer].reshape(1, d),
                  *_ffn_weights(ffn1_w_in[layer], ffn1_w_out[layer]))
        yc, hh, qa, ka, vat, cend = _mix_in(x2.reshape(b, l, d), p)
        yh = _hgrn(hh, p, layer)
        cend = cend[:, :, :tm // KV_TILE, :FOX_HEADS].reshape(b, l // KV_TILE, FOX_HEADS)
        cpre = jnp.pad(cend.transpose(0, 2, 1), ((0, 0), (0, 0), (1, 0))) * LOG2E
        yf = _fox(p["qkb"], cpre.reshape(-1), qa, ka, vat)
        x2 = _mix_out(x2, yc.reshape(n, CONV_CH), yh.reshape(n, HGRN_W), yf.reshape(n, FOX_W),
                      p["wo_c"], p["wo_h"], p["wo_f"])
        x2 = _ffn(x2, ffn2_norm[layer].reshape(1, d),
                  *_ffn_weights(ffn2_w_in[layer], ffn2_w_out[layer]))
        x = x2.reshape(b, l, d)
    return x
```

```python
import functools

import jax
import jax.numpy as jnp
import numpy as np
from jax import lax
from jax.experimental import pallas as pl
from jax.experimental.pallas import tpu as pltpu

F32 = jnp.float32
BF16 = jnp.bfloat16

D_MODEL = 1024
D_FF = 2816
HEAD_DIM = 64
CONV_CH = 256
CONV_WIDTH = 3
HGRN_HEADS = 4
HGRN_W = 256
FOX_HEADS = 8
FOX_W = 512
CHUNK = 64
EPS = 1e-6
MASK_VALUE = -1e30

LANES = 128
FF_TILE = 256
N_FF_TILES = D_FF // FF_TILE
ROW_TILE = 512
Q_TILE = 512
KV_TILE = 256
HGRN_ROWS = 256
SUB = 16
VMEM_LIMIT = 56 * 1024 * 1024

Q_C_LANE = HEAD_DIM
K_ONE_LANE = HEAD_DIM
N_PIECES = 3
LOG2E = 1.4426950408889634
SKIP_BELOW = -153.0


def _dot(a, b):
    return jnp.dot(a, b, preferred_element_type=F32)


def _dot_nt(a, b):
    return lax.dot_general(a, b, (((1,), (1,)), ((), ())), preferred_element_type=F32)


def _dot_tn(a, b):
    return lax.dot_general(a, b, (((0,), (0,)), ((), ())), preferred_element_type=F32)


def _rms_norm(x, gain):
    inv = lax.rsqrt(jnp.mean(x * x, axis=-1, keepdims=True) + EPS)
    return x * inv * gain


def _sigmoid(x):
    return 1.0 / (1.0 + jnp.exp(-x))


def _log_sigmoid(x):
    return jnp.minimum(x, 0.0) - jnp.log(1.0 + jnp.exp(-jnp.abs(x)))


def _split3(x):
    hi = x.astype(BF16)
    r = x - hi.astype(F32)
    mid = r.astype(BF16)
    lo = (r - mid.astype(F32)).astype(BF16)
    return hi, mid, lo


def _const_spec(shape):
    nd = len(shape)
    return pl.BlockSpec(shape, lambda *_: (0,) * nd, pipeline_mode=pl.Buffered(1))


def _params(sem):
    return pltpu.CompilerParams(dimension_semantics=sem, vmem_limit_bytes=VMEM_LIMIT)


def _swiglu_half_step(x, g_ref, wg_ref, wu_ref, wo_ref, o_ref, acc_ref):
    xn = _rms_norm(x, g_ref[...]).astype(BF16)
    for c in range(N_FF_TILES):
        gate = _dot(xn, wg_ref[c])
        up = _dot(xn, wu_ref[c])
        act = (gate * _sigmoid(gate) * up).astype(BF16)
        part = _dot(act, wo_ref[c])
        if c == 0:
            acc_ref[...] = part
        else:
            acc_ref[...] += part
    o_ref[...] = x + 0.5 * acc_ref[...]


def _ffn_body(x_ref, g_ref, wg_ref, wu_ref, wo_ref, o_ref, acc_ref):
    _swiglu_half_step(x_ref[...], g_ref, wg_ref, wu_ref, wo_ref, o_ref, acc_ref)


def _mix_out_ffn_body(x_ref, yc_ref, yh_ref, yf_ref, wc_ref, wh_ref, wf_ref,
                      g_ref, wg_ref, wu_ref, wo_ref, o_ref, acc_ref):
    x = (x_ref[...] + _dot(yc_ref[...], wc_ref[...]) + _dot(yh_ref[...], wh_ref[...])
         + _dot(yf_ref[...], wf_ref[...]))
    _swiglu_half_step(x, g_ref, wg_ref, wu_ref, wo_ref, o_ref, acc_ref)


def _mix_out_ffn(x2d, yc, yh, yf, wc, wh, wf, gain, wg, wu, wo):
    n = x2d.shape[0]
    tm = min(ROW_TILE, n)

    def row(w):
        return pl.BlockSpec((tm, w), lambda i: (i, 0))

    consts = [wc, wh, wf, gain, wg, wu, wo]
    return pl.pallas_call(
        _mix_out_ffn_body,
        grid=(n // tm,),
        in_specs=[row(D_MODEL), row(CONV_CH), row(HGRN_W), row(FOX_W)]
        + [_const_spec(c.shape) for c in consts],
        out_specs=row(D_MODEL),
        out_shape=jax.ShapeDtypeStruct((n, D_MODEL), F32),
        scratch_shapes=[pltpu.VMEM((tm, D_MODEL), F32)],
        compiler_params=_params(("arbitrary",)),
        name="mix_out_ffn",
    )(x2d, yc, yh, yf, *consts)


def _ffn(x2d, gain, wg, wu, wo):
    n = x2d.shape[0]
    tm = min(ROW_TILE, n)
    row = pl.BlockSpec((tm, D_MODEL), lambda i: (i, 0))
    return pl.pallas_call(
        _ffn_body,
        grid=(n // tm,),
        in_specs=[row, _const_spec(gain.shape), _const_spec(wg.shape),
                  _const_spec(wu.shape), _const_spec(wo.shape)],
        out_specs=row,
        out_shape=jax.ShapeDtypeStruct((n, D_MODEL), F32),
        scratch_shapes=[pltpu.VMEM((tm, D_MODEL), F32)],
        compiler_params=_params(("arbitrary",)),
        name="ffn",
    )(x2d, gain, wg, wu, wo)


def _mix_in_body(x_ref, g_ref, wc_ref, cw_ref, wh_ref, wq_ref, wk_ref, wvt_ref, wf_ref,
                 fb_ref, gq_ref, gk_ref, tri_ref, place_ref,
                 yc_ref, hh_ref, qa_ref, ka_ref, vat_ref, cend_ref, ubuf_ref, carry_ref, *, tm):
    @pl.when(pl.program_id(1) == 0)
    def _():
        ubuf_ref[0:8, :] = jnp.zeros((8, CONV_CH), F32)
        carry_ref[...] = jnp.zeros_like(carry_ref)

    xn = _rms_norm(x_ref[0], g_ref[...]).astype(BF16)

    lf = _log_sigmoid(_dot(xn, wf_ref[...]) + fb_ref[...])
    c = carry_ref[...] + _dot(tri_ref[...], jnp.concatenate(_split3(lf), axis=0))
    carry_ref[...] = c[tm - 1:tm, :]
    ends = [c[(n + 1) * KV_TILE - 1:(n + 1) * KV_TILE, :] for n in range(tm // KV_TILE)]
    cend_ref[0, 0] = jnp.concatenate(ends + [jnp.zeros((8 - len(ends), LANES), F32)], axis=0)
    placed = _dot(jnp.concatenate(_split3(c * LOG2E), axis=1), place_ref[...])

    lane = lax.broadcasted_iota(jnp.int32, (1, LANES), 1)
    low = lane < HEAD_DIM
    q_takes_c = jnp.logical_and(lane >= Q_C_LANE, lane < Q_C_LANE + N_PIECES)
    k_takes_c = jnp.logical_and(lane >= K_ONE_LANE + N_PIECES, lane < K_ONE_LANE + 2 * N_PIECES)
    one_q = k_takes_c.astype(F32)
    one_k = q_takes_c.astype(F32)

    def head_pair_norm(x2, gain2):
        sq = x2 * x2
        ss_lo = jnp.sum(jnp.where(low, sq, 0.0), axis=-1, keepdims=True)
        ss_hi = jnp.sum(jnp.where(low, 0.0, sq), axis=-1, keepdims=True)
        inv = lax.rsqrt(jnp.where(low, ss_lo, ss_hi) * (1.0 / HEAD_DIM) + EPS)
        return x2 * inv * gain2

    hq = _dot(xn, wq_ref[...])
    hk = _dot(xn, wk_ref[...])
    for pair in range(FOX_HEADS // 2):
        cols = slice(pair * LANES, (pair + 1) * LANES)
        qn = head_pair_norm(hq[:, cols], gq_ref[...])
        kn = head_pair_norm(hk[:, cols], gk_ref[...])
        for half in range(2):
            h = 2 * pair + half
            extra = placed[:, h * LANES:(h + 1) * LANES]
            q_h = qn if half == 0 else pltpu.roll(qn, HEAD_DIM, axis=1)
            k_h = kn if half == 0 else pltpu.roll(kn, HEAD_DIM, axis=1)
            qa_ref[0, h] = jnp.where(low, q_h, jnp.where(q_takes_c, extra, one_q)).astype(BF16)
            ka_ref[0, h] = jnp.where(low, k_h, jnp.where(k_takes_c, extra, one_k)).astype(BF16)

    vt = _dot_nt(wvt_ref[...], xn).astype(BF16)
    tail = (lax.broadcasted_iota(jnp.int32, (LANES - HEAD_DIM, KV_TILE), 0) == 0).astype(BF16)
    for h in range(FOX_HEADS):
        for n in range(tm // KV_TILE):
            vat_ref[0, h, n] = jnp.concatenate(
                [vt[h * HEAD_DIM:(h + 1) * HEAD_DIM, n * KV_TILE:(n + 1) * KV_TILE], tail], axis=0)

    hh_ref[0] = _dot(xn, wh_ref[...])

    hc = _dot(xn, wc_ref[...])
    u = hc[:, 2 * CONV_CH:3 * CONV_CH] * hc[:, 0:CONV_CH]
    ubuf_ref[8:8 + tm, :] = u
    u1 = ubuf_ref[7:7 + tm, :]
    u2 = ubuf_ref[6:6 + tm, :]
    cw = cw_ref[...]
    conv = cw[0:1, :] * u2 + cw[1:2, :] * u1 + cw[2:3, :] * u
    yc_ref[0] = (hc[:, CONV_CH:2 * CONV_CH] * conv).astype(BF16)
    ubuf_ref[0:8, :] = u[tm - 8:tm, :]


def _mix_in(x, p):
    b, l, _ = x.shape
    tm = min(ROW_TILE, l)
    nt = l // tm
    consts = [p["mix_norm"], p["wc"], p["conv_w"], p["wh"], p["wq"], p["wk"], p["wvt"], p["wf"],
              p["fbias"], p["gq"], p["gk"], p["tri"], p["place"]]
    out_shape = [
        jax.ShapeDtypeStruct((b, l, CONV_CH), BF16),
        jax.ShapeDtypeStruct((b, l, 4 * HGRN_W), F32),
        jax.ShapeDtypeStruct((b, FOX_HEADS, l, LANES), BF16),
        jax.ShapeDtypeStruct((b, FOX_HEADS, l, LANES), BF16),
        jax.ShapeDtypeStruct((b, FOX_HEADS, l // KV_TILE, LANES, KV_TILE), BF16),
        jax.ShapeDtypeStruct((b, nt, 8, LANES), F32),
    ]
    out_specs = [
        pl.BlockSpec((1, tm, CONV_CH), lambda bi, i: (bi, i, 0)),
        pl.BlockSpec((1, tm, 4 * HGRN_W), lambda bi, i: (bi, i, 0)),
        pl.BlockSpec((1, FOX_HEADS, tm, LANES), lambda bi, i: (bi, 0, i, 0)),
        pl.BlockSpec((1, FOX_HEADS, tm, LANES), lambda bi, i: (bi, 0, i, 0)),
        pl.BlockSpec((1, FOX_HEADS, tm // KV_TILE, LANES, KV_TILE), lambda bi, i: (bi, 0, i, 0, 0)),
        pl.BlockSpec((1, 1, 8, LANES), lambda bi, i: (bi, i, 0, 0)),
    ]
    return pl.pallas_call(
        functools.partial(_mix_in_body, tm=tm),
        grid=(b, nt),
        in_specs=[pl.BlockSpec((1, tm, D_MODEL), lambda bi, i: (bi, i, 0))]
        + [_const_spec(c.shape) for c in consts],
        out_specs=out_specs,
        out_shape=out_shape,
        scratch_shapes=[pltpu.VMEM((tm + 8, CONV_CH), F32), pltpu.VMEM((1, LANES), F32)],
        compiler_params=_params(("arbitrary", "arbitrary")),
        name="mix_in",
    )(x, *consts)


def _hgrn_body(q_ref, z_ref, v_ref, g_ref, lbl_ref, gain_ref, cum_ref, rep_ref, bd_ref, bmask_ref,
               o_ref, st_ref, vv_ref, *, layer, rows):
    @pl.when(pl.program_id(1) == 0)
    def _():
        st_ref[...] = jnp.zeros_like(st_ref)

    lbl = lbl_ref[...]
    e = jnp.exp(lbl - jnp.max(lbl, axis=0, keepdims=True))
    soft = e / jnp.sum(e, axis=0, keepdims=True)
    lb = jnp.clip(jnp.sum(soft[0:layer + 1, :], axis=0, keepdims=True) - soft[0:1, :], 0.0, 1.0)

    bd = bd_ref[...]
    n_sub = CHUNK // SUB
    lane = lax.broadcasted_iota(jnp.int32, (1, LANES), 1)
    sub_t = lax.broadcasted_iota(jnp.int32, (CHUNK, CHUNK), 0) // SUB
    sub_s = lax.broadcasted_iota(jnp.int32, (CHUNK, CHUNK), 1) // SUB

    def within_chunk(ch, slot):
        rs = pl.ds(pl.multiple_of(ch * CHUNK, CHUNK), CHUNK)
        q = q_ref[0, rs, :]
        z = z_ref[0, rs, :]
        v = v_ref[0, rs, :]
        lf = _log_sigmoid(z) + jnp.log(1.0 + lb * jnp.exp(-z))
        k = (1.0 - lb) * _sigmoid(-z)
        sums = _dot(cum_ref[...], jnp.concatenate(_split3(lf), axis=0))
        bc = sums[0:CHUNK]
        bs = sums[CHUNK:2 * CHUNK]
        be = sums[2 * CHUNK:3 * CHUNK]
        pair = sums[3 * CHUNK:]
        vv_ref[slot] = v

        k_rep = _dot(rep_ref[...], k.astype(BF16))
        a = []
        for i in range(n_sub):
            rows_i = slice(i * SUB * SUB, (i + 1) * SUB * SUB)
            q_rep = jnp.concatenate([q[i * SUB:(i + 1) * SUB]] * SUB, axis=0)
            a.append((q_rep * jnp.exp(pair[rows_i]) * k_rep[rows_i]).astype(BF16))
        r = _dot(jnp.concatenate(a, axis=0), bd)
        o_sub = []
        for i in range(n_sub):
            acc = None
            for sl in range(SUB):
                s = i * SUB + sl
                term = r[s * SUB:(s + 1) * SUB] * vv_ref[slot, s:s + 1, :]
                acc = term if acc is None else acc + term
            o_sub.append(acc)
        o = jnp.concatenate(o_sub, axis=0)

        q_t = q * jnp.exp(bc - bs)
        k_t = (k * jnp.exp(be - bc)).astype(BF16)
        lags = [q_t]
        for lag in range(2, n_sub):
            be_shift = jnp.concatenate([jnp.zeros((lag * SUB, HGRN_W), F32),
                                        be[0:CHUNK - lag * SUB]], axis=0)
            lags.append(q_t * jnp.exp(jnp.minimum(bs - be_shift, 0.0)))
        q_lags = jnp.concatenate(lags, axis=0)
        v16 = v.astype(BF16)
        o_pairs = []
        for pair in range(HGRN_HEADS // 2):
            cols = slice(pair * LANES, (pair + 1) * LANES)
            o_pair = None
            for half in range(2):
                own = (lane < HEAD_DIM) if half == 0 else (lane >= HEAD_DIM)
                sc = _dot_nt(jnp.where(own, q_lags[:, cols], 0.0).astype(BF16), k_t[:, cols])
                a_h = jnp.zeros((CHUNK, CHUNK), F32)
                for n, lag in enumerate(range(1, n_sub)):
                    a_h = jnp.where(sub_t - sub_s == lag, sc[n * CHUNK:(n + 1) * CHUNK], a_h)
                o_h = jnp.where(own, _dot(a_h.astype(BF16), v16[:, cols]), 0.0)
                o_pair = o_h if o_pair is None else o_pair + o_h
            o_pairs.append(o_pair)
        o = o + jnp.concatenate(o_pairs, axis=1)

        b_last = bc[CHUNK - 1:CHUNK, :]
        q_hat = (q * jnp.exp(bc)).astype(BF16)
        k_hat = (k * jnp.exp(b_last - bc)).astype(BF16)
        return rs, o, q_hat, k_hat, v16, jnp.exp(b_last)

    def finish(rs, o, q_hat, k_hat, v16, decay_last):
        st = st_ref[...]
        o = o + _dot_nt(q_hat, st.astype(BF16))
        st_ref[...] = st * decay_last + bmask_ref[...] * _dot_tn(v16, k_hat)

        oo = o * o
        hi = oo.astype(BF16)
        lo = (oo - hi.astype(F32)).astype(BF16)
        ms = (_dot(hi, bd) + _dot(lo, bd)) * (1.0 / HEAD_DIM)
        gate = g_ref[0, rs, :]
        y = o * lax.rsqrt(ms + EPS) * gain_ref[...] * (gate * _sigmoid(gate))
        o_ref[0, rs, :] = y.astype(BF16)

    def chunk_pair(cc, carry):
        parts = [within_chunk(2 * cc + slot, slot) for slot in range(2)]
        for part in parts:
            finish(*part)
        return carry

    lax.fori_loop(0, rows // (2 * CHUNK), chunk_pair, 0)


def _hgrn(hh, p, layer):
    b, l, _ = hh.shape
    rows = min(HGRN_ROWS, l)
    consts = [p["lb_logits"], p["hgrn_gain"], p["cum64"], p["rep64"], p["bd"], p["bmask"]]

    def section(k):
        return pl.BlockSpec((1, rows, HGRN_W), lambda bi, i: (bi, i, k))

    return pl.pallas_call(
        functools.partial(_hgrn_body, layer=layer, rows=rows),
        grid=(b, l // rows),
        in_specs=[section(0), section(1), section(2), section(3)]
        + [_const_spec(c.shape) for c in consts],
        out_specs=pl.BlockSpec((1, rows, HGRN_W), lambda bi, i: (bi, i, 0)),
        out_shape=jax.ShapeDtypeStruct((b, l, HGRN_W), BF16),
        scratch_shapes=[pltpu.VMEM((HGRN_W, HGRN_W), F32), pltpu.VMEM((2, CHUNK, HGRN_W), F32)],
        compiler_params=_params(("arbitrary", "arbitrary")),
        name="hgrn",
    )(hh, hh, hh, hh, *consts)


def _fox_body(qkb_ref, cpre_ref, qa_ref, ka_ref, vat_ref, o_ref, m_ref, acc_ref, *, tq, tk, n_pre):
    bi, hp, qi = pl.program_id(0), pl.program_id(1), pl.program_id(2)
    per_q = tq // tk
    row = lax.broadcasted_iota(jnp.int32, (tk, tq), 0)
    col = lax.broadcasted_iota(jnp.int32, (tk, tq), 1)

    def c_before(hh, n):
        return cpre_ref[(bi * FOX_HEADS + 2 * hp + hh) * n_pre + n]

    def scores(hh, j):
        return _dot_nt(ka_ref[0, hh, pl.ds(pl.multiple_of(j * tk, tk), tk), :], qa_ref[0, hh])

    def update(hh, j, s, mask):
        if mask is not None:
            s = jnp.where(mask, s, MASK_VALUE)
        m_old = m_ref[hh]
        m_new = jnp.maximum(m_old, jnp.max(s, axis=0, keepdims=True))
        p = jnp.exp2(s - m_new).astype(BF16)
        acc_ref[hh] = acc_ref[hh] * jnp.exp2(m_old - m_new) + _dot(vat_ref[0, hh, j], p)
        m_ref[hh] = m_new

    for hh in range(2):
        m_ref[hh] = jnp.full((1, tq), MASK_VALUE, F32)
        acc_ref[hh] = jnp.zeros((LANES, tq), F32)

    diag = [(hh, dj) for dj in range(per_q) for hh in range(2)]
    s_diag = [scores(hh, qi * per_q + dj) for hh, dj in diag]
    for (hh, dj), s in zip(diag, s_diag):
        update(hh, qi * per_q + dj, s, row + dj * tk <= col)

    def needed(j):
        out = None
        for hh in range(2):
            gap = (qkb_ref[0] + c_before(hh, qi * per_q) - c_before(hh, j + 1)
                   - jnp.min(m_ref[hh]))
            out = gap >= SKIP_BELOW if out is None else jnp.logical_or(out, gap >= SKIP_BELOW)
        return out.astype(jnp.int32)

    def cond(carry):
        g, go = carry
        return jnp.logical_and(g >= 0, go > 0)

    def body(carry):
        g, _ = carry
        group = [(hh, g * per_q + per_q - 1 - d) for d in range(per_q) for hh in range(2)]
        s_group = [scores(hh, j) for hh, j in group]
        for (hh, j), s in zip(group, s_group):
            update(hh, j, s, None)
        return g - 1, needed(jnp.maximum(g * per_q - 1, 0))

    lax.while_loop(cond, body, (qi - 1, needed(jnp.maximum(qi * per_q - 1, 0))))

    outs = []
    for hh in range(2):
        acc = acc_ref[hh]
        outs.append(acc[0:HEAD_DIM, :] / acc[HEAD_DIM:HEAD_DIM + 1, :])
    o_ref[0] = jnp.concatenate(outs, axis=0).T.astype(BF16)


def _fox(qkb, cpre, qa, ka, vat):
    b, _, l, _ = qa.shape
    tq, tk = min(Q_TILE, l), vat.shape[-1]
    smem = pl.BlockSpec(memory_space=pltpu.SMEM)
    return pl.pallas_call(
        functools.partial(_fox_body, tq=tq, tk=tk, n_pre=l // tk + 1),
        grid=(b, FOX_HEADS // 2, l // tq),
        in_specs=[
            smem, smem,
            pl.BlockSpec((1, 2, tq, LANES), lambda bi, hp, i: (bi, hp, i, 0)),
            pl.BlockSpec((1, 2, l, LANES), lambda bi, hp, i: (bi, hp, 0, 0)),
            pl.BlockSpec((1, 2, l // tk, LANES, tk), lambda bi, hp, i: (bi, hp, 0, 0, 0)),
        ],
        out_specs=pl.BlockSpec((1, tq, LANES), lambda bi, hp, i: (bi, i, hp)),
        out_shape=jax.ShapeDtypeStruct((b, l, FOX_W), BF16),
        scratch_shapes=[pltpu.VMEM((2, 1, tq), F32), pltpu.VMEM((2, LANES, tq), F32)],
        compiler_params=_params(("arbitrary", "arbitrary", "arbitrary")),
        name="fox",
    )(qkb, cpre, qa, ka, vat)


def _pad_lanes(v, width=LANES):
    return jnp.pad(v, (0, width - v.shape[0])).reshape(1, width)


def _placement():
    m = np.zeros((N_PIECES * LANES, FOX_HEADS * LANES), np.float32)
    for p in range(N_PIECES):
        for h in range(FOX_HEADS):
            m[p * LANES + h, h * LANES + Q_C_LANE + p] = 1.0
            m[p * LANES + h, h * LANES + K_ONE_LANE + N_PIECES + p] = -1.0
    return jnp.asarray(m, BF16)


def _tri3(n):
    return jnp.asarray(np.tile(np.tril(np.ones((n, n), np.float32)), (1, N_PIECES)), BF16)


def _chunk_sums():
    t = np.arange(CHUNK)[:, None]
    r = np.arange(CHUNK)[None, :]
    first = (t // SUB) * SUB
    s_pair, t_pair = _sub_chunk_pairs()
    rows = [r <= t, r < first, r < first + SUB,
            (r > s_pair[:, None]) & (r <= t_pair[:, None])]
    m = np.concatenate(rows, axis=0).astype(np.float32)
    return jnp.asarray(np.tile(m, (1, N_PIECES)), BF16)


def _sub_chunk_pairs():
    s = np.repeat(np.arange(CHUNK), SUB)
    t = (s // SUB) * SUB + np.tile(np.arange(SUB), CHUNK)
    return s, t


def _causal_replicate():
    s, t = _sub_chunk_pairs()
    m = (np.arange(CHUNK)[None, :] == s[:, None]) & (t >= s)[:, None]
    return jnp.asarray(m.astype(np.float32), BF16)


def _ffn_weights(w_in, w_out):
    def tiles(w):
        return w.reshape(D_MODEL, N_FF_TILES, FF_TILE).transpose(1, 0, 2).astype(BF16)

    return (tiles(w_in[:, :D_FF]), tiles(w_in[:, D_FF:]),
            w_out.reshape(N_FF_TILES, FF_TILE, D_MODEL).astype(BF16))


def _layer_operands(l, tm, mix_norm, w_mix_in, conv_w, hgrn_lb_logits, hgrn_out_gain,
                    fox_q_gain, fox_k_gain, fox_f_bias, w_mix_out):
    w = w_mix_in[l]
    o_h = 3 * CONV_CH
    o_q = o_h + 4 * HGRN_W
    o_k, o_v, o_f = o_q + FOX_W, o_q + 2 * FOX_W, o_q + 3 * FOX_W
    head_blocks = np.kron(np.eye(HGRN_HEADS, dtype=np.float32),
                          np.ones((HEAD_DIM, HEAD_DIM), np.float32))
    wo = w_mix_out[l].astype(BF16)
    q_scale = fox_q_gain[l] * (HEAD_DIM ** -0.5 * LOG2E)
    return {
        "mix_norm": mix_norm[l].reshape(1, D_MODEL),
        "wc": w[:, :o_h].astype(BF16),
        "conv_w": jnp.pad(conv_w[l], ((0, 8 - CONV_WIDTH), (0, 0))),
        "wh": w[:, o_h:o_q].astype(BF16),
        "wq": w[:, o_q:o_k].astype(BF16),
        "wk": w[:, o_k:o_v].astype(BF16),
        "wvt": w[:, o_v:o_f].T.astype(BF16),
        "wf": jnp.pad(w[:, o_f:], ((0, 0), (0, LANES - FOX_HEADS))).astype(BF16),
        "fbias": _pad_lanes(fox_f_bias[l]),
        "gq": jnp.tile(q_scale, LANES // HEAD_DIM).reshape(1, LANES),
        "gk": jnp.tile(fox_k_gain[l], LANES // HEAD_DIM).reshape(1, LANES),
        "qkb": (1.01 * HEAD_DIM * jnp.max(jnp.abs(q_scale))
                * jnp.max(jnp.abs(fox_k_gain[l]))).reshape(1),
        "tri": _tri3(tm),
        "place": _placement(),
        "lb_logits": hgrn_lb_logits,
        "hgrn_gain": jnp.tile(hgrn_out_gain[l], HGRN_HEADS).reshape(1, HGRN_W),
        "cum64": _chunk_sums(),
        "rep64": _causal_replicate(),
        "bd": jnp.asarray(head_blocks, BF16),
        "bmask": jnp.asarray(head_blocks),
        "wo_c": wo[:CONV_CH],
        "wo_h": wo[CONV_CH:CONV_CH + HGRN_W],
        "wo_f": wo[CONV_CH + HGRN_W:],
    }


def kernel(x, ffn1_norm, ffn1_w_in, ffn1_w_out, mix_norm, w_mix_in, conv_w, hgrn_lb_logits,
           hgrn_out_gain, fox_q_gain, fox_k_gain, fox_f_bias, w_mix_out, ffn2_norm, ffn2_w_in,
           ffn2_w_out):
    b, l, d = x.shape
    assert d == D_MODEL and l % ROW_TILE == 0 and ROW_TILE == Q_TILE and Q_TILE % KV_TILE == 0
    depth = ffn1_norm.shape[0]
    tm = min(ROW_TILE, l)
    n = b * l
    for layer in range(depth):
        p = _layer_operands(layer, tm, mix_norm, w_mix_in, conv_w, hgrn_lb_logits, hgrn_out_gain,
                            fox_q_gain, fox_k_gain, fox_f_bias, w_mix_out)
        x2 = _ffn(x.reshape(n, d), ffn1_norm[layer].reshape(1, d),
                  *_ffn_weights(ffn1_w_in[layer], ffn1_w_out[layer]))
        yc, hh, qa, ka, vat, cend = _mix_in(x2.reshape(b, l, d), p)
        yh = _hgrn(hh, p, layer)
        cend = cend[:, :, :tm // KV_TILE, :FOX_HEADS].reshape(b, l // KV_TILE, FOX_HEADS)
        cpre = jnp.pad(cend.transpose(0, 2, 1), ((0, 0), (0, 0), (1, 0))) * LOG2E
        yf = _fox(p["qkb"], cpre.reshape(-1), qa, ka, vat)
        x2 = _mix_out_ffn(x2, yc.reshape(n, CONV_CH), yh.reshape(n, HGRN_W), yf.reshape(n, FOX_W),
                          p["wo_c"], p["wo_h"], p["wo_f"], ffn2_norm[layer].reshape(1, d),
                          *_ffn_weights(ffn2_w_in[layer], ffn2_w_out[layer]))
        x = x2.reshape(b, l, d)
    return x
```

```python
import functools

import jax
import jax.numpy as jnp
import numpy as np
from jax import lax
from jax.experimental import pallas as pl
from jax.experimental.pallas import tpu as pltpu

F32 = jnp.float32
BF16 = jnp.bfloat16

D_MODEL = 1024
D_FF = 2816
HEAD_DIM = 64
CONV_CH = 256
CONV_WIDTH = 3
HGRN_HEADS = 4
HGRN_W = 256
FOX_HEADS = 8
FOX_W = 512
CHUNK = 64
EPS = 1e-6
MASK_VALUE = -1e30

LANES = 128
FF_TILE = 256
N_FF_TILES = D_FF // FF_TILE
ROW_TILE = 512
Q_TILE = 512
KV_TILE = 256
HGRN_ROWS = 512
SUB = 16
CHUNKS_PER_TRIP = 4
VMEM_LIMIT = 56 * 1024 * 1024

Q_C_LANE = HEAD_DIM
K_ONE_LANE = HEAD_DIM
N_PIECES = 3
LOG2E = 1.4426950408889634
SKIP_BELOW = -153.0


def _dot(a, b):
    return jnp.dot(a, b, preferred_element_type=F32)


def _dot_nt(a, b):
    return lax.dot_general(a, b, (((1,), (1,)), ((), ())), preferred_element_type=F32)


def _dot_tn(a, b):
    return lax.dot_general(a, b, (((0,), (0,)), ((), ())), preferred_element_type=F32)


def _rms_norm(x, gain):
    inv = lax.rsqrt(jnp.mean(x * x, axis=-1, keepdims=True) + EPS)
    return x * inv * gain


def _sigmoid(x):
    return 1.0 / (1.0 + jnp.exp(-x))


def _log_sigmoid(x):
    return jnp.minimum(x, 0.0) - jnp.log(1.0 + jnp.exp(-jnp.abs(x)))


def _split3(x):
    hi = x.astype(BF16)
    r = x - hi.astype(F32)
    mid = r.astype(BF16)
    lo = (r - mid.astype(F32)).astype(BF16)
    return hi, mid, lo


def _const_spec(shape):
    nd = len(shape)
    return pl.BlockSpec(shape, lambda *_: (0,) * nd, pipeline_mode=pl.Buffered(1))


def _params(sem):
    return pltpu.CompilerParams(dimension_semantics=sem, vmem_limit_bytes=VMEM_LIMIT)


def _swiglu_half_step(x, g_ref, wg_ref, wu_ref, wo_ref, o_ref, acc_ref):
    xn = _rms_norm(x, g_ref[...]).astype(BF16)
    for c in range(N_FF_TILES):
        gate = _dot(xn, wg_ref[c])
        up = _dot(xn, wu_ref[c])
        act = (gate * _sigmoid(gate) * up).astype(BF16)
        part = _dot(act, wo_ref[c])
        if c == 0:
            acc_ref[...] = part
        else:
            acc_ref[...] += part
    o_ref[...] = x + 0.5 * acc_ref[...]


def _ffn_body(x_ref, g_ref, wg_ref, wu_ref, wo_ref, o_ref, acc_ref):
    _swiglu_half_step(x_ref[...], g_ref, wg_ref, wu_ref, wo_ref, o_ref, acc_ref)


def _mix_out_ffn_body(x_ref, yc_ref, yh_ref, yf_ref, wc_ref, wh_ref, wf_ref,
                      g_ref, wg_ref, wu_ref, wo_ref, o_ref, acc_ref):
    x = (x_ref[...] + _dot(yc_ref[...], wc_ref[...]) + _dot(yh_ref[...], wh_ref[...])
         + _dot(yf_ref[...], wf_ref[...]))
    _swiglu_half_step(x, g_ref, wg_ref, wu_ref, wo_ref, o_ref, acc_ref)


def _mix_out_ffn(x2d, yc, yh, yf, wc, wh, wf, gain, wg, wu, wo):
    n = x2d.shape[0]
    tm = min(ROW_TILE, n)

    def row(w):
        return pl.BlockSpec((tm, w), lambda i: (i, 0))

    consts = [wc, wh, wf, gain, wg, wu, wo]
    return pl.pallas_call(
        _mix_out_ffn_body,
        grid=(n // tm,),
        in_specs=[row(D_MODEL), row(CONV_CH), row(HGRN_W), row(FOX_W)]
        + [_const_spec(c.shape) for c in consts],
        out_specs=row(D_MODEL),
        out_shape=jax.ShapeDtypeStruct((n, D_MODEL), F32),
        scratch_shapes=[pltpu.VMEM((tm, D_MODEL), F32)],
        compiler_params=_params(("arbitrary",)),
        name="mix_out_ffn",
    )(x2d, yc, yh, yf, *consts)


def _ffn(x2d, gain, wg, wu, wo):
    n = x2d.shape[0]
    tm = min(ROW_TILE, n)
    row = pl.BlockSpec((tm, D_MODEL), lambda i: (i, 0))
    return pl.pallas_call(
        _ffn_body,
        grid=(n // tm,),
        in_specs=[row, _const_spec(gain.shape), _const_spec(wg.shape),
                  _const_spec(wu.shape), _const_spec(wo.shape)],
        out_specs=row,
        out_shape=jax.ShapeDtypeStruct((n, D_MODEL), F32),
        scratch_shapes=[pltpu.VMEM((tm, D_MODEL), F32)],
        compiler_params=_params(("arbitrary",)),
        name="ffn",
    )(x2d, gain, wg, wu, wo)


def _mix_in_body(x_ref, g_ref, wc_ref, cw_ref, wh_ref, wq_ref, wk_ref, wvt_ref, wf_ref,
                 fb_ref, gq_ref, gk_ref, tri_ref, place_ref,
                 yc_ref, hh_ref, qa_ref, ka_ref, vat_ref, cend_ref, ubuf_ref, carry_ref, *, tm):
    @pl.when(pl.program_id(1) == 0)
    def _():
        ubuf_ref[0:8, :] = jnp.zeros((8, CONV_CH), F32)
        carry_ref[...] = jnp.zeros_like(carry_ref)

    xn = _rms_norm(x_ref[0], g_ref[...]).astype(BF16)

    lf = _log_sigmoid(_dot(xn, wf_ref[...]) + fb_ref[...])
    c = carry_ref[...] + _dot(tri_ref[...], jnp.concatenate(_split3(lf), axis=0))
    carry_ref[...] = c[tm - 1:tm, :]
    ends = [c[(n + 1) * KV_TILE - 1:(n + 1) * KV_TILE, :] for n in range(tm // KV_TILE)]
    cend_ref[0, 0] = jnp.concatenate(ends + [jnp.zeros((8 - len(ends), LANES), F32)], axis=0)
    placed = _dot(jnp.concatenate(_split3(c * LOG2E), axis=1), place_ref[...])

    lane = lax.broadcasted_iota(jnp.int32, (1, LANES), 1)
    low = lane < HEAD_DIM
    q_takes_c = jnp.logical_and(lane >= Q_C_LANE, lane < Q_C_LANE + N_PIECES)
    k_takes_c = jnp.logical_and(lane >= K_ONE_LANE + N_PIECES, lane < K_ONE_LANE + 2 * N_PIECES)
    one_q = k_takes_c.astype(F32)
    one_k = q_takes_c.astype(F32)

    def head_pair_norm(x2, gain2):
        sq = x2 * x2
        ss_lo = jnp.sum(jnp.where(low, sq, 0.0), axis=-1, keepdims=True)
        ss_hi = jnp.sum(jnp.where(low, 0.0, sq), axis=-1, keepdims=True)
        inv = lax.rsqrt(jnp.where(low, ss_lo, ss_hi) * (1.0 / HEAD_DIM) + EPS)
        return x2 * inv * gain2

    hq = _dot(xn, wq_ref[...])
    hk = _dot(xn, wk_ref[...])
    for pair in range(FOX_HEADS // 2):
        cols = slice(pair * LANES, (pair + 1) * LANES)
        qn = head_pair_norm(hq[:, cols], gq_ref[...])
        kn = head_pair_norm(hk[:, cols], gk_ref[...])
        for half in range(2):
            h = 2 * pair + half
            extra = placed[:, h * LANES:(h + 1) * LANES]
            q_h = qn if half == 0 else pltpu.roll(qn, HEAD_DIM, axis=1)
            k_h = kn if half == 0 else pltpu.roll(kn, HEAD_DIM, axis=1)
            qa_ref[0, h] = jnp.where(low, q_h, jnp.where(q_takes_c, extra, one_q)).astype(BF16)
            ka_ref[0, h] = jnp.where(low, k_h, jnp.where(k_takes_c, extra, one_k)).astype(BF16)

    vt = _dot_nt(wvt_ref[...], xn).astype(BF16)
    tail = (lax.broadcasted_iota(jnp.int32, (LANES - HEAD_DIM, KV_TILE), 0) == 0).astype(BF16)
    for h in range(FOX_HEADS):
        for n in range(tm // KV_TILE):
            vat_ref[0, h, n] = jnp.concatenate(
                [vt[h * HEAD_DIM:(h + 1) * HEAD_DIM, n * KV_TILE:(n + 1) * KV_TILE], tail], axis=0)

    hh_ref[0] = _dot(xn, wh_ref[...])

    hc = _dot(xn, wc_ref[...])
    u = hc[:, 2 * CONV_CH:3 * CONV_CH] * hc[:, 0:CONV_CH]
    ubuf_ref[8:8 + tm, :] = u
    u1 = ubuf_ref[7:7 + tm, :]
    u2 = ubuf_ref[6:6 + tm, :]
    cw = cw_ref[...]
    conv = cw[0:1, :] * u2 + cw[1:2, :] * u1 + cw[2:3, :] * u
    yc_ref[0] = (hc[:, CONV_CH:2 * CONV_CH] * conv).astype(BF16)
    ubuf_ref[0:8, :] = u[tm - 8:tm, :]


def _mix_in(x, p):
    b, l, _ = x.shape
    tm = min(ROW_TILE, l)
    nt = l // tm
    consts = [p["mix_norm"], p["wc"], p["conv_w"], p["wh"], p["wq"], p["wk"], p["wvt"], p["wf"],
              p["fbias"], p["gq"], p["gk"], p["tri"], p["place"]]
    out_shape = [
        jax.ShapeDtypeStruct((b, l, CONV_CH), BF16),
        jax.ShapeDtypeStruct((b, l, 4 * HGRN_W), F32),
        jax.ShapeDtypeStruct((b, FOX_HEADS, l, LANES), BF16),
        jax.ShapeDtypeStruct((b, FOX_HEADS, l, LANES), BF16),
        jax.ShapeDtypeStruct((b, FOX_HEADS, l // KV_TILE, LANES, KV_TILE), BF16),
        jax.ShapeDtypeStruct((b, nt, 8, LANES), F32),
    ]
    out_specs = [
        pl.BlockSpec((1, tm, CONV_CH), lambda bi, i: (bi, i, 0)),
        pl.BlockSpec((1, tm, 4 * HGRN_W), lambda bi, i: (bi, i, 0)),
        pl.BlockSpec((1, FOX_HEADS, tm, LANES), lambda bi, i: (bi, 0, i, 0)),
        pl.BlockSpec((1, FOX_HEADS, tm, LANES), lambda bi, i: (bi, 0, i, 0)),
        pl.BlockSpec((1, FOX_HEADS, tm // KV_TILE, LANES, KV_TILE), lambda bi, i: (bi, 0, i, 0, 0)),
        pl.BlockSpec((1, 1, 8, LANES), lambda bi, i: (bi, i, 0, 0)),
    ]
    return pl.pallas_call(
        functools.partial(_mix_in_body, tm=tm),
        grid=(b, nt),
        in_specs=[pl.BlockSpec((1, tm, D_MODEL), lambda bi, i: (bi, i, 0))]
        + [_const_spec(c.shape) for c in consts],
        out_specs=out_specs,
        out_shape=out_shape,
        scratch_shapes=[pltpu.VMEM((tm + 8, CONV_CH), F32), pltpu.VMEM((1, LANES), F32)],
        compiler_params=_params(("arbitrary", "arbitrary")),
        name="mix_in",
    )(x, *consts)


def _hgrn_body(q_ref, z_ref, v_ref, g_ref, lbl_ref, gain_ref, cum_ref, rep_ref, bd_ref, bmask_ref,
               o_ref, st_ref, vv_ref, *, layer, rows):
    @pl.when(pl.program_id(1) == 0)
    def _():
        st_ref[...] = jnp.zeros_like(st_ref)

    lbl = lbl_ref[...]
    e = jnp.exp(lbl - jnp.max(lbl, axis=0, keepdims=True))
    soft = e / jnp.sum(e, axis=0, keepdims=True)
    lb = jnp.clip(jnp.sum(soft[0:layer + 1, :], axis=0, keepdims=True) - soft[0:1, :], 0.0, 1.0)

    bd = bd_ref[...]
    n_sub = CHUNK // SUB
    lane = lax.broadcasted_iota(jnp.int32, (1, LANES), 1)
    sub_t = lax.broadcasted_iota(jnp.int32, (CHUNK, CHUNK), 0) // SUB
    sub_s = lax.broadcasted_iota(jnp.int32, (CHUNK, CHUNK), 1) // SUB

    def within_chunk(ch, slot):
        rs = pl.ds(pl.multiple_of(ch * CHUNK, CHUNK), CHUNK)
        q = q_ref[0, rs, :]
        z = z_ref[0, rs, :]
        v = v_ref[0, rs, :]
        lf = _log_sigmoid(z) + jnp.log(1.0 + lb * jnp.exp(-z))
        k = (1.0 - lb) * _sigmoid(-z)
        pieces = jnp.concatenate(_split3(lf), axis=0)
        sums = _dot(cum_ref[0:3 * CHUNK, :], pieces)
        vv_ref[slot] = v
        v16 = v.astype(BF16)
        k16 = k.astype(BF16)
        yield

        bc = sums[0:CHUNK]
        bs = sums[CHUNK:2 * CHUNK]
        be = sums[2 * CHUNK:3 * CHUNK]

        q_t = q * jnp.exp(bc - bs)
        k_t = (k * jnp.exp(be - bc)).astype(BF16)
        lags = [q_t]
        for lag in range(2, n_sub):
            be_shift = jnp.concatenate([jnp.zeros((lag * SUB, HGRN_W), F32),
                                        be[0:CHUNK - lag * SUB]], axis=0)
            lags.append(q_t * jnp.exp(jnp.minimum(bs - be_shift, 0.0)))
        q_lags = jnp.concatenate(lags, axis=0)
        heads = [(slice(h // 2 * LANES, (h // 2 + 1) * LANES),
                  (lane < HEAD_DIM) if h % 2 == 0 else (lane >= HEAD_DIM))
                 for h in range(HGRN_HEADS)]
        sc = [_dot_nt(jnp.where(own, q_lags[:, cols], 0.0).astype(BF16), k_t[:, cols])
              for cols, own in heads]

        half_rows = CHUNK * SUB // 2
        pair = jnp.concatenate(
            [_dot(cum_ref[3 * CHUNK + n * half_rows:3 * CHUNK + (n + 1) * half_rows, :], pieces)
             for n in range(2)], axis=0)
        k_rep = jnp.concatenate(
            [_dot(rep_ref[n * half_rows:(n + 1) * half_rows, :], k16) for n in range(2)], axis=0)
        yield

        o_heads = []
        for (cols, own), sc_h in zip(heads, sc):
            a_h = jnp.zeros((CHUNK, CHUNK), F32)
            for n, lag in enumerate(range(1, n_sub)):
                a_h = jnp.where(sub_t - sub_s == lag, sc_h[n * CHUNK:(n + 1) * CHUNK], a_h)
            o_heads.append(_dot(a_h.astype(BF16), v16[:, cols]))
        a = []
        for i in range(n_sub):
            rows_i = slice(i * SUB * SUB, (i + 1) * SUB * SUB)
            q_rep = jnp.concatenate([q[i * SUB:(i + 1) * SUB]] * SUB, axis=0)
            a.append((q_rep * jnp.exp(pair[rows_i]) * k_rep[rows_i]).astype(BF16))
        r = jnp.concatenate(
            [_dot(jnp.concatenate(a[n * n_sub // 2:(n + 1) * n_sub // 2], axis=0), bd)
             for n in range(2)], axis=0)
        b_last = bc[CHUNK - 1:CHUNK, :]
        q_hat = (q * jnp.exp(bc)).astype(BF16)
        k_hat = (k * jnp.exp(b_last - bc)).astype(BF16)
        yield

        o_lag = [jnp.where(heads[h][1], o_heads[h], 0.0)
                 + jnp.where(heads[h + 1][1], o_heads[h + 1], 0.0)
                 for h in range(0, HGRN_HEADS, 2)]
        o_sub = []
        for i in range(n_sub):
            acc = None
            for sl in range(SUB):
                s = i * SUB + sl
                term = r[s * SUB:(s + 1) * SUB] * vv_ref[slot, s:s + 1, :]
                acc = term if acc is None else acc + term
            o_sub.append(acc)
        o = jnp.concatenate(o_lag, axis=1) + jnp.concatenate(o_sub, axis=0)
        return rs, o, q_hat, k_hat, v16, jnp.exp(b_last)

    def finish(rs, o, q_hat, k_hat, v16, decay_last):
        st = st_ref[...]
        o = o + _dot_nt(q_hat, st.astype(BF16))
        st_ref[...] = st * decay_last + bmask_ref[...] * _dot_tn(v16, k_hat)

        oo = o * o
        hi = oo.astype(BF16)
        lo = (oo - hi.astype(F32)).astype(BF16)
        ms = (_dot(hi, bd) + _dot(lo, bd)) * (1.0 / HEAD_DIM)
        gate = g_ref[0, rs, :]
        y = o * lax.rsqrt(ms + EPS) * gain_ref[...] * (gate * _sigmoid(gate))
        o_ref[0, rs, :] = y.astype(BF16)

    def chunk_group(cc, carry):
        gens = [within_chunk(CHUNKS_PER_TRIP * cc + slot, slot) for slot in range(CHUNKS_PER_TRIP)]
        parts = [None] * CHUNKS_PER_TRIP
        while any(part is None for part in parts):
            for slot, gen in enumerate(gens):
                if parts[slot] is None:
                    try:
                        next(gen)
                    except StopIteration as done:
                        parts[slot] = done.value
        for part in parts:
            finish(*part)
        return carry

    lax.fori_loop(0, rows // (CHUNKS_PER_TRIP * CHUNK), chunk_group, 0)


def _hgrn(hh, p, layer):
    b, l, _ = hh.shape
    rows = min(HGRN_ROWS, l)
    consts = [p["lb_logits"], p["hgrn_gain"], p["cum64"], p["rep64"], p["bd"], p["bmask"]]

    def section(k):
        return pl.BlockSpec((1, rows, HGRN_W), lambda bi, i: (bi, i, k))

    return pl.pallas_call(
        functools.partial(_hgrn_body, layer=layer, rows=rows),
        grid=(b, l // rows),
        in_specs=[section(0), section(1), section(2), section(3)]
        + [_const_spec(c.shape) for c in consts],
        out_specs=pl.BlockSpec((1, rows, HGRN_W), lambda bi, i: (bi, i, 0)),
        out_shape=jax.ShapeDtypeStruct((b, l, HGRN_W), BF16),
        scratch_shapes=[pltpu.VMEM((HGRN_W, HGRN_W), F32), pltpu.VMEM((CHUNKS_PER_TRIP, CHUNK, HGRN_W), F32)],
        compiler_params=_params(("arbitrary", "arbitrary")),
        name="hgrn",
    )(hh, hh, hh, hh, *consts)


def _fox_body(qkb_ref, cpre_ref, qa_ref, ka_ref, vat_ref, o_ref, m_ref, acc_ref, s_ref,
              *, tq, tk, n_pre):
    bi, hp, qi = pl.program_id(0), pl.program_id(1), pl.program_id(2)
    per_q = tq // tk
    row = lax.broadcasted_iota(jnp.int32, (tk, tq), 0)
    col = lax.broadcasted_iota(jnp.int32, (tk, tq), 1)

    def c_before(hh, n):
        return cpre_ref[(bi * FOX_HEADS + 2 * hp + hh) * n_pre + n]

    def scores(hh, j):
        return _dot_nt(ka_ref[0, hh, pl.ds(pl.multiple_of(j * tk, tk), tk), :], qa_ref[0, hh])

    def update(hh, j, s, mask):
        if mask is not None:
            s = jnp.where(mask, s, MASK_VALUE)
        m_old = m_ref[hh]
        m_new = jnp.maximum(m_old, jnp.max(s, axis=0, keepdims=True))
        p = jnp.exp2(s - m_new).astype(BF16)
        acc_ref[hh] = acc_ref[hh] * jnp.exp2(m_old - m_new) + _dot(vat_ref[0, hh, j], p)
        m_ref[hh] = m_new

    for hh in range(2):
        m_ref[hh] = jnp.full((1, tq), MASK_VALUE, F32)
        acc_ref[hh] = jnp.zeros((LANES, tq), F32)

    diag = [(hh, dj) for dj in range(per_q) for hh in range(2)]
    s_diag = [scores(hh, qi * per_q + dj) for hh, dj in diag]
    j_first = jnp.maximum(qi * per_q - 1, 0)
    s_first = [scores(hh, j_first) for hh in range(2)]
    for (hh, dj), s in zip(diag, s_diag):
        update(hh, qi * per_q + dj, s, row + dj * tk <= col)

    def needed(j):
        out = None
        for hh in range(2):
            gap = (qkb_ref[0] + c_before(hh, qi * per_q) - c_before(hh, j + 1)
                   - jnp.min(m_ref[hh]))
            out = gap >= SKIP_BELOW if out is None else jnp.logical_or(out, gap >= SKIP_BELOW)
        return out.astype(jnp.int32)

    def cond(carry):
        j, go = carry
        return jnp.logical_and(j >= 0, go > 0)

    def stage(src, dst, j):
        j_next = jnp.maximum(j - 1, 0)
        for hh in range(2):
            s_ref[dst, hh] = scores(hh, j_next)
        for hh in range(2):
            update(hh, j, s_ref[src, hh], None)
        return jnp.where(j >= 1, needed(j_next), 0)

    def body(carry):
        j, _ = carry
        go = stage(0, 1, j)
        go = lax.cond(go > 0, lambda: stage(1, 0, j - 1), lambda: jnp.int32(0))
        return j - 2, go

    for hh in range(2):
        s_ref[0, hh] = s_first[hh]
    lax.while_loop(cond, body, (qi * per_q - 1, needed(j_first)))

    outs = []
    for hh in range(2):
        acc = acc_ref[hh]
        outs.append(acc[0:HEAD_DIM, :] / acc[HEAD_DIM:HEAD_DIM + 1, :])
    o_ref[0] = jnp.concatenate(outs, axis=0).T.astype(BF16)


def _fox(qkb, cpre, qa, ka, vat):
    b, _, l, _ = qa.shape
    tq, tk = min(Q_TILE, l), vat.shape[-1]
    smem = pl.BlockSpec(memory_space=pltpu.SMEM)
    return pl.pallas_call(
        functools.partial(_fox_body, tq=tq, tk=tk, n_pre=l // tk + 1),
        grid=(b, FOX_HEADS // 2, l // tq),
        in_specs=[
            smem, smem,
            pl.BlockSpec((1, 2, tq, LANES), lambda bi, hp, i: (bi, hp, i, 0)),
            pl.BlockSpec((1, 2, l, LANES), lambda bi, hp, i: (bi, hp, 0, 0)),
            pl.BlockSpec((1, 2, l // tk, LANES, tk), lambda bi, hp, i: (bi, hp, 0, 0, 0)),
        ],
        out_specs=pl.BlockSpec((1, tq, LANES), lambda bi, hp, i: (bi, i, hp)),
        out_shape=jax.ShapeDtypeStruct((b, l, FOX_W), BF16),
        scratch_shapes=[pltpu.VMEM((2, 1, tq), F32), pltpu.VMEM((2, LANES, tq), F32),
                        pltpu.VMEM((2, 2, tk, tq), F32)],
        compiler_params=_params(("arbitrary", "arbitrary", "arbitrary")),
        name="fox",
    )(qkb, cpre, qa, ka, vat)


def _pad_lanes(v, width=LANES):
    return jnp.pad(v, (0, width - v.shape[0])).reshape(1, width)


def _placement():
    m = np.zeros((N_PIECES * LANES, FOX_HEADS * LANES), np.float32)
    for p in range(N_PIECES):
        for h in range(FOX_HEADS):
            m[p * LANES + h, h * LANES + Q_C_LANE + p] = 1.0
            m[p * LANES + h, h * LANES + K_ONE_LANE + N_PIECES + p] = -1.0
    return jnp.asarray(m, BF16)


def _tri3(n):
    return jnp.asarray(np.tile(np.tril(np.ones((n, n), np.float32)), (1, N_PIECES)), BF16)


def _chunk_sums():
    t = np.arange(CHUNK)[:, None]
    r = np.arange(CHUNK)[None, :]
    first = (t // SUB) * SUB
    s_pair, t_pair = _sub_chunk_pairs()
    rows = [r <= t, r < first, r < first + SUB,
            (r > s_pair[:, None]) & (r <= t_pair[:, None])]
    m = np.concatenate(rows, axis=0).astype(np.float32)
    return jnp.asarray(np.tile(m, (1, N_PIECES)), BF16)


def _sub_chunk_pairs():
    s = np.repeat(np.arange(CHUNK), SUB)
    t = (s // SUB) * SUB + np.tile(np.arange(SUB), CHUNK)
    return s, t


def _causal_replicate():
    s, t = _sub_chunk_pairs()
    m = (np.arange(CHUNK)[None, :] == s[:, None]) & (t >= s)[:, None]
    return jnp.asarray(m.astype(np.float32), BF16)


def _ffn_weights(w_in, w_out):
    def tiles(w):
        return w.reshape(D_MODEL, N_FF_TILES, FF_TILE).transpose(1, 0, 2).astype(BF16)

    return (tiles(w_in[:, :D_FF]), tiles(w_in[:, D_FF:]),
            w_out.reshape(N_FF_TILES, FF_TILE, D_MODEL).astype(BF16))


def _layer_operands(l, tm, mix_norm, w_mix_in, conv_w, hgrn_lb_logits, hgrn_out_gain,
                    fox_q_gain, fox_k_gain, fox_f_bias, w_mix_out):
    w = w_mix_in[l]
    o_h = 3 * CONV_CH
    o_q = o_h + 4 * HGRN_W
    o_k, o_v, o_f = o_q + FOX_W, o_q + 2 * FOX_W, o_q + 3 * FOX_W
    head_blocks = np.kron(np.eye(HGRN_HEADS, dtype=np.float32),
                          np.ones((HEAD_DIM, HEAD_DIM), np.float32))
    wo = w_mix_out[l].astype(BF16)
    q_scale = fox_q_gain[l] * (HEAD_DIM ** -0.5 * LOG2E)
    return {
        "mix_norm": mix_norm[l].reshape(1, D_MODEL),
        "wc": w[:, :o_h].astype(BF16),
        "conv_w": jnp.pad(conv_w[l], ((0, 8 - CONV_WIDTH), (0, 0))),
        "wh": w[:, o_h:o_q].astype(BF16),
        "wq": w[:, o_q:o_k].astype(BF16),
        "wk": w[:, o_k:o_v].astype(BF16),
        "wvt": w[:, o_v:o_f].T.astype(BF16),
        "wf": jnp.pad(w[:, o_f:], ((0, 0), (0, LANES - FOX_HEADS))).astype(BF16),
        "fbias": _pad_lanes(fox_f_bias[l]),
        "gq": jnp.tile(q_scale, LANES // HEAD_DIM).reshape(1, LANES),
        "gk": jnp.tile(fox_k_gain[l], LANES // HEAD_DIM).reshape(1, LANES),
        "qkb": (1.01 * HEAD_DIM * jnp.max(jnp.abs(q_scale))
                * jnp.max(jnp.abs(fox_k_gain[l]))).reshape(1),
        "tri": _tri3(tm),
        "place": _placement(),
        "lb_logits": hgrn_lb_logits,
        "hgrn_gain": jnp.tile(hgrn_out_gain[l], HGRN_HEADS).reshape(1, HGRN_W),
        "cum64": _chunk_sums(),
        "rep64": _causal_replicate(),
        "bd": jnp.asarray(head_blocks, BF16),
        "bmask": jnp.asarray(head_blocks),
        "wo_c": wo[:CONV_CH],
        "wo_h": wo[CONV_CH:CONV_CH + HGRN_W],
        "wo_f": wo[CONV_CH + HGRN_W:],
    }


def kernel(x, ffn1_norm, ffn1_w_in, ffn1_w_out, mix_norm, w_mix_in, conv_w, hgrn_lb_logits,
           hgrn_out_gain, fox_q_gain, fox_k_gain, fox_f_bias, w_mix_out, ffn2_norm, ffn2_w_in,
           ffn2_w_out):
    b, l, d = x.shape
    assert d == D_MODEL and l % ROW_TILE == 0 and ROW_TILE == Q_TILE and Q_TILE % KV_TILE == 0
    depth = ffn1_norm.shape[0]
    tm = min(ROW_TILE, l)
    n = b * l
    for layer in range(depth):
        p = _layer_operands(layer, tm, mix_norm, w_mix_in, conv_w, hgrn_lb_logits, hgrn_out_gain,
                            fox_q_gain, fox_k_gain, fox_f_bias, w_mix_out)
        x2 = _ffn(x.reshape(n, d), ffn1_norm[layer].reshape(1, d),
                  *_ffn_weights(ffn1_w_in[layer], ffn1_w_out[layer]))
        yc, hh, qa, ka, vat, cend = _mix_in(x2.reshape(b, l, d), p)
        yh = _hgrn(hh, p, layer)
        cend = cend[:, :, :tm // KV_TILE, :FOX_HEADS].reshape(b, l // KV_TILE, FOX_HEADS)
        cpre = jnp.pad(cend.transpose(0, 2, 1), ((0, 0), (0, 0), (1, 0))) * LOG2E
        yf = _fox(p["qkb"], cpre.reshape(-1), qa, ka, vat)
        x2 = _mix_out_ffn(x2, yc.reshape(n, CONV_CH), yh.reshape(n, HGRN_W), yf.reshape(n, FOX_W),
                          p["wo_c"], p["wo_h"], p["wo_f"], ffn2_norm[layer].reshape(1, d),
                          *_ffn_weights(ffn2_w_in[layer], ffn2_w_out[layer]))
        x = x2.reshape(b, l, d)
    return x
```

```python
import functools

import jax
import jax.numpy as jnp
import numpy as np
from jax import lax
from jax.experimental import pallas as pl
from jax.experimental.pallas import tpu as pltpu

F32 = jnp.float32
BF16 = jnp.bfloat16

D_MODEL = 1024
D_FF = 2816
HEAD_DIM = 64
CONV_CH = 256
CONV_WIDTH = 3
HGRN_HEADS = 4
HGRN_W = 256
FOX_HEADS = 8
FOX_W = 512
CHUNK = 64
EPS = 1e-6
MASK_VALUE = -1e30

LANES = 128
FF_TILE = 256
N_FF_TILES = D_FF // FF_TILE
ROW_TILE = 512
Q_TILE = 512
KV_TILE = 256
HGRN_ROWS = 512
SUB = 16
CHUNKS_PER_TRIP = 4
VMEM_LIMIT = 56 * 1024 * 1024

Q_C_LANE = HEAD_DIM
K_ONE_LANE = HEAD_DIM
N_PIECES = 3
LOG2E = 1.4426950408889634
SKIP_BELOW = -138.0


def _dot(a, b):
    return jnp.dot(a, b, preferred_element_type=F32)


def _dot_nt(a, b):
    return lax.dot_general(a, b, (((1,), (1,)), ((), ())), preferred_element_type=F32)


def _dot_tn(a, b):
    return lax.dot_general(a, b, (((0,), (0,)), ((), ())), preferred_element_type=F32)


def _rms_norm(x, gain):
    inv = lax.rsqrt(jnp.mean(x * x, axis=-1, keepdims=True) + EPS)
    return x * inv * gain


def _sigmoid(x):
    return 1.0 / (1.0 + jnp.exp(-x))


def _log_sigmoid(x):
    return jnp.minimum(x, 0.0) - jnp.log(1.0 + jnp.exp(-jnp.abs(x)))


def _split3(x):
    hi = x.astype(BF16)
    r = x - hi.astype(F32)
    mid = r.astype(BF16)
    lo = (r - mid.astype(F32)).astype(BF16)
    return hi, mid, lo


def _const_spec(shape):
    nd = len(shape)
    return pl.BlockSpec(shape, lambda *_: (0,) * nd, pipeline_mode=pl.Buffered(1))


def _params(sem):
    return pltpu.CompilerParams(dimension_semantics=sem, vmem_limit_bytes=VMEM_LIMIT)


def _swiglu_half_step(x, g_ref, wi_ref, wo_ref, o_ref, acc_ref):
    xn = _rms_norm(x, g_ref[...]).astype(BF16)
    for c in range(N_FF_TILES):
        gate = _dot(xn, wi_ref[:, c * FF_TILE:(c + 1) * FF_TILE])
        up = _dot(xn, wi_ref[:, D_FF + c * FF_TILE:D_FF + (c + 1) * FF_TILE])
        act = (gate * _sigmoid(gate) * up).astype(BF16)
        part = _dot(act, wo_ref[c * FF_TILE:(c + 1) * FF_TILE, :])
        if c == 0:
            acc_ref[...] = part
        else:
            acc_ref[...] += part
    o_ref[...] = x + 0.5 * acc_ref[...]


def _ffn_body(x_ref, g_ref, wi_ref, wo_ref, o_ref, acc_ref):
    _swiglu_half_step(x_ref[...], g_ref, wi_ref, wo_ref, o_ref, acc_ref)


def _mix_out_ffn_body(x_ref, yc_ref, yh_ref, yf_ref, wc_ref, wh_ref, wf_ref,
                      g_ref, wi_ref, wo_ref, o_ref, acc_ref):
    x = (x_ref[...] + _dot(yc_ref[...], wc_ref[...]) + _dot(yh_ref[...], wh_ref[...])
         + _dot(yf_ref[...], wf_ref[...]))
    _swiglu_half_step(x, g_ref, wi_ref, wo_ref, o_ref, acc_ref)


def _mix_out_ffn(x2d, yc, yh, yf, wc, wh, wf, gain, wi, wo):
    n = x2d.shape[0]
    tm = min(ROW_TILE, n)

    def row(w):
        return pl.BlockSpec((tm, w), lambda i: (i, 0))

    consts = [wc, wh, wf, gain, wi, wo]
    return pl.pallas_call(
        _mix_out_ffn_body,
        grid=(n // tm,),
        in_specs=[row(D_MODEL), row(CONV_CH), row(HGRN_W), row(FOX_W)]
        + [_const_spec(c.shape) for c in consts],
        out_specs=row(D_MODEL),
        out_shape=jax.ShapeDtypeStruct((n, D_MODEL), F32),
        scratch_shapes=[pltpu.VMEM((tm, D_MODEL), F32)],
        compiler_params=_params(("arbitrary",)),
        name="mix_out_ffn",
    )(x2d, yc, yh, yf, *consts)


def _ffn(x2d, gain, wi, wo):
    n = x2d.shape[0]
    tm = min(ROW_TILE, n)
    row = pl.BlockSpec((tm, D_MODEL), lambda i: (i, 0))
    return pl.pallas_call(
        _ffn_body,
        grid=(n // tm,),
        in_specs=[row, _const_spec(gain.shape), _const_spec(wi.shape), _const_spec(wo.shape)],
        out_specs=row,
        out_shape=jax.ShapeDtypeStruct((n, D_MODEL), F32),
        scratch_shapes=[pltpu.VMEM((tm, D_MODEL), F32)],
        compiler_params=_params(("arbitrary",)),
        name="ffn",
    )(x2d, gain, wi, wo)


def _mix_in_body(x_ref, g_ref, wc_ref, cw_ref, wh_ref, wq_ref, wk_ref, wvt_ref, wf_ref,
                 fb_ref, gq_ref, gk_ref, tri_ref, place_ref,
                 yc_ref, hh_ref, qa_ref, ka_ref, vat_ref, cend_ref, ubuf_ref, carry_ref, *, tm):
    @pl.when(pl.program_id(1) == 0)
    def _():
        ubuf_ref[0:8, :] = jnp.zeros((8, CONV_CH), F32)
        carry_ref[...] = jnp.zeros_like(carry_ref)

    xn = _rms_norm(x_ref[0], g_ref[...]).astype(BF16)

    lf = _log_sigmoid(_dot(xn, wf_ref[...]) + fb_ref[...])
    c = carry_ref[...] + _dot(tri_ref[...], jnp.concatenate(_split3(lf), axis=0))
    carry_ref[...] = c[tm - 1:tm, :]
    ends = [c[(n + 1) * KV_TILE - 1:(n + 1) * KV_TILE, :] for n in range(tm // KV_TILE)]
    cend_ref[0, 0] = jnp.concatenate(ends + [jnp.zeros((8 - len(ends), LANES), F32)], axis=0)
    placed = _dot(jnp.concatenate(_split3(c * LOG2E), axis=1), place_ref[...])

    lane = lax.broadcasted_iota(jnp.int32, (1, LANES), 1)
    low = lane < HEAD_DIM
    q_takes_c = jnp.logical_and(lane >= Q_C_LANE, lane < Q_C_LANE + N_PIECES)
    k_takes_c = jnp.logical_and(lane >= K_ONE_LANE + N_PIECES, lane < K_ONE_LANE + 2 * N_PIECES)
    one_q = k_takes_c.astype(F32)
    one_k = q_takes_c.astype(F32)

    def head_pair_norm(x2, gain2):
        sq = x2 * x2
        ss_lo = jnp.sum(jnp.where(low, sq, 0.0), axis=-1, keepdims=True)
        ss_hi = jnp.sum(jnp.where(low, 0.0, sq), axis=-1, keepdims=True)
        inv = lax.rsqrt(jnp.where(low, ss_lo, ss_hi) * (1.0 / HEAD_DIM) + EPS)
        return x2 * inv * gain2

    hq = _dot(xn, wq_ref[...])
    hk = _dot(xn, wk_ref[...])
    for pair in range(FOX_HEADS // 2):
        cols = slice(pair * LANES, (pair + 1) * LANES)
        qn = head_pair_norm(hq[:, cols], gq_ref[...])
        kn = head_pair_norm(hk[:, cols], gk_ref[...])
        for half in range(2):
            h = 2 * pair + half
            extra = placed[:, h * LANES:(h + 1) * LANES]
            q_h = qn if half == 0 else pltpu.roll(qn, HEAD_DIM, axis=1)
            k_h = kn if half == 0 else pltpu.roll(kn, HEAD_DIM, axis=1)
            qa_ref[0, h] = jnp.where(low, q_h, jnp.where(q_takes_c, extra, one_q)).astype(BF16)
            ka_ref[0, h] = jnp.where(low, k_h, jnp.where(k_takes_c, extra, one_k)).astype(BF16)

    vt = _dot_nt(wvt_ref[...], xn).astype(BF16)
    tail = (lax.broadcasted_iota(jnp.int32, (LANES - HEAD_DIM, KV_TILE), 0) == 0).astype(BF16)
    for h in range(FOX_HEADS):
        for n in range(tm // KV_TILE):
            vat_ref[0, h, n] = jnp.concatenate(
                [vt[h * HEAD_DIM:(h + 1) * HEAD_DIM, n * KV_TILE:(n + 1) * KV_TILE], tail], axis=0)

    hh_ref[0] = _dot(xn, wh_ref[...])

    hc = _dot(xn, wc_ref[...])
    u = hc[:, 2 * CONV_CH:3 * CONV_CH] * hc[:, 0:CONV_CH]
    ubuf_ref[8:8 + tm, :] = u
    u1 = ubuf_ref[7:7 + tm, :]
    u2 = ubuf_ref[6:6 + tm, :]
    cw = cw_ref[...]
    conv = cw[0:1, :] * u2 + cw[1:2, :] * u1 + cw[2:3, :] * u
    yc_ref[0] = (hc[:, CONV_CH:2 * CONV_CH] * conv).astype(BF16)
    ubuf_ref[0:8, :] = u[tm - 8:tm, :]


def _mix_in(x, p):
    b, l, _ = x.shape
    tm = min(ROW_TILE, l)
    nt = l // tm
    consts = [p["mix_norm"], p["wc"], p["conv_w"], p["wh"], p["wq"], p["wk"], p["wvt"], p["wf"],
              p["fbias"], p["gq"], p["gk"], p["tri"], p["place"]]
    out_shape = [
        jax.ShapeDtypeStruct((b, l, CONV_CH), BF16),
        jax.ShapeDtypeStruct((b, l, 4 * HGRN_W), F32),
        jax.ShapeDtypeStruct((b, FOX_HEADS, l, LANES), BF16),
        jax.ShapeDtypeStruct((b, FOX_HEADS, l, LANES), BF16),
        jax.ShapeDtypeStruct((b, FOX_HEADS, l // KV_TILE, LANES, KV_TILE), BF16),
        jax.ShapeDtypeStruct((b, nt, 8, LANES), F32),
    ]
    out_specs = [
        pl.BlockSpec((1, tm, CONV_CH), lambda bi, i: (bi, i, 0)),
        pl.BlockSpec((1, tm, 4 * HGRN_W), lambda bi, i: (bi, i, 0)),
        pl.BlockSpec((1, FOX_HEADS, tm, LANES), lambda bi, i: (bi, 0, i, 0)),
        pl.BlockSpec((1, FOX_HEADS, tm, LANES), lambda bi, i: (bi, 0, i, 0)),
        pl.BlockSpec((1, FOX_HEADS, tm // KV_TILE, LANES, KV_TILE), lambda bi, i: (bi, 0, i, 0, 0)),
        pl.BlockSpec((1, 1, 8, LANES), lambda bi, i: (bi, i, 0, 0)),
    ]
    return pl.pallas_call(
        functools.partial(_mix_in_body, tm=tm),
        grid=(b, nt),
        in_specs=[pl.BlockSpec((1, tm, D_MODEL), lambda bi, i: (bi, i, 0))]
        + [_const_spec(c.shape) for c in consts],
        out_specs=out_specs,
        out_shape=out_shape,
        scratch_shapes=[pltpu.VMEM((tm + 8, CONV_CH), F32), pltpu.VMEM((1, LANES), F32)],
        compiler_params=_params(("arbitrary", "arbitrary")),
        name="mix_in",
    )(x, *consts)


def _hgrn_body(q_ref, z_ref, v_ref, g_ref, lbl_ref, gain_ref, cum_ref, rep_ref, bd_ref, bmask_ref,
               o_ref, st_ref, vv_ref, *, layer, rows):
    @pl.when(pl.program_id(1) == 0)
    def _():
        st_ref[...] = jnp.zeros_like(st_ref)

    lbl = lbl_ref[...]
    e = jnp.exp(lbl - jnp.max(lbl, axis=0, keepdims=True))
    soft = e / jnp.sum(e, axis=0, keepdims=True)
    lb = jnp.clip(jnp.sum(soft[0:layer + 1, :], axis=0, keepdims=True) - soft[0:1, :], 0.0, 1.0)

    bd = bd_ref[...]
    n_sub = CHUNK // SUB
    sub_t = lax.broadcasted_iota(jnp.int32, (CHUNK, HGRN_W), 0) // SUB
    sub_s = lax.broadcasted_iota(jnp.int32, (CHUNK, HGRN_W), 1) % CHUNK // SUB

    def within_chunk(ch, slot):
        rs = pl.ds(pl.multiple_of(ch * CHUNK, CHUNK), CHUNK)
        q = q_ref[0, rs, :]
        z = z_ref[0, rs, :]
        v = v_ref[0, rs, :]
        lf = _log_sigmoid(z) + jnp.log(1.0 + lb * jnp.exp(-z))
        k = (1.0 - lb) * _sigmoid(-z)
        pieces = jnp.concatenate(_split3(lf), axis=0)
        sums = _dot(cum_ref[0:3 * CHUNK, :], pieces)
        vv_ref[slot] = v
        v16 = v.astype(BF16)
        k16 = k.astype(BF16)
        yield

        bc = sums[0:CHUNK]
        bs = sums[CHUNK:2 * CHUNK]
        be = sums[2 * CHUNK:3 * CHUNK]

        q_t = q * jnp.exp(bc - bs)
        k_t = (k * jnp.exp(be - bc)).astype(BF16)
        lags = [q_t]
        for lag in range(2, n_sub):
            be_shift = jnp.concatenate([jnp.zeros((lag * SUB, HGRN_W), F32),
                                        be[0:CHUNK - lag * SUB]], axis=0)
            lags.append(q_t * jnp.exp(jnp.minimum(bs - be_shift, 0.0)))
        q_lags = jnp.concatenate(lags, axis=0).astype(BF16)
        sc = _dot_nt(q_lags, jnp.concatenate([k_t] * HGRN_HEADS, axis=0) * bd)

        half_rows = CHUNK * SUB // 2
        pair = jnp.concatenate(
            [_dot(cum_ref[3 * CHUNK + n * half_rows:3 * CHUNK + (n + 1) * half_rows, :], pieces)
             for n in range(2)], axis=0)
        k_rep = jnp.concatenate(
            [_dot(rep_ref[n * half_rows:(n + 1) * half_rows, :], k16) for n in range(2)], axis=0)
        yield

        a_lag = jnp.zeros((CHUNK, HGRN_W), F32)
        for n, lag in enumerate(range(1, n_sub)):
            a_lag = jnp.where(sub_t - sub_s == lag, sc[n * CHUNK:(n + 1) * CHUNK], a_lag)
        o_lag = _dot(a_lag.astype(BF16), jnp.concatenate([v16] * HGRN_HEADS, axis=0) * bd)
        a = []
        for i in range(n_sub):
            rows_i = slice(i * SUB * SUB, (i + 1) * SUB * SUB)
            q_rep = jnp.concatenate([q[i * SUB:(i + 1) * SUB]] * SUB, axis=0)
            a.append((q_rep * jnp.exp(pair[rows_i]) * k_rep[rows_i]).astype(BF16))
        r = jnp.concatenate(
            [_dot(jnp.concatenate(a[n * n_sub // 2:(n + 1) * n_sub // 2], axis=0), bd)
             for n in range(2)], axis=0)
        b_last = bc[CHUNK - 1:CHUNK, :]
        q_hat = (q * jnp.exp(bc)).astype(BF16)
        k_hat = (k * jnp.exp(b_last - bc)).astype(BF16)
        yield

        o_sub = []
        for i in range(n_sub):
            acc = None
            for sl in range(SUB):
                s = i * SUB + sl
                term = r[s * SUB:(s + 1) * SUB] * vv_ref[slot, s:s + 1, :]
                acc = term if acc is None else acc + term
            o_sub.append(acc)
        o = o_lag + jnp.concatenate(o_sub, axis=0)
        return rs, o, q_hat, k_hat, v16, jnp.exp(b_last)

    def finish(rs, o, q_hat, k_hat, v16, decay_last):
        st = st_ref[...]
        o = o + _dot_nt(q_hat, st.astype(BF16))
        st_ref[...] = st * decay_last + bmask_ref[...] * _dot_tn(v16, k_hat)

        oo = o * o
        hi = oo.astype(BF16)
        lo = (oo - hi.astype(F32)).astype(BF16)
        ms = (_dot(hi, bd) + _dot(lo, bd)) * (1.0 / HEAD_DIM)
        gate = g_ref[0, rs, :]
        y = o * lax.rsqrt(ms + EPS) * gain_ref[...] * (gate * _sigmoid(gate))
        o_ref[0, rs, :] = y.astype(BF16)

    def chunk_group(cc, carry):
        gens = [within_chunk(CHUNKS_PER_TRIP * cc + slot, slot) for slot in range(CHUNKS_PER_TRIP)]
        parts = [None] * CHUNKS_PER_TRIP
        while any(part is None for part in parts):
            for slot, gen in enumerate(gens):
                if parts[slot] is None:
                    try:
                        next(gen)
                    except StopIteration as done:
                        parts[slot] = done.value
        for part in parts:
            finish(*part)
        return carry

    lax.fori_loop(0, rows // (CHUNKS_PER_TRIP * CHUNK), chunk_group, 0)


def _hgrn(hh, p, layer):
    b, l, _ = hh.shape
    rows = min(HGRN_ROWS, l)
    consts = [p["lb_logits"], p["hgrn_gain"], p["cum64"], p["rep64"], p["bd"], p["bmask"]]

    def section(k):
        return pl.BlockSpec((1, rows, HGRN_W), lambda bi, i: (bi, i, k))

    return pl.pallas_call(
        functools.partial(_hgrn_body, layer=layer, rows=rows),
        grid=(b, l // rows),
        in_specs=[section(0), section(1), section(2), section(3)]
        + [_const_spec(c.shape) for c in consts],
        out_specs=pl.BlockSpec((1, rows, HGRN_W), lambda bi, i: (bi, i, 0)),
        out_shape=jax.ShapeDtypeStruct((b, l, HGRN_W), BF16),
        scratch_shapes=[pltpu.VMEM((HGRN_W, HGRN_W), F32), pltpu.VMEM((CHUNKS_PER_TRIP, CHUNK, HGRN_W), F32)],
        compiler_params=_params(("arbitrary", "arbitrary")),
        name="hgrn",
    )(hh, hh, hh, hh, *consts)


def _fox_body(qkb_ref, cpre_ref, qa_ref, ka_ref, vat_ref, o_ref, m_ref, acc_ref, s_ref,
              *, tq, tk, n_pre):
    bi, hp, qi = pl.program_id(0), pl.program_id(1), pl.program_id(2)
    per_q = tq // tk
    row = lax.broadcasted_iota(jnp.int32, (tk, tq), 0)
    col = lax.broadcasted_iota(jnp.int32, (tk, tq), 1)

    def c_before(hh, n):
        return cpre_ref[(bi * FOX_HEADS + 2 * hp + hh) * n_pre + n]

    def scores(hh, j):
        return _dot_nt(ka_ref[0, hh, pl.ds(pl.multiple_of(j * tk, tk), tk), :], qa_ref[0, hh])

    def update(hh, j, s, mask):
        if mask is not None:
            s = jnp.where(mask, s, MASK_VALUE)
        m_old = m_ref[hh]
        m_new = jnp.maximum(m_old, jnp.max(s, axis=0, keepdims=True))
        p = jnp.exp2(s - m_new).astype(BF16)
        acc_ref[hh] = acc_ref[hh] * jnp.exp2(m_old - m_new) + _dot(vat_ref[0, hh, j], p)
        m_ref[hh] = m_new

    for hh in range(2):
        m_ref[hh] = jnp.full((1, tq), MASK_VALUE, F32)
        acc_ref[hh] = jnp.zeros((LANES, tq), F32)

    diag = [(hh, dj) for dj in range(per_q) for hh in range(2)]
    s_diag = [scores(hh, qi * per_q + dj) for hh, dj in diag]
    j_first = jnp.maximum(qi * per_q - 1, 0)
    s_first = [scores(hh, j_first) for hh in range(2)]
    for (hh, dj), s in zip(diag, s_diag):
        update(hh, qi * per_q + dj, s, row + dj * tk <= col)

    def needed(j):
        out = None
        for hh in range(2):
            gap = (qkb_ref[0] + c_before(hh, qi * per_q) - c_before(hh, j + 1)
                   - jnp.min(m_ref[hh]))
            out = gap >= SKIP_BELOW if out is None else jnp.logical_or(out, gap >= SKIP_BELOW)
        return out.astype(jnp.int32)

    def cond(carry):
        j, go = carry
        return jnp.logical_and(j >= 0, go > 0)

    def stage(src, dst, j):
        j_next = jnp.maximum(j - 1, 0)
        for hh in range(2):
            s_ref[dst, hh] = scores(hh, j_next)
        for hh in range(2):
            update(hh, j, s_ref[src, hh], None)
        return jnp.where(j >= 1, needed(j_next), 0)

    def body(carry):
        j, _ = carry
        go = stage(0, 1, j)
        go = lax.cond(go > 0, lambda: stage(1, 0, j - 1), lambda: jnp.int32(0))
        return j - 2, go

    for hh in range(2):
        s_ref[0, hh] = s_first[hh]
    lax.while_loop(cond, body, (qi * per_q - 1, needed(j_first)))

    outs = []
    for hh in range(2):
        acc = acc_ref[hh]
        outs.append(acc[0:HEAD_DIM, :] / acc[HEAD_DIM:HEAD_DIM + 1, :])
    o_ref[0] = jnp.concatenate(outs, axis=0).T.astype(BF16)


def _fox(qkb, cpre, qa, ka, vat):
    b, _, l, _ = qa.shape
    tq, tk = min(Q_TILE, l), vat.shape[-1]
    smem = pl.BlockSpec(memory_space=pltpu.SMEM)
    return pl.pallas_call(
        functools.partial(_fox_body, tq=tq, tk=tk, n_pre=l // tk + 1),
        grid=(b, FOX_HEADS // 2, l // tq),
        in_specs=[
            smem, smem,
            pl.BlockSpec((1, 2, tq, LANES), lambda bi, hp, i: (bi, hp, i, 0)),
            pl.BlockSpec((1, 2, l, LANES), lambda bi, hp, i: (bi, hp, 0, 0)),
            pl.BlockSpec((1, 2, l // tk, LANES, tk), lambda bi, hp, i: (bi, hp, 0, 0, 0)),
        ],
        out_specs=pl.BlockSpec((1, tq, LANES), lambda bi, hp, i: (bi, i, hp)),
        out_shape=jax.ShapeDtypeStruct((b, l, FOX_W), BF16),
        scratch_shapes=[pltpu.VMEM((2, 1, tq), F32), pltpu.VMEM((2, LANES, tq), F32),
                        pltpu.VMEM((2, 2, tk, tq), F32)],
        compiler_params=_params(("arbitrary", "arbitrary", "arbitrary")),
        name="fox",
    )(qkb, cpre, qa, ka, vat)


def _pad_lanes(v, width=LANES):
    return jnp.pad(v, (0, width - v.shape[0])).reshape(1, width)


def _placement():
    m = np.zeros((N_PIECES * LANES, FOX_HEADS * LANES), np.float32)
    for p in range(N_PIECES):
        for h in range(FOX_HEADS):
            m[p * LANES + h, h * LANES + Q_C_LANE + p] = 1.0
            m[p * LANES + h, h * LANES + K_ONE_LANE + N_PIECES + p] = -1.0
    return jnp.asarray(m, BF16)


def _tri3(n):
    return jnp.asarray(np.tile(np.tril(np.ones((n, n), np.float32)), (1, N_PIECES)), BF16)


def _chunk_sums():
    t = np.arange(CHUNK)[:, None]
    r = np.arange(CHUNK)[None, :]
    first = (t // SUB) * SUB
    s_pair, t_pair = _sub_chunk_pairs()
    rows = [r <= t, r < first, r < first + SUB,
            (r > s_pair[:, None]) & (r <= t_pair[:, None])]
    m = np.concatenate(rows, axis=0).astype(np.float32)
    return jnp.asarray(np.tile(m, (1, N_PIECES)), BF16)


def _sub_chunk_pairs():
    s = np.repeat(np.arange(CHUNK), SUB)
    t = (s // SUB) * SUB + np.tile(np.arange(SUB), CHUNK)
    return s, t


def _causal_replicate():
    s, t = _sub_chunk_pairs()
    m = (np.arange(CHUNK)[None, :] == s[:, None]) & (t >= s)[:, None]
    return jnp.asarray(m.astype(np.float32), BF16)


def _ffn_weights(w_in, w_out):
    return w_in.astype(BF16), w_out.astype(BF16)


def _layer_operands(l, tm, mix_norm, w_mix_in, conv_w, hgrn_lb_logits, hgrn_out_gain,
                    fox_q_gain, fox_k_gain, fox_f_bias, w_mix_out):
    w = w_mix_in[l]
    o_h = 3 * CONV_CH
    o_q = o_h + 4 * HGRN_W
    o_k, o_v, o_f = o_q + FOX_W, o_q + 2 * FOX_W, o_q + 3 * FOX_W
    head_blocks = np.kron(np.eye(HGRN_HEADS, dtype=np.float32),
                          np.ones((HEAD_DIM, HEAD_DIM), np.float32))
    wo = w_mix_out[l].astype(BF16)
    q_scale = fox_q_gain[l] * (HEAD_DIM ** -0.5 * LOG2E)
    return {
        "mix_norm": mix_norm[l].reshape(1, D_MODEL),
        "wc": w[:, :o_h].astype(BF16),
        "conv_w": jnp.pad(conv_w[l], ((0, 8 - CONV_WIDTH), (0, 0))),
        "wh": w[:, o_h:o_q].astype(BF16),
        "wq": w[:, o_q:o_k].astype(BF16),
        "wk": w[:, o_k:o_v].astype(BF16),
        "wvt": w[:, o_v:o_f].T.astype(BF16),
        "wf": jnp.pad(w[:, o_f:], ((0, 0), (0, LANES - FOX_HEADS))).astype(BF16),
        "fbias": _pad_lanes(fox_f_bias[l]),
        "gq": jnp.tile(q_scale, LANES // HEAD_DIM).reshape(1, LANES),
        "gk": jnp.tile(fox_k_gain[l], LANES // HEAD_DIM).reshape(1, LANES),
        "qkb": (1.01 * HEAD_DIM * jnp.max(jnp.abs(q_scale))
                * jnp.max(jnp.abs(fox_k_gain[l]))).reshape(1),
        "tri": _tri3(tm),
        "place": _placement(),
        "lb_logits": hgrn_lb_logits,
        "hgrn_gain": jnp.tile(hgrn_out_gain[l], HGRN_HEADS).reshape(1, HGRN_W),
        "cum64": _chunk_sums(),
        "rep64": _causal_replicate(),
        "bd": jnp.asarray(head_blocks, BF16),
        "bmask": jnp.asarray(head_blocks),
        "wo_c": wo[:CONV_CH],
        "wo_h": wo[CONV_CH:CONV_CH + HGRN_W],
        "wo_f": wo[CONV_CH + HGRN_W:],
    }


def kernel(x, ffn1_norm, ffn1_w_in, ffn1_w_out, mix_norm, w_mix_in, conv_w, hgrn_lb_logits,
           hgrn_out_gain, fox_q_gain, fox_k_gain, fox_f_bias, w_mix_out, ffn2_norm, ffn2_w_in,
           ffn2_w_out):
    b, l, d = x.shape
    assert d == D_MODEL and l % ROW_TILE == 0 and ROW_TILE == Q_TILE and Q_TILE % KV_TILE == 0
    depth = ffn1_norm.shape[0]
    tm = min(ROW_TILE, l)
    n = b * l
    for layer in range(depth):
        p = _layer_operands(layer, tm, mix_norm, w_mix_in, conv_w, hgrn_lb_logits, hgrn_out_gain,
                            fox_q_gain, fox_k_gain, fox_f_bias, w_mix_out)
        x2 = _ffn(x.reshape(n, d), ffn1_norm[layer].reshape(1, d),
                  *_ffn_weights(ffn1_w_in[layer], ffn1_w_out[layer]))
        yc, hh, qa, ka, vat, cend = _mix_in(x2.reshape(b, l, d), p)
        yh = _hgrn(hh, p, layer)
        cend = cend[:, :, :tm // KV_TILE, :FOX_HEADS].reshape(b, l // KV_TILE, FOX_HEADS)
        cpre = jnp.pad(cend.transpose(0, 2, 1), ((0, 0), (0, 0), (1, 0))) * LOG2E
        yf = _fox(p["qkb"], cpre.reshape(-1), qa, ka, vat)
        x2 = _mix_out_ffn(x2, yc.reshape(n, CONV_CH), yh.reshape(n, HGRN_W), yf.reshape(n, FOX_W),
                          p["wo_c"], p["wo_h"], p["wo_f"], ffn2_norm[layer].reshape(1, d),
                          *_ffn_weights(ffn2_w_in[layer], ffn2_w_out[layer]))
        x = x2.reshape(b, l, d)
    return x
```

```python
import functools

import jax
import jax.numpy as jnp
import numpy as np
from jax import lax
from jax.experimental import pallas as pl
from jax.experimental.pallas import tpu as pltpu

F32 = jnp.float32
BF16 = jnp.bfloat16

D_MODEL = 1024
D_FF = 2816
HEAD_DIM = 64
CONV_CH = 256
CONV_WIDTH = 3
HGRN_HEADS = 4
HGRN_W = 256
FOX_HEADS = 8
FOX_W = 512
CHUNK = 64
EPS = 1e-6
MASK_VALUE = -1e30

LANES = 128
FF_TILE = 256
N_FF_TILES = D_FF // FF_TILE
ROW_TILE = 512
Q_TILE = 512
KV_TILE = 256
HGRN_ROWS = 512
SUB = 16
CHUNKS_PER_TRIP = 4
VMEM_LIMIT = 56 * 1024 * 1024

Q_C_LANE = HEAD_DIM
K_ONE_LANE = HEAD_DIM
N_PIECES = 3
LOG2E = 1.4426950408889634
SKIP_BELOW = -138.0


def _dot(a, b):
    return jnp.dot(a, b, preferred_element_type=F32)


def _dot_nt(a, b):
    return lax.dot_general(a, b, (((1,), (1,)), ((), ())), preferred_element_type=F32)


def _dot_tn(a, b):
    return lax.dot_general(a, b, (((0,), (0,)), ((), ())), preferred_element_type=F32)


def _rms_norm(x, gain):
    inv = lax.rsqrt(jnp.mean(x * x, axis=-1, keepdims=True) + EPS)
    return x * inv * gain


def _sigmoid(x):
    return 1.0 / (1.0 + jnp.exp(-x))


def _log_sigmoid(x):
    return jnp.minimum(x, 0.0) - jnp.log(1.0 + jnp.exp(-jnp.abs(x)))


def _split3(x):
    hi = x.astype(BF16)
    r = x - hi.astype(F32)
    mid = r.astype(BF16)
    lo = (r - mid.astype(F32)).astype(BF16)
    return hi, mid, lo


def _const_spec(shape):
    nd = len(shape)
    return pl.BlockSpec(shape, lambda *_: (0,) * nd, pipeline_mode=pl.Buffered(1))


def _layer_spec(stacked_shape, layer):
    nd = len(stacked_shape)
    return pl.BlockSpec((None,) + tuple(stacked_shape[1:]), lambda *_: (layer,) + (0,) * (nd - 1),
                        pipeline_mode=pl.Buffered(1))


def _params(sem):
    return pltpu.CompilerParams(dimension_semantics=sem, vmem_limit_bytes=VMEM_LIMIT)


def _swiglu_half_step(x, g_ref, wi_ref, wo_ref, o_ref, acc_ref):
    xn = _rms_norm(x, g_ref[...]).astype(BF16)
    for c in range(N_FF_TILES):
        gate = _dot(xn, wi_ref[:, c * FF_TILE:(c + 1) * FF_TILE])
        up = _dot(xn, wi_ref[:, D_FF + c * FF_TILE:D_FF + (c + 1) * FF_TILE])
        act = (gate * _sigmoid(gate) * up).astype(BF16)
        part = _dot(act, wo_ref[c * FF_TILE:(c + 1) * FF_TILE, :])
        if c == 0:
            acc_ref[...] = part
        else:
            acc_ref[...] += part
    o_ref[...] = x + 0.5 * acc_ref[...]


def _ffn_body(x_ref, g_ref, wi_ref, wo_ref, o_ref, acc_ref):
    _swiglu_half_step(x_ref[...], g_ref, wi_ref, wo_ref, o_ref, acc_ref)


def _mix_out_ffn_body(x_ref, yc_ref, yh_ref, yf_ref, wc_ref, wh_ref, wf_ref,
                      g_ref, wi_ref, wo_ref, o_ref, acc_ref):
    x = (x_ref[...] + _dot(yc_ref[...], wc_ref[...]) + _dot(yh_ref[...], wh_ref[...])
         + _dot(yf_ref[...], wf_ref[...]))
    _swiglu_half_step(x, g_ref, wi_ref, wo_ref, o_ref, acc_ref)


def _mix_out_ffn(x2d, yc, yh, yf, wc, wh, wf, gain, wi, wo, layer):
    n = x2d.shape[0]
    tm = min(ROW_TILE, n)

    def row(w):
        return pl.BlockSpec((tm, w), lambda i: (i, 0))

    consts = [wc, wh, wf, gain]
    return pl.pallas_call(
        _mix_out_ffn_body,
        grid=(n // tm,),
        in_specs=[row(D_MODEL), row(CONV_CH), row(HGRN_W), row(FOX_W)]
        + [_const_spec(c.shape) for c in consts]
        + [_layer_spec(wi.shape, layer), _layer_spec(wo.shape, layer)],
        out_specs=row(D_MODEL),
        out_shape=jax.ShapeDtypeStruct((n, D_MODEL), F32),
        scratch_shapes=[pltpu.VMEM((tm, D_MODEL), F32)],
        compiler_params=_params(("arbitrary",)),
        name="mix_out_ffn",
    )(x2d, yc, yh, yf, *consts, wi, wo)


def _ffn(x2d, gain, wi, wo, layer):
    n = x2d.shape[0]
    tm = min(ROW_TILE, n)
    row = pl.BlockSpec((tm, D_MODEL), lambda i: (i, 0))
    return pl.pallas_call(
        _ffn_body,
        grid=(n // tm,),
        in_specs=[row, _const_spec(gain.shape), _layer_spec(wi.shape, layer),
                  _layer_spec(wo.shape, layer)],
        out_specs=row,
        out_shape=jax.ShapeDtypeStruct((n, D_MODEL), F32),
        scratch_shapes=[pltpu.VMEM((tm, D_MODEL), F32)],
        compiler_params=_params(("arbitrary",)),
        name="ffn",
    )(x2d, gain, wi, wo)


def _mix_in_body(x_ref, g_ref, wc_ref, cw_ref, wh_ref, wq_ref, wk_ref, wvt_ref, wf_ref,
                 fb_ref, gq_ref, gk_ref, tri_ref, place_ref,
                 yc_ref, hh_ref, qa_ref, ka_ref, vat_ref, cend_ref, ubuf_ref, carry_ref, *, tm):
    @pl.when(pl.program_id(1) == 0)
    def _():
        ubuf_ref[0:8, :] = jnp.zeros((8, CONV_CH), F32)
        carry_ref[...] = jnp.zeros_like(carry_ref)

    xn = _rms_norm(x_ref[0], g_ref[...]).astype(BF16)

    lf = _log_sigmoid(_dot(xn, wf_ref[...]) + fb_ref[...])
    c = carry_ref[...] + _dot(tri_ref[...], jnp.concatenate(_split3(lf), axis=0))
    carry_ref[...] = c[tm - 1:tm, :]
    ends = [c[(n + 1) * KV_TILE - 1:(n + 1) * KV_TILE, :] for n in range(tm // KV_TILE)]
    cend_ref[0, 0] = jnp.concatenate(ends + [jnp.zeros((8 - len(ends), LANES), F32)], axis=0)
    placed = _dot(jnp.concatenate(_split3(c * LOG2E), axis=1), place_ref[...])

    lane = lax.broadcasted_iota(jnp.int32, (1, LANES), 1)
    low = lane < HEAD_DIM
    q_takes_c = jnp.logical_and(lane >= Q_C_LANE, lane < Q_C_LANE + N_PIECES)
    k_takes_c = jnp.logical_and(lane >= K_ONE_LANE + N_PIECES, lane < K_ONE_LANE + 2 * N_PIECES)
    one_q = k_takes_c.astype(F32)
    one_k = q_takes_c.astype(F32)

    def head_pair_norm(x2, gain2):
        sq = x2 * x2
        ss_lo = jnp.sum(jnp.where(low, sq, 0.0), axis=-1, keepdims=True)
        ss_hi = jnp.sum(jnp.where(low, 0.0, sq), axis=-1, keepdims=True)
        inv = lax.rsqrt(jnp.where(low, ss_lo, ss_hi) * (1.0 / HEAD_DIM) + EPS)
        return x2 * inv * gain2

    hq = _dot(xn, wq_ref[...])
    hk = _dot(xn, wk_ref[...])
    for pair in range(FOX_HEADS // 2):
        cols = slice(pair * LANES, (pair + 1) * LANES)
        qn = head_pair_norm(hq[:, cols], gq_ref[...])
        kn = head_pair_norm(hk[:, cols], gk_ref[...])
        for half in range(2):
            h = 2 * pair + half
            extra = placed[:, h * LANES:(h + 1) * LANES]
            q_h = qn if half == 0 else pltpu.roll(qn, HEAD_DIM, axis=1)
            k_h = kn if half == 0 else pltpu.roll(kn, HEAD_DIM, axis=1)
            qa_ref[0, h] = jnp.where(low, q_h, jnp.where(q_takes_c, extra, one_q)).astype(BF16)
            ka_ref[0, h] = jnp.where(low, k_h, jnp.where(k_takes_c, extra, one_k)).astype(BF16)

    vt = _dot_nt(wvt_ref[...], xn).astype(BF16)
    tail = (lax.broadcasted_iota(jnp.int32, (LANES - HEAD_DIM, KV_TILE), 0) == 0).astype(BF16)
    for h in range(FOX_HEADS):
        for n in range(tm // KV_TILE):
            vat_ref[0, h, n] = jnp.concatenate(
                [vt[h * HEAD_DIM:(h + 1) * HEAD_DIM, n * KV_TILE:(n + 1) * KV_TILE], tail], axis=0)

    hh_ref[0] = _dot(xn, wh_ref[...])

    hc = _dot(xn, wc_ref[...])
    u = hc[:, 2 * CONV_CH:3 * CONV_CH] * hc[:, 0:CONV_CH]
    ubuf_ref[8:8 + tm, :] = u
    u1 = ubuf_ref[7:7 + tm, :]
    u2 = ubuf_ref[6:6 + tm, :]
    cw = cw_ref[...]
    conv = cw[0:1, :] * u2 + cw[1:2, :] * u1 + cw[2:3, :] * u
    yc_ref[0] = (hc[:, CONV_CH:2 * CONV_CH] * conv).astype(BF16)
    ubuf_ref[0:8, :] = u[tm - 8:tm, :]


def _mix_in(x, p):
    b, l, _ = x.shape
    tm = min(ROW_TILE, l)
    nt = l // tm
    consts = [p["mix_norm"], p["wc"], p["conv_w"], p["wh"], p["wq"], p["wk"], p["wvt"], p["wf"],
              p["fbias"], p["gq"], p["gk"], p["tri"], p["place"]]
    out_shape = [
        jax.ShapeDtypeStruct((b, l, CONV_CH), BF16),
        jax.ShapeDtypeStruct((b, l, 4 * HGRN_W), F32),
        jax.ShapeDtypeStruct((b, FOX_HEADS, l, LANES), BF16),
        jax.ShapeDtypeStruct((b, FOX_HEADS, l, LANES), BF16),
        jax.ShapeDtypeStruct((b, FOX_HEADS, l // KV_TILE, LANES, KV_TILE), BF16),
        jax.ShapeDtypeStruct((b, nt, 8, LANES), F32),
    ]
    out_specs = [
        pl.BlockSpec((1, tm, CONV_CH), lambda bi, i: (bi, i, 0)),
        pl.BlockSpec((1, tm, 4 * HGRN_W), lambda bi, i: (bi, i, 0)),
        pl.BlockSpec((1, FOX_HEADS, tm, LANES), lambda bi, i: (bi, 0, i, 0)),
        pl.BlockSpec((1, FOX_HEADS, tm, LANES), lambda bi, i: (bi, 0, i, 0)),
        pl.BlockSpec((1, FOX_HEADS, tm // KV_TILE, LANES, KV_TILE), lambda bi, i: (bi, 0, i, 0, 0)),
        pl.BlockSpec((1, 1, 8, LANES), lambda bi, i: (bi, i, 0, 0)),
    ]
    return pl.pallas_call(
        functools.partial(_mix_in_body, tm=tm),
        grid=(b, nt),
        in_specs=[pl.BlockSpec((1, tm, D_MODEL), lambda bi, i: (bi, i, 0))]
        + [_const_spec(c.shape) for c in consts],
        out_specs=out_specs,
        out_shape=out_shape,
        scratch_shapes=[pltpu.VMEM((tm + 8, CONV_CH), F32), pltpu.VMEM((1, LANES), F32)],
        compiler_params=_params(("arbitrary", "arbitrary")),
        name="mix_in",
    )(x, *consts)


def _hgrn_body(q_ref, z_ref, v_ref, g_ref, lbl_ref, gain_ref, cum_ref, rep_ref, bd_ref, bmask_ref,
               o_ref, st_ref, vv_ref, *, layer, rows):
    @pl.when(pl.program_id(1) == 0)
    def _():
        st_ref[...] = jnp.zeros_like(st_ref)

    lbl = lbl_ref[...]
    e = jnp.exp(lbl - jnp.max(lbl, axis=0, keepdims=True))
    soft = e / jnp.sum(e, axis=0, keepdims=True)
    lb = jnp.clip(jnp.sum(soft[0:layer + 1, :], axis=0, keepdims=True) - soft[0:1, :], 0.0, 1.0)

    bd = bd_ref[...]
    n_sub = CHUNK // SUB
    sub_t = lax.broadcasted_iota(jnp.int32, (CHUNK, HGRN_W), 0) // SUB
    sub_s = lax.broadcasted_iota(jnp.int32, (CHUNK, HGRN_W), 1) % CHUNK // SUB

    def within_chunk(ch, slot):
        rs = pl.ds(pl.multiple_of(ch * CHUNK, CHUNK), CHUNK)
        q = q_ref[0, rs, :]
        z = z_ref[0, rs, :]
        v = v_ref[0, rs, :]
        lf = _log_sigmoid(z) + jnp.log(1.0 + lb * jnp.exp(-z))
        k = (1.0 - lb) * _sigmoid(-z)
        pieces = jnp.concatenate(_split3(lf), axis=0)
        sums = _dot(cum_ref[0:3 * CHUNK, :], pieces)
        vv_ref[slot] = v
        v16 = v.astype(BF16)
        k16 = k.astype(BF16)
        yield

        bc = sums[0:CHUNK]
        bs = sums[CHUNK:2 * CHUNK]
        be = sums[2 * CHUNK:3 * CHUNK]

        q_t = q * jnp.exp(bc - bs)
        k_t = (k * jnp.exp(be - bc)).astype(BF16)
        lags = [q_t]
        for lag in range(2, n_sub):
            be_shift = jnp.concatenate([jnp.zeros((lag * SUB, HGRN_W), F32),
                                        be[0:CHUNK - lag * SUB]], axis=0)
            lags.append(q_t * jnp.exp(jnp.minimum(bs - be_shift, 0.0)))
        q_lags = jnp.concatenate(lags, axis=0).astype(BF16)
        sc = _dot_nt(q_lags, jnp.concatenate([k_t] * HGRN_HEADS, axis=0) * bd)

        half_rows = CHUNK * SUB // 2
        pair = jnp.concatenate(
            [_dot(cum_ref[3 * CHUNK + n * half_rows:3 * CHUNK + (n + 1) * half_rows, :], pieces)
             for n in range(2)], axis=0)
        k_rep = jnp.concatenate(
            [_dot(rep_ref[n * half_rows:(n + 1) * half_rows, :], k16) for n in range(2)], axis=0)
        yield

        a_lag = jnp.zeros((CHUNK, HGRN_W), F32)
        for n, lag in enumerate(range(1, n_sub)):
            a_lag = jnp.where(sub_t - sub_s == lag, sc[n * CHUNK:(n + 1) * CHUNK], a_lag)
        o_lag = _dot(a_lag.astype(BF16), jnp.concatenate([v16] * HGRN_HEADS, axis=0) * bd)
        a = []
        for i in range(n_sub):
            rows_i = slice(i * SUB * SUB, (i + 1) * SUB * SUB)
            q_rep = jnp.concatenate([q[i * SUB:(i + 1) * SUB]] * SUB, axis=0)
            a.append((q_rep * jnp.exp(pair[rows_i]) * k_rep[rows_i]).astype(BF16))
        r = jnp.concatenate(
            [_dot(jnp.concatenate(a[n * n_sub // 2:(n + 1) * n_sub // 2], axis=0), bd)
             for n in range(2)], axis=0)
        b_last = bc[CHUNK - 1:CHUNK, :]
        q_hat = (q * jnp.exp(bc)).astype(BF16)
        k_hat = (k * jnp.exp(b_last - bc)).astype(BF16)
        yield

        o_sub = []
        for i in range(n_sub):
            acc = None
            for sl in range(SUB):
                s = i * SUB + sl
                term = r[s * SUB:(s + 1) * SUB] * vv_ref[slot, s:s + 1, :]
                acc = term if acc is None else acc + term
            o_sub.append(acc)
        o = o_lag + jnp.concatenate(o_sub, axis=0)
        return rs, o, q_hat, k_hat, v16, jnp.exp(b_last)

    def finish(rs, o, q_hat, k_hat, v16, decay_last):
        st = st_ref[...]
        o = o + _dot_nt(q_hat, st.astype(BF16))
        st_ref[...] = st * decay_last + bmask_ref[...] * _dot_tn(v16, k_hat)

        oo = o * o
        hi = oo.astype(BF16)
        lo = (oo - hi.astype(F32)).astype(BF16)
        ms = (_dot(hi, bd) + _dot(lo, bd)) * (1.0 / HEAD_DIM)
        gate = g_ref[0, rs, :]
        y = o * lax.rsqrt(ms + EPS) * gain_ref[...] * (gate * _sigmoid(gate))
        o_ref[0, rs, :] = y.astype(BF16)

    def chunk_group(cc, carry):
        gens = [within_chunk(CHUNKS_PER_TRIP * cc + slot, slot) for slot in range(CHUNKS_PER_TRIP)]
        parts = [None] * CHUNKS_PER_TRIP
        while any(part is None for part in parts):
            for slot, gen in enumerate(gens):
                if parts[slot] is None:
                    try:
                        next(gen)
                    except StopIteration as done:
                        parts[slot] = done.value
        for part in parts:
            finish(*part)
        return carry

    lax.fori_loop(0, rows // (CHUNKS_PER_TRIP * CHUNK), chunk_group, 0)


def _hgrn(hh, p, layer):
    b, l, _ = hh.shape
    rows = min(HGRN_ROWS, l)
    consts = [p["lb_logits"], p["hgrn_gain"], p["cum64"], p["rep64"], p["bd"], p["bmask"]]

    def section(k):
        return pl.BlockSpec((1, rows, HGRN_W), lambda bi, i: (bi, i, k))

    return pl.pallas_call(
        functools.partial(_hgrn_body, layer=layer, rows=rows),
        grid=(b, l // rows),
        in_specs=[section(0), section(1), section(2), section(3)]
        + [_const_spec(c.shape) for c in consts],
        out_specs=pl.BlockSpec((1, rows, HGRN_W), lambda bi, i: (bi, i, 0)),
        out_shape=jax.ShapeDtypeStruct((b, l, HGRN_W), BF16),
        scratch_shapes=[pltpu.VMEM((HGRN_W, HGRN_W), F32), pltpu.VMEM((CHUNKS_PER_TRIP, CHUNK, HGRN_W), F32)],
        compiler_params=_params(("arbitrary", "arbitrary")),
        name="hgrn",
    )(hh, hh, hh, hh, *consts)


def _fox_body(qkb_ref, cpre_ref, qa_ref, ka_ref, vat_ref, o_ref, m_ref, acc_ref, s_ref,
              *, tq, tk, n_pre):
    bi, hp, qi = pl.program_id(0), pl.program_id(1), pl.program_id(2)
    per_q = tq // tk
    row = lax.broadcasted_iota(jnp.int32, (tk, tq), 0)
    col = lax.broadcasted_iota(jnp.int32, (tk, tq), 1)

    def c_before(hh, n):
        return cpre_ref[(bi * FOX_HEADS + 2 * hp + hh) * n_pre + n]

    def scores(hh, j):
        return _dot_nt(ka_ref[0, hh, pl.ds(pl.multiple_of(j * tk, tk), tk), :], qa_ref[0, hh])

    def update(hh, j, s, mask):
        if mask is not None:
            s = jnp.where(mask, s, MASK_VALUE)
        m_old = m_ref[hh]
        m_new = jnp.maximum(m_old, jnp.max(s, axis=0, keepdims=True))
        p = jnp.exp2(s - m_new).astype(BF16)
        acc_ref[hh] = acc_ref[hh] * jnp.exp2(m_old - m_new) + _dot(vat_ref[0, hh, j], p)
        m_ref[hh] = m_new

    for hh in range(2):
        m_ref[hh] = jnp.full((1, tq), MASK_VALUE, F32)
        acc_ref[hh] = jnp.zeros((LANES, tq), F32)

    diag = [(hh, dj) for dj in range(per_q) for hh in range(2)]
    s_diag = [scores(hh, qi * per_q + dj) for hh, dj in diag]
    j_first = jnp.maximum(qi * per_q - 1, 0)
    s_first = [scores(hh, j_first) for hh in range(2)]
    for (hh, dj), s in zip(diag, s_diag):
        update(hh, qi * per_q + dj, s, row + dj * tk <= col)

    def needed(j):
        out = None
        for hh in range(2):
            gap = (qkb_ref[0] + c_before(hh, qi * per_q) - c_before(hh, j + 1)
                   - jnp.min(m_ref[hh]))
            out = gap >= SKIP_BELOW if out is None else jnp.logical_or(out, gap >= SKIP_BELOW)
        return out.astype(jnp.int32)

    def cond(carry):
        j, go = carry
        return jnp.logical_and(j >= 0, go > 0)

    def stage(src, dst, j):
        j_next = jnp.maximum(j - 1, 0)
        for hh in range(2):
            s_ref[dst, hh] = scores(hh, j_next)
        for hh in range(2):
            update(hh, j, s_ref[src, hh], None)
        return jnp.where(j >= 1, needed(j_next), 0)

    def body(carry):
        j, _ = carry
        go = stage(0, 1, j)
        go = lax.cond(go > 0, lambda: stage(1, 0, j - 1), lambda: jnp.int32(0))
        return j - 2, go

    for hh in range(2):
        s_ref[0, hh] = s_first[hh]
    lax.while_loop(cond, body, (qi * per_q - 1, needed(j_first)))

    outs = []
    for hh in range(2):
        acc = acc_ref[hh]
        outs.append(acc[0:HEAD_DIM, :] / acc[HEAD_DIM:HEAD_DIM + 1, :])
    o_ref[0] = jnp.concatenate(outs, axis=0).T.astype(BF16)


def _fox(qkb, cpre, qa, ka, vat):
    b, _, l, _ = qa.shape
    tq, tk = min(Q_TILE, l), vat.shape[-1]
    smem = pl.BlockSpec(memory_space=pltpu.SMEM)
    return pl.pallas_call(
        functools.partial(_fox_body, tq=tq, tk=tk, n_pre=l // tk + 1),
        grid=(b, FOX_HEADS // 2, l // tq),
        in_specs=[
            smem, smem,
            pl.BlockSpec((1, 2, tq, LANES), lambda bi, hp, i: (bi, hp, i, 0)),
            pl.BlockSpec((1, 2, l, LANES), lambda bi, hp, i: (bi, hp, 0, 0)),
            pl.BlockSpec((1, 2, l // tk, LANES, tk), lambda bi, hp, i: (bi, hp, 0, 0, 0)),
        ],
        out_specs=pl.BlockSpec((1, tq, LANES), lambda bi, hp, i: (bi, i, hp)),
        out_shape=jax.ShapeDtypeStruct((b, l, FOX_W), BF16),
        scratch_shapes=[pltpu.VMEM((2, 1, tq), F32), pltpu.VMEM((2, LANES, tq), F32),
                        pltpu.VMEM((2, 2, tk, tq), F32)],
        compiler_params=_params(("arbitrary", "arbitrary", "arbitrary")),
        name="fox",
    )(qkb, cpre, qa, ka, vat)


def _pad_lanes(v, width=LANES):
    return jnp.pad(v, (0, width - v.shape[0])).reshape(1, width)


def _placement():
    m = np.zeros((N_PIECES * LANES, FOX_HEADS * LANES), np.float32)
    for p in range(N_PIECES):
        for h in range(FOX_HEADS):
            m[p * LANES + h, h * LANES + Q_C_LANE + p] = 1.0
            m[p * LANES + h, h * LANES + K_ONE_LANE + N_PIECES + p] = -1.0
    return jnp.asarray(m, BF16)


def _tri3(n):
    return jnp.asarray(np.tile(np.tril(np.ones((n, n), np.float32)), (1, N_PIECES)), BF16)


def _chunk_sums():
    t = np.arange(CHUNK)[:, None]
    r = np.arange(CHUNK)[None, :]
    first = (t // SUB) * SUB
    s_pair, t_pair = _sub_chunk_pairs()
    rows = [r <= t, r < first, r < first + SUB,
            (r > s_pair[:, None]) & (r <= t_pair[:, None])]
    m = np.concatenate(rows, axis=0).astype(np.float32)
    return jnp.asarray(np.tile(m, (1, N_PIECES)), BF16)


def _sub_chunk_pairs():
    s = np.repeat(np.arange(CHUNK), SUB)
    t = (s // SUB) * SUB + np.tile(np.arange(SUB), CHUNK)
    return s, t


def _causal_replicate():
    s, t = _sub_chunk_pairs()
    m = (np.arange(CHUNK)[None, :] == s[:, None]) & (t >= s)[:, None]
    return jnp.asarray(m.astype(np.float32), BF16)


def _layer_operands(l, tm, mix_norm, w_mix_in, conv_w, hgrn_lb_logits, hgrn_out_gain,
                    fox_q_gain, fox_k_gain, fox_f_bias, w_mix_out):
    w = w_mix_in[l]
    o_h = 3 * CONV_CH
    o_q = o_h + 4 * HGRN_W
    o_k, o_v, o_f = o_q + FOX_W, o_q + 2 * FOX_W, o_q + 3 * FOX_W
    head_blocks = np.kron(np.eye(HGRN_HEADS, dtype=np.float32),
                          np.ones((HEAD_DIM, HEAD_DIM), np.float32))
    wo = w_mix_out[l].astype(BF16)
    q_scale = fox_q_gain[l] * (HEAD_DIM ** -0.5 * LOG2E)
    return {
        "mix_norm": mix_norm[l].reshape(1, D_MODEL),
        "wc": w[:, :o_h].astype(BF16),
        "conv_w": jnp.pad(conv_w[l], ((0, 8 - CONV_WIDTH), (0, 0))),
        "wh": w[:, o_h:o_q].astype(BF16),
        "wq": w[:, o_q:o_k].astype(BF16),
        "wk": w[:, o_k:o_v].astype(BF16),
        "wvt": w[:, o_v:o_f].T.astype(BF16),
        "wf": jnp.pad(w[:, o_f:], ((0, 0), (0, LANES - FOX_HEADS))).astype(BF16),
        "fbias": _pad_lanes(fox_f_bias[l]),
        "gq": jnp.tile(q_scale, LANES // HEAD_DIM).reshape(1, LANES),
        "gk": jnp.tile(fox_k_gain[l], LANES // HEAD_DIM).reshape(1, LANES),
        "qkb": (1.01 * HEAD_DIM * jnp.max(jnp.abs(q_scale))
                * jnp.max(jnp.abs(fox_k_gain[l]))).reshape(1),
        "tri": _tri3(tm),
        "place": _placement(),
        "lb_logits": hgrn_lb_logits,
        "hgrn_gain": jnp.tile(hgrn_out_gain[l], HGRN_HEADS).reshape(1, HGRN_W),
        "cum64": _chunk_sums(),
        "rep64": _causal_replicate(),
        "bd": jnp.asarray(head_blocks, BF16),
        "bmask": jnp.asarray(head_blocks),
        "wo_c": wo[:CONV_CH],
        "wo_h": wo[CONV_CH:CONV_CH + HGRN_W],
        "wo_f": wo[CONV_CH + HGRN_W:],
    }


def kernel(x, ffn1_norm, ffn1_w_in, ffn1_w_out, mix_norm, w_mix_in, conv_w, hgrn_lb_logits,
           hgrn_out_gain, fox_q_gain, fox_k_gain, fox_f_bias, w_mix_out, ffn2_norm, ffn2_w_in,
           ffn2_w_out):
    b, l, d = x.shape
    assert d == D_MODEL and l % ROW_TILE == 0 and ROW_TILE == Q_TILE and Q_TILE % KV_TILE == 0
    depth = ffn1_norm.shape[0]
    tm = min(ROW_TILE, l)
    n = b * l
    w1_in, w1_out = ffn1_w_in.astype(BF16), ffn1_w_out.astype(BF16)
    w2_in, w2_out = ffn2_w_in.astype(BF16), ffn2_w_out.astype(BF16)
    for layer in range(depth):
        p = _layer_operands(layer, tm, mix_norm, w_mix_in, conv_w, hgrn_lb_logits, hgrn_out_gain,
                            fox_q_gain, fox_k_gain, fox_f_bias, w_mix_out)
        x2 = _ffn(x.reshape(n, d), ffn1_norm[layer].reshape(1, d), w1_in, w1_out, layer)
        yc, hh, qa, ka, vat, cend = _mix_in(x2.reshape(b, l, d), p)
        yh = _hgrn(hh, p, layer)
        cend = cend[:, :, :tm // KV_TILE, :FOX_HEADS].reshape(b, l // KV_TILE, FOX_HEADS)
        cpre = jnp.pad(cend.transpose(0, 2, 1), ((0, 0), (0, 0), (1, 0))) * LOG2E
        yf = _fox(p["qkb"], cpre.reshape(-1), qa, ka, vat)
        x2 = _mix_out_ffn(x2, yc.reshape(n, CONV_CH), yh.reshape(n, HGRN_W), yf.reshape(n, FOX_W),
                          p["wo_c"], p["wo_h"], p["wo_f"], ffn2_norm[layer].reshape(1, d),
                          w2_in, w2_out, layer)
        x = x2.reshape(b, l, d)
    return x
```

```python
import functools

import jax
import jax.numpy as jnp
import numpy as np
from jax import lax
from jax.experimental import pallas as pl
from jax.experimental.pallas import tpu as pltpu

F32 = jnp.float32
BF16 = jnp.bfloat16

D_MODEL = 1024
D_FF = 2816
HEAD_DIM = 64
CONV_CH = 256
CONV_WIDTH = 3
HGRN_HEADS = 4
HGRN_W = 256
FOX_HEADS = 8
FOX_W = 512
CHUNK = 64
EPS = 1e-6
MASK_VALUE = -1e30

LANES = 128
FF_TILE = 256
N_FF_TILES = D_FF // FF_TILE
ROW_TILE = 512
Q_TILE = 512
KV_TILE = 256
HGRN_ROWS = 512
SUB = 8
CHUNKS_PER_TRIP = 4
VMEM_LIMIT = 56 * 1024 * 1024

Q_C_LANE = HEAD_DIM
K_ONE_LANE = HEAD_DIM
N_PIECES = 3
LOG2E = 1.4426950408889634
SKIP_BELOW = -138.0


def _dot(a, b):
    return jnp.dot(a, b, preferred_element_type=F32)


def _dot_nt(a, b):
    return lax.dot_general(a, b, (((1,), (1,)), ((), ())), preferred_element_type=F32)


def _dot_tn(a, b):
    return lax.dot_general(a, b, (((0,), (0,)), ((), ())), preferred_element_type=F32)


def _rms_norm(x, gain):
    inv = lax.rsqrt(jnp.mean(x * x, axis=-1, keepdims=True) + EPS)
    return x * inv * gain


def _sigmoid(x):
    return 1.0 / (1.0 + jnp.exp(-x))


def _log_sigmoid(x):
    return jnp.minimum(x, 0.0) - jnp.log(1.0 + jnp.exp(-jnp.abs(x)))


def _split3(x):
    hi = x.astype(BF16)
    r = x - hi.astype(F32)
    mid = r.astype(BF16)
    lo = (r - mid.astype(F32)).astype(BF16)
    return hi, mid, lo


def _const_spec(shape):
    nd = len(shape)
    return pl.BlockSpec(shape, lambda *_: (0,) * nd, pipeline_mode=pl.Buffered(1))


def _layer_spec(stacked_shape, layer):
    nd = len(stacked_shape)
    return pl.BlockSpec((None,) + tuple(stacked_shape[1:]), lambda *_: (layer,) + (0,) * (nd - 1),
                        pipeline_mode=pl.Buffered(1))


def _params(sem):
    return pltpu.CompilerParams(dimension_semantics=sem, vmem_limit_bytes=VMEM_LIMIT)


def _swiglu_half_step(x, g_ref, wi_ref, wo_ref, o_ref, acc_ref):
    xn = _rms_norm(x, g_ref[...]).astype(BF16)
    for c in range(N_FF_TILES):
        gate = _dot(xn, wi_ref[:, c * FF_TILE:(c + 1) * FF_TILE])
        up = _dot(xn, wi_ref[:, D_FF + c * FF_TILE:D_FF + (c + 1) * FF_TILE])
        act = (gate * _sigmoid(gate) * up).astype(BF16)
        part = _dot(act, wo_ref[c * FF_TILE:(c + 1) * FF_TILE, :])
        if c == 0:
            acc_ref[...] = part
        else:
            acc_ref[...] += part
    o_ref[...] = x + 0.5 * acc_ref[...]


def _ffn_body(x_ref, g_ref, wi_ref, wo_ref, o_ref, acc_ref):
    _swiglu_half_step(x_ref[...], g_ref, wi_ref, wo_ref, o_ref, acc_ref)


def _mix_out_ffn_body(x_ref, yc_ref, yh_ref, yf_ref, wc_ref, wh_ref, wf_ref,
                      g_ref, wi_ref, wo_ref, o_ref, acc_ref):
    x = (x_ref[...] + _dot(yc_ref[...], wc_ref[...]) + _dot(yh_ref[...], wh_ref[...])
         + _dot(yf_ref[...], wf_ref[...]))
    _swiglu_half_step(x, g_ref, wi_ref, wo_ref, o_ref, acc_ref)


def _mix_out_ffn(x2d, yc, yh, yf, wc, wh, wf, gain, wi, wo, layer):
    n = x2d.shape[0]
    tm = min(ROW_TILE, n)

    def row(w):
        return pl.BlockSpec((tm, w), lambda i: (i, 0))

    consts = [wc, wh, wf, gain]
    return pl.pallas_call(
        _mix_out_ffn_body,
        grid=(n // tm,),
        in_specs=[row(D_MODEL), row(CONV_CH), row(HGRN_W), row(FOX_W)]
        + [_const_spec(c.shape) for c in consts]
        + [_layer_spec(wi.shape, layer), _layer_spec(wo.shape, layer)],
        out_specs=row(D_MODEL),
        out_shape=jax.ShapeDtypeStruct((n, D_MODEL), F32),
        scratch_shapes=[pltpu.VMEM((tm, D_MODEL), F32)],
        compiler_params=_params(("arbitrary",)),
        name="mix_out_ffn",
    )(x2d, yc, yh, yf, *consts, wi, wo)


def _ffn(x2d, gain, wi, wo, layer):
    n = x2d.shape[0]
    tm = min(ROW_TILE, n)
    row = pl.BlockSpec((tm, D_MODEL), lambda i: (i, 0))
    return pl.pallas_call(
        _ffn_body,
        grid=(n // tm,),
        in_specs=[row, _const_spec(gain.shape), _layer_spec(wi.shape, layer),
                  _layer_spec(wo.shape, layer)],
        out_specs=row,
        out_shape=jax.ShapeDtypeStruct((n, D_MODEL), F32),
        scratch_shapes=[pltpu.VMEM((tm, D_MODEL), F32)],
        compiler_params=_params(("arbitrary",)),
        name="ffn",
    )(x2d, gain, wi, wo)


def _mix_in_body(x_ref, g_ref, wc_ref, cw_ref, wh_ref, wq_ref, wk_ref, wvt_ref, wf_ref,
                 fb_ref, gq_ref, gk_ref, tri_ref, place_ref,
                 yc_ref, hh_ref, qa_ref, ka_ref, vat_ref, cend_ref, ubuf_ref, carry_ref, *, tm):
    @pl.when(pl.program_id(1) == 0)
    def _():
        ubuf_ref[0:8, :] = jnp.zeros((8, CONV_CH), F32)
        carry_ref[...] = jnp.zeros_like(carry_ref)

    xn = _rms_norm(x_ref[0], g_ref[...]).astype(BF16)

    lf = _log_sigmoid(_dot(xn, wf_ref[...]) + fb_ref[...])
    lf3 = jnp.concatenate(_split3(lf), axis=1)
    last = carry_ref[...]
    blocks, ends = [], []
    for n in range(tm // KV_TILE):
        part = _dot(tri_ref[...], lf3[n * KV_TILE:(n + 1) * KV_TILE])
        blocks.append(last + (part[:, 0:LANES] + part[:, LANES:2 * LANES] + part[:, 2 * LANES:]))
        last = blocks[-1][KV_TILE - 1:KV_TILE, :]
        ends.append(last)
    c = jnp.concatenate(blocks, axis=0)
    carry_ref[...] = last
    cend_ref[0, 0] = jnp.concatenate(ends + [jnp.zeros((8 - len(ends), LANES), F32)], axis=0)

    lane = lax.broadcasted_iota(jnp.int32, (1, LANES), 1)
    c_hi, c_mid, c_lo = [piece.astype(F32) for piece in _split3(c * LOG2E)]
    packed = jnp.where(lane < FOX_HEADS, c_hi,
                       jnp.where(lane < 2 * FOX_HEADS, pltpu.roll(c_mid, FOX_HEADS, axis=1),
                                 pltpu.roll(c_lo, 2 * FOX_HEADS, axis=1)))
    placed = _dot(packed.astype(BF16), place_ref[...])

    low = lane < HEAD_DIM
    q_takes_c = jnp.logical_and(lane >= Q_C_LANE, lane < Q_C_LANE + N_PIECES)
    k_takes_c = jnp.logical_and(lane >= K_ONE_LANE + N_PIECES, lane < K_ONE_LANE + 2 * N_PIECES)
    one_q = k_takes_c.astype(F32)
    one_k = q_takes_c.astype(F32)

    def head_pair_norm(x2, gain2):
        sq = x2 * x2
        ss_lo = jnp.sum(jnp.where(low, sq, 0.0), axis=-1, keepdims=True)
        ss_hi = jnp.sum(jnp.where(low, 0.0, sq), axis=-1, keepdims=True)
        inv = lax.rsqrt(jnp.where(low, ss_lo, ss_hi) * (1.0 / HEAD_DIM) + EPS)
        return x2 * inv * gain2

    hq = _dot(xn, wq_ref[...])
    hk = _dot(xn, wk_ref[...])
    for pair in range(FOX_HEADS // 2):
        cols = slice(pair * LANES, (pair + 1) * LANES)
        qn = head_pair_norm(hq[:, cols], gq_ref[...])
        kn = head_pair_norm(hk[:, cols], gk_ref[...])
        for half in range(2):
            h = 2 * pair + half
            extra = placed[:, h * LANES:(h + 1) * LANES]
            q_h = qn if half == 0 else pltpu.roll(qn, HEAD_DIM, axis=1)
            k_h = kn if half == 0 else pltpu.roll(kn, HEAD_DIM, axis=1)
            qa_ref[0, h] = jnp.where(low, q_h, jnp.where(q_takes_c, extra, one_q)).astype(BF16)
            ka_ref[0, h] = jnp.where(low, k_h, jnp.where(k_takes_c, extra, one_k)).astype(BF16)

    vt = _dot_nt(wvt_ref[...], xn).astype(BF16)
    tail = (lax.broadcasted_iota(jnp.int32, (LANES - HEAD_DIM, KV_TILE), 0) == 0).astype(BF16)
    for h in range(FOX_HEADS):
        for n in range(tm // KV_TILE):
            vat_ref[0, h, n] = jnp.concatenate(
                [vt[h * HEAD_DIM:(h + 1) * HEAD_DIM, n * KV_TILE:(n + 1) * KV_TILE], tail], axis=0)

    hh_ref[0] = _dot(xn, wh_ref[...])

    hc = _dot(xn, wc_ref[...])
    u = hc[:, 2 * CONV_CH:3 * CONV_CH] * hc[:, 0:CONV_CH]
    ubuf_ref[8:8 + tm, :] = u
    u1 = ubuf_ref[7:7 + tm, :]
    u2 = ubuf_ref[6:6 + tm, :]
    cw = cw_ref[...]
    conv = cw[0:1, :] * u2 + cw[1:2, :] * u1 + cw[2:3, :] * u
    yc_ref[0] = (hc[:, CONV_CH:2 * CONV_CH] * conv).astype(BF16)
    ubuf_ref[0:8, :] = u[tm - 8:tm, :]


def _mix_in(x, p):
    b, l, _ = x.shape
    tm = min(ROW_TILE, l)
    nt = l // tm
    consts = [p["mix_norm"], p["wc"], p["conv_w"], p["wh"], p["wq"], p["wk"], p["wvt"], p["wf"],
              p["fbias"], p["gq"], p["gk"], p["tri"], p["place"]]
    out_shape = [
        jax.ShapeDtypeStruct((b, l, CONV_CH), BF16),
        jax.ShapeDtypeStruct((b, l, 4 * HGRN_W), F32),
        jax.ShapeDtypeStruct((b, FOX_HEADS, l, LANES), BF16),
        jax.ShapeDtypeStruct((b, FOX_HEADS, l, LANES), BF16),
        jax.ShapeDtypeStruct((b, FOX_HEADS, l // KV_TILE, LANES, KV_TILE), BF16),
        jax.ShapeDtypeStruct((b, nt, 8, LANES), F32),
    ]
    out_specs = [
        pl.BlockSpec((1, tm, CONV_CH), lambda bi, i: (bi, i, 0)),
        pl.BlockSpec((1, tm, 4 * HGRN_W), lambda bi, i: (bi, i, 0)),
        pl.BlockSpec((1, FOX_HEADS, tm, LANES), lambda bi, i: (bi, 0, i, 0)),
        pl.BlockSpec((1, FOX_HEADS, tm, LANES), lambda bi, i: (bi, 0, i, 0)),
        pl.BlockSpec((1, FOX_HEADS, tm // KV_TILE, LANES, KV_TILE), lambda bi, i: (bi, 0, i, 0, 0)),
        pl.BlockSpec((1, 1, 8, LANES), lambda bi, i: (bi, i, 0, 0)),
    ]
    return pl.pallas_call(
        functools.partial(_mix_in_body, tm=tm),
        grid=(b, nt),
        in_specs=[pl.BlockSpec((1, tm, D_MODEL), lambda bi, i: (bi, i, 0))]
        + [_const_spec(c.shape) for c in consts],
        out_specs=out_specs,
        out_shape=out_shape,
        scratch_shapes=[pltpu.VMEM((tm + 8, CONV_CH), F32), pltpu.VMEM((1, LANES), F32)],
        compiler_params=_params(("arbitrary", "arbitrary")),
        name="mix_in",
    )(x, *consts)


def _hgrn_body(q_ref, z_ref, v_ref, g_ref, lbl_ref, gain_ref, cum_ref, rep_ref, bd_ref, bmask_ref,
               o_ref, st_ref, vv_ref, *, layer, rows):
    @pl.when(pl.program_id(1) == 0)
    def _():
        st_ref[...] = jnp.zeros_like(st_ref)

    lbl = lbl_ref[...]
    e = jnp.exp(lbl - jnp.max(lbl, axis=0, keepdims=True))
    soft = e / jnp.sum(e, axis=0, keepdims=True)
    lb = jnp.clip(jnp.sum(soft[0:layer + 1, :], axis=0, keepdims=True) - soft[0:1, :], 0.0, 1.0)

    bd = bd_ref[...]
    n_sub = CHUNK // SUB
    sub_t = lax.broadcasted_iota(jnp.int32, (CHUNK, HGRN_W), 0) // SUB
    sub_s = lax.broadcasted_iota(jnp.int32, (CHUNK, HGRN_W), 1) % CHUNK // SUB

    def within_chunk(ch, slot):
        rs = pl.ds(pl.multiple_of(ch * CHUNK, CHUNK), CHUNK)
        q = q_ref[0, rs, :]
        z = z_ref[0, rs, :]
        v = v_ref[0, rs, :]
        lf = _log_sigmoid(z) + jnp.log(1.0 + lb * jnp.exp(-z))
        k = (1.0 - lb) * _sigmoid(-z)
        pieces = jnp.concatenate(_split3(lf), axis=0)
        sums = _dot(cum_ref[0:3 * CHUNK, :], pieces)
        vv_ref[slot] = v
        v16 = v.astype(BF16)
        k16 = k.astype(BF16)
        yield

        bc = sums[0:CHUNK]
        bs = sums[CHUNK:2 * CHUNK]
        be = sums[2 * CHUNK:3 * CHUNK]

        q_t = q * jnp.exp(bc - bs)
        k_t = (k * jnp.exp(be - bc)).astype(BF16)
        lags = [q_t]
        for lag in range(2, n_sub):
            be_shift = jnp.concatenate([jnp.zeros((lag * SUB, HGRN_W), F32),
                                        be[0:CHUNK - lag * SUB]], axis=0)
            lags.append(q_t * jnp.exp(jnp.minimum(bs - be_shift, 0.0)))
        q_lags = jnp.concatenate(lags, axis=0).astype(BF16)
        sc = _dot_nt(q_lags, jnp.concatenate([k_t] * HGRN_HEADS, axis=0) * bd)

        half_rows = CHUNK * SUB // 2
        pair = jnp.concatenate(
            [_dot(cum_ref[3 * CHUNK + n * half_rows:3 * CHUNK + (n + 1) * half_rows, :], pieces)
             for n in range(2)], axis=0)
        k_rep = jnp.concatenate(
            [_dot(rep_ref[n * half_rows:(n + 1) * half_rows, :], k16) for n in range(2)], axis=0)
        yield

        a_lag = jnp.zeros((CHUNK, HGRN_W), F32)
        for n, lag in enumerate(range(1, n_sub)):
            a_lag = jnp.where(sub_t - sub_s == lag, sc[n * CHUNK:(n + 1) * CHUNK], a_lag)
        o_lag = _dot(a_lag.astype(BF16), jnp.concatenate([v16] * HGRN_HEADS, axis=0) * bd)
        a = []
        for i in range(n_sub):
            rows_i = slice(i * SUB * SUB, (i + 1) * SUB * SUB)
            q_rep = jnp.concatenate([q[i * SUB:(i + 1) * SUB]] * SUB, axis=0)
            a.append((q_rep * jnp.exp(pair[rows_i]) * k_rep[rows_i]).astype(BF16))
        r = jnp.concatenate(
            [_dot(jnp.concatenate(a[n * n_sub // 2:(n + 1) * n_sub // 2], axis=0), bd)
             for n in range(2)], axis=0)
        b_last = bc[CHUNK - 1:CHUNK, :]
        q_hat = (q * jnp.exp(bc)).astype(BF16)
        k_hat = (k * jnp.exp(b_last - bc)).astype(BF16)
        yield

        o_sub = []
        for i in range(n_sub):
            acc = None
            for sl in range(SUB):
                s = i * SUB + sl
                term = r[s * SUB:(s + 1) * SUB] * vv_ref[slot, s:s + 1, :]
                acc = term if acc is None else acc + term
            o_sub.append(acc)
        o = o_lag + jnp.concatenate(o_sub, axis=0)
        return rs, o, q_hat, k_hat, v16, jnp.exp(b_last)

    def finish(rs, o, q_hat, k_hat, v16, decay_last):
        st = st_ref[...]
        o = o + _dot_nt(q_hat, st.astype(BF16))
        st_ref[...] = st * decay_last + bmask_ref[...] * _dot_tn(v16, k_hat)

        oo = o * o
        hi = oo.astype(BF16)
        lo = (oo - hi.astype(F32)).astype(BF16)
        ms = (_dot(hi, bd) + _dot(lo, bd)) * (1.0 / HEAD_DIM)
        gate = g_ref[0, rs, :]
        y = o * lax.rsqrt(ms + EPS) * gain_ref[...] * (gate * _sigmoid(gate))
        o_ref[0, rs, :] = y.astype(BF16)

    def chunk_group(cc, carry):
        gens = [within_chunk(CHUNKS_PER_TRIP * cc + slot, slot) for slot in range(CHUNKS_PER_TRIP)]
        parts = [None] * CHUNKS_PER_TRIP
        while any(part is None for part in parts):
            for slot, gen in enumerate(gens):
                if parts[slot] is None:
                    try:
                        next(gen)
                    except StopIteration as done:
                        parts[slot] = done.value
        for part in parts:
            finish(*part)
        return carry

    lax.fori_loop(0, rows // (CHUNKS_PER_TRIP * CHUNK), chunk_group, 0)


def _hgrn(hh, p, layer):
    b, l, _ = hh.shape
    rows = min(HGRN_ROWS, l)
    consts = [p["lb_logits"], p["hgrn_gain"], p["cum64"], p["rep64"], p["bd"], p["bmask"]]

    def section(k):
        return pl.BlockSpec((1, rows, HGRN_W), lambda bi, i: (bi, i, k))

    return pl.pallas_call(
        functools.partial(_hgrn_body, layer=layer, rows=rows),
        grid=(b, l // rows),
        in_specs=[section(0), section(1), section(2), section(3)]
        + [_const_spec(c.shape) for c in consts],
        out_specs=pl.BlockSpec((1, rows, HGRN_W), lambda bi, i: (bi, i, 0)),
        out_shape=jax.ShapeDtypeStruct((b, l, HGRN_W), BF16),
        scratch_shapes=[pltpu.VMEM((HGRN_W, HGRN_W), F32),
                        pltpu.VMEM((CHUNKS_PER_TRIP, CHUNK, HGRN_W), F32)],
        compiler_params=_params(("arbitrary", "arbitrary")),
        name="hgrn",
    )(hh, hh, hh, hh, *consts)


def _fox_body(qkb_ref, cpre_ref, qa_ref, ka_ref, vat_ref, o_ref, m_ref, acc_ref, s_ref,
              *, tq, tk, n_pre):
    bi, hp, qi = pl.program_id(0), pl.program_id(1), pl.program_id(2)
    per_q = tq // tk
    row = lax.broadcasted_iota(jnp.int32, (tk, tq), 0)
    col = lax.broadcasted_iota(jnp.int32, (tk, tq), 1)

    def c_before(hh, n):
        return cpre_ref[(bi * FOX_HEADS + 2 * hp + hh) * n_pre + n]

    def scores(hh, j):
        return _dot_nt(ka_ref[0, hh, pl.ds(pl.multiple_of(j * tk, tk), tk), :], qa_ref[0, hh])

    def update(hh, j, s, mask):
        if mask is not None:
            s = jnp.where(mask, s, MASK_VALUE)
        m_old = m_ref[hh]
        m_new = jnp.maximum(m_old, jnp.max(s, axis=0, keepdims=True))
        p = jnp.exp2(s - m_new).astype(BF16)
        acc_ref[hh] = acc_ref[hh] * jnp.exp2(m_old - m_new) + _dot(vat_ref[0, hh, j], p)
        m_ref[hh] = m_new

    for hh in range(2):
        m_ref[hh] = jnp.full((1, tq), MASK_VALUE, F32)
        acc_ref[hh] = jnp.zeros((LANES, tq), F32)

    diag = [(hh, dj) for dj in range(per_q) for hh in range(2)]
    s_diag = [scores(hh, qi * per_q + dj) for hh, dj in diag]
    j_first = jnp.maximum(qi * per_q - 1, 0)
    s_first = [scores(hh, j_first) for hh in range(2)]
    for (hh, dj), s in zip(diag, s_diag):
        update(hh, qi * per_q + dj, s, row + dj * tk <= col)

    def needed(j):
        out = None
        for hh in range(2):
            gap = (qkb_ref[0] + c_before(hh, qi * per_q) - c_before(hh, j + 1)
                   - jnp.min(m_ref[hh]))
            out = gap >= SKIP_BELOW if out is None else jnp.logical_or(out, gap >= SKIP_BELOW)
        return out.astype(jnp.int32)

    def cond(carry):
        j, go = carry
        return jnp.logical_and(j >= 0, go > 0)

    def stage(src, dst, j):
        j_next = jnp.maximum(j - 1, 0)
        for hh in range(2):
            s_ref[dst, hh] = scores(hh, j_next)
        for hh in range(2):
            update(hh, j, s_ref[src, hh], None)
        return jnp.where(j >= 1, needed(j_next), 0)

    def body(carry):
        j, _ = carry
        go = stage(0, 1, j)
        go = lax.cond(go > 0, lambda: stage(1, 0, j - 1), lambda: jnp.int32(0))
        return j - 2, go

    for hh in range(2):
        s_ref[0, hh] = s_first[hh]
    lax.while_loop(cond, body, (qi * per_q - 1, needed(j_first)))

    outs = []
    for hh in range(2):
        acc = acc_ref[hh]
        outs.append(acc[0:HEAD_DIM, :] / acc[HEAD_DIM:HEAD_DIM + 1, :])
    o_ref[0] = jnp.concatenate(outs, axis=0).T.astype(BF16)


def _fox(qkb, cpre, qa, ka, vat):
    b, _, l, _ = qa.shape
    tq, tk = min(Q_TILE, l), vat.shape[-1]
    smem = pl.BlockSpec(memory_space=pltpu.SMEM)
    return pl.pallas_call(
        functools.partial(_fox_body, tq=tq, tk=tk, n_pre=l // tk + 1),
        grid=(b, FOX_HEADS // 2, l // tq),
        in_specs=[
            smem, smem,
            pl.BlockSpec((1, 2, tq, LANES), lambda bi, hp, i: (bi, hp, i, 0)),
            pl.BlockSpec((1, 2, l, LANES), lambda bi, hp, i: (bi, hp, 0, 0)),
            pl.BlockSpec((1, 2, l // tk, LANES, tk), lambda bi, hp, i: (bi, hp, 0, 0, 0)),
        ],
        out_specs=pl.BlockSpec((1, tq, LANES), lambda bi, hp, i: (bi, i, hp)),
        out_shape=jax.ShapeDtypeStruct((b, l, FOX_W), BF16),
        scratch_shapes=[pltpu.VMEM((2, 1, tq), F32), pltpu.VMEM((2, LANES, tq), F32),
                        pltpu.VMEM((2, 2, tk, tq), F32)],
        compiler_params=_params(("arbitrary", "arbitrary", "arbitrary")),
        name="fox",
    )(qkb, cpre, qa, ka, vat)


def _pad_lanes(v, width=LANES):
    return jnp.pad(v, (0, width - v.shape[0])).reshape(1, width)


def _placement():
    m = np.zeros((LANES, FOX_HEADS * LANES), np.float32)
    for p in range(N_PIECES):
        for h in range(FOX_HEADS):
            m[p * FOX_HEADS + h, h * LANES + Q_C_LANE + p] = 1.0
            m[p * FOX_HEADS + h, h * LANES + K_ONE_LANE + N_PIECES + p] = -1.0
    return jnp.asarray(m, BF16)


def _tri(n):
    return jnp.asarray(np.tril(np.ones((n, n), np.float32)), BF16)


def _chunk_sums():
    t = np.arange(CHUNK)[:, None]
    r = np.arange(CHUNK)[None, :]
    first = (t // SUB) * SUB
    s_pair, t_pair = _sub_chunk_pairs()
    rows = [r <= t, r < first, r < first + SUB,
            (r > s_pair[:, None]) & (r <= t_pair[:, None])]
    m = np.concatenate(rows, axis=0).astype(np.float32)
    return jnp.asarray(np.tile(m, (1, N_PIECES)), BF16)


def _sub_chunk_pairs():
    s = np.repeat(np.arange(CHUNK), SUB)
    t = (s // SUB) * SUB + np.tile(np.arange(SUB), CHUNK)
    return s, t


def _causal_replicate():
    s, t = _sub_chunk_pairs()
    m = (np.arange(CHUNK)[None, :] == s[:, None]) & (t >= s)[:, None]
    return jnp.asarray(m.astype(np.float32), BF16)


def _layer_operands(l, mix_norm, w_mix_in, conv_w, hgrn_lb_logits, hgrn_out_gain,
                    fox_q_gain, fox_k_gain, fox_f_bias, w_mix_out):
    w = w_mix_in[l]
    o_h = 3 * CONV_CH
    o_q = o_h + 4 * HGRN_W
    o_k, o_v, o_f = o_q + FOX_W, o_q + 2 * FOX_W, o_q + 3 * FOX_W
    head_blocks = np.kron(np.eye(HGRN_HEADS, dtype=np.float32),
                          np.ones((HEAD_DIM, HEAD_DIM), np.float32))
    wo = w_mix_out[l].astype(BF16)
    q_scale = fox_q_gain[l] * (HEAD_DIM ** -0.5 * LOG2E)
    return {
        "mix_norm": mix_norm[l].reshape(1, D_MODEL),
        "wc": w[:, :o_h].astype(BF16),
        "conv_w": jnp.pad(conv_w[l], ((0, 8 - CONV_WIDTH), (0, 0))),
        "wh": w[:, o_h:o_q].astype(BF16),
        "wq": w[:, o_q:o_k].astype(BF16),
        "wk": w[:, o_k:o_v].astype(BF16),
        "wvt": w[:, o_v:o_f].T.astype(BF16),
        "wf": jnp.pad(w[:, o_f:], ((0, 0), (0, LANES - FOX_HEADS))).astype(BF16),
        "fbias": _pad_lanes(fox_f_bias[l]),
        "gq": jnp.tile(q_scale, LANES // HEAD_DIM).reshape(1, LANES),
        "gk": jnp.tile(fox_k_gain[l], LANES // HEAD_DIM).reshape(1, LANES),
        "qkb": (1.01 * HEAD_DIM * jnp.max(jnp.abs(q_scale))
                * jnp.max(jnp.abs(fox_k_gain[l]))).reshape(1),
        "tri": _tri(KV_TILE),
        "place": _placement(),
        "lb_logits": hgrn_lb_logits,
        "hgrn_gain": jnp.tile(hgrn_out_gain[l], HGRN_HEADS).reshape(1, HGRN_W),
        "cum64": _chunk_sums(),
        "rep64": _causal_replicate(),
        "bd": jnp.asarray(head_blocks, BF16),
        "bmask": jnp.asarray(head_blocks),
        "wo_c": wo[:CONV_CH],
        "wo_h": wo[CONV_CH:CONV_CH + HGRN_W],
        "wo_f": wo[CONV_CH + HGRN_W:],
    }


def kernel(x, ffn1_norm, ffn1_w_in, ffn1_w_out, mix_norm, w_mix_in, conv_w, hgrn_lb_logits,
           hgrn_out_gain, fox_q_gain, fox_k_gain, fox_f_bias, w_mix_out, ffn2_norm, ffn2_w_in,
           ffn2_w_out):
    b, l, d = x.shape
    assert d == D_MODEL and l % ROW_TILE == 0 and ROW_TILE == Q_TILE and Q_TILE % KV_TILE == 0
    depth = ffn1_norm.shape[0]
    tm = min(ROW_TILE, l)
    n = b * l
    w1_in, w1_out = ffn1_w_in.astype(BF16), ffn1_w_out.astype(BF16)
    w2_in, w2_out = ffn2_w_in.astype(BF16), ffn2_w_out.astype(BF16)
    for layer in range(depth):
        p = _layer_operands(layer, mix_norm, w_mix_in, conv_w, hgrn_lb_logits, hgrn_out_gain,
                            fox_q_gain, fox_k_gain, fox_f_bias, w_mix_out)
        x2 = _ffn(x.reshape(n, d), ffn1_norm[layer].reshape(1, d), w1_in, w1_out, layer)
        yc, hh, qa, ka, vat, cend = _mix_in(x2.reshape(b, l, d), p)
        yh = _hgrn(hh, p, layer)
        cend = cend[:, :, :tm // KV_TILE, :FOX_HEADS].reshape(b, l // KV_TILE, FOX_HEADS)
        cpre = jnp.pad(cend.transpose(0, 2, 1), ((0, 0), (0, 0), (1, 0))) * LOG2E
        yf = _fox(p["qkb"], cpre.reshape(-1), qa, ka, vat)
        x2 = _mix_out_ffn(x2, yc.reshape(n, CONV_CH), yh.reshape(n, HGRN_W), yf.reshape(n, FOX_W),
                          p["wo_c"], p["wo_h"], p["wo_f"], ffn2_norm[layer].reshape(1, d),
                          w2_in, w2_out, layer)
        x = x2.reshape(b, l, d)
    return x
```

```python
import functools

import jax
import jax.numpy as jnp
import numpy as np
from jax import lax
from jax.experimental import pallas as pl
from jax.experimental.pallas import tpu as pltpu

F32 = jnp.float32
BF16 = jnp.bfloat16

D_MODEL = 1024
D_FF = 2816
HEAD_DIM = 64
CONV_CH = 256
CONV_WIDTH = 3
HGRN_HEADS = 4
HGRN_W = 256
FOX_HEADS = 8
FOX_W = 512
CHUNK = 64
EPS = 1e-6
MASK_VALUE = -1e30

LANES = 128
FF_TILE = 256
N_FF_TILES = D_FF // FF_TILE
ROW_TILE = 512
FFN_ROWS = 1024
Q_TILE = 512
KV_TILE = 256
HGRN_ROWS = 512
SUB = 8
CHUNKS_PER_TRIP = 4
VMEM_LIMIT = 56 * 1024 * 1024

Q_C_LANE = HEAD_DIM
K_ONE_LANE = HEAD_DIM
N_PIECES = 3
LOG2E = 1.4426950408889634
SKIP_BELOW = -138.0


def _dot(a, b):
    return jnp.dot(a, b, preferred_element_type=F32)


def _dot_nt(a, b):
    return lax.dot_general(a, b, (((1,), (1,)), ((), ())), preferred_element_type=F32)


def _dot_tn(a, b):
    return lax.dot_general(a, b, (((0,), (0,)), ((), ())), preferred_element_type=F32)


def _rms_norm(x, gain):
    inv = lax.rsqrt(jnp.mean(x * x, axis=-1, keepdims=True) + EPS)
    return x * inv * gain


def _sigmoid(x):
    return 1.0 / (1.0 + jnp.exp(-x))


def _log_sigmoid(x):
    return jnp.minimum(x, 0.0) - jnp.log(1.0 + jnp.exp(-jnp.abs(x)))


def _split3(x):
    hi = x.astype(BF16)
    r = x - hi.astype(F32)
    mid = r.astype(BF16)
    lo = (r - mid.astype(F32)).astype(BF16)
    return hi, mid, lo


def _const_spec(shape):
    nd = len(shape)
    return pl.BlockSpec(shape, lambda *_: (0,) * nd, pipeline_mode=pl.Buffered(1))


def _layer_spec(stacked_shape, layer):
    nd = len(stacked_shape)
    return pl.BlockSpec((None,) + tuple(stacked_shape[1:]), lambda *_: (layer,) + (0,) * (nd - 1),
                        pipeline_mode=pl.Buffered(1))


def _params(sem):
    return pltpu.CompilerParams(dimension_semantics=sem, vmem_limit_bytes=VMEM_LIMIT)


def _swiglu_half_step(x, g_ref, wi_ref, wo_ref, o_ref, acc_ref):
    xn = _rms_norm(x, g_ref[...]).astype(BF16)
    for c in range(N_FF_TILES):
        gate = _dot(xn, wi_ref[:, c * FF_TILE:(c + 1) * FF_TILE])
        up = _dot(xn, wi_ref[:, D_FF + c * FF_TILE:D_FF + (c + 1) * FF_TILE])
        act = (gate * _sigmoid(gate) * up).astype(BF16)
        part = _dot(act, wo_ref[c * FF_TILE:(c + 1) * FF_TILE, :])
        if c == 0:
            acc_ref[...] = part
        else:
            acc_ref[...] += part
    o_ref[...] = x + 0.5 * acc_ref[...]


def _ffn_body(x_ref, g_ref, wi_ref, wo_ref, o_ref, acc_ref):
    _swiglu_half_step(x_ref[...], g_ref, wi_ref, wo_ref, o_ref, acc_ref)


def _mix_out_ffn_body(x_ref, yc_ref, yh_ref, yf_ref, wc_ref, wh_ref, wf_ref,
                      g_ref, wi_ref, wo_ref, o_ref, acc_ref):
    x = (x_ref[...] + _dot(yc_ref[...], wc_ref[...]) + _dot(yh_ref[...], wh_ref[...])
         + _dot(yf_ref[...], wf_ref[...]))
    _swiglu_half_step(x, g_ref, wi_ref, wo_ref, o_ref, acc_ref)


def _mix_out_ffn(x2d, yc, yh, yf, wc, wh, wf, gain, wi, wo, layer):
    n = x2d.shape[0]
    tm = min(FFN_ROWS, n)

    def row(w):
        return pl.BlockSpec((tm, w), lambda i: (i, 0))

    consts = [wc, wh, wf, gain]
    return pl.pallas_call(
        _mix_out_ffn_body,
        grid=(n // tm,),
        in_specs=[row(D_MODEL), row(CONV_CH), row(HGRN_W), row(FOX_W)]
        + [_const_spec(c.shape) for c in consts]
        + [_layer_spec(wi.shape, layer), _layer_spec(wo.shape, layer)],
        out_specs=row(D_MODEL),
        out_shape=jax.ShapeDtypeStruct((n, D_MODEL), F32),
        scratch_shapes=[pltpu.VMEM((tm, D_MODEL), F32)],
        compiler_params=_params(("arbitrary",)),
        name="mix_out_ffn",
    )(x2d, yc, yh, yf, *consts, wi, wo)


def _ffn(x2d, gain, wi, wo, layer):
    n = x2d.shape[0]
    tm = min(FFN_ROWS, n)
    row = pl.BlockSpec((tm, D_MODEL), lambda i: (i, 0))
    return pl.pallas_call(
        _ffn_body,
        grid=(n // tm,),
        in_specs=[row, _const_spec(gain.shape), _layer_spec(wi.shape, layer),
                  _layer_spec(wo.shape, layer)],
        out_specs=row,
        out_shape=jax.ShapeDtypeStruct((n, D_MODEL), F32),
        scratch_shapes=[pltpu.VMEM((tm, D_MODEL), F32)],
        compiler_params=_params(("arbitrary",)),
        name="ffn",
    )(x2d, gain, wi, wo)


def _mix_in_body(x_ref, g_ref, wc_ref, cw_ref, wh_ref, wq_ref, wk_ref, wvt_ref, wf_ref,
                 fb_ref, gq_ref, gk_ref, tri_ref, place_ref,
                 yc_ref, hh_ref, qa_ref, ka_ref, vat_ref, cend_ref, ubuf_ref, carry_ref, *, tm):
    @pl.when(pl.program_id(1) == 0)
    def _():
        ubuf_ref[0:8, :] = jnp.zeros((8, CONV_CH), F32)
        carry_ref[...] = jnp.zeros_like(carry_ref)

    xn = _rms_norm(x_ref[0], g_ref[...]).astype(BF16)

    lf = _log_sigmoid(_dot(xn, wf_ref[...]) + fb_ref[...])
    lf3 = jnp.concatenate(_split3(lf), axis=1)
    last = carry_ref[...]
    blocks, ends = [], []
    for n in range(tm // KV_TILE):
        part = _dot(tri_ref[...], lf3[n * KV_TILE:(n + 1) * KV_TILE])
        blocks.append(last + (part[:, 0:LANES] + part[:, LANES:2 * LANES] + part[:, 2 * LANES:]))
        last = blocks[-1][KV_TILE - 1:KV_TILE, :]
        ends.append(last)
    c = jnp.concatenate(blocks, axis=0)
    carry_ref[...] = last
    cend_ref[0, 0] = jnp.concatenate(ends + [jnp.zeros((8 - len(ends), LANES), F32)], axis=0)

    lane = lax.broadcasted_iota(jnp.int32, (1, LANES), 1)
    c_hi, c_mid, c_lo = [piece.astype(F32) for piece in _split3(c * LOG2E)]
    packed = jnp.where(lane < FOX_HEADS, c_hi,
                       jnp.where(lane < 2 * FOX_HEADS, pltpu.roll(c_mid, FOX_HEADS, axis=1),
                                 pltpu.roll(c_lo, 2 * FOX_HEADS, axis=1)))
    placed = _dot(packed.astype(BF16), place_ref[...])

    low = lane < HEAD_DIM
    q_takes_c = jnp.logical_and(lane >= Q_C_LANE, lane < Q_C_LANE + N_PIECES)
    k_takes_c = jnp.logical_and(lane >= K_ONE_LANE + N_PIECES, lane < K_ONE_LANE + 2 * N_PIECES)
    one_q = k_takes_c.astype(F32)
    one_k = q_takes_c.astype(F32)

    def head_pair_norm(x2, gain2):
        sq = x2 * x2
        ss_lo = jnp.sum(jnp.where(low, sq, 0.0), axis=-1, keepdims=True)
        ss_hi = jnp.sum(jnp.where(low, 0.0, sq), axis=-1, keepdims=True)
        inv = lax.rsqrt(jnp.where(low, ss_lo, ss_hi) * (1.0 / HEAD_DIM) + EPS)
        return x2 * inv * gain2

    hq = _dot(xn, wq_ref[...])
    hk = _dot(xn, wk_ref[...])
    for pair in range(FOX_HEADS // 2):
        cols = slice(pair * LANES, (pair + 1) * LANES)
        qn = head_pair_norm(hq[:, cols], gq_ref[...])
        kn = head_pair_norm(hk[:, cols], gk_ref[...])
        for half in range(2):
            h = 2 * pair + half
            extra = placed[:, h * LANES:(h + 1) * LANES]
            q_h = qn if half == 0 else pltpu.roll(qn, HEAD_DIM, axis=1)
            k_h = kn if half == 0 else pltpu.roll(kn, HEAD_DIM, axis=1)
            qa_ref[0, h] = jnp.where(low, q_h, jnp.where(q_takes_c, extra, one_q)).astype(BF16)
            ka_ref[0, h] = jnp.where(low, k_h, jnp.where(k_takes_c, extra, one_k)).astype(BF16)

    vt = _dot_nt(wvt_ref[...], xn).astype(BF16)
    tail = (lax.broadcasted_iota(jnp.int32, (LANES - HEAD_DIM, KV_TILE), 0) == 0).astype(BF16)
    for h in range(FOX_HEADS):
        for n in range(tm // KV_TILE):
            vat_ref[0, h, n] = jnp.concatenate(
                [vt[h * HEAD_DIM:(h + 1) * HEAD_DIM, n * KV_TILE:(n + 1) * KV_TILE], tail], axis=0)

    hh_ref[0] = _dot(xn, wh_ref[...])

    hc = _dot(xn, wc_ref[...])
    u = hc[:, 2 * CONV_CH:3 * CONV_CH] * hc[:, 0:CONV_CH]
    ubuf_ref[8:8 + tm, :] = u
    u1 = ubuf_ref[7:7 + tm, :]
    u2 = ubuf_ref[6:6 + tm, :]
    cw = cw_ref[...]
    conv = cw[0:1, :] * u2 + cw[1:2, :] * u1 + cw[2:3, :] * u
    yc_ref[0] = (hc[:, CONV_CH:2 * CONV_CH] * conv).astype(BF16)
    ubuf_ref[0:8, :] = u[tm - 8:tm, :]


def _mix_in(x, p):
    b, l, _ = x.shape
    tm = min(ROW_TILE, l)
    nt = l // tm
    consts = [p["mix_norm"], p["wc"], p["conv_w"], p["wh"], p["wq"], p["wk"], p["wvt"], p["wf"],
              p["fbias"], p["gq"], p["gk"], p["tri"], p["place"]]
    out_shape = [
        jax.ShapeDtypeStruct((b, l, CONV_CH), BF16),
        jax.ShapeDtypeStruct((b, l, 4 * HGRN_W), F32),
        jax.ShapeDtypeStruct((b, FOX_HEADS, l, LANES), BF16),
        jax.ShapeDtypeStruct((b, FOX_HEADS, l, LANES), BF16),
        jax.ShapeDtypeStruct((b, FOX_HEADS, l // KV_TILE, LANES, KV_TILE), BF16),
        jax.ShapeDtypeStruct((b, nt, 8, LANES), F32),
    ]
    out_specs = [
        pl.BlockSpec((1, tm, CONV_CH), lambda bi, i: (bi, i, 0)),
        pl.BlockSpec((1, tm, 4 * HGRN_W), lambda bi, i: (bi, i, 0)),
        pl.BlockSpec((1, FOX_HEADS, tm, LANES), lambda bi, i: (bi, 0, i, 0)),
        pl.BlockSpec((1, FOX_HEADS, tm, LANES), lambda bi, i: (bi, 0, i, 0)),
        pl.BlockSpec((1, FOX_HEADS, tm // KV_TILE, LANES, KV_TILE), lambda bi, i: (bi, 0, i, 0, 0)),
        pl.BlockSpec((1, 1, 8, LANES), lambda bi, i: (bi, i, 0, 0)),
    ]
    return pl.pallas_call(
        functools.partial(_mix_in_body, tm=tm),
        grid=(b, nt),
        in_specs=[pl.BlockSpec((1, tm, D_MODEL), lambda bi, i: (bi, i, 0))]
        + [_const_spec(c.shape) for c in consts],
        out_specs=out_specs,
        out_shape=out_shape,
        scratch_shapes=[pltpu.VMEM((tm + 8, CONV_CH), F32), pltpu.VMEM((1, LANES), F32)],
        compiler_params=_params(("arbitrary", "arbitrary")),
        name="mix_in",
    )(x, *consts)


def _hgrn_body(q_ref, z_ref, v_ref, g_ref, lbl_ref, gain_ref, cum_ref, rep_ref, bd_ref, bmask_ref,
               o_ref, st_ref, vv_ref, *, layer, rows):
    @pl.when(pl.program_id(1) == 0)
    def _():
        st_ref[...] = jnp.zeros_like(st_ref)

    lbl = lbl_ref[...]
    e = jnp.exp(lbl - jnp.max(lbl, axis=0, keepdims=True))
    soft = e / jnp.sum(e, axis=0, keepdims=True)
    lb = jnp.clip(jnp.sum(soft[0:layer + 1, :], axis=0, keepdims=True) - soft[0:1, :], 0.0, 1.0)

    bd = bd_ref[...]
    n_sub = CHUNK // SUB
    sub_t = lax.broadcasted_iota(jnp.int32, (CHUNK, HGRN_W), 0) // SUB
    sub_s = lax.broadcasted_iota(jnp.int32, (CHUNK, HGRN_W), 1) % CHUNK // SUB

    def within_chunk(ch, slot):
        rs = pl.ds(pl.multiple_of(ch * CHUNK, CHUNK), CHUNK)
        q = q_ref[0, rs, :]
        z = z_ref[0, rs, :]
        v = v_ref[0, rs, :]
        lf = (_log_sigmoid(z) + jnp.log(1.0 + lb * jnp.exp(-z))) * LOG2E
        k = (1.0 - lb) * _sigmoid(-z)
        pieces = jnp.concatenate(_split3(lf), axis=0)
        sums = _dot(cum_ref[0:3 * CHUNK, :], pieces)
        vv_ref[slot] = v
        v16 = v.astype(BF16)
        k16 = k.astype(BF16)
        yield

        bc = sums[0:CHUNK]
        bs = sums[CHUNK:2 * CHUNK]
        be = sums[2 * CHUNK:3 * CHUNK]

        q_t = q * jnp.exp2(bc - bs)
        k_t = (k * jnp.exp2(be - bc)).astype(BF16)
        lags = [q_t]
        for lag in range(2, n_sub):
            be_shift = jnp.concatenate([jnp.zeros((lag * SUB, HGRN_W), F32),
                                        be[0:CHUNK - lag * SUB]], axis=0)
            lags.append(q_t * jnp.exp2(bs - be_shift))
        q_lags = jnp.concatenate(lags, axis=0).astype(BF16)
        sc = _dot_nt(q_lags, jnp.concatenate([k_t] * HGRN_HEADS, axis=0) * bd)

        half_rows = CHUNK * SUB // 2
        pair = jnp.concatenate(
            [_dot(cum_ref[3 * CHUNK + n * half_rows:3 * CHUNK + (n + 1) * half_rows, :], pieces)
             for n in range(2)], axis=0)
        k_rep = jnp.concatenate(
            [_dot(rep_ref[n * half_rows:(n + 1) * half_rows, :], k16) for n in range(2)], axis=0)
        yield

        a_lag = jnp.zeros((CHUNK, HGRN_W), F32)
        for n, lag in enumerate(range(1, n_sub)):
            a_lag = jnp.where(sub_t - sub_s == lag, sc[n * CHUNK:(n + 1) * CHUNK], a_lag)
        o_lag = _dot(a_lag.astype(BF16), jnp.concatenate([v16] * HGRN_HEADS, axis=0) * bd)
        a = []
        for i in range(n_sub):
            rows_i = slice(i * SUB * SUB, (i + 1) * SUB * SUB)
            q_rep = jnp.concatenate([q[i * SUB:(i + 1) * SUB]] * SUB, axis=0)
            a.append((q_rep * jnp.exp2(pair[rows_i]) * k_rep[rows_i]).astype(BF16))
        r = jnp.concatenate(
            [_dot(jnp.concatenate(a[n * n_sub // 2:(n + 1) * n_sub // 2], axis=0), bd)
             for n in range(2)], axis=0)
        b_last = bc[CHUNK - 1:CHUNK, :]
        q_hat = (q * jnp.exp2(bc)).astype(BF16)
        k_hat = (k * jnp.exp2(b_last - bc)).astype(BF16)
        yield

        o_sub = []
        for i in range(n_sub):
            acc = None
            for sl in range(SUB):
                s = i * SUB + sl
                term = r[s * SUB:(s + 1) * SUB] * vv_ref[slot, s:s + 1, :]
                acc = term if acc is None else acc + term
            o_sub.append(acc)
        o = o_lag + jnp.concatenate(o_sub, axis=0)
        return rs, o, q_hat, k_hat, v16, jnp.exp2(b_last)

    def finish(rs, o, q_hat, k_hat, v16, decay_last):
        st = st_ref[...]
        o = o + _dot_nt(q_hat, st.astype(BF16))
        st_ref[...] = st * decay_last + bmask_ref[...] * _dot_tn(v16, k_hat)

        oo = o * o
        hi = oo.astype(BF16)
        lo = (oo - hi.astype(F32)).astype(BF16)
        ms = (_dot(hi, bd) + _dot(lo, bd)) * (1.0 / HEAD_DIM)
        gate = g_ref[0, rs, :]
        y = o * lax.rsqrt(ms + EPS) * gain_ref[...] * (gate * _sigmoid(gate))
        o_ref[0, rs, :] = y.astype(BF16)

    def chunk_group(cc, carry):
        gens = [within_chunk(CHUNKS_PER_TRIP * cc + slot, slot) for slot in range(CHUNKS_PER_TRIP)]
        parts = [None] * CHUNKS_PER_TRIP
        while any(part is None for part in parts):
            for slot, gen in enumerate(gens):
                if parts[slot] is None:
                    try:
                        next(gen)
                    except StopIteration as done:
                        parts[slot] = done.value
        for part in parts:
            finish(*part)
        return carry

    lax.fori_loop(0, rows // (CHUNKS_PER_TRIP * CHUNK), chunk_group, 0)


def _hgrn(hh, p, layer):
    b, l, _ = hh.shape
    rows = min(HGRN_ROWS, l)
    consts = [p["lb_logits"], p["hgrn_gain"], p["cum64"], p["rep64"], p["bd"], p["bmask"]]

    def section(k):
        return pl.BlockSpec((1, rows, HGRN_W), lambda bi, i: (bi, i, k))

    return pl.pallas_call(
        functools.partial(_hgrn_body, layer=layer, rows=rows),
        grid=(b, l // rows),
        in_specs=[section(0), section(1), section(2), section(3)]
        + [_const_spec(c.shape) for c in consts],
        out_specs=pl.BlockSpec((1, rows, HGRN_W), lambda bi, i: (bi, i, 0)),
        out_shape=jax.ShapeDtypeStruct((b, l, HGRN_W), BF16),
        scratch_shapes=[pltpu.VMEM((HGRN_W, HGRN_W), F32),
                        pltpu.VMEM((CHUNKS_PER_TRIP, CHUNK, HGRN_W), F32)],
        compiler_params=_params(("arbitrary", "arbitrary")),
        name="hgrn",
    )(hh, hh, hh, hh, *consts)


def _fox_body(qkb_ref, cpre_ref, qa_ref, ka_ref, vat_ref, o_ref, m_ref, acc_ref, s_ref,
              *, tq, tk, n_pre):
    bi, hp, qi = pl.program_id(0), pl.program_id(1), pl.program_id(2)
    per_q = tq // tk
    row = lax.broadcasted_iota(jnp.int32, (tk, tq), 0)
    col = lax.broadcasted_iota(jnp.int32, (tk, tq), 1)

    def c_before(hh, n):
        return cpre_ref[(bi * FOX_HEADS + 2 * hp + hh) * n_pre + n]

    def scores(hh, j):
        return _dot_nt(ka_ref[0, hh, pl.ds(pl.multiple_of(j * tk, tk), tk), :], qa_ref[0, hh])

    def update(hh, j, s, mask):
        if mask is not None:
            s = jnp.where(mask, s, MASK_VALUE)
        m_old = m_ref[hh]
        m_new = jnp.maximum(m_old, jnp.max(s, axis=0, keepdims=True))
        p = jnp.exp2(s - m_new).astype(BF16)
        acc_ref[hh] = acc_ref[hh] * jnp.exp2(m_old - m_new) + _dot(vat_ref[0, hh, j], p)
        m_ref[hh] = m_new

    for hh in range(2):
        m_ref[hh] = jnp.full((1, tq), MASK_VALUE, F32)
        acc_ref[hh] = jnp.zeros((LANES, tq), F32)

    diag = [(hh, dj) for dj in range(per_q) for hh in range(2)]
    s_diag = [scores(hh, qi * per_q + dj) for hh, dj in diag]
    j_first = jnp.maximum(qi * per_q - 1, 0)
    s_first = [scores(hh, j_first) for hh in range(2)]
    for (hh, dj), s in zip(diag, s_diag):
        update(hh, qi * per_q + dj, s, row + dj * tk <= col)

    def needed(j):
        out = None
        for hh in range(2):
            gap = (qkb_ref[0] + c_before(hh, qi * per_q) - c_before(hh, j + 1)
                   - jnp.min(m_ref[hh]))
            out = gap >= SKIP_BELOW if out is None else jnp.logical_or(out, gap >= SKIP_BELOW)
        return out.astype(jnp.int32)

    def cond(carry):
        j, go = carry
        return jnp.logical_and(j >= 0, go > 0)

    def stage(src, dst, j):
        j_next = jnp.maximum(j - 1, 0)
        for hh in range(2):
            s_ref[dst, hh] = scores(hh, j_next)
        for hh in range(2):
            update(hh, j, s_ref[src, hh], None)
        return jnp.where(j >= 1, needed(j_next), 0)

    def body(carry):
        j, _ = carry
        go = stage(0, 1, j)
        go = lax.cond(go > 0, lambda: stage(1, 0, j - 1), lambda: jnp.int32(0))
        return j - 2, go

    for hh in range(2):
        s_ref[0, hh] = s_first[hh]
    lax.while_loop(cond, body, (qi * per_q - 1, needed(j_first)))

    outs = []
    for hh in range(2):
        acc = acc_ref[hh]
        outs.append(acc[0:HEAD_DIM, :] / acc[HEAD_DIM:HEAD_DIM + 1, :])
    o_ref[0] = jnp.concatenate(outs, axis=0).T.astype(BF16)


def _fox(qkb, cpre, qa, ka, vat):
    b, _, l, _ = qa.shape
    tq, tk = min(Q_TILE, l), vat.shape[-1]
    smem = pl.BlockSpec(memory_space=pltpu.SMEM)
    return pl.pallas_call(
        functools.partial(_fox_body, tq=tq, tk=tk, n_pre=l // tk + 1),
        grid=(b, FOX_HEADS // 2, l // tq),
        in_specs=[
            smem, smem,
            pl.BlockSpec((1, 2, tq, LANES), lambda bi, hp, i: (bi, hp, i, 0)),
            pl.BlockSpec((1, 2, l, LANES), lambda bi, hp, i: (bi, hp, 0, 0)),
            pl.BlockSpec((1, 2, l // tk, LANES, tk), lambda bi, hp, i: (bi, hp, 0, 0, 0)),
        ],
        out_specs=pl.BlockSpec((1, tq, LANES), lambda bi, hp, i: (bi, i, hp)),
        out_shape=jax.ShapeDtypeStruct((b, l, FOX_W), BF16),
        scratch_shapes=[pltpu.VMEM((2, 1, tq), F32), pltpu.VMEM((2, LANES, tq), F32),
                        pltpu.VMEM((2, 2, tk, tq), F32)],
        compiler_params=_params(("arbitrary", "arbitrary", "arbitrary")),
        name="fox",
    )(qkb, cpre, qa, ka, vat)


def _pad_lanes(v, width=LANES):
    return jnp.pad(v, (0, width - v.shape[0])).reshape(1, width)


def _placement():
    m = np.zeros((LANES, FOX_HEADS * LANES), np.float32)
    for p in range(N_PIECES):
        for h in range(FOX_HEADS):
            m[p * FOX_HEADS + h, h * LANES + Q_C_LANE + p] = 1.0
            m[p * FOX_HEADS + h, h * LANES + K_ONE_LANE + N_PIECES + p] = -1.0
    return jnp.asarray(m, BF16)


def _tri(n):
    return jnp.asarray(np.tril(np.ones((n, n), np.float32)), BF16)


def _chunk_sums():
    t = np.arange(CHUNK)[:, None]
    r = np.arange(CHUNK)[None, :]
    first = (t // SUB) * SUB
    s_pair, t_pair = _sub_chunk_pairs()
    rows = [r <= t, r < first, r < first + SUB,
            (r > s_pair[:, None]) & (r <= t_pair[:, None])]
    m = np.concatenate(rows, axis=0).astype(np.float32)
    return jnp.asarray(np.tile(m, (1, N_PIECES)), BF16)


def _sub_chunk_pairs():
    s = np.repeat(np.arange(CHUNK), SUB)
    t = (s // SUB) * SUB + np.tile(np.arange(SUB), CHUNK)
    return s, t


def _causal_replicate():
    s, t = _sub_chunk_pairs()
    m = (np.arange(CHUNK)[None, :] == s[:, None]) & (t >= s)[:, None]
    return jnp.asarray(m.astype(np.float32), BF16)


def _layer_operands(l, mix_norm, w_mix_in, conv_w, hgrn_lb_logits, hgrn_out_gain,
                    fox_q_gain, fox_k_gain, fox_f_bias, w_mix_out):
    w = w_mix_in[l]
    o_h = 3 * CONV_CH
    o_q = o_h + 4 * HGRN_W
    o_k, o_v, o_f = o_q + FOX_W, o_q + 2 * FOX_W, o_q + 3 * FOX_W
    head_blocks = np.kron(np.eye(HGRN_HEADS, dtype=np.float32),
                          np.ones((HEAD_DIM, HEAD_DIM), np.float32))
    wo = w_mix_out[l].astype(BF16)
    q_scale = fox_q_gain[l] * (HEAD_DIM ** -0.5 * LOG2E)
    return {
        "mix_norm": mix_norm[l].reshape(1, D_MODEL),
        "wc": w[:, :o_h].astype(BF16),
        "conv_w": jnp.pad(conv_w[l], ((0, 8 - CONV_WIDTH), (0, 0))),
        "wh": w[:, o_h:o_q].astype(BF16),
        "wq": w[:, o_q:o_k].astype(BF16),
        "wk": w[:, o_k:o_v].astype(BF16),
        "wvt": w[:, o_v:o_f].T.astype(BF16),
        "wf": jnp.pad(w[:, o_f:], ((0, 0), (0, LANES - FOX_HEADS))).astype(BF16),
        "fbias": _pad_lanes(fox_f_bias[l]),
        "gq": jnp.tile(q_scale, LANES // HEAD_DIM).reshape(1, LANES),
        "gk": jnp.tile(fox_k_gain[l], LANES // HEAD_DIM).reshape(1, LANES),
        "qkb": (1.01 * HEAD_DIM * jnp.max(jnp.abs(q_scale))
                * jnp.max(jnp.abs(fox_k_gain[l]))).reshape(1),
        "tri": _tri(KV_TILE),
        "place": _placement(),
        "lb_logits": hgrn_lb_logits,
        "hgrn_gain": jnp.tile(hgrn_out_gain[l], HGRN_HEADS).reshape(1, HGRN_W),
        "cum64": _chunk_sums(),
        "rep64": _causal_replicate(),
        "bd": jnp.asarray(head_blocks, BF16),
        "bmask": jnp.asarray(head_blocks),
        "wo_c": wo[:CONV_CH],
        "wo_h": wo[CONV_CH:CONV_CH + HGRN_W],
        "wo_f": wo[CONV_CH + HGRN_W:],
    }


def kernel(x, ffn1_norm, ffn1_w_in, ffn1_w_out, mix_norm, w_mix_in, conv_w, hgrn_lb_logits,
           hgrn_out_gain, fox_q_gain, fox_k_gain, fox_f_bias, w_mix_out, ffn2_norm, ffn2_w_in,
           ffn2_w_out):
    b, l, d = x.shape
    assert d == D_MODEL and l % ROW_TILE == 0 and ROW_TILE == Q_TILE and Q_TILE % KV_TILE == 0
    depth = ffn1_norm.shape[0]
    tm = min(ROW_TILE, l)
    n = b * l
    w1_in, w1_out = ffn1_w_in.astype(BF16), ffn1_w_out.astype(BF16)
    w2_in, w2_out = ffn2_w_in.astype(BF16), ffn2_w_out.astype(BF16)
    for layer in range(depth):
        p = _layer_operands(layer, mix_norm, w_mix_in, conv_w, hgrn_lb_logits, hgrn_out_gain,
                            fox_q_gain, fox_k_gain, fox_f_bias, w_mix_out)
        x2 = _ffn(x.reshape(n, d), ffn1_norm[layer].reshape(1, d), w1_in, w1_out, layer)
        yc, hh, qa, ka, vat, cend = _mix_in(x2.reshape(b, l, d), p)
        yh = _hgrn(hh, p, layer)
        cend = cend[:, :, :tm // KV_TILE, :FOX_HEADS].reshape(b, l // KV_TILE, FOX_HEADS)
        cpre = jnp.pad(cend.transpose(0, 2, 1), ((0, 0), (0, 0), (1, 0))) * LOG2E
        yf = _fox(p["qkb"], cpre.reshape(-1), qa, ka, vat)
        x2 = _mix_out_ffn(x2, yc.reshape(n, CONV_CH), yh.reshape(n, HGRN_W), yf.reshape(n, FOX_W),
                          p["wo_c"], p["wo_h"], p["wo_f"], ffn2_norm[layer].reshape(1, d),
                          w2_in, w2_out, layer)
        x = x2.reshape(b, l, d)
    return x
```

```python
import functools

import jax
import jax.numpy as jnp
import numpy as np
from jax import lax
from jax.experimental import pallas as pl
from jax.experimental.pallas import tpu as pltpu

F32 = jnp.float32
BF16 = jnp.bfloat16

D_MODEL = 1024
D_FF = 2816
HEAD_DIM = 64
CONV_CH = 256
CONV_WIDTH = 3
HGRN_HEADS = 4
HGRN_W = 256
FOX_HEADS = 8
FOX_W = 512
CHUNK = 64
EPS = 1e-6
MASK_VALUE = -1e30

LANES = 128
FF_TILE = 256
N_FF_TILES = D_FF // FF_TILE
ROW_TILE = 512
FFN_ROWS = 1024
Q_TILE = 512
KV_TILE = 256
HGRN_ROWS = 512
SUB = 8
CHUNKS_PER_TRIP = 4
VMEM_LIMIT = 56 * 1024 * 1024

Q_C_LANE = HEAD_DIM
K_ONE_LANE = HEAD_DIM
N_PIECES = 3
LOG2E = 1.4426950408889634
SKIP_BELOW = -138.0


def _dot(a, b):
    return jnp.dot(a, b, preferred_element_type=F32)


def _dot_nt(a, b):
    return lax.dot_general(a, b, (((1,), (1,)), ((), ())), preferred_element_type=F32)


def _dot_tn(a, b):
    return lax.dot_general(a, b, (((0,), (0,)), ((), ())), preferred_element_type=F32)


def _rms_norm(x, gain):
    inv = lax.rsqrt(jnp.mean(x * x, axis=-1, keepdims=True) + EPS)
    return x * inv * gain


def _sigmoid(x):
    return 1.0 / (1.0 + jnp.exp(-x))


def _log_sigmoid(x):
    return jnp.minimum(x, 0.0) - jnp.log(1.0 + jnp.exp(-jnp.abs(x)))


def _split3(x):
    hi = x.astype(BF16)
    r = x - hi.astype(F32)
    mid = r.astype(BF16)
    lo = (r - mid.astype(F32)).astype(BF16)
    return hi, mid, lo


def _const_spec(shape):
    nd = len(shape)
    return pl.BlockSpec(shape, lambda *_: (0,) * nd, pipeline_mode=pl.Buffered(1))


def _layer_spec(stacked_shape, layer):
    nd = len(stacked_shape)
    return pl.BlockSpec((None,) + tuple(stacked_shape[1:]), lambda *_: (layer,) + (0,) * (nd - 1),
                        pipeline_mode=pl.Buffered(1))


def _params(sem):
    return pltpu.CompilerParams(dimension_semantics=sem, vmem_limit_bytes=VMEM_LIMIT)


def _swiglu_half_step(x, g_ref, wi_ref, wo_ref, o_ref, acc_ref):
    xn = _rms_norm(x, g_ref[...]).astype(BF16)
    for c in range(N_FF_TILES):
        gate = _dot(xn, wi_ref[:, c * FF_TILE:(c + 1) * FF_TILE])
        up = _dot(xn, wi_ref[:, D_FF + c * FF_TILE:D_FF + (c + 1) * FF_TILE])
        act = (gate * _sigmoid(gate) * up).astype(BF16)
        part = _dot(act, wo_ref[c * FF_TILE:(c + 1) * FF_TILE, :])
        if c == 0:
            acc_ref[...] = part
        else:
            acc_ref[...] += part
    o_ref[...] = x + 0.5 * acc_ref[...]


def _ffn_body(x_ref, g_ref, wi_ref, wo_ref, o_ref, acc_ref):
    _swiglu_half_step(x_ref[...], g_ref, wi_ref, wo_ref, o_ref, acc_ref)


def _mix_out_ffn_body(x_ref, yc_ref, yh_ref, yf_ref, wc_ref, wh_ref, wf_ref,
                      g_ref, wi_ref, wo_ref, o_ref, acc_ref):
    x = (x_ref[...] + _dot(yc_ref[...], wc_ref[...]) + _dot(yh_ref[...], wh_ref[...])
         + _dot(yf_ref[...], wf_ref[...]))
    _swiglu_half_step(x, g_ref, wi_ref, wo_ref, o_ref, acc_ref)


def _mix_out_ffn(x2d, yc, yh, yf, wc, wh, wf, gain, wi, wo, layer):
    n = x2d.shape[0]
    tm = min(FFN_ROWS, n)

    def row(w):
        return pl.BlockSpec((tm, w), lambda i: (i, 0))

    consts = [wc, wh, wf, gain]
    return pl.pallas_call(
        _mix_out_ffn_body,
        grid=(n // tm,),
        in_specs=[row(D_MODEL), row(CONV_CH), row(HGRN_W), row(FOX_W)]
        + [_const_spec(c.shape) for c in consts]
        + [_layer_spec(wi.shape, layer), _layer_spec(wo.shape, layer)],
        out_specs=row(D_MODEL),
        out_shape=jax.ShapeDtypeStruct((n, D_MODEL), F32),
        scratch_shapes=[pltpu.VMEM((tm, D_MODEL), F32)],
        compiler_params=_params(("arbitrary",)),
        name="mix_out_ffn",
    )(x2d, yc, yh, yf, *consts, wi, wo)


def _ffn(x2d, gain, wi, wo, layer):
    n = x2d.shape[0]
    tm = min(FFN_ROWS, n)
    row = pl.BlockSpec((tm, D_MODEL), lambda i: (i, 0))
    return pl.pallas_call(
        _ffn_body,
        grid=(n // tm,),
        in_specs=[row, _const_spec(gain.shape), _layer_spec(wi.shape, layer),
                  _layer_spec(wo.shape, layer)],
        out_specs=row,
        out_shape=jax.ShapeDtypeStruct((n, D_MODEL), F32),
        scratch_shapes=[pltpu.VMEM((tm, D_MODEL), F32)],
        compiler_params=_params(("arbitrary",)),
        name="ffn",
    )(x2d, gain, wi, wo)


def _mix_in_body(x_ref, g_ref, wc_ref, cw_ref, wh_ref, wq_ref, wk_ref, wvt_ref, wf_ref,
                 fb_ref, gq_ref, gk_ref, tri_ref, place_ref,
                 yc_ref, hh_ref, qa_ref, ka_ref, vat_ref, cend_ref, ubuf_ref, carry_ref, *, tm):
    @pl.when(pl.program_id(1) == 0)
    def _():
        ubuf_ref[0:8, :] = jnp.zeros((8, CONV_CH), F32)
        carry_ref[...] = jnp.zeros_like(carry_ref)

    xn = _rms_norm(x_ref[0], g_ref[...]).astype(BF16)

    lf = _log_sigmoid(_dot(xn, wf_ref[...]) + fb_ref[...])
    lf3 = jnp.concatenate(_split3(lf), axis=1)
    last = carry_ref[...]
    blocks, ends = [], []
    for n in range(tm // KV_TILE):
        part = _dot(tri_ref[...], lf3[n * KV_TILE:(n + 1) * KV_TILE])
        blocks.append(last + (part[:, 0:LANES] + part[:, LANES:2 * LANES] + part[:, 2 * LANES:]))
        last = blocks[-1][KV_TILE - 1:KV_TILE, :]
        ends.append(last)
    c = jnp.concatenate(blocks, axis=0)
    carry_ref[...] = last
    cend_ref[0, 0] = jnp.concatenate(ends + [jnp.zeros((8 - len(ends), LANES), F32)], axis=0)

    lane = lax.broadcasted_iota(jnp.int32, (1, LANES), 1)
    c_hi, c_mid, c_lo = [piece.astype(F32) for piece in _split3(c * LOG2E)]
    packed = jnp.where(lane < FOX_HEADS, c_hi,
                       jnp.where(lane < 2 * FOX_HEADS, pltpu.roll(c_mid, FOX_HEADS, axis=1),
                                 pltpu.roll(c_lo, 2 * FOX_HEADS, axis=1)))
    placed = _dot(packed.astype(BF16), place_ref[...])

    low = lane < HEAD_DIM
    q_takes_c = jnp.logical_and(lane >= Q_C_LANE, lane < Q_C_LANE + N_PIECES)
    k_takes_c = jnp.logical_and(lane >= K_ONE_LANE + N_PIECES, lane < K_ONE_LANE + 2 * N_PIECES)
    one_q = k_takes_c.astype(F32)
    one_k = q_takes_c.astype(F32)

    def head_pair_norm(x2, gain2):
        sq = x2 * x2
        ss_lo = jnp.sum(jnp.where(low, sq, 0.0), axis=-1, keepdims=True)
        ss_hi = jnp.sum(jnp.where(low, 0.0, sq), axis=-1, keepdims=True)
        inv = lax.rsqrt(jnp.where(low, ss_lo, ss_hi) * (1.0 / HEAD_DIM) + EPS)
        return x2 * inv * gain2

    hq = _dot(xn, wq_ref[...])
    hk = _dot(xn, wk_ref[...])
    for pair in range(FOX_HEADS // 2):
        cols = slice(pair * LANES, (pair + 1) * LANES)
        qn = head_pair_norm(hq[:, cols], gq_ref[...])
        kn = head_pair_norm(hk[:, cols], gk_ref[...])
        for half in range(2):
            h = 2 * pair + half
            extra = placed[:, h * LANES:(h + 1) * LANES]
            q_h = qn if half == 0 else pltpu.roll(qn, HEAD_DIM, axis=1)
            k_h = kn if half == 0 else pltpu.roll(kn, HEAD_DIM, axis=1)
            qa_ref[0, h] = jnp.where(low, q_h, jnp.where(q_takes_c, extra, one_q)).astype(BF16)
            ka_ref[0, h] = jnp.where(low, k_h, jnp.where(k_takes_c, extra, one_k)).astype(BF16)

    vt = _dot_nt(wvt_ref[...], xn).astype(BF16)
    tail = (lax.broadcasted_iota(jnp.int32, (LANES - HEAD_DIM, KV_TILE), 0) == 0).astype(BF16)
    for h in range(FOX_HEADS):
        for n in range(tm // KV_TILE):
            vat_ref[0, h, n] = jnp.concatenate(
                [vt[h * HEAD_DIM:(h + 1) * HEAD_DIM, n * KV_TILE:(n + 1) * KV_TILE], tail], axis=0)

    hh_ref[0] = _dot(xn, wh_ref[...])

    hc = _dot(xn, wc_ref[...])
    u = hc[:, 2 * CONV_CH:3 * CONV_CH] * hc[:, 0:CONV_CH]
    ubuf_ref[8:8 + tm, :] = u
    u1 = ubuf_ref[7:7 + tm, :]
    u2 = ubuf_ref[6:6 + tm, :]
    cw = cw_ref[...]
    conv = cw[0:1, :] * u2 + cw[1:2, :] * u1 + cw[2:3, :] * u
    yc_ref[0] = (hc[:, CONV_CH:2 * CONV_CH] * conv).astype(BF16)
    ubuf_ref[0:8, :] = u[tm - 8:tm, :]


def _mix_in(x, p):
    b, l, _ = x.shape
    tm = min(ROW_TILE, l)
    nt = l // tm
    consts = [p["mix_norm"], p["wc"], p["conv_w"], p["wh"], p["wq"], p["wk"], p["wvt"], p["wf"],
              p["fbias"], p["gq"], p["gk"], p["tri"], p["place"]]
    out_shape = [
        jax.ShapeDtypeStruct((b, l, CONV_CH), BF16),
        jax.ShapeDtypeStruct((b, l, 4 * HGRN_W), F32),
        jax.ShapeDtypeStruct((b, FOX_HEADS, l, LANES), BF16),
        jax.ShapeDtypeStruct((b, FOX_HEADS, l, LANES), BF16),
        jax.ShapeDtypeStruct((b, FOX_HEADS, l // KV_TILE, LANES, KV_TILE), BF16),
        jax.ShapeDtypeStruct((b, nt, 8, LANES), F32),
    ]
    out_specs = [
        pl.BlockSpec((1, tm, CONV_CH), lambda bi, i: (bi, i, 0)),
        pl.BlockSpec((1, tm, 4 * HGRN_W), lambda bi, i: (bi, i, 0)),
        pl.BlockSpec((1, FOX_HEADS, tm, LANES), lambda bi, i: (bi, 0, i, 0)),
        pl.BlockSpec((1, FOX_HEADS, tm, LANES), lambda bi, i: (bi, 0, i, 0)),
        pl.BlockSpec((1, FOX_HEADS, tm // KV_TILE, LANES, KV_TILE), lambda bi, i: (bi, 0, i, 0, 0)),
        pl.BlockSpec((1, 1, 8, LANES), lambda bi, i: (bi, i, 0, 0)),
    ]
    return pl.pallas_call(
        functools.partial(_mix_in_body, tm=tm),
        grid=(b, nt),
        in_specs=[pl.BlockSpec((1, tm, D_MODEL), lambda bi, i: (bi, i, 0))]
        + [_const_spec(c.shape) for c in consts],
        out_specs=out_specs,
        out_shape=out_shape,
        scratch_shapes=[pltpu.VMEM((tm + 8, CONV_CH), F32), pltpu.VMEM((1, LANES), F32)],
        compiler_params=_params(("arbitrary", "arbitrary")),
        name="mix_in",
    )(x, *consts)


def _hgrn_body(q_ref, z_ref, v_ref, g_ref, lbl_ref, gain_ref, cum_ref, rep_ref, bd_ref, bmask_ref,
               o_ref, st_ref, vv_ref, *, layer, rows):
    @pl.when(pl.program_id(1) == 0)
    def _():
        st_ref[...] = jnp.zeros_like(st_ref)

    lbl = lbl_ref[...]
    e = jnp.exp(lbl - jnp.max(lbl, axis=0, keepdims=True))
    soft = e / jnp.sum(e, axis=0, keepdims=True)
    lb = jnp.clip(jnp.sum(soft[0:layer + 1, :], axis=0, keepdims=True) - soft[0:1, :], 0.0, 1.0)

    bd = bd_ref[...]
    n_sub = CHUNK // SUB
    sub_t = lax.broadcasted_iota(jnp.int32, (CHUNK, HGRN_W), 0) // SUB
    sub_s = lax.broadcasted_iota(jnp.int32, (CHUNK, HGRN_W), 1) % CHUNK // SUB

    def within_chunk(ch, slot):
        rs = pl.ds(pl.multiple_of(ch * CHUNK, CHUNK), CHUNK)
        q = q_ref[0, rs, :]
        z = z_ref[0, rs, :]
        v = v_ref[0, rs, :]
        lf = (_log_sigmoid(z) + jnp.log(1.0 + lb * jnp.exp(-z))) * LOG2E
        k = (1.0 - lb) * _sigmoid(-z)
        pieces = jnp.concatenate(_split3(lf), axis=0)
        sums = _dot(cum_ref[0:3 * CHUNK, :], pieces)
        vv_ref[slot] = v
        v16 = v.astype(BF16)
        k16 = k.astype(BF16)
        yield

        bc = sums[0:CHUNK]
        bs = sums[CHUNK:2 * CHUNK]
        be = sums[2 * CHUNK:3 * CHUNK]

        q_t = q * jnp.exp2(bc - bs)
        k_t = (k * jnp.exp2(be - bc)).astype(BF16)
        lags = [q_t]
        for lag in range(2, n_sub):
            be_shift = jnp.concatenate([jnp.zeros((lag * SUB, HGRN_W), F32),
                                        be[0:CHUNK - lag * SUB]], axis=0)
            lags.append(q_t * jnp.exp2(bs - be_shift))
        q_lags = jnp.concatenate(lags, axis=0).astype(BF16)
        sc = _dot_nt(q_lags, jnp.concatenate([k_t] * HGRN_HEADS, axis=0) * bd)

        half_rows = CHUNK * SUB // 2
        pair = jnp.concatenate(
            [_dot(cum_ref[3 * CHUNK + n * half_rows:3 * CHUNK + (n + 1) * half_rows, :], pieces)
             for n in range(2)], axis=0)
        k_rep = jnp.concatenate(
            [_dot(rep_ref[n * half_rows:(n + 1) * half_rows, :], k16) for n in range(2)], axis=0)
        yield

        a_lag = jnp.zeros((CHUNK, HGRN_W), F32)
        for n, lag in enumerate(range(1, n_sub)):
            a_lag = jnp.where(sub_t - sub_s == lag, sc[n * CHUNK:(n + 1) * CHUNK], a_lag)
        o_lag = _dot(a_lag.astype(BF16), jnp.concatenate([v16] * HGRN_HEADS, axis=0) * bd)
        a = []
        for i in range(n_sub):
            rows_i = slice(i * SUB * SUB, (i + 1) * SUB * SUB)
            q_rep = jnp.concatenate([q[i * SUB:(i + 1) * SUB]] * SUB, axis=0)
            a.append((q_rep * jnp.exp2(pair[rows_i]) * k_rep[rows_i]).astype(BF16))
        r = jnp.concatenate(
            [_dot(jnp.concatenate(a[n * n_sub // 2:(n + 1) * n_sub // 2], axis=0), bd)
             for n in range(2)], axis=0)
        b_last = bc[CHUNK - 1:CHUNK, :]
        q_hat = (q * jnp.exp2(bc)).astype(BF16)
        k_hat = (k * jnp.exp2(b_last - bc)).astype(BF16)
        yield

        o_sub = []
        for i in range(n_sub):
            acc = None
            for sl in range(SUB):
                s = i * SUB + sl
                term = r[s * SUB:(s + 1) * SUB] * vv_ref[slot, s:s + 1, :]
                acc = term if acc is None else acc + term
            o_sub.append(acc)
        o = o_lag + jnp.concatenate(o_sub, axis=0)
        return rs, o, q_hat, k_hat, v16, jnp.exp2(b_last)

    def finish(rs, o, q_hat, k_hat, v16, decay_last):
        st = st_ref[...]
        o = o + _dot_nt(q_hat, st.astype(BF16))
        st_ref[...] = st * decay_last + bmask_ref[...] * _dot_tn(v16, k_hat)

        oo = o * o
        hi = oo.astype(BF16)
        lo = (oo - hi.astype(F32)).astype(BF16)
        ms = (_dot(hi, bd) + _dot(lo, bd)) * (1.0 / HEAD_DIM)
        gate = g_ref[0, rs, :]
        y = o * lax.rsqrt(ms + EPS) * gain_ref[...] * (gate * _sigmoid(gate))
        o_ref[0, rs, :] = y.astype(BF16)

    def chunk_group(cc, carry):
        gens = [within_chunk(CHUNKS_PER_TRIP * cc + slot, slot) for slot in range(CHUNKS_PER_TRIP)]
        parts = [None] * CHUNKS_PER_TRIP
        while any(part is None for part in parts):
            for slot, gen in enumerate(gens):
                if parts[slot] is None:
                    try:
                        next(gen)
                    except StopIteration as done:
                        parts[slot] = done.value
        for part in parts:
            finish(*part)
        return carry

    lax.fori_loop(0, rows // (CHUNKS_PER_TRIP * CHUNK), chunk_group, 0)


def _hgrn(hh, p, layer):
    b, l, _ = hh.shape
    rows = min(HGRN_ROWS, l)
    consts = [p["lb_logits"], p["hgrn_gain"], p["cum64"], p["rep64"], p["bd"], p["bmask"]]

    def section(k):
        return pl.BlockSpec((1, rows, HGRN_W), lambda bi, i: (bi, i, k))

    return pl.pallas_call(
        functools.partial(_hgrn_body, layer=layer, rows=rows),
        grid=(b, l // rows),
        in_specs=[section(0), section(1), section(2), section(3)]
        + [_const_spec(c.shape) for c in consts],
        out_specs=pl.BlockSpec((1, rows, HGRN_W), lambda bi, i: (bi, i, 0)),
        out_shape=jax.ShapeDtypeStruct((b, l, HGRN_W), BF16),
        scratch_shapes=[pltpu.VMEM((HGRN_W, HGRN_W), F32),
                        pltpu.VMEM((CHUNKS_PER_TRIP, CHUNK, HGRN_W), F32)],
        compiler_params=_params(("arbitrary", "arbitrary")),
        name="hgrn",
    )(hh, hh, hh, hh, *consts)


def _fox_body(qkb_ref, cpre_ref, qa_ref, ka_ref, vat_ref, o_ref, m_ref, acc_ref, s_ref,
              *, tq, tk, n_pre):
    bi, hp, qi = pl.program_id(0), pl.program_id(1), pl.program_id(2)
    per_q = tq // tk
    row = lax.broadcasted_iota(jnp.int32, (tk, tq), 0)
    col = lax.broadcasted_iota(jnp.int32, (tk, tq), 1)

    def c_before(hh, n):
        return cpre_ref[(bi * FOX_HEADS + 2 * hp + hh) * n_pre + n]

    def scores(hh, j):
        return _dot_nt(ka_ref[0, hh, pl.ds(pl.multiple_of(j * tk, tk), tk), :], qa_ref[0, hh])

    def update(hh, j, s, mask=None, off=None):
        if mask is not None:
            s = jnp.where(mask, s, MASK_VALUE)
        m_old = m_ref[hh]
        tile_max = jnp.max(s, axis=0, keepdims=True)
        m_new = jnp.maximum(m_old, tile_max if off is None else tile_max + off)
        p = jnp.exp2(s - (m_new if off is None else m_new - off)).astype(BF16)
        acc_ref[hh] = acc_ref[hh] * jnp.exp2(m_old - m_new) + _dot(vat_ref[0, hh, j], p)
        m_ref[hh] = m_new

    for hh in range(2):
        m_ref[hh] = jnp.full((1, tq), MASK_VALUE, F32)
        acc_ref[hh] = jnp.zeros((LANES, tq), F32)

    diag = [(hh, dj) for dj in range(per_q) for hh in range(2)]
    s_diag = [scores(hh, qi * per_q + dj) for hh, dj in diag]
    near = [(hh, jnp.maximum(qi * per_q - 1 - e, 0)) for e in range(per_q) for hh in range(2)]
    s_near = [scores(hh, j) for hh, j in near]
    j_start = (qi - 1) * per_q - 1
    j_first = jnp.maximum(j_start, 0)
    s_first = [scores(hh, j_first) for hh in range(2)]
    for (hh, dj), s in zip(diag, s_diag):
        update(hh, qi * per_q + dj, s, mask=row + dj * tk <= col)
    off = jnp.where(qi > 0, 0.0, MASK_VALUE)
    for (hh, j), s in zip(near, s_near):
        update(hh, j, s, off=off)

    def needed(j):
        out = None
        for hh in range(2):
            gap = (qkb_ref[0] + c_before(hh, qi * per_q) - c_before(hh, j + 1)
                   - jnp.min(m_ref[hh]))
            out = gap >= SKIP_BELOW if out is None else jnp.logical_or(out, gap >= SKIP_BELOW)
        return out.astype(jnp.int32)

    def cond(carry):
        j, go = carry
        return jnp.logical_and(j >= 0, go > 0)

    def stage(src, dst, j):
        j_next = jnp.maximum(j - 1, 0)
        for hh in range(2):
            s_ref[dst, hh] = scores(hh, j_next)
        for hh in range(2):
            update(hh, j, s_ref[src, hh])
        return jnp.where(j >= 1, needed(j_next), 0)

    def body(carry):
        j, _ = carry
        go = stage(0, 1, j)
        go = lax.cond(go > 0, lambda: stage(1, 0, j - 1), lambda: jnp.int32(0))
        return j - 2, go

    for hh in range(2):
        s_ref[0, hh] = s_first[hh]
    lax.while_loop(cond, body, (j_start, needed(j_first)))

    outs = []
    for hh in range(2):
        acc = acc_ref[hh]
        outs.append(acc[0:HEAD_DIM, :] / acc[HEAD_DIM:HEAD_DIM + 1, :])
    o_ref[0] = jnp.concatenate(outs, axis=0).T.astype(BF16)


def _fox(qkb, cpre, qa, ka, vat):
    b, _, l, _ = qa.shape
    tq, tk = min(Q_TILE, l), vat.shape[-1]
    smem = pl.BlockSpec(memory_space=pltpu.SMEM)
    return pl.pallas_call(
        functools.partial(_fox_body, tq=tq, tk=tk, n_pre=l // tk + 1),
        grid=(b, FOX_HEADS // 2, l // tq),
        in_specs=[
            smem, smem,
            pl.BlockSpec((1, 2, tq, LANES), lambda bi, hp, i: (bi, hp, i, 0)),
            pl.BlockSpec((1, 2, l, LANES), lambda bi, hp, i: (bi, hp, 0, 0)),
            pl.BlockSpec((1, 2, l // tk, LANES, tk), lambda bi, hp, i: (bi, hp, 0, 0, 0)),
        ],
        out_specs=pl.BlockSpec((1, tq, LANES), lambda bi, hp, i: (bi, i, hp)),
        out_shape=jax.ShapeDtypeStruct((b, l, FOX_W), BF16),
        scratch_shapes=[pltpu.VMEM((2, 1, tq), F32), pltpu.VMEM((2, LANES, tq), F32),
                        pltpu.VMEM((2, 2, tk, tq), F32)],
        compiler_params=_params(("arbitrary", "arbitrary", "arbitrary")),
        name="fox",
    )(qkb, cpre, qa, ka, vat)


def _pad_lanes(v, width=LANES):
    return jnp.pad(v, (0, width - v.shape[0])).reshape(1, width)


def _placement():
    m = np.zeros((LANES, FOX_HEADS * LANES), np.float32)
    for p in range(N_PIECES):
        for h in range(FOX_HEADS):
            m[p * FOX_HEADS + h, h * LANES + Q_C_LANE + p] = 1.0
            m[p * FOX_HEADS + h, h * LANES + K_ONE_LANE + N_PIECES + p] = -1.0
    return jnp.asarray(m, BF16)


def _tri(n):
    return jnp.asarray(np.tril(np.ones((n, n), np.float32)), BF16)


def _chunk_sums():
    t = np.arange(CHUNK)[:, None]
    r = np.arange(CHUNK)[None, :]
    first = (t // SUB) * SUB
    s_pair, t_pair = _sub_chunk_pairs()
    rows = [r <= t, r < first, r < first + SUB,
            (r > s_pair[:, None]) & (r <= t_pair[:, None])]
    m = np.concatenate(rows, axis=0).astype(np.float32)
    return jnp.asarray(np.tile(m, (1, N_PIECES)), BF16)


def _sub_chunk_pairs():
    s = np.repeat(np.arange(CHUNK), SUB)
    t = (s // SUB) * SUB + np.tile(np.arange(SUB), CHUNK)
    return s, t


def _causal_replicate():
    s, t = _sub_chunk_pairs()
    m = (np.arange(CHUNK)[None, :] == s[:, None]) & (t >= s)[:, None]
    return jnp.asarray(m.astype(np.float32), BF16)


def _layer_operands(l, mix_norm, w_mix_in, conv_w, hgrn_lb_logits, hgrn_out_gain,
                    fox_q_gain, fox_k_gain, fox_f_bias, w_mix_out):
    w = w_mix_in[l]
    o_h = 3 * CONV_CH
    o_q = o_h + 4 * HGRN_W
    o_k, o_v, o_f = o_q + FOX_W, o_q + 2 * FOX_W, o_q + 3 * FOX_W
    head_blocks = np.kron(np.eye(HGRN_HEADS, dtype=np.float32),
                          np.ones((HEAD_DIM, HEAD_DIM), np.float32))
    wo = w_mix_out[l].astype(BF16)
    q_scale = fox_q_gain[l] * (HEAD_DIM ** -0.5 * LOG2E)
    return {
        "mix_norm": mix_norm[l].reshape(1, D_MODEL),
        "wc": w[:, :o_h].astype(BF16),
        "conv_w": jnp.pad(conv_w[l], ((0, 8 - CONV_WIDTH), (0, 0))),
        "wh": w[:, o_h:o_q].astype(BF16),
        "wq": w[:, o_q:o_k].astype(BF16),
        "wk": w[:, o_k:o_v].astype(BF16),
        "wvt": w[:, o_v:o_f].T.astype(BF16),
        "wf": jnp.pad(w[:, o_f:], ((0, 0), (0, LANES - FOX_HEADS))).astype(BF16),
        "fbias": _pad_lanes(fox_f_bias[l]),
        "gq": jnp.tile(q_scale, LANES // HEAD_DIM).reshape(1, LANES),
        "gk": jnp.tile(fox_k_gain[l], LANES // HEAD_DIM).reshape(1, LANES),
        "qkb": (1.01 * HEAD_DIM * jnp.max(jnp.abs(q_scale))
                * jnp.max(jnp.abs(fox_k_gain[l]))).reshape(1),
        "tri": _tri(KV_TILE),
        "place": _placement(),
        "lb_logits": hgrn_lb_logits,
        "hgrn_gain": jnp.tile(hgrn_out_gain[l], HGRN_HEADS).reshape(1, HGRN_W),
        "cum64": _chunk_sums(),
        "rep64": _causal_replicate(),
        "bd": jnp.asarray(head_blocks, BF16),
        "bmask": jnp.asarray(head_blocks),
        "wo_c": wo[:CONV_CH],
        "wo_h": wo[CONV_CH:CONV_CH + HGRN_W],
        "wo_f": wo[CONV_CH + HGRN_W:],
    }


def kernel(x, ffn1_norm, ffn1_w_in, ffn1_w_out, mix_norm, w_mix_in, conv_w, hgrn_lb_logits,
           hgrn_out_gain, fox_q_gain, fox_k_gain, fox_f_bias, w_mix_out, ffn2_norm, ffn2_w_in,
           ffn2_w_out):
    b, l, d = x.shape
    assert d == D_MODEL and l % ROW_TILE == 0 and ROW_TILE == Q_TILE and Q_TILE % KV_TILE == 0
    depth = ffn1_norm.shape[0]
    tm = min(ROW_TILE, l)
    n = b * l
    w1_in, w1_out = ffn1_w_in.astype(BF16), ffn1_w_out.astype(BF16)
    w2_in, w2_out = ffn2_w_in.astype(BF16), ffn2_w_out.astype(BF16)
    for layer in range(depth):
        p = _layer_operands(layer, mix_norm, w_mix_in, conv_w, hgrn_lb_logits, hgrn_out_gain,
                            fox_q_gain, fox_k_gain, fox_f_bias, w_mix_out)
        x2 = _ffn(x.reshape(n, d), ffn1_norm[layer].reshape(1, d), w1_in, w1_out, layer)
        yc, hh, qa, ka, vat, cend = _mix_in(x2.reshape(b, l, d), p)
        yh = _hgrn(hh, p, layer)
        cend = cend[:, :, :tm // KV_TILE, :FOX_HEADS].reshape(b, l // KV_TILE, FOX_HEADS)
        cpre = jnp.pad(cend.transpose(0, 2, 1), ((0, 0), (0, 0), (1, 0))) * LOG2E
        yf = _fox(p["qkb"], cpre.reshape(-1), qa, ka, vat)
        x2 = _mix_out_ffn(x2, yc.reshape(n, CONV_CH), yh.reshape(n, HGRN_W), yf.reshape(n, FOX_W),
                          p["wo_c"], p["wo_h"], p["wo_f"], ffn2_norm[layer].reshape(1, d),
                          w2_in, w2_out, layer)
        x = x2.reshape(b, l, d)
    return x
```

```python
import functools

import jax
import jax.numpy as jnp
import numpy as np
from jax import lax
from jax.experimental import pallas as pl
from jax.experimental.pallas import tpu as pltpu

F32 = jnp.float32
BF16 = jnp.bfloat16

D_MODEL = 1024
D_FF = 2816
HEAD_DIM = 64
CONV_CH = 256
CONV_WIDTH = 3
HGRN_HEADS = 4
HGRN_W = 256
FOX_HEADS = 8
FOX_W = 512
CHUNK = 64
EPS = 1e-6
MASK_VALUE = -1e30

LANES = 128
MXU_TILE = 256
FF_TILE = MXU_TILE
N_FF_TILES = D_FF // FF_TILE
ROW_TILE = 512
FFN_ROWS = 1024
Q_TILE = 512
KV_TILE = 256
HGRN_ROWS = 512
SUB = 8
CHUNKS_PER_TRIP = 4
VMEM_LIMIT = 56 * 1024 * 1024

Q_C_LANE = HEAD_DIM
K_ONE_LANE = HEAD_DIM
N_PIECES = 3
LOG2E = 1.4426950408889634
SKIP_BELOW = -138.0


def _dot(a, b):
    return jnp.dot(a, b, preferred_element_type=F32)


def _dot_nt(a, b):
    return lax.dot_general(a, b, (((1,), (1,)), ((), ())), preferred_element_type=F32)


def _dot_tn(a, b):
    return lax.dot_general(a, b, (((0,), (0,)), ((), ())), preferred_element_type=F32)


def _rms_norm(x, gain):
    inv = lax.rsqrt(jnp.mean(x * x, axis=-1, keepdims=True) + EPS)
    return x * inv * gain


def _sigmoid(x):
    return 1.0 / (1.0 + jnp.exp(-x))


def _log_sigmoid(x):
    return jnp.minimum(x, 0.0) - jnp.log(1.0 + jnp.exp(-jnp.abs(x)))


def _split3(x):
    hi = x.astype(BF16)
    r = x - hi.astype(F32)
    mid = r.astype(BF16)
    lo = (r - mid.astype(F32)).astype(BF16)
    return hi, mid, lo


def _const_spec(shape):
    nd = len(shape)
    return pl.BlockSpec(shape, lambda *_: (0,) * nd, pipeline_mode=pl.Buffered(1))


def _layer_spec(stacked_shape, layer):
    nd = len(stacked_shape)
    return pl.BlockSpec((None,) + tuple(stacked_shape[1:]), lambda *_: (layer,) + (0,) * (nd - 1),
                        pipeline_mode=pl.Buffered(1))


def _params(sem):
    return pltpu.CompilerParams(dimension_semantics=sem, vmem_limit_bytes=VMEM_LIMIT)


def _swiglu_half_step(x, g_ref, wi_ref, wo_ref, o_ref, acc_ref):
    xn = _rms_norm(x, g_ref[...]).astype(BF16)
    for c in range(N_FF_TILES):
        gate = _dot(xn, wi_ref[:, c * FF_TILE:(c + 1) * FF_TILE])
        up = _dot(xn, wi_ref[:, D_FF + c * FF_TILE:D_FF + (c + 1) * FF_TILE])
        act = (gate * _sigmoid(gate) * up).astype(BF16)
        part = _dot(act, wo_ref[c * FF_TILE:(c + 1) * FF_TILE, :])
        if c == 0:
            acc_ref[...] = part
        else:
            acc_ref[...] += part
    o_ref[...] = x + 0.5 * acc_ref[...]


def _ffn_body(x_ref, g_ref, wi_ref, wo_ref, o_ref, acc_ref):
    _swiglu_half_step(x_ref[...], g_ref, wi_ref, wo_ref, o_ref, acc_ref)


def _mix_out_ffn_body(x_ref, yc_ref, yh_ref, yf_ref, wc_ref, wh_ref, wf_ref,
                      g_ref, wi_ref, wo_ref, o_ref, acc_ref):
    x = (x_ref[...] + _dot(yc_ref[...], wc_ref[...]) + _dot(yh_ref[...], wh_ref[...])
         + _dot(yf_ref[...], wf_ref[...]))
    _swiglu_half_step(x, g_ref, wi_ref, wo_ref, o_ref, acc_ref)


def _mix_out_ffn(x2d, yc, yh, yf, wc, wh, wf, gain, wi, wo, layer):
    n = x2d.shape[0]
    tm = min(FFN_ROWS, n)

    def row(w):
        return pl.BlockSpec((tm, w), lambda i: (i, 0))

    consts = [wc, wh, wf, gain]
    return pl.pallas_call(
        _mix_out_ffn_body,
        grid=(n // tm,),
        in_specs=[row(D_MODEL), row(CONV_CH), row(HGRN_W), row(FOX_W)]
        + [_const_spec(c.shape) for c in consts]
        + [_layer_spec(wi.shape, layer), _layer_spec(wo.shape, layer)],
        out_specs=row(D_MODEL),
        out_shape=jax.ShapeDtypeStruct((n, D_MODEL), F32),
        scratch_shapes=[pltpu.VMEM((tm, D_MODEL), F32)],
        compiler_params=_params(("arbitrary",)),
        name="mix_out_ffn",
    )(x2d, yc, yh, yf, *consts, wi, wo)


def _ffn(x2d, gain, wi, wo, layer):
    n = x2d.shape[0]
    tm = min(FFN_ROWS, n)
    row = pl.BlockSpec((tm, D_MODEL), lambda i: (i, 0))
    return pl.pallas_call(
        _ffn_body,
        grid=(n // tm,),
        in_specs=[row, _const_spec(gain.shape), _layer_spec(wi.shape, layer),
                  _layer_spec(wo.shape, layer)],
        out_specs=row,
        out_shape=jax.ShapeDtypeStruct((n, D_MODEL), F32),
        scratch_shapes=[pltpu.VMEM((tm, D_MODEL), F32)],
        compiler_params=_params(("arbitrary",)),
        name="ffn",
    )(x2d, gain, wi, wo)


def _mix_in_body(x_ref, g_ref, wc_ref, cw_ref, wh_ref, wq_ref, wk_ref, wvt_ref, wf_ref,
                 fb_ref, gq_ref, gk_ref, tri_ref, place_ref,
                 yc_ref, hh_ref, qa_ref, ka_ref, vat_ref, cend_ref, ubuf_ref, carry_ref, *, tm):
    @pl.when(pl.program_id(1) == 0)
    def _():
        ubuf_ref[0:8, :] = jnp.zeros((8, CONV_CH), F32)
        carry_ref[...] = jnp.zeros_like(carry_ref)

    xn = _rms_norm(x_ref[0], g_ref[...]).astype(BF16)

    lf = _log_sigmoid(_dot(xn, wf_ref[...]) + fb_ref[...])
    hq = _dot(xn, wq_ref[...])
    hk = _dot(xn, wk_ref[...])
    lf3 = jnp.concatenate(_split3(lf), axis=1)
    last = carry_ref[...]
    blocks, ends = [], []
    for n in range(tm // KV_TILE):
        part = _dot(tri_ref[...], lf3[n * KV_TILE:(n + 1) * KV_TILE])
        blocks.append(last + (part[:, 0:LANES] + part[:, LANES:2 * LANES] + part[:, 2 * LANES:]))
        last = blocks[-1][KV_TILE - 1:KV_TILE, :]
        ends.append(last)
    c = jnp.concatenate(blocks, axis=0)
    carry_ref[...] = last
    cend_ref[0, 0] = jnp.concatenate(ends + [jnp.zeros((8 - len(ends), LANES), F32)], axis=0)
    vt = _dot_nt(wvt_ref[...], xn).astype(BF16)

    lane = lax.broadcasted_iota(jnp.int32, (1, LANES), 1)
    c_hi, c_mid, c_lo = [piece.astype(F32) for piece in _split3(c * LOG2E)]
    packed = jnp.where(lane < FOX_HEADS, c_hi,
                       jnp.where(lane < 2 * FOX_HEADS, pltpu.roll(c_mid, FOX_HEADS, axis=1),
                                 pltpu.roll(c_lo, 2 * FOX_HEADS, axis=1)))
    placed = _dot(packed.astype(BF16), place_ref[...])

    low = lane < HEAD_DIM
    q_takes_c = jnp.logical_and(lane >= Q_C_LANE, lane < Q_C_LANE + N_PIECES)
    k_takes_c = jnp.logical_and(lane >= K_ONE_LANE + N_PIECES, lane < K_ONE_LANE + 2 * N_PIECES)
    one_q = k_takes_c.astype(F32)
    one_k = q_takes_c.astype(F32)

    def head_pair_norm(x2, gain2):
        sq = x2 * x2
        ss_lo = jnp.sum(jnp.where(low, sq, 0.0), axis=-1, keepdims=True)
        ss_hi = jnp.sum(jnp.where(low, 0.0, sq), axis=-1, keepdims=True)
        inv = lax.rsqrt(jnp.where(low, ss_lo, ss_hi) * (1.0 / HEAD_DIM) + EPS)
        return x2 * inv * gain2

    for pair in range(FOX_HEADS // 2):
        cols = slice(pair * LANES, (pair + 1) * LANES)
        qn = head_pair_norm(hq[:, cols], gq_ref[...])
        kn = head_pair_norm(hk[:, cols], gk_ref[...])
        for half in range(2):
            h = 2 * pair + half
            extra = placed[:, h * LANES:(h + 1) * LANES]
            q_h = qn if half == 0 else pltpu.roll(qn, HEAD_DIM, axis=1)
            k_h = kn if half == 0 else pltpu.roll(kn, HEAD_DIM, axis=1)
            qa_ref[0, h] = jnp.where(low, q_h, jnp.where(q_takes_c, extra, one_q)).astype(BF16)
            ka_ref[0, h] = jnp.where(low, k_h, jnp.where(k_takes_c, extra, one_k)).astype(BF16)

    tail = (lax.broadcasted_iota(jnp.int32, (LANES - HEAD_DIM, KV_TILE), 0) == 0).astype(BF16)
    for h in range(FOX_HEADS):
        for n in range(tm // KV_TILE):
            vat_ref[0, h, n] = jnp.concatenate(
                [vt[h * HEAD_DIM:(h + 1) * HEAD_DIM, n * KV_TILE:(n + 1) * KV_TILE], tail], axis=0)

    hc = _dot(xn, wc_ref[...])
    u = hc[:, 2 * CONV_CH:3 * CONV_CH] * hc[:, 0:CONV_CH]
    ubuf_ref[8:8 + tm, :] = u
    u1 = ubuf_ref[7:7 + tm, :]
    u2 = ubuf_ref[6:6 + tm, :]
    cw = cw_ref[...]
    conv = cw[0:1, :] * u2 + cw[1:2, :] * u1 + cw[2:3, :] * u
    yc_ref[0] = (hc[:, CONV_CH:2 * CONV_CH] * conv).astype(BF16)
    ubuf_ref[0:8, :] = u[tm - 8:tm, :]

    hh_ref[0] = _dot(xn, wh_ref[...])


def _mix_in(x, p):
    b, l, _ = x.shape
    tm = min(ROW_TILE, l)
    nt = l // tm
    consts = [p["mix_norm"], p["wc"], p["conv_w"], p["wh"], p["wq"], p["wk"], p["wvt"], p["wf"],
              p["fbias"], p["gq"], p["gk"], p["tri"], p["place"]]
    out_shape = [
        jax.ShapeDtypeStruct((b, l, CONV_CH), BF16),
        jax.ShapeDtypeStruct((b, l, 4 * HGRN_W), F32),
        jax.ShapeDtypeStruct((b, FOX_HEADS, l, LANES), BF16),
        jax.ShapeDtypeStruct((b, FOX_HEADS, l, LANES), BF16),
        jax.ShapeDtypeStruct((b, FOX_HEADS, l // KV_TILE, LANES, KV_TILE), BF16),
        jax.ShapeDtypeStruct((b, nt, 8, LANES), F32),
    ]
    out_specs = [
        pl.BlockSpec((1, tm, CONV_CH), lambda bi, i: (bi, i, 0)),
        pl.BlockSpec((1, tm, 4 * HGRN_W), lambda bi, i: (bi, i, 0)),
        pl.BlockSpec((1, FOX_HEADS, tm, LANES), lambda bi, i: (bi, 0, i, 0)),
        pl.BlockSpec((1, FOX_HEADS, tm, LANES), lambda bi, i: (bi, 0, i, 0)),
        pl.BlockSpec((1, FOX_HEADS, tm // KV_TILE, LANES, KV_TILE), lambda bi, i: (bi, 0, i, 0, 0)),
        pl.BlockSpec((1, 1, 8, LANES), lambda bi, i: (bi, i, 0, 0)),
    ]
    return pl.pallas_call(
        functools.partial(_mix_in_body, tm=tm),
        grid=(b, nt),
        in_specs=[pl.BlockSpec((1, tm, D_MODEL), lambda bi, i: (bi, i, 0))]
        + [_const_spec(c.shape) for c in consts],
        out_specs=out_specs,
        out_shape=out_shape,
        scratch_shapes=[pltpu.VMEM((tm + 8, CONV_CH), F32), pltpu.VMEM((1, LANES), F32)],
        compiler_params=_params(("arbitrary", "arbitrary")),
        name="mix_in",
    )(x, *consts)


def _hgrn_body(q_ref, z_ref, v_ref, g_ref, lbl_ref, gain_ref, cum_ref, rep_ref, bd_ref, bmask_ref,
               o_ref, st_ref, vv_ref, *, layer, rows):
    @pl.when(pl.program_id(1) == 0)
    def _():
        st_ref[...] = jnp.zeros_like(st_ref)

    lbl = lbl_ref[...]
    e = jnp.exp(lbl - jnp.max(lbl, axis=0, keepdims=True))
    soft = e / jnp.sum(e, axis=0, keepdims=True)
    lb = jnp.clip(jnp.sum(soft[0:layer + 1, :], axis=0, keepdims=True) - soft[0:1, :], 0.0, 1.0)

    bd = bd_ref[...]
    n_sub = CHUNK // SUB
    sub_t = lax.broadcasted_iota(jnp.int32, (CHUNK, HGRN_W), 0) // SUB
    sub_s = lax.broadcasted_iota(jnp.int32, (CHUNK, HGRN_W), 1) % CHUNK // SUB

    def within_chunk(ch, slot):
        rs = pl.ds(pl.multiple_of(ch * CHUNK, CHUNK), CHUNK)
        q = q_ref[0, rs, :]
        z = z_ref[0, rs, :]
        v = v_ref[0, rs, :]
        lf = (_log_sigmoid(z) + jnp.log(1.0 + lb * jnp.exp(-z))) * LOG2E
        k = (1.0 - lb) * _sigmoid(-z)
        pieces = jnp.concatenate(_split3(lf), axis=0)
        sums = _dot(cum_ref[0:3 * CHUNK, :], pieces)
        vv_ref[slot] = v
        v16 = v.astype(BF16)
        k16 = k.astype(BF16)
        yield

        bc = sums[0:CHUNK]
        bs = sums[CHUNK:2 * CHUNK]
        be = sums[2 * CHUNK:3 * CHUNK]

        q_t = q * jnp.exp2(bc - bs)
        k_t = (k * jnp.exp2(be - bc)).astype(BF16)
        lags = [q_t]
        for lag in range(2, n_sub):
            be_shift = jnp.concatenate([jnp.zeros((lag * SUB, HGRN_W), F32),
                                        be[0:CHUNK - lag * SUB]], axis=0)
            lags.append(q_t * jnp.exp2(bs - be_shift))
        q_lags = jnp.concatenate(lags, axis=0).astype(BF16)
        sc = _dot_nt(q_lags, jnp.concatenate([k_t] * HGRN_HEADS, axis=0) * bd)

        half_rows = CHUNK * SUB // 2
        pair = jnp.concatenate(
            [_dot(cum_ref[3 * CHUNK + n * half_rows:3 * CHUNK + (n + 1) * half_rows, :], pieces)
             for n in range(2)], axis=0)
        k_rep = jnp.concatenate(
            [_dot(rep_ref[n * half_rows:(n + 1) * half_rows, :], k16) for n in range(2)], axis=0)
        yield

        a_lag = jnp.zeros((CHUNK, HGRN_W), F32)
        for n, lag in enumerate(range(1, n_sub)):
            a_lag = jnp.where(sub_t - sub_s == lag, sc[n * CHUNK:(n + 1) * CHUNK], a_lag)
        o_lag = _dot(a_lag.astype(BF16), jnp.concatenate([v16] * HGRN_HEADS, axis=0) * bd)
        a = []
        for i in range(n_sub):
            rows_i = slice(i * SUB * SUB, (i + 1) * SUB * SUB)
            q_rep = jnp.concatenate([q[i * SUB:(i + 1) * SUB]] * SUB, axis=0)
            a.append((q_rep * jnp.exp2(pair[rows_i]) * k_rep[rows_i]).astype(BF16))
        r = jnp.concatenate(
            [_dot(jnp.concatenate(a[n * n_sub // 2:(n + 1) * n_sub // 2], axis=0), bd)
             for n in range(2)], axis=0)
        b_last = bc[CHUNK - 1:CHUNK, :]
        q_hat = (q * jnp.exp2(bc)).astype(BF16)
        k_hat = (k * jnp.exp2(b_last - bc)).astype(BF16)
        yield

        o_sub = []
        for i in range(n_sub):
            acc = None
            for sl in range(SUB):
                s = i * SUB + sl
                term = r[s * SUB:(s + 1) * SUB] * vv_ref[slot, s:s + 1, :]
                acc = term if acc is None else acc + term
            o_sub.append(acc)
        o = o_lag + jnp.concatenate(o_sub, axis=0)
        return rs, o, q_hat, k_hat, v16, jnp.exp2(b_last)

    def finish(rs, o, q_hat, k_hat, v16, decay_last):
        st = st_ref[...]
        o = o + _dot_nt(q_hat, st.astype(BF16))
        st_ref[...] = st * decay_last + bmask_ref[...] * _dot_tn(v16, k_hat)

        oo = o * o
        hi = oo.astype(BF16)
        lo = (oo - hi.astype(F32)).astype(BF16)
        ms = (_dot(hi, bd) + _dot(lo, bd)) * (1.0 / HEAD_DIM)
        gate = g_ref[0, rs, :]
        y = o * lax.rsqrt(ms + EPS) * gain_ref[...] * (gate * _sigmoid(gate))
        o_ref[0, rs, :] = y.astype(BF16)

    def chunk_group(cc, carry):
        gens = [within_chunk(CHUNKS_PER_TRIP * cc + slot, slot) for slot in range(CHUNKS_PER_TRIP)]
        parts = [None] * CHUNKS_PER_TRIP
        while any(part is None for part in parts):
            for slot, gen in enumerate(gens):
                if parts[slot] is None:
                    try:
                        next(gen)
                    except StopIteration as done:
                        parts[slot] = done.value
        for part in parts:
            finish(*part)
        return carry

    lax.fori_loop(0, rows // (CHUNKS_PER_TRIP * CHUNK), chunk_group, 0)


def _hgrn(hh, p, layer):
    b, l, _ = hh.shape
    rows = min(HGRN_ROWS, l)
    consts = [p["lb_logits"], p["hgrn_gain"], p["cum64"], p["rep64"], p["bd"], p["bmask"]]

    def section(k):
        return pl.BlockSpec((1, rows, HGRN_W), lambda bi, i: (bi, i, k))

    return pl.pallas_call(
        functools.partial(_hgrn_body, layer=layer, rows=rows),
        grid=(b, l // rows),
        in_specs=[section(0), section(1), section(2), section(3)]
        + [_const_spec(c.shape) for c in consts],
        out_specs=pl.BlockSpec((1, rows, HGRN_W), lambda bi, i: (bi, i, 0)),
        out_shape=jax.ShapeDtypeStruct((b, l, HGRN_W), BF16),
        scratch_shapes=[pltpu.VMEM((HGRN_W, HGRN_W), F32),
                        pltpu.VMEM((CHUNKS_PER_TRIP, CHUNK, HGRN_W), F32)],
        compiler_params=_params(("arbitrary", "arbitrary")),
        name="hgrn",
    )(hh, hh, hh, hh, *consts)


def _fox_body(qkb_ref, cpre_ref, qa_ref, ka_ref, vat_ref, o_ref, m_ref, acc_ref, s_ref,
              *, tq, tk, n_pre):
    bi, hp, qi = pl.program_id(0), pl.program_id(1), pl.program_id(2)
    per_q = tq // tk
    row = lax.broadcasted_iota(jnp.int32, (tk, tq), 0)
    col = lax.broadcasted_iota(jnp.int32, (tk, tq), 1)

    def c_before(hh, n):
        return cpre_ref[(bi * FOX_HEADS + 2 * hp + hh) * n_pre + n]

    def keys(hh, j):
        return ka_ref[0, hh, pl.ds(pl.multiple_of(j * tk, tk), tk), :]

    def scores(hh, j):
        return _dot_nt(keys(hh, j), qa_ref[0, hh])

    def update(hh, j, s, mask=None, off=None, cols=slice(None)):
        if mask is not None:
            s = jnp.where(mask, s, MASK_VALUE)
        m_old = m_ref[hh, :, cols]
        tile_max = jnp.max(s, axis=0, keepdims=True)
        m_new = jnp.maximum(m_old, tile_max if off is None else tile_max + off)
        p = jnp.exp2(s - (m_new if off is None else m_new - off)).astype(BF16)
        acc_ref[hh, :, cols] = (acc_ref[hh, :, cols] * jnp.exp2(m_old - m_new)
                                + _dot(vat_ref[0, hh, j], p))
        m_ref[hh, :, cols] = m_new

    for hh in range(2):
        m_ref[hh] = jnp.full((1, tq), MASK_VALUE, F32)
        acc_ref[hh] = jnp.zeros((LANES, tq), F32)

    diag = [(hh, dj) for dj in range(per_q) for hh in range(2)]
    s_diag = [_dot_nt(keys(hh, qi * per_q + dj), qa_ref[0, hh, dj * tk:tq, :]) for hh, dj in diag]
    near = [(hh, jnp.maximum(qi * per_q - 1 - e, 0)) for e in range(per_q) for hh in range(2)]
    s_near = [scores(hh, j) for hh, j in near]
    j_start = (qi - 1) * per_q - 1
    j_first = jnp.maximum(j_start, 0)
    s_first = [scores(hh, j_first) for hh in range(2)]
    for (hh, dj), s in zip(diag, s_diag):
        update(hh, qi * per_q + dj, s, mask=(row <= col)[:, 0:tq - dj * tk], cols=slice(dj * tk, tq))
    off = jnp.where(qi > 0, 0.0, MASK_VALUE)
    for (hh, j), s in zip(near, s_near):
        update(hh, j, s, off=off)

    def needed(j):
        out = None
        for hh in range(2):
            gap = (qkb_ref[0] + c_before(hh, qi * per_q) - c_before(hh, j + 1)
                   - jnp.min(m_ref[hh]))
            out = gap >= SKIP_BELOW if out is None else jnp.logical_or(out, gap >= SKIP_BELOW)
        return out.astype(jnp.int32)

    def cond(carry):
        j, go = carry
        return jnp.logical_and(j >= 0, go > 0)

    def stage(src, dst, j):
        j_next = jnp.maximum(j - 1, 0)
        for hh in range(2):
            s_ref[dst, hh] = scores(hh, j_next)
        for hh in range(2):
            update(hh, j, s_ref[src, hh])
        return jnp.where(j >= 1, needed(j_next), 0)

    def body(carry):
        j, _ = carry
        go = stage(0, 1, j)
        go = lax.cond(go > 0, lambda: stage(1, 0, j - 1), lambda: jnp.int32(0))
        return j - 2, go

    for hh in range(2):
        s_ref[0, hh] = s_first[hh]
    lax.while_loop(cond, body, (j_start, needed(j_first)))

    outs = []
    for hh in range(2):
        acc = acc_ref[hh]
        outs.append(acc[0:HEAD_DIM, :] / acc[HEAD_DIM:HEAD_DIM + 1, :])
    o_ref[0] = jnp.concatenate(outs, axis=0).T.astype(BF16)


def _fox(qkb, cpre, qa, ka, vat):
    b, _, l, _ = qa.shape
    tq, tk = min(Q_TILE, l), vat.shape[-1]
    smem = pl.BlockSpec(memory_space=pltpu.SMEM)
    return pl.pallas_call(
        functools.partial(_fox_body, tq=tq, tk=tk, n_pre=l // tk + 1),
        grid=(b, FOX_HEADS // 2, l // tq),
        in_specs=[
            smem, smem,
            pl.BlockSpec((1, 2, tq, LANES), lambda bi, hp, i: (bi, hp, i, 0)),
            pl.BlockSpec((1, 2, l, LANES), lambda bi, hp, i: (bi, hp, 0, 0)),
            pl.BlockSpec((1, 2, l // tk, LANES, tk), lambda bi, hp, i: (bi, hp, 0, 0, 0)),
        ],
        out_specs=pl.BlockSpec((1, tq, LANES), lambda bi, hp, i: (bi, i, hp)),
        out_shape=jax.ShapeDtypeStruct((b, l, FOX_W), BF16),
        scratch_shapes=[pltpu.VMEM((2, 1, tq), F32), pltpu.VMEM((2, LANES, tq), F32),
                        pltpu.VMEM((2, 2, tk, tq), F32)],
        compiler_params=_params(("arbitrary", "arbitrary", "arbitrary")),
        name="fox",
    )(qkb, cpre, qa, ka, vat)


def _pad_lanes(v, width=LANES):
    return jnp.pad(v, (0, width - v.shape[0])).reshape(1, width)


def _placement():
    m = np.zeros((LANES, FOX_HEADS * LANES), np.float32)
    for p in range(N_PIECES):
        for h in range(FOX_HEADS):
            m[p * FOX_HEADS + h, h * LANES + Q_C_LANE + p] = 1.0
            m[p * FOX_HEADS + h, h * LANES + K_ONE_LANE + N_PIECES + p] = -1.0
    return jnp.asarray(m, BF16)


def _tri(n):
    return jnp.asarray(np.tril(np.ones((n, n), np.float32)), BF16)


def _chunk_sums():
    t = np.arange(CHUNK)[:, None]
    r = np.arange(CHUNK)[None, :]
    first = (t // SUB) * SUB
    s_pair, t_pair = _sub_chunk_pairs()
    rows = [r <= t, r < first, r < first + SUB,
            (r > s_pair[:, None]) & (r <= t_pair[:, None])]
    m = np.concatenate(rows, axis=0).astype(np.float32)
    return jnp.asarray(np.tile(m, (1, N_PIECES)), BF16)


def _sub_chunk_pairs():
    s = np.repeat(np.arange(CHUNK), SUB)
    t = (s // SUB) * SUB + np.tile(np.arange(SUB), CHUNK)
    return s, t


def _causal_replicate():
    s, t = _sub_chunk_pairs()
    m = (np.arange(CHUNK)[None, :] == s[:, None]) & (t >= s)[:, None]
    return jnp.asarray(m.astype(np.float32), BF16)


def _layer_operands(l, mix_norm, w_mix_in, conv_w, hgrn_lb_logits, hgrn_out_gain,
                    fox_q_gain, fox_k_gain, fox_f_bias, w_mix_out):
    w = w_mix_in[l]
    o_h = 3 * CONV_CH
    o_q = o_h + 4 * HGRN_W
    o_k, o_v, o_f = o_q + FOX_W, o_q + 2 * FOX_W, o_q + 3 * FOX_W
    head_blocks = np.kron(np.eye(HGRN_HEADS, dtype=np.float32),
                          np.ones((HEAD_DIM, HEAD_DIM), np.float32))
    wo = w_mix_out[l].astype(BF16)
    q_scale = fox_q_gain[l] * (HEAD_DIM ** -0.5 * LOG2E)
    return {
        "mix_norm": mix_norm[l].reshape(1, D_MODEL),
        "wc": w[:, :o_h].astype(BF16),
        "conv_w": jnp.pad(conv_w[l], ((0, 8 - CONV_WIDTH), (0, 0))),
        "wh": w[:, o_h:o_q].astype(BF16),
        "wq": w[:, o_q:o_k].astype(BF16),
        "wk": w[:, o_k:o_v].astype(BF16),
        "wvt": w[:, o_v:o_f].T.astype(BF16),
        "wf": jnp.pad(w[:, o_f:], ((0, 0), (0, LANES - FOX_HEADS))).astype(BF16),
        "fbias": _pad_lanes(fox_f_bias[l]),
        "gq": jnp.tile(q_scale, LANES // HEAD_DIM).reshape(1, LANES),
        "gk": jnp.tile(fox_k_gain[l], LANES // HEAD_DIM).reshape(1, LANES),
        "qkb": (1.01 * HEAD_DIM * jnp.max(jnp.abs(q_scale))
                * jnp.max(jnp.abs(fox_k_gain[l]))).reshape(1),
        "tri": _tri(KV_TILE),
        "place": _placement(),
        "lb_logits": hgrn_lb_logits,
        "hgrn_gain": jnp.tile(hgrn_out_gain[l], HGRN_HEADS).reshape(1, HGRN_W),
        "cum64": _chunk_sums(),
        "rep64": _causal_replicate(),
        "bd": jnp.asarray(head_blocks, BF16),
        "bmask": jnp.asarray(head_blocks),
        "wo_c": wo[:CONV_CH],
        "wo_h": wo[CONV_CH:CONV_CH + HGRN_W],
        "wo_f": wo[CONV_CH + HGRN_W:],
    }


def kernel(x, ffn1_norm, ffn1_w_in, ffn1_w_out, mix_norm, w_mix_in, conv_w, hgrn_lb_logits,
           hgrn_out_gain, fox_q_gain, fox_k_gain, fox_f_bias, w_mix_out, ffn2_norm, ffn2_w_in,
           ffn2_w_out):
    b, l, d = x.shape
    assert d == D_MODEL and l % ROW_TILE == 0 and ROW_TILE == Q_TILE and Q_TILE % KV_TILE == 0
    depth = ffn1_norm.shape[0]
    tm = min(ROW_TILE, l)
    n = b * l
    w1_in, w1_out = ffn1_w_in.astype(BF16), ffn1_w_out.astype(BF16)
    w2_in, w2_out = ffn2_w_in.astype(BF16), ffn2_w_out.astype(BF16)
    for layer in range(depth):
        p = _layer_operands(layer, mix_norm, w_mix_in, conv_w, hgrn_lb_logits, hgrn_out_gain,
                            fox_q_gain, fox_k_gain, fox_f_bias, w_mix_out)
        x2 = _ffn(x.reshape(n, d), ffn1_norm[layer].reshape(1, d), w1_in, w1_out, layer)
        yc, hh, qa, ka, vat, cend = _mix_in(x2.reshape(b, l, d), p)
        yh = _hgrn(hh, p, layer)
        cend = cend[:, :, :tm // KV_TILE, :FOX_HEADS].reshape(b, l // KV_TILE, FOX_HEADS)
        cpre = jnp.pad(cend.transpose(0, 2, 1), ((0, 0), (0, 0), (1, 0))) * LOG2E
        yf = _fox(p["qkb"], cpre.reshape(-1), qa, ka, vat)
        x2 = _mix_out_ffn(x2, yc.reshape(n, CONV_CH), yh.reshape(n, HGRN_W), yf.reshape(n, FOX_W),
                          p["wo_c"], p["wo_h"], p["wo_f"], ffn2_norm[layer].reshape(1, d),
                          w2_in, w2_out, layer)
        x = x2.reshape(b, l, d)
    return x
```

```python
import functools

import jax
import jax.numpy as jnp
import numpy as np
from jax import lax
from jax.experimental import pallas as pl
from jax.experimental.pallas import tpu as pltpu

F32 = jnp.float32
BF16 = jnp.bfloat16

D_MODEL = 1024
D_FF = 2816
HEAD_DIM = 64
CONV_CH = 256
CONV_WIDTH = 3
HGRN_HEADS = 4
HGRN_W = 256
FOX_HEADS = 8
FOX_W = 512
CHUNK = 64
EPS = 1e-6
MASK_VALUE = -1e30

LANES = 128
SUBLANES = 8
MXU_TILE = 256
FF_TILE = MXU_TILE
N_FF_TILES = D_FF // FF_TILE
ROW_TILE = 512
FFN_ROWS = 1024
Q_TILE = 512
KV_TILE = 256
HGRN_ROWS = 512
SUB = 8
CHUNKS_PER_TRIP = 8
VMEM_LIMIT = 56 * 1024 * 1024

Q_C_LANE = HEAD_DIM
K_ONE_LANE = HEAD_DIM
N_PIECES = 3
LOG2E = 1.4426950408889634
SKIP_BELOW = -138.0


def _dot(a, b):
    return jnp.dot(a, b, preferred_element_type=F32)


def _dot_nt(a, b):
    return lax.dot_general(a, b, (((1,), (1,)), ((), ())), preferred_element_type=F32)


def _dot_tn(a, b):
    return lax.dot_general(a, b, (((0,), (0,)), ((), ())), preferred_element_type=F32)


def _rms_norm(x, gain):
    inv = lax.rsqrt(jnp.mean(x * x, axis=-1, keepdims=True) + EPS)
    return x * inv * gain


def _sigmoid(x):
    return 1.0 / (1.0 + jnp.exp(-x))


def _log_sigmoid(x):
    return jnp.minimum(x, 0.0) - jnp.log(1.0 + jnp.exp(-jnp.abs(x)))


def _split3(x):
    hi = x.astype(BF16)
    r = x - hi.astype(F32)
    mid = r.astype(BF16)
    lo = (r - mid.astype(F32)).astype(BF16)
    return hi, mid, lo


def _const_spec(shape):
    nd = len(shape)
    return pl.BlockSpec(shape, lambda *_: (0,) * nd, pipeline_mode=pl.Buffered(1))


def _layer_spec(stacked_shape, layer):
    nd = len(stacked_shape)
    return pl.BlockSpec((None,) + tuple(stacked_shape[1:]), lambda *_: (layer,) + (0,) * (nd - 1),
                        pipeline_mode=pl.Buffered(1))


def _params(sem):
    return pltpu.CompilerParams(dimension_semantics=sem, vmem_limit_bytes=VMEM_LIMIT)


def _swiglu_half_step(x, g_ref, wi_ref, wo_ref, o_ref, acc_ref):
    xn = _rms_norm(x, g_ref[...]).astype(BF16)
    for c in range(N_FF_TILES):
        gate = _dot(xn, wi_ref[:, c * FF_TILE:(c + 1) * FF_TILE])
        up = _dot(xn, wi_ref[:, D_FF + c * FF_TILE:D_FF + (c + 1) * FF_TILE])
        act = (gate * _sigmoid(gate) * up).astype(BF16)
        part = _dot(act, wo_ref[c * FF_TILE:(c + 1) * FF_TILE, :])
        if c == 0:
            acc_ref[...] = part
        else:
            acc_ref[...] += part
    o_ref[...] = x + 0.5 * acc_ref[...]


def _ffn_body(x_ref, g_ref, wi_ref, wo_ref, o_ref, acc_ref):
    _swiglu_half_step(x_ref[...], g_ref, wi_ref, wo_ref, o_ref, acc_ref)


def _mix_out_ffn_body(x_ref, yc_ref, yh_ref, yf_ref, wc_ref, wh_ref, wf_ref,
                      g_ref, wi_ref, wo_ref, o_ref, acc_ref):
    x = (x_ref[...] + _dot(yc_ref[...], wc_ref[...]) + _dot(yh_ref[...], wh_ref[...])
         + _dot(yf_ref[...], wf_ref[...]))
    _swiglu_half_step(x, g_ref, wi_ref, wo_ref, o_ref, acc_ref)


def _mix_out_ffn(x2d, yc, yh, yf, wc, wh, wf, gain, wi, wo, layer):
    n = x2d.shape[0]
    tm = min(FFN_ROWS, n)

    def row(w):
        return pl.BlockSpec((tm, w), lambda i: (i, 0))

    consts = [wc, wh, wf, gain]
    return pl.pallas_call(
        _mix_out_ffn_body,
        grid=(n // tm,),
        in_specs=[row(D_MODEL), row(CONV_CH), row(HGRN_W), row(FOX_W)]
        + [_const_spec(c.shape) for c in consts]
        + [_layer_spec(wi.shape, layer), _layer_spec(wo.shape, layer)],
        out_specs=row(D_MODEL),
        out_shape=jax.ShapeDtypeStruct((n, D_MODEL), F32),
        scratch_shapes=[pltpu.VMEM((tm, D_MODEL), F32)],
        compiler_params=_params(("arbitrary",)),
        name="mix_out_ffn",
    )(x2d, yc, yh, yf, *consts, wi, wo)


def _ffn(x2d, gain, wi, wo, layer):
    n = x2d.shape[0]
    tm = min(FFN_ROWS, n)
    row = pl.BlockSpec((tm, D_MODEL), lambda i: (i, 0))
    return pl.pallas_call(
        _ffn_body,
        grid=(n // tm,),
        in_specs=[row, _const_spec(gain.shape), _layer_spec(wi.shape, layer),
                  _layer_spec(wo.shape, layer)],
        out_specs=row,
        out_shape=jax.ShapeDtypeStruct((n, D_MODEL), F32),
        scratch_shapes=[pltpu.VMEM((tm, D_MODEL), F32)],
        compiler_params=_params(("arbitrary",)),
        name="ffn",
    )(x2d, gain, wi, wo)


def _mix_in_body(x_ref, g_ref, wc_ref, cw_ref, wh_ref, wq_ref, wk_ref, wvt_ref, wf_ref,
                 fb_ref, gq_ref, gk_ref, tri_ref, place_ref,
                 yc_ref, hh_ref, qa_ref, ka_ref, vat_ref, cend_ref, ubuf_ref, carry_ref, *, tm):
    @pl.when(pl.program_id(1) == 0)
    def _():
        ubuf_ref[0:SUBLANES, :] = jnp.zeros((SUBLANES, CONV_CH), F32)
        carry_ref[...] = jnp.zeros_like(carry_ref)

    xn = _rms_norm(x_ref[0], g_ref[...]).astype(BF16)

    lf = _log_sigmoid(_dot(xn, wf_ref[...]) + fb_ref[...])
    hq = _dot(xn, wq_ref[...])
    hk = _dot(xn, wk_ref[...])
    lf3 = jnp.concatenate(_split3(lf), axis=1)
    last = carry_ref[...]
    blocks, ends = [], []
    for n in range(tm // KV_TILE):
        part = _dot(tri_ref[...], lf3[n * KV_TILE:(n + 1) * KV_TILE])
        blocks.append(last + (part[:, 0:LANES] + part[:, LANES:2 * LANES] + part[:, 2 * LANES:]))
        last = blocks[-1][KV_TILE - 1:KV_TILE, :]
        ends.append(last)
    c = jnp.concatenate(blocks, axis=0)
    carry_ref[...] = last
    cend_ref[0, 0] = jnp.concatenate(
        ends + [jnp.zeros((SUBLANES - len(ends), LANES), F32)], axis=0)
    vt = _dot_nt(wvt_ref[...], xn).astype(BF16)

    lane = lax.broadcasted_iota(jnp.int32, (1, LANES), 1)
    c_hi, c_mid, c_lo = [piece.astype(F32) for piece in _split3(c * LOG2E)]
    packed = jnp.where(lane < FOX_HEADS, c_hi,
                       jnp.where(lane < 2 * FOX_HEADS, pltpu.roll(c_mid, FOX_HEADS, axis=1),
                                 pltpu.roll(c_lo, 2 * FOX_HEADS, axis=1)))
    placed = _dot(packed.astype(BF16), place_ref[...])

    low = lane < HEAD_DIM
    q_takes_c = jnp.logical_and(lane >= Q_C_LANE, lane < Q_C_LANE + N_PIECES)
    k_takes_c = jnp.logical_and(lane >= K_ONE_LANE + N_PIECES, lane < K_ONE_LANE + 2 * N_PIECES)
    one_q = k_takes_c.astype(F32)
    one_k = q_takes_c.astype(F32)

    def head_pair_norm(x2, gain2):
        sq = x2 * x2
        ss_lo = jnp.sum(jnp.where(low, sq, 0.0), axis=-1, keepdims=True)
        ss_hi = jnp.sum(jnp.where(low, 0.0, sq), axis=-1, keepdims=True)
        inv = lax.rsqrt(jnp.where(low, ss_lo, ss_hi) * (1.0 / HEAD_DIM) + EPS)
        return x2 * inv * gain2

    for pair in range(FOX_HEADS // 2):
        cols = slice(pair * LANES, (pair + 1) * LANES)
        qn = head_pair_norm(hq[:, cols], gq_ref[...])
        kn = head_pair_norm(hk[:, cols], gk_ref[...])
        for half in range(2):
            h = 2 * pair + half
            extra = placed[:, h * LANES:(h + 1) * LANES]
            q_h = qn if half == 0 else pltpu.roll(qn, HEAD_DIM, axis=1)
            k_h = kn if half == 0 else pltpu.roll(kn, HEAD_DIM, axis=1)
            qa_ref[0, h] = jnp.where(low, q_h, jnp.where(q_takes_c, extra, one_q)).astype(BF16)
            ka_ref[0, h] = jnp.where(low, k_h, jnp.where(k_takes_c, extra, one_k)).astype(BF16)

    tail = (lax.broadcasted_iota(jnp.int32, (LANES - HEAD_DIM, KV_TILE), 0) == 0).astype(BF16)
    for h in range(FOX_HEADS):
        for n in range(tm // KV_TILE):
            vat_ref[0, h, n] = jnp.concatenate(
                [vt[h * HEAD_DIM:(h + 1) * HEAD_DIM, n * KV_TILE:(n + 1) * KV_TILE], tail], axis=0)

    hc = _dot(xn, wc_ref[...])
    u = hc[:, 2 * CONV_CH:3 * CONV_CH] * hc[:, 0:CONV_CH]
    ubuf_ref[SUBLANES:SUBLANES + tm, :] = u
    u1 = ubuf_ref[SUBLANES - 1:SUBLANES - 1 + tm, :]
    u2 = ubuf_ref[SUBLANES - 2:SUBLANES - 2 + tm, :]
    cw = cw_ref[...]
    conv = cw[0:1, :] * u2 + cw[1:2, :] * u1 + cw[2:3, :] * u
    yc_ref[0] = (hc[:, CONV_CH:2 * CONV_CH] * conv).astype(BF16)
    ubuf_ref[0:SUBLANES, :] = u[tm - SUBLANES:tm, :]

    hh_ref[0] = _dot(xn, wh_ref[...])


def _mix_in(x, p):
    b, l, _ = x.shape
    tm = min(ROW_TILE, l)
    nt = l // tm
    consts = [p["mix_norm"], p["wc"], p["conv_w"], p["wh"], p["wq"], p["wk"], p["wvt"], p["wf"],
              p["fbias"], p["gq"], p["gk"], p["tri"], p["place"]]
    out_shape = [
        jax.ShapeDtypeStruct((b, l, CONV_CH), BF16),
        jax.ShapeDtypeStruct((b, l, 4 * HGRN_W), F32),
        jax.ShapeDtypeStruct((b, FOX_HEADS, l, LANES), BF16),
        jax.ShapeDtypeStruct((b, FOX_HEADS, l, LANES), BF16),
        jax.ShapeDtypeStruct((b, FOX_HEADS, l // KV_TILE, LANES, KV_TILE), BF16),
        jax.ShapeDtypeStruct((b, nt, SUBLANES, LANES), F32),
    ]
    out_specs = [
        pl.BlockSpec((1, tm, CONV_CH), lambda bi, i: (bi, i, 0)),
        pl.BlockSpec((1, tm, 4 * HGRN_W), lambda bi, i: (bi, i, 0)),
        pl.BlockSpec((1, FOX_HEADS, tm, LANES), lambda bi, i: (bi, 0, i, 0)),
        pl.BlockSpec((1, FOX_HEADS, tm, LANES), lambda bi, i: (bi, 0, i, 0)),
        pl.BlockSpec((1, FOX_HEADS, tm // KV_TILE, LANES, KV_TILE), lambda bi, i: (bi, 0, i, 0, 0)),
        pl.BlockSpec((1, 1, SUBLANES, LANES), lambda bi, i: (bi, i, 0, 0)),
    ]
    return pl.pallas_call(
        functools.partial(_mix_in_body, tm=tm),
        grid=(b, nt),
        in_specs=[pl.BlockSpec((1, tm, D_MODEL), lambda bi, i: (bi, i, 0))]
        + [_const_spec(c.shape) for c in consts],
        out_specs=out_specs,
        out_shape=out_shape,
        scratch_shapes=[pltpu.VMEM((tm + SUBLANES, CONV_CH), F32), pltpu.VMEM((1, LANES), F32)],
        compiler_params=_params(("arbitrary", "arbitrary")),
        name="mix_in",
    )(x, *consts)


def _hgrn_body(q_ref, z_ref, v_ref, g_ref, lbl_ref, gain_ref, cum_ref, rep_ref, bd_ref, bmask_ref,
               o_ref, st_ref, vv_ref, lb_ref, *, layer, rows):
    @pl.when(pl.program_id(1) == 0)
    def _():
        st_ref[...] = jnp.zeros_like(st_ref)
        lbl = lbl_ref[...]
        e = jnp.exp(lbl - jnp.max(lbl, axis=0, keepdims=True))
        soft = e / jnp.sum(e, axis=0, keepdims=True)
        lb_ref[...] = jnp.clip(
            jnp.sum(soft[0:layer + 1, :], axis=0, keepdims=True) - soft[0:1, :], 0.0, 1.0)

    lb = lb_ref[...]
    bd = bd_ref[...]
    n_sub = CHUNK // SUB
    sub_t = lax.broadcasted_iota(jnp.int32, (CHUNK, HGRN_W), 0) // SUB
    sub_s = lax.broadcasted_iota(jnp.int32, (CHUNK, HGRN_W), 1) % CHUNK // SUB

    def within_chunk(ch, slot):
        rs = pl.ds(pl.multiple_of(ch * CHUNK, CHUNK), CHUNK)
        q = q_ref[0, rs, :]
        z = z_ref[0, rs, :]
        v = v_ref[0, rs, :]
        lf = (_log_sigmoid(z) + jnp.log(1.0 + lb * jnp.exp(-z))) * LOG2E
        k = (1.0 - lb) * _sigmoid(-z)
        pieces = jnp.concatenate(_split3(lf), axis=0)
        sums = _dot(cum_ref[0:3 * CHUNK, :], pieces)
        vv_ref[slot] = v
        v16 = v.astype(BF16)
        k16 = k.astype(BF16)
        yield

        bc = sums[0:CHUNK]
        bs = sums[CHUNK:2 * CHUNK]
        be = sums[2 * CHUNK:3 * CHUNK]

        q_t = q * jnp.exp2(bc - bs)
        k_t = (k * jnp.exp2(be - bc)).astype(BF16)
        lags = [q_t]
        for lag in range(2, n_sub):
            be_shift = jnp.concatenate([jnp.zeros((lag * SUB, HGRN_W), F32),
                                        be[0:CHUNK - lag * SUB]], axis=0)
            lags.append(q_t * jnp.exp2(bs - be_shift))
        q_lags = jnp.concatenate(lags, axis=0).astype(BF16)
        sc = _dot_nt(q_lags, jnp.concatenate([k_t] * HGRN_HEADS, axis=0) * bd)

        half_rows = CHUNK * SUB // 2
        pair = jnp.concatenate(
            [_dot(cum_ref[3 * CHUNK + n * half_rows:3 * CHUNK + (n + 1) * half_rows, :], pieces)
             for n in range(2)], axis=0)
        k_rep = jnp.concatenate(
            [_dot(rep_ref[n * half_rows:(n + 1) * half_rows, :], k16) for n in range(2)], axis=0)
        yield

        a_lag = jnp.zeros((CHUNK, HGRN_W), F32)
        for n, lag in enumerate(range(1, n_sub)):
            a_lag = jnp.where(sub_t - sub_s == lag, sc[n * CHUNK:(n + 1) * CHUNK], a_lag)
        o_lag = _dot(a_lag.astype(BF16), jnp.concatenate([v16] * HGRN_HEADS, axis=0) * bd)
        a = []
        for i in range(n_sub):
            rows_i = slice(i * SUB * SUB, (i + 1) * SUB * SUB)
            q_rep = jnp.concatenate([q[i * SUB:(i + 1) * SUB]] * SUB, axis=0)
            a.append((q_rep * jnp.exp2(pair[rows_i]) * k_rep[rows_i]).astype(BF16))
        r = jnp.concatenate(
            [_dot(jnp.concatenate(a[n * n_sub // 2:(n + 1) * n_sub // 2], axis=0), bd)
             for n in range(2)], axis=0)
        b_last = bc[CHUNK - 1:CHUNK, :]
        q_hat = (q * jnp.exp2(bc)).astype(BF16)
        k_hat = (k * jnp.exp2(b_last - bc)).astype(BF16)
        yield

        o_sub = []
        for i in range(n_sub):
            acc = None
            for sl in range(SUB):
                s = i * SUB + sl
                term = r[s * SUB:(s + 1) * SUB] * vv_ref[slot, s:s + 1, :]
                acc = term if acc is None else acc + term
            o_sub.append(acc)
        o = o_lag + jnp.concatenate(o_sub, axis=0)
        return rs, o, q_hat, k_hat, v16, jnp.exp2(b_last)

    def finish(parts):
        updates = [_dot_tn(v16, k_hat) for _, _, _, k_hat, v16, _ in parts]
        st = st_ref[...]
        outs = []
        for (_, o, q_hat, _, _, decay_last), update in zip(parts, updates):
            outs.append(o + _dot_nt(q_hat, st.astype(BF16)))
            st = st * decay_last + bmask_ref[...] * update
        st_ref[...] = st

        means = []
        for o in outs:
            oo = o * o
            hi = oo.astype(BF16)
            lo = (oo - hi.astype(F32)).astype(BF16)
            means.append((_dot(hi, bd) + _dot(lo, bd)) * (1.0 / HEAD_DIM))
        for (rs, *_), o, ms in zip(parts, outs, means):
            gate = g_ref[0, rs, :]
            y = o * lax.rsqrt(ms + EPS) * gain_ref[...] * (gate * _sigmoid(gate))
            o_ref[0, rs, :] = y.astype(BF16)

    def chunk_group(cc, carry):
        gens = [within_chunk(CHUNKS_PER_TRIP * cc + slot, slot) for slot in range(CHUNKS_PER_TRIP)]
        parts = [None] * CHUNKS_PER_TRIP
        while any(part is None for part in parts):
            for slot, gen in enumerate(gens):
                if parts[slot] is None:
                    try:
                        next(gen)
                    except StopIteration as done:
                        parts[slot] = done.value
        finish(parts)
        return carry

    lax.fori_loop(0, rows // (CHUNKS_PER_TRIP * CHUNK), chunk_group, 0)


def _hgrn(hh, p, layer):
    b, l, _ = hh.shape
    rows = min(HGRN_ROWS, l)
    consts = [p["lb_logits"], p["hgrn_gain"], p["cum64"], p["rep64"], p["bd"], p["bmask"]]

    def section(k):
        return pl.BlockSpec((1, rows, HGRN_W), lambda bi, i: (bi, i, k))

    return pl.pallas_call(
        functools.partial(_hgrn_body, layer=layer, rows=rows),
        grid=(b, l // rows),
        in_specs=[section(0), section(1), section(2), section(3)]
        + [_const_spec(c.shape) for c in consts],
        out_specs=pl.BlockSpec((1, rows, HGRN_W), lambda bi, i: (bi, i, 0)),
        out_shape=jax.ShapeDtypeStruct((b, l, HGRN_W), BF16),
        scratch_shapes=[pltpu.VMEM((HGRN_W, HGRN_W), F32),
                        pltpu.VMEM((CHUNKS_PER_TRIP, CHUNK, HGRN_W), F32),
                        pltpu.VMEM((1, HGRN_W), F32)],
        compiler_params=_params(("arbitrary", "arbitrary")),
        name="hgrn",
    )(hh, hh, hh, hh, *consts)


def _fox_body(qkb_ref, cpre_ref, qa_ref, ka_ref, vat_ref, o_ref, m_ref, acc_ref, s_ref,
              *, tq, tk, n_pre):
    bi, hp, qi = pl.program_id(0), pl.program_id(1), pl.program_id(2)
    per_q = tq // tk
    row = lax.broadcasted_iota(jnp.int32, (tk, tq), 0)
    col = lax.broadcasted_iota(jnp.int32, (tk, tq), 1)

    def c_before(hh, n):
        return cpre_ref[(bi * FOX_HEADS + 2 * hp + hh) * n_pre + n]

    def keys(hh, j):
        return ka_ref[0, hh, pl.ds(pl.multiple_of(j * tk, tk), tk), :]

    def scores(hh, j):
        return _dot_nt(keys(hh, j), qa_ref[0, hh])

    def update(hh, j, s, mask=None, off=None, cols=slice(None)):
        if mask is not None:
            s = jnp.where(mask, s, MASK_VALUE)
        m_old = m_ref[hh, :, cols]
        tile_max = jnp.max(s, axis=0, keepdims=True)
        m_new = jnp.maximum(m_old, tile_max if off is None else tile_max + off)
        p = jnp.exp2(s - (m_new if off is None else m_new - off)).astype(BF16)
        acc_ref[hh, :, cols] = (acc_ref[hh, :, cols] * jnp.exp2(m_old - m_new)
                                + _dot(vat_ref[0, hh, j], p))
        m_ref[hh, :, cols] = m_new

    for hh in range(2):
        m_ref[hh] = jnp.full((1, tq), MASK_VALUE, F32)
        acc_ref[hh] = jnp.zeros((LANES, tq), F32)

    diag = [(hh, dj) for dj in range(per_q) for hh in range(2)]
    s_diag = [_dot_nt(keys(hh, qi * per_q + dj), qa_ref[0, hh, dj * tk:tq, :]) for hh, dj in diag]
    near = [(hh, jnp.maximum(qi * per_q - 1 - e, 0)) for e in range(per_q) for hh in range(2)]
    s_near = [scores(hh, j) for hh, j in near]
    j_start = (qi - 1) * per_q - 1
    j_first = jnp.maximum(j_start, 0)
    s_first = [scores(hh, j_first) for hh in range(2)]
    for (hh, dj), s in zip(diag, s_diag):
        update(hh, qi * per_q + dj, s, mask=(row <= col)[:, 0:tq - dj * tk], cols=slice(dj * tk, tq))
    off = jnp.where(qi > 0, 0.0, MASK_VALUE)
    for (hh, j), s in zip(near, s_near):
        update(hh, j, s, off=off)

    def needed(j):
        out = None
        for hh in range(2):
            gap = (qkb_ref[0] + c_before(hh, qi * per_q) - c_before(hh, j + 1)
                   - jnp.min(m_ref[hh]))
            out = gap >= SKIP_BELOW if out is None else jnp.logical_or(out, gap >= SKIP_BELOW)
        return out.astype(jnp.int32)

    def cond(carry):
        j, go = carry
        return jnp.logical_and(j >= 0, go > 0)

    def stage(src, dst, j):
        j_next = jnp.maximum(j - 1, 0)
        for hh in range(2):
            s_ref[dst, hh] = scores(hh, j_next)
        for hh in range(2):
            update(hh, j, s_ref[src, hh])
        return jnp.where(j >= 1, needed(j_next), 0)

    def body(carry):
        j, _ = carry
        go = stage(0, 1, j)
        go = lax.cond(go > 0, lambda: stage(1, 0, j - 1), lambda: jnp.int32(0))
        return j - 2, go

    for hh in range(2):
        s_ref[0, hh] = s_first[hh]
    lax.while_loop(cond, body, (j_start, needed(j_first)))

    outs = []
    for hh in range(2):
        acc = acc_ref[hh]
        outs.append(acc[0:HEAD_DIM, :] / acc[HEAD_DIM:HEAD_DIM + 1, :])
    o_ref[0] = jnp.concatenate(outs, axis=0).T.astype(BF16)


def _fox(qkb, cpre, qa, ka, vat):
    b, _, l, _ = qa.shape
    tq, tk = min(Q_TILE, l), vat.shape[-1]
    smem = pl.BlockSpec(memory_space=pltpu.SMEM)
    return pl.pallas_call(
        functools.partial(_fox_body, tq=tq, tk=tk, n_pre=l // tk + 1),
        grid=(b, FOX_HEADS // 2, l // tq),
        in_specs=[
            smem, smem,
            pl.BlockSpec((1, 2, tq, LANES), lambda bi, hp, i: (bi, hp, i, 0)),
            pl.BlockSpec((1, 2, l, LANES), lambda bi, hp, i: (bi, hp, 0, 0)),
            pl.BlockSpec((1, 2, l // tk, LANES, tk), lambda bi, hp, i: (bi, hp, 0, 0, 0)),
        ],
        out_specs=pl.BlockSpec((1, tq, LANES), lambda bi, hp, i: (bi, i, hp)),
        out_shape=jax.ShapeDtypeStruct((b, l, FOX_W), BF16),
        scratch_shapes=[pltpu.VMEM((2, 1, tq), F32), pltpu.VMEM((2, LANES, tq), F32),
                        pltpu.VMEM((2, 2, tk, tq), F32)],
        compiler_params=_params(("arbitrary", "arbitrary", "arbitrary")),
        name="fox",
    )(qkb, cpre, qa, ka, vat)


def _pad_lanes(v, width=LANES):
    return jnp.pad(v, (0, width - v.shape[0])).reshape(1, width)


def _placement():
    m = np.zeros((LANES, FOX_HEADS * LANES), np.float32)
    for p in range(N_PIECES):
        for h in range(FOX_HEADS):
            m[p * FOX_HEADS + h, h * LANES + Q_C_LANE + p] = 1.0
            m[p * FOX_HEADS + h, h * LANES + K_ONE_LANE + N_PIECES + p] = -1.0
    return jnp.asarray(m, BF16)


def _tri(n):
    return jnp.asarray(np.tril(np.ones((n, n), np.float32)), BF16)


def _chunk_sums():
    t = np.arange(CHUNK)[:, None]
    r = np.arange(CHUNK)[None, :]
    first = (t // SUB) * SUB
    s_pair, t_pair = _sub_chunk_pairs()
    rows = [r <= t, r < first, r < first + SUB,
            (r > s_pair[:, None]) & (r <= t_pair[:, None])]
    m = np.concatenate(rows, axis=0).astype(np.float32)
    return jnp.asarray(np.tile(m, (1, N_PIECES)), BF16)


def _sub_chunk_pairs():
    s = np.repeat(np.arange(CHUNK), SUB)
    t = (s // SUB) * SUB + np.tile(np.arange(SUB), CHUNK)
    return s, t


def _causal_replicate():
    s, t = _sub_chunk_pairs()
    m = (np.arange(CHUNK)[None, :] == s[:, None]) & (t >= s)[:, None]
    return jnp.asarray(m.astype(np.float32), BF16)


def _layer_operands(l, mix_norm, w_mix_in, conv_w, hgrn_lb_logits, hgrn_out_gain,
                    fox_q_gain, fox_k_gain, fox_f_bias, w_mix_out):
    w = w_mix_in[l]
    o_h = 3 * CONV_CH
    o_q = o_h + 4 * HGRN_W
    o_k, o_v, o_f = o_q + FOX_W, o_q + 2 * FOX_W, o_q + 3 * FOX_W
    head_blocks = np.kron(np.eye(HGRN_HEADS, dtype=np.float32),
                          np.ones((HEAD_DIM, HEAD_DIM), np.float32))
    wo = w_mix_out[l].astype(BF16)
    q_scale = fox_q_gain[l] * (HEAD_DIM ** -0.5 * LOG2E)
    return {
        "mix_norm": mix_norm[l].reshape(1, D_MODEL),
        "wc": w[:, :o_h].astype(BF16),
        "conv_w": jnp.pad(conv_w[l], ((0, SUBLANES - CONV_WIDTH), (0, 0))),
        "wh": w[:, o_h:o_q].astype(BF16),
        "wq": w[:, o_q:o_k].astype(BF16),
        "wk": w[:, o_k:o_v].astype(BF16),
        "wvt": w[:, o_v:o_f].T.astype(BF16),
        "wf": jnp.pad(w[:, o_f:], ((0, 0), (0, LANES - FOX_HEADS))).astype(BF16),
        "fbias": _pad_lanes(fox_f_bias[l]),
        "gq": jnp.tile(q_scale, LANES // HEAD_DIM).reshape(1, LANES),
        "gk": jnp.tile(fox_k_gain[l], LANES // HEAD_DIM).reshape(1, LANES),
        "qkb": (1.01 * HEAD_DIM * jnp.max(jnp.abs(q_scale))
                * jnp.max(jnp.abs(fox_k_gain[l]))).reshape(1),
        "tri": _tri(KV_TILE),
        "place": _placement(),
        "lb_logits": hgrn_lb_logits,
        "hgrn_gain": jnp.tile(hgrn_out_gain[l], HGRN_HEADS).reshape(1, HGRN_W),
        "cum64": _chunk_sums(),
        "rep64": _causal_replicate(),
        "bd": jnp.asarray(head_blocks, BF16),
        "bmask": jnp.asarray(head_blocks),
        "wo_c": wo[:CONV_CH],
        "wo_h": wo[CONV_CH:CONV_CH + HGRN_W],
        "wo_f": wo[CONV_CH + HGRN_W:],
    }


def kernel(x, ffn1_norm, ffn1_w_in, ffn1_w_out, mix_norm, w_mix_in, conv_w, hgrn_lb_logits,
           hgrn_out_gain, fox_q_gain, fox_k_gain, fox_f_bias, w_mix_out, ffn2_norm, ffn2_w_in,
           ffn2_w_out):
    b, l, d = x.shape
    assert d == D_MODEL and l % ROW_TILE == 0 and ROW_TILE % KV_TILE == 0
    assert l % Q_TILE == 0 and Q_TILE % KV_TILE == 0 and (b * l) % min(FFN_ROWS, b * l) == 0
    assert l % HGRN_ROWS == 0 and HGRN_ROWS % (CHUNKS_PER_TRIP * CHUNK) == 0
    depth = ffn1_norm.shape[0]
    tm = min(ROW_TILE, l)
    n = b * l
    w1_in, w1_out = ffn1_w_in.astype(BF16), ffn1_w_out.astype(BF16)
    w2_in, w2_out = ffn2_w_in.astype(BF16), ffn2_w_out.astype(BF16)
    for layer in range(depth):
        p = _layer_operands(layer, mix_norm, w_mix_in, conv_w, hgrn_lb_logits, hgrn_out_gain,
                            fox_q_gain, fox_k_gain, fox_f_bias, w_mix_out)
        x2 = _ffn(x.reshape(n, d), ffn1_norm[layer].reshape(1, d), w1_in, w1_out, layer)
        yc, hh, qa, ka, vat, cend = _mix_in(x2.reshape(b, l, d), p)
        yh = _hgrn(hh, p, layer)
        cend = cend[:, :, :tm // KV_TILE, :FOX_HEADS].reshape(b, l // KV_TILE, FOX_HEADS)
        cpre = jnp.pad(cend.transpose(0, 2, 1), ((0, 0), (0, 0), (1, 0))) * LOG2E
        yf = _fox(p["qkb"], cpre.reshape(-1), qa, ka, vat)
        x2 = _mix_out_ffn(x2, yc.reshape(n, CONV_CH), yh.reshape(n, HGRN_W), yf.reshape(n, FOX_W),
                          p["wo_c"], p["wo_h"], p["wo_f"], ffn2_norm[layer].reshape(1, d),
                          w2_in, w2_out, layer)
        x = x2.reshape(b, l, d)
    return x
```

```python
import functools

import jax
import jax.numpy as jnp
import numpy as np
from jax import lax
from jax.experimental import pallas as pl
from jax.experimental.pallas import tpu as pltpu

F32 = jnp.float32
BF16 = jnp.bfloat16

D_MODEL = 1024
D_FF = 2816
HEAD_DIM = 64
CONV_CH = 256
CONV_WIDTH = 3
HGRN_HEADS = 4
HGRN_W = 256
FOX_HEADS = 8
FOX_W = 512
CHUNK = 64
EPS = 1e-6
MASK_VALUE = -1e30

LANES = 128
SUBLANES = 8
MXU_TILE = 256
FF_TILE = MXU_TILE
N_FF_TILES = D_FF // FF_TILE
ROW_TILE = 512
FFN_ROWS = 1024
Q_TILE = 512
KV_TILE = 256
N_NEAR = 3
HGRN_ROWS = 512
SUB = 8
CHUNKS_PER_TRIP = 8
VMEM_LIMIT = 56 * 1024 * 1024

Q_C_LANE = HEAD_DIM
K_ONE_LANE = HEAD_DIM
N_PIECES = 3
LOG2E = 1.4426950408889634
SKIP_BELOW = -138.0


def _dot(a, b):
    return jnp.dot(a, b, preferred_element_type=F32)


def _dot_nt(a, b):
    return lax.dot_general(a, b, (((1,), (1,)), ((), ())), preferred_element_type=F32)


def _dot_tn(a, b):
    return lax.dot_general(a, b, (((0,), (0,)), ((), ())), preferred_element_type=F32)


def _rms_norm(x, gain):
    inv = lax.rsqrt(jnp.mean(x * x, axis=-1, keepdims=True) + EPS)
    return x * inv * gain


def _sigmoid(x):
    return 1.0 / (1.0 + jnp.exp(-x))


def _log_sigmoid(x):
    return jnp.minimum(x, 0.0) - jnp.log(1.0 + jnp.exp(-jnp.abs(x)))


def _split3(x):
    hi = x.astype(BF16)
    r = x - hi.astype(F32)
    mid = r.astype(BF16)
    lo = (r - mid.astype(F32)).astype(BF16)
    return hi, mid, lo


def _const_spec(shape):
    nd = len(shape)
    return pl.BlockSpec(shape, lambda *_: (0,) * nd, pipeline_mode=pl.Buffered(1))


def _layer_spec(stacked_shape, layer):
    nd = len(stacked_shape)
    return pl.BlockSpec((None,) + tuple(stacked_shape[1:]), lambda *_: (layer,) + (0,) * (nd - 1),
                        pipeline_mode=pl.Buffered(1))


def _params(sem):
    return pltpu.CompilerParams(dimension_semantics=sem, vmem_limit_bytes=VMEM_LIMIT)


def _swiglu_half_step(x, g_ref, wi_ref, wo_ref, o_ref, acc_ref):
    xn = _rms_norm(x, g_ref[...]).astype(BF16)
    for c in range(N_FF_TILES):
        gate = _dot(xn, wi_ref[:, c * FF_TILE:(c + 1) * FF_TILE])
        up = _dot(xn, wi_ref[:, D_FF + c * FF_TILE:D_FF + (c + 1) * FF_TILE])
        act = (gate * _sigmoid(gate) * up).astype(BF16)
        part = _dot(act, wo_ref[c * FF_TILE:(c + 1) * FF_TILE, :])
        if c == 0:
            acc_ref[...] = part
        else:
            acc_ref[...] += part
    o_ref[...] = x + 0.5 * acc_ref[...]


def _ffn_body(x_ref, g_ref, wi_ref, wo_ref, o_ref, acc_ref):
    _swiglu_half_step(x_ref[...], g_ref, wi_ref, wo_ref, o_ref, acc_ref)


def _mix_out_ffn_body(x_ref, yc_ref, yh_ref, yf_ref, wc_ref, wh_ref, wf_ref,
                      g_ref, wi_ref, wo_ref, o_ref, acc_ref):
    x = (x_ref[...] + _dot(yc_ref[...], wc_ref[...]) + _dot(yh_ref[...], wh_ref[...])
         + _dot(yf_ref[...], wf_ref[...]))
    _swiglu_half_step(x, g_ref, wi_ref, wo_ref, o_ref, acc_ref)


def _mix_out_ffn(x2d, yc, yh, yf, wc, wh, wf, gain, wi, wo, layer):
    n = x2d.shape[0]
    tm = min(FFN_ROWS, n)

    def row(w):
        return pl.BlockSpec((tm, w), lambda i: (i, 0))

    consts = [wc, wh, wf, gain]
    return pl.pallas_call(
        _mix_out_ffn_body,
        grid=(n // tm,),
        in_specs=[row(D_MODEL), row(CONV_CH), row(HGRN_W), row(FOX_W)]
        + [_const_spec(c.shape) for c in consts]
        + [_layer_spec(wi.shape, layer), _layer_spec(wo.shape, layer)],
        out_specs=row(D_MODEL),
        out_shape=jax.ShapeDtypeStruct((n, D_MODEL), F32),
        scratch_shapes=[pltpu.VMEM((tm, D_MODEL), F32)],
        compiler_params=_params(("arbitrary",)),
        name="mix_out_ffn",
    )(x2d, yc, yh, yf, *consts, wi, wo)


def _ffn(x2d, gain, wi, wo, layer):
    n = x2d.shape[0]
    tm = min(FFN_ROWS, n)
    row = pl.BlockSpec((tm, D_MODEL), lambda i: (i, 0))
    return pl.pallas_call(
        _ffn_body,
        grid=(n // tm,),
        in_specs=[row, _const_spec(gain.shape), _layer_spec(wi.shape, layer),
                  _layer_spec(wo.shape, layer)],
        out_specs=row,
        out_shape=jax.ShapeDtypeStruct((n, D_MODEL), F32),
        scratch_shapes=[pltpu.VMEM((tm, D_MODEL), F32)],
        compiler_params=_params(("arbitrary",)),
        name="ffn",
    )(x2d, gain, wi, wo)


def _mix_in_body(x_ref, g_ref, wc_ref, cw_ref, wh_ref, wq_ref, wk_ref, wvt_ref, wf_ref,
                 fb_ref, gq_ref, gk_ref, tri_ref, place_ref,
                 yc_ref, hh_ref, qa_ref, ka_ref, vat_ref, cend_ref, ubuf_ref, carry_ref, *, tm):
    @pl.when(pl.program_id(1) == 0)
    def _():
        ubuf_ref[0:SUBLANES, :] = jnp.zeros((SUBLANES, CONV_CH), F32)
        carry_ref[...] = jnp.zeros_like(carry_ref)

    xn = _rms_norm(x_ref[0], g_ref[...]).astype(BF16)

    lf = _log_sigmoid(_dot(xn, wf_ref[...]) + fb_ref[...])
    hq = _dot(xn, wq_ref[...])
    hk = _dot(xn, wk_ref[...])
    lf3 = jnp.concatenate(_split3(lf), axis=1)
    last = carry_ref[...]
    blocks, ends = [], []
    for n in range(tm // KV_TILE):
        part = _dot(tri_ref[...], lf3[n * KV_TILE:(n + 1) * KV_TILE])
        blocks.append(last + (part[:, 0:LANES] + part[:, LANES:2 * LANES] + part[:, 2 * LANES:]))
        last = blocks[-1][KV_TILE - 1:KV_TILE, :]
        ends.append(last)
    c = jnp.concatenate(blocks, axis=0)
    carry_ref[...] = last
    cend_ref[0, 0] = jnp.concatenate(
        ends + [jnp.zeros((SUBLANES - len(ends), LANES), F32)], axis=0)
    vt = _dot_nt(wvt_ref[...], xn).astype(BF16)

    lane = lax.broadcasted_iota(jnp.int32, (1, LANES), 1)
    c_hi, c_mid, c_lo = [piece.astype(F32) for piece in _split3(c * LOG2E)]
    packed = jnp.where(lane < FOX_HEADS, c_hi,
                       jnp.where(lane < 2 * FOX_HEADS, pltpu.roll(c_mid, FOX_HEADS, axis=1),
                                 pltpu.roll(c_lo, 2 * FOX_HEADS, axis=1)))
    placed = _dot(packed.astype(BF16), place_ref[...])

    low = lane < HEAD_DIM
    q_takes_c = jnp.logical_and(lane >= Q_C_LANE, lane < Q_C_LANE + N_PIECES)
    k_takes_c = jnp.logical_and(lane >= K_ONE_LANE + N_PIECES, lane < K_ONE_LANE + 2 * N_PIECES)
    one_q = k_takes_c.astype(F32)
    one_k = q_takes_c.astype(F32)

    def head_pair_norm(x2, gain2):
        sq = x2 * x2
        ss_lo = jnp.sum(jnp.where(low, sq, 0.0), axis=-1, keepdims=True)
        ss_hi = jnp.sum(jnp.where(low, 0.0, sq), axis=-1, keepdims=True)
        inv = lax.rsqrt(jnp.where(low, ss_lo, ss_hi) * (1.0 / HEAD_DIM) + EPS)
        return x2 * inv * gain2

    for pair in range(FOX_HEADS // 2):
        cols = slice(pair * LANES, (pair + 1) * LANES)
        qn = head_pair_norm(hq[:, cols], gq_ref[...])
        kn = head_pair_norm(hk[:, cols], gk_ref[...])
        for half in range(2):
            h = 2 * pair + half
            extra = placed[:, h * LANES:(h + 1) * LANES]
            q_h = qn if half == 0 else pltpu.roll(qn, HEAD_DIM, axis=1)
            k_h = kn if half == 0 else pltpu.roll(kn, HEAD_DIM, axis=1)
            qa_ref[0, h] = jnp.where(low, q_h, jnp.where(q_takes_c, extra, one_q)).astype(BF16)
            ka_ref[0, h] = jnp.where(low, k_h, jnp.where(k_takes_c, extra, one_k)).astype(BF16)

    tail = (lax.broadcasted_iota(jnp.int32, (LANES - HEAD_DIM, KV_TILE), 0) == 0).astype(BF16)
    for h in range(FOX_HEADS):
        for n in range(tm // KV_TILE):
            vat_ref[0, h, n] = jnp.concatenate(
                [vt[h * HEAD_DIM:(h + 1) * HEAD_DIM, n * KV_TILE:(n + 1) * KV_TILE], tail], axis=0)

    hc = _dot(xn, wc_ref[...])
    u = hc[:, 2 * CONV_CH:3 * CONV_CH] * hc[:, 0:CONV_CH]
    ubuf_ref[SUBLANES:SUBLANES + tm, :] = u
    u1 = ubuf_ref[SUBLANES - 1:SUBLANES - 1 + tm, :]
    u2 = ubuf_ref[SUBLANES - 2:SUBLANES - 2 + tm, :]
    cw = cw_ref[...]
    conv = cw[0:1, :] * u2 + cw[1:2, :] * u1 + cw[2:3, :] * u
    yc_ref[0] = (hc[:, CONV_CH:2 * CONV_CH] * conv).astype(BF16)
    ubuf_ref[0:SUBLANES, :] = u[tm - SUBLANES:tm, :]

    hh_ref[0] = _dot(xn, wh_ref[...])


def _mix_in(x, p):
    b, l, _ = x.shape
    tm = min(ROW_TILE, l)
    nt = l // tm
    consts = [p["mix_norm"], p["wc"], p["conv_w"], p["wh"], p["wq"], p["wk"], p["wvt"], p["wf"],
              p["fbias"], p["gq"], p["gk"], p["tri"], p["place"]]
    out_shape = [
        jax.ShapeDtypeStruct((b, l, CONV_CH), BF16),
        jax.ShapeDtypeStruct((b, l, 4 * HGRN_W), F32),
        jax.ShapeDtypeStruct((b, FOX_HEADS, l, LANES), BF16),
        jax.ShapeDtypeStruct((b, FOX_HEADS, l, LANES), BF16),
        jax.ShapeDtypeStruct((b, FOX_HEADS, l // KV_TILE, LANES, KV_TILE), BF16),
        jax.ShapeDtypeStruct((b, nt, SUBLANES, LANES), F32),
    ]
    out_specs = [
        pl.BlockSpec((1, tm, CONV_CH), lambda bi, i: (bi, i, 0)),
        pl.BlockSpec((1, tm, 4 * HGRN_W), lambda bi, i: (bi, i, 0)),
        pl.BlockSpec((1, FOX_HEADS, tm, LANES), lambda bi, i: (bi, 0, i, 0)),
        pl.BlockSpec((1, FOX_HEADS, tm, LANES), lambda bi, i: (bi, 0, i, 0)),
        pl.BlockSpec((1, FOX_HEADS, tm // KV_TILE, LANES, KV_TILE), lambda bi, i: (bi, 0, i, 0, 0)),
        pl.BlockSpec((1, 1, SUBLANES, LANES), lambda bi, i: (bi, i, 0, 0)),
    ]
    return pl.pallas_call(
        functools.partial(_mix_in_body, tm=tm),
        grid=(b, nt),
        in_specs=[pl.BlockSpec((1, tm, D_MODEL), lambda bi, i: (bi, i, 0))]
        + [_const_spec(c.shape) for c in consts],
        out_specs=out_specs,
        out_shape=out_shape,
        scratch_shapes=[pltpu.VMEM((tm + SUBLANES, CONV_CH), F32), pltpu.VMEM((1, LANES), F32)],
        compiler_params=_params(("arbitrary", "arbitrary")),
        name="mix_in",
    )(x, *consts)


def _hgrn_body(q_ref, z_ref, v_ref, g_ref, lbl_ref, gain_ref, cum_ref, rep_ref, bd_ref, bmask_ref,
               o_ref, st_ref, vv_ref, lb_ref, *, layer, rows):
    @pl.when(pl.program_id(1) == 0)
    def _():
        st_ref[...] = jnp.zeros_like(st_ref)
        lbl = lbl_ref[...]
        e = jnp.exp(lbl - jnp.max(lbl, axis=0, keepdims=True))
        soft = e / jnp.sum(e, axis=0, keepdims=True)
        lb_ref[...] = jnp.clip(
            jnp.sum(soft[0:layer + 1, :], axis=0, keepdims=True) - soft[0:1, :], 0.0, 1.0)

    lb = lb_ref[...]
    bd = bd_ref[...]
    n_sub = CHUNK // SUB
    sub_t = lax.broadcasted_iota(jnp.int32, (CHUNK, HGRN_W), 0) // SUB
    sub_s = lax.broadcasted_iota(jnp.int32, (CHUNK, HGRN_W), 1) % CHUNK // SUB

    def within_chunk(ch, slot):
        rs = pl.ds(pl.multiple_of(ch * CHUNK, CHUNK), CHUNK)
        q = q_ref[0, rs, :]
        z = z_ref[0, rs, :]
        v = v_ref[0, rs, :]
        lf = (_log_sigmoid(z) + jnp.log(1.0 + lb * jnp.exp(-z))) * LOG2E
        k = (1.0 - lb) * _sigmoid(-z)
        pieces = jnp.concatenate(_split3(lf), axis=0)
        sums = _dot(cum_ref[0:3 * CHUNK, :], pieces)
        vv_ref[slot] = v
        v16 = v.astype(BF16)
        k16 = k.astype(BF16)
        yield

        bc = sums[0:CHUNK]
        bs = sums[CHUNK:2 * CHUNK]
        be = sums[2 * CHUNK:3 * CHUNK]

        q_t = q * jnp.exp2(bc - bs)
        k_t = (k * jnp.exp2(be - bc)).astype(BF16)
        lags = [q_t]
        for lag in range(2, n_sub):
            be_shift = jnp.concatenate([jnp.zeros((lag * SUB, HGRN_W), F32),
                                        be[0:CHUNK - lag * SUB]], axis=0)
            lags.append(q_t * jnp.exp2(bs - be_shift))
        q_lags = jnp.concatenate(lags, axis=0).astype(BF16)
        sc = _dot_nt(q_lags, jnp.concatenate([k_t] * HGRN_HEADS, axis=0) * bd)

        half_rows = CHUNK * SUB // 2
        pair = jnp.concatenate(
            [_dot(cum_ref[3 * CHUNK + n * half_rows:3 * CHUNK + (n + 1) * half_rows, :], pieces)
             for n in range(2)], axis=0)
        k_rep = jnp.concatenate(
            [_dot(rep_ref[n * half_rows:(n + 1) * half_rows, :], k16) for n in range(2)], axis=0)
        yield

        a_lag = jnp.zeros((CHUNK, HGRN_W), F32)
        for n, lag in enumerate(range(1, n_sub)):
            a_lag = jnp.where(sub_t - sub_s == lag, sc[n * CHUNK:(n + 1) * CHUNK], a_lag)
        o_lag = _dot(a_lag.astype(BF16), jnp.concatenate([v16] * HGRN_HEADS, axis=0) * bd)
        a = []
        for i in range(n_sub):
            rows_i = slice(i * SUB * SUB, (i + 1) * SUB * SUB)
            q_rep = jnp.concatenate([q[i * SUB:(i + 1) * SUB]] * SUB, axis=0)
            a.append((q_rep * jnp.exp2(pair[rows_i]) * k_rep[rows_i]).astype(BF16))
        r = jnp.concatenate(
            [_dot(jnp.concatenate(a[n * n_sub // 2:(n + 1) * n_sub // 2], axis=0), bd)
             for n in range(2)], axis=0)
        b_last = bc[CHUNK - 1:CHUNK, :]
        q_hat = (q * jnp.exp2(bc)).astype(BF16)
        k_hat = (k * jnp.exp2(b_last - bc)).astype(BF16)
        yield

        o_sub = []
        for i in range(n_sub):
            acc = None
            for sl in range(SUB):
                s = i * SUB + sl
                term = r[s * SUB:(s + 1) * SUB] * vv_ref[slot, s:s + 1, :]
                acc = term if acc is None else acc + term
            o_sub.append(acc)
        o = o_lag + jnp.concatenate(o_sub, axis=0)
        return rs, o, q_hat, k_hat, v16, jnp.exp2(b_last)

    def finish(parts):
        updates = [_dot_tn(v16, k_hat) for _, _, _, k_hat, v16, _ in parts]
        st = st_ref[...]
        outs = []
        for (_, o, q_hat, _, _, decay_last), update in zip(parts, updates):
            outs.append(o + _dot_nt(q_hat, st.astype(BF16)))
            st = st * decay_last + bmask_ref[...] * update
        st_ref[...] = st

        means = []
        for o in outs:
            oo = o * o
            hi = oo.astype(BF16)
            lo = (oo - hi.astype(F32)).astype(BF16)
            means.append((_dot(hi, bd) + _dot(lo, bd)) * (1.0 / HEAD_DIM))
        for (rs, *_), o, ms in zip(parts, outs, means):
            gate = g_ref[0, rs, :]
            y = o * lax.rsqrt(ms + EPS) * gain_ref[...] * (gate * _sigmoid(gate))
            o_ref[0, rs, :] = y.astype(BF16)

    def chunk_group(cc, carry):
        gens = [within_chunk(CHUNKS_PER_TRIP * cc + slot, slot) for slot in range(CHUNKS_PER_TRIP)]
        parts = [None] * CHUNKS_PER_TRIP
        while any(part is None for part in parts):
            for slot, gen in enumerate(gens):
                if parts[slot] is None:
                    try:
                        next(gen)
                    except StopIteration as done:
                        parts[slot] = done.value
        finish(parts)
        return carry

    lax.fori_loop(0, rows // (CHUNKS_PER_TRIP * CHUNK), chunk_group, 0)


def _hgrn(hh, p, layer):
    b, l, _ = hh.shape
    rows = min(HGRN_ROWS, l)
    consts = [p["lb_logits"], p["hgrn_gain"], p["cum64"], p["rep64"], p["bd"], p["bmask"]]

    def section(k):
        return pl.BlockSpec((1, rows, HGRN_W), lambda bi, i: (bi, i, k))

    return pl.pallas_call(
        functools.partial(_hgrn_body, layer=layer, rows=rows),
        grid=(b, l // rows),
        in_specs=[section(0), section(1), section(2), section(3)]
        + [_const_spec(c.shape) for c in consts],
        out_specs=pl.BlockSpec((1, rows, HGRN_W), lambda bi, i: (bi, i, 0)),
        out_shape=jax.ShapeDtypeStruct((b, l, HGRN_W), BF16),
        scratch_shapes=[pltpu.VMEM((HGRN_W, HGRN_W), F32),
                        pltpu.VMEM((CHUNKS_PER_TRIP, CHUNK, HGRN_W), F32),
                        pltpu.VMEM((1, HGRN_W), F32)],
        compiler_params=_params(("arbitrary", "arbitrary")),
        name="hgrn",
    )(hh, hh, hh, hh, *consts)


def _fox_body(qkb_ref, cpre_ref, qa_ref, ka_ref, vat_ref, o_ref, m_ref, acc_ref, s_ref,
              *, tq, tk, n_pre):
    bi, hp, qi = pl.program_id(0), pl.program_id(1), pl.program_id(2)
    per_q = tq // tk
    row = lax.broadcasted_iota(jnp.int32, (tk, tq), 0)
    col = lax.broadcasted_iota(jnp.int32, (tk, tq), 1)

    def c_before(hh, n):
        return cpre_ref[(bi * FOX_HEADS + 2 * hp + hh) * n_pre + n]

    def keys(hh, j):
        return ka_ref[0, hh, pl.ds(pl.multiple_of(j * tk, tk), tk), :]

    def scores(hh, j):
        return _dot_nt(keys(hh, j), qa_ref[0, hh])

    def update(hh, j, s, mask=None, off=None, cols=slice(None)):
        if mask is not None:
            s = jnp.where(mask, s, MASK_VALUE)
        m_old = m_ref[hh, :, cols]
        tile_max = jnp.max(s, axis=0, keepdims=True)
        m_new = jnp.maximum(m_old, tile_max if off is None else tile_max + off)
        p = jnp.exp2(s - (m_new if off is None else m_new - off)).astype(BF16)
        acc_ref[hh, :, cols] = (acc_ref[hh, :, cols] * jnp.exp2(m_old - m_new)
                                + _dot(vat_ref[0, hh, j], p))
        m_ref[hh, :, cols] = m_new

    for hh in range(2):
        m_ref[hh] = jnp.full((1, tq), MASK_VALUE, F32)
        acc_ref[hh] = jnp.zeros((LANES, tq), F32)

    diag = [(hh, dj) for dj in range(per_q) for hh in range(2)]
    s_diag = [_dot_nt(keys(hh, qi * per_q + dj), qa_ref[0, hh, dj * tk:tq, :]) for hh, dj in diag]
    near = [(hh, qi * per_q - 1 - e) for e in range(N_NEAR) for hh in range(2)]
    s_near = [scores(hh, jnp.maximum(j, 0)) for hh, j in near]
    j_start = qi * per_q - 1 - N_NEAR
    j_first = jnp.maximum(j_start, 0)
    s_first = [scores(hh, j_first) for hh in range(2)]
    for (hh, dj), s in zip(diag, s_diag):
        update(hh, qi * per_q + dj, s, mask=(row <= col)[:, 0:tq - dj * tk], cols=slice(dj * tk, tq))
    for (hh, j), s in zip(near, s_near):
        update(hh, jnp.maximum(j, 0), s, off=jnp.where(j >= 0, 0.0, MASK_VALUE))

    def needed(j):
        out = None
        for hh in range(2):
            gap = (qkb_ref[0] + c_before(hh, qi * per_q) - c_before(hh, j + 1)
                   - jnp.min(m_ref[hh]))
            out = gap >= SKIP_BELOW if out is None else jnp.logical_or(out, gap >= SKIP_BELOW)
        return out.astype(jnp.int32)

    def cond(carry):
        j, go = carry
        return jnp.logical_and(j >= 0, go > 0)

    def stage(src, dst, j):
        j_next = jnp.maximum(j - 1, 0)
        for hh in range(2):
            s_ref[dst, hh] = scores(hh, j_next)
        for hh in range(2):
            update(hh, j, s_ref[src, hh])
        return jnp.where(j >= 1, needed(j_next), 0)

    def body(carry):
        j, _ = carry
        go = stage(0, 1, j)
        go = lax.cond(go > 0, lambda: stage(1, 0, j - 1), lambda: jnp.int32(0))
        return j - 2, go

    for hh in range(2):
        s_ref[0, hh] = s_first[hh]
    lax.while_loop(cond, body, (j_start, needed(j_first)))

    outs = []
    for hh in range(2):
        acc = acc_ref[hh]
        outs.append(acc[0:HEAD_DIM, :] / acc[HEAD_DIM:HEAD_DIM + 1, :])
    o_ref[0] = jnp.concatenate(outs, axis=0).T.astype(BF16)


def _fox(qkb, cpre, qa, ka, vat):
    b, _, l, _ = qa.shape
    tq, tk = min(Q_TILE, l), vat.shape[-1]
    smem = pl.BlockSpec(memory_space=pltpu.SMEM)
    return pl.pallas_call(
        functools.partial(_fox_body, tq=tq, tk=tk, n_pre=l // tk + 1),
        grid=(b, FOX_HEADS // 2, l // tq),
        in_specs=[
            smem, smem,
            pl.BlockSpec((1, 2, tq, LANES), lambda bi, hp, i: (bi, hp, i, 0)),
            pl.BlockSpec((1, 2, l, LANES), lambda bi, hp, i: (bi, hp, 0, 0)),
            pl.BlockSpec((1, 2, l // tk, LANES, tk), lambda bi, hp, i: (bi, hp, 0, 0, 0)),
        ],
        out_specs=pl.BlockSpec((1, tq, LANES), lambda bi, hp, i: (bi, i, hp)),
        out_shape=jax.ShapeDtypeStruct((b, l, FOX_W), BF16),
        scratch_shapes=[pltpu.VMEM((2, 1, tq), F32), pltpu.VMEM((2, LANES, tq), F32),
                        pltpu.VMEM((2, 2, tk, tq), F32)],
        compiler_params=_params(("arbitrary", "arbitrary", "arbitrary")),
        name="fox",
    )(qkb, cpre, qa, ka, vat)


def _pad_lanes(v, width=LANES):
    return jnp.pad(v, (0, width - v.shape[0])).reshape(1, width)


def _placement():
    m = np.zeros((LANES, FOX_HEADS * LANES), np.float32)
    for p in range(N_PIECES):
        for h in range(FOX_HEADS):
            m[p * FOX_HEADS + h, h * LANES + Q_C_LANE + p] = 1.0
            m[p * FOX_HEADS + h, h * LANES + K_ONE_LANE + N_PIECES + p] = -1.0
    return jnp.asarray(m, BF16)


def _tri(n):
    return jnp.asarray(np.tril(np.ones((n, n), np.float32)), BF16)


def _chunk_sums():
    t = np.arange(CHUNK)[:, None]
    r = np.arange(CHUNK)[None, :]
    first = (t // SUB) * SUB
    s_pair, t_pair = _sub_chunk_pairs()
    rows = [r <= t, r < first, r < first + SUB,
            (r > s_pair[:, None]) & (r <= t_pair[:, None])]
    m = np.concatenate(rows, axis=0).astype(np.float32)
    return jnp.asarray(np.tile(m, (1, N_PIECES)), BF16)


def _sub_chunk_pairs():
    s = np.repeat(np.arange(CHUNK), SUB)
    t = (s // SUB) * SUB + np.tile(np.arange(SUB), CHUNK)
    return s, t


def _causal_replicate():
    s, t = _sub_chunk_pairs()
    m = (np.arange(CHUNK)[None, :] == s[:, None]) & (t >= s)[:, None]
    return jnp.asarray(m.astype(np.float32), BF16)


def _layer_operands(l, mix_norm, w_mix_in, conv_w, hgrn_lb_logits, hgrn_out_gain,
                    fox_q_gain, fox_k_gain, fox_f_bias, w_mix_out):
    w = w_mix_in[l]
    o_h = 3 * CONV_CH
    o_q = o_h + 4 * HGRN_W
    o_k, o_v, o_f = o_q + FOX_W, o_q + 2 * FOX_W, o_q + 3 * FOX_W
    head_blocks = np.kron(np.eye(HGRN_HEADS, dtype=np.float32),
                          np.ones((HEAD_DIM, HEAD_DIM), np.float32))
    wo = w_mix_out[l].astype(BF16)
    q_scale = fox_q_gain[l] * (HEAD_DIM ** -0.5 * LOG2E)
    return {
        "mix_norm": mix_norm[l].reshape(1, D_MODEL),
        "wc": w[:, :o_h].astype(BF16),
        "conv_w": jnp.pad(conv_w[l], ((0, SUBLANES - CONV_WIDTH), (0, 0))),
        "wh": w[:, o_h:o_q].astype(BF16),
        "wq": w[:, o_q:o_k].astype(BF16),
        "wk": w[:, o_k:o_v].astype(BF16),
        "wvt": w[:, o_v:o_f].T.astype(BF16),
        "wf": jnp.pad(w[:, o_f:], ((0, 0), (0, LANES - FOX_HEADS))).astype(BF16),
        "fbias": _pad_lanes(fox_f_bias[l]),
        "gq": jnp.tile(q_scale, LANES // HEAD_DIM).reshape(1, LANES),
        "gk": jnp.tile(fox_k_gain[l], LANES // HEAD_DIM).reshape(1, LANES),
        "qkb": (1.01 * HEAD_DIM * jnp.max(jnp.abs(q_scale))
                * jnp.max(jnp.abs(fox_k_gain[l]))).reshape(1),
        "tri": _tri(KV_TILE),
        "place": _placement(),
        "lb_logits": hgrn_lb_logits,
        "hgrn_gain": jnp.tile(hgrn_out_gain[l], HGRN_HEADS).reshape(1, HGRN_W),
        "cum64": _chunk_sums(),
        "rep64": _causal_replicate(),
        "bd": jnp.asarray(head_blocks, BF16),
        "bmask": jnp.asarray(head_blocks),
        "wo_c": wo[:CONV_CH],
        "wo_h": wo[CONV_CH:CONV_CH + HGRN_W],
        "wo_f": wo[CONV_CH + HGRN_W:],
    }


def kernel(x, ffn1_norm, ffn1_w_in, ffn1_w_out, mix_norm, w_mix_in, conv_w, hgrn_lb_logits,
           hgrn_out_gain, fox_q_gain, fox_k_gain, fox_f_bias, w_mix_out, ffn2_norm, ffn2_w_in,
           ffn2_w_out):
    b, l, d = x.shape
    assert d == D_MODEL and l % ROW_TILE == 0 and ROW_TILE % KV_TILE == 0
    assert l % Q_TILE == 0 and Q_TILE % KV_TILE == 0 and (b * l) % min(FFN_ROWS, b * l) == 0
    assert l % HGRN_ROWS == 0 and HGRN_ROWS % (CHUNKS_PER_TRIP * CHUNK) == 0
    depth = ffn1_norm.shape[0]
    tm = min(ROW_TILE, l)
    n = b * l
    w1_in, w1_out = ffn1_w_in.astype(BF16), ffn1_w_out.astype(BF16)
    w2_in, w2_out = ffn2_w_in.astype(BF16), ffn2_w_out.astype(BF16)
    for layer in range(depth):
        p = _layer_operands(layer, mix_norm, w_mix_in, conv_w, hgrn_lb_logits, hgrn_out_gain,
                            fox_q_gain, fox_k_gain, fox_f_bias, w_mix_out)
        x2 = _ffn(x.reshape(n, d), ffn1_norm[layer].reshape(1, d), w1_in, w1_out, layer)
        yc, hh, qa, ka, vat, cend = _mix_in(x2.reshape(b, l, d), p)
        yh = _hgrn(hh, p, layer)
        cend = cend[:, :, :tm // KV_TILE, :FOX_HEADS].reshape(b, l // KV_TILE, FOX_HEADS)
        cpre = jnp.pad(cend.transpose(0, 2, 1), ((0, 0), (0, 0), (1, 0))) * LOG2E
        yf = _fox(p["qkb"], cpre.reshape(-1), qa, ka, vat)
        x2 = _mix_out_ffn(x2, yc.reshape(n, CONV_CH), yh.reshape(n, HGRN_W), yf.reshape(n, FOX_W),
                          p["wo_c"], p["wo_h"], p["wo_f"], ffn2_norm[layer].reshape(1, d),
                          w2_in, w2_out, layer)
        x = x2.reshape(b, l, d)
    return x
```

```python
import functools

import jax
import jax.numpy as jnp
import numpy as np
from jax import lax
from jax.experimental import pallas as pl
from jax.experimental.pallas import tpu as pltpu

F32 = jnp.float32
BF16 = jnp.bfloat16

D_MODEL = 1024
D_FF = 2816
HEAD_DIM = 64
CONV_CH = 256
CONV_WIDTH = 3
HGRN_HEADS = 4
HGRN_W = 256
FOX_HEADS = 8
FOX_W = 512
CHUNK = 64
EPS = 1e-6
MASK_VALUE = -1e30

LANES = 128
SUBLANES = 8
MXU_TILE = 256
FF_TILE = MXU_TILE
N_FF_TILES = D_FF // FF_TILE
ROW_TILE = 1024
FFN_ROWS = 1024
Q_TILE = 512
KV_TILE = 256
N_NEAR = 3
HGRN_ROWS = 1024
SUB = 8
CHUNKS_PER_TRIP = 8
VMEM_LIMIT = 56 * 1024 * 1024

Q_C_LANE = HEAD_DIM
K_ONE_LANE = HEAD_DIM
N_PIECES = 3
LOG2E = 1.4426950408889634
SKIP_BELOW = -138.0


def _dot(a, b):
    return jnp.dot(a, b, preferred_element_type=F32)


def _dot_nt(a, b):
    return lax.dot_general(a, b, (((1,), (1,)), ((), ())), preferred_element_type=F32)


def _dot_tn(a, b):
    return lax.dot_general(a, b, (((0,), (0,)), ((), ())), preferred_element_type=F32)


def _rms_norm(x, gain):
    inv = lax.rsqrt(jnp.mean(x * x, axis=-1, keepdims=True) + EPS)
    return x * inv * gain


def _sigmoid(x):
    return 1.0 / (1.0 + jnp.exp(-x))


def _log_sigmoid(x):
    return jnp.minimum(x, 0.0) - jnp.log(1.0 + jnp.exp(-jnp.abs(x)))


def _split3(x):
    hi = x.astype(BF16)
    r = x - hi.astype(F32)
    mid = r.astype(BF16)
    lo = (r - mid.astype(F32)).astype(BF16)
    return hi, mid, lo


def _const_spec(shape):
    nd = len(shape)
    return pl.BlockSpec(shape, lambda *_: (0,) * nd, pipeline_mode=pl.Buffered(1))


def _layer_spec(stacked_shape, layer):
    nd = len(stacked_shape)
    return pl.BlockSpec((None,) + tuple(stacked_shape[1:]), lambda *_: (layer,) + (0,) * (nd - 1),
                        pipeline_mode=pl.Buffered(1))


def _params(sem):
    return pltpu.CompilerParams(dimension_semantics=sem, vmem_limit_bytes=VMEM_LIMIT)


def _swiglu_half_step(x, g_ref, wi_ref, wo_ref, o_ref, acc_ref):
    xn = _rms_norm(x, g_ref[...]).astype(BF16)
    for c in range(N_FF_TILES):
        gate = _dot(xn, wi_ref[:, c * FF_TILE:(c + 1) * FF_TILE])
        up = _dot(xn, wi_ref[:, D_FF + c * FF_TILE:D_FF + (c + 1) * FF_TILE])
        act = (gate * _sigmoid(gate) * up).astype(BF16)
        part = _dot(act, wo_ref[c * FF_TILE:(c + 1) * FF_TILE, :])
        if c == 0:
            acc_ref[...] = part
        else:
            acc_ref[...] += part
    o_ref[...] = x + 0.5 * acc_ref[...]


def _ffn_body(x_ref, g_ref, wi_ref, wo_ref, o_ref, acc_ref):
    _swiglu_half_step(x_ref[...], g_ref, wi_ref, wo_ref, o_ref, acc_ref)


def _mix_out_ffn_body(x_ref, yc_ref, yh_ref, yf_ref, wc_ref, wh_ref, wf_ref,
                      g_ref, wi_ref, wo_ref, o_ref, acc_ref):
    x = (x_ref[...] + _dot(yc_ref[...], wc_ref[...]) + _dot(yh_ref[...], wh_ref[...])
         + _dot(yf_ref[...], wf_ref[...]))
    _swiglu_half_step(x, g_ref, wi_ref, wo_ref, o_ref, acc_ref)


def _mix_out_ffn(x2d, yc, yh, yf, wc, wh, wf, gain, wi, wo, layer):
    n = x2d.shape[0]
    tm = min(FFN_ROWS, n)

    def row(w):
        return pl.BlockSpec((tm, w), lambda i: (i, 0))

    consts = [wc, wh, wf, gain]
    return pl.pallas_call(
        _mix_out_ffn_body,
        grid=(n // tm,),
        in_specs=[row(D_MODEL), row(CONV_CH), row(HGRN_W), row(FOX_W)]
        + [_const_spec(c.shape) for c in consts]
        + [_layer_spec(wi.shape, layer), _layer_spec(wo.shape, layer)],
        out_specs=row(D_MODEL),
        out_shape=jax.ShapeDtypeStruct((n, D_MODEL), F32),
        scratch_shapes=[pltpu.VMEM((tm, D_MODEL), F32)],
        compiler_params=_params(("arbitrary",)),
        name="mix_out_ffn",
    )(x2d, yc, yh, yf, *consts, wi, wo)


def _ffn(x2d, gain, wi, wo, layer):
    n = x2d.shape[0]
    tm = min(FFN_ROWS, n)
    row = pl.BlockSpec((tm, D_MODEL), lambda i: (i, 0))
    return pl.pallas_call(
        _ffn_body,
        grid=(n // tm,),
        in_specs=[row, _const_spec(gain.shape), _layer_spec(wi.shape, layer),
                  _layer_spec(wo.shape, layer)],
        out_specs=row,
        out_shape=jax.ShapeDtypeStruct((n, D_MODEL), F32),
        scratch_shapes=[pltpu.VMEM((tm, D_MODEL), F32)],
        compiler_params=_params(("arbitrary",)),
        name="ffn",
    )(x2d, gain, wi, wo)


def _mix_in_body(x_ref, g_ref, wc_ref, cw_ref, wh_ref, wq_ref, wk_ref, wvt_ref, wf_ref,
                 fb_ref, gq_ref, gk_ref, tri_ref, place_ref,
                 yc_ref, hh_ref, qa_ref, ka_ref, vat_ref, cend_ref, ubuf_ref, carry_ref, *, tm):
    @pl.when(pl.program_id(1) == 0)
    def _():
        ubuf_ref[0:SUBLANES, :] = jnp.zeros((SUBLANES, CONV_CH), F32)
        carry_ref[...] = jnp.zeros_like(carry_ref)

    xn = _rms_norm(x_ref[0], g_ref[...]).astype(BF16)

    lf = _log_sigmoid(_dot(xn, wf_ref[...]) + fb_ref[...])
    hq = _dot(xn, wq_ref[...])
    hk = _dot(xn, wk_ref[...])
    lf3 = jnp.concatenate(_split3(lf), axis=1)
    last = carry_ref[...]
    blocks, ends = [], []
    for n in range(tm // KV_TILE):
        part = _dot(tri_ref[...], lf3[n * KV_TILE:(n + 1) * KV_TILE])
        blocks.append(last + (part[:, 0:LANES] + part[:, LANES:2 * LANES] + part[:, 2 * LANES:]))
        last = blocks[-1][KV_TILE - 1:KV_TILE, :]
        ends.append(last)
    c = jnp.concatenate(blocks, axis=0)
    carry_ref[...] = last
    cend_ref[0, 0] = jnp.concatenate(
        ends + [jnp.zeros((SUBLANES - len(ends), LANES), F32)], axis=0)
    vt = _dot_nt(wvt_ref[...], xn).astype(BF16)

    lane = lax.broadcasted_iota(jnp.int32, (1, LANES), 1)
    c_hi, c_mid, c_lo = [piece.astype(F32) for piece in _split3(c * LOG2E)]
    packed = jnp.where(lane < FOX_HEADS, c_hi,
                       jnp.where(lane < 2 * FOX_HEADS, pltpu.roll(c_mid, FOX_HEADS, axis=1),
                                 pltpu.roll(c_lo, 2 * FOX_HEADS, axis=1)))
    placed = _dot(packed.astype(BF16), place_ref[...])

    low = lane < HEAD_DIM
    q_takes_c = jnp.logical_and(lane >= Q_C_LANE, lane < Q_C_LANE + N_PIECES)
    k_takes_c = jnp.logical_and(lane >= K_ONE_LANE + N_PIECES, lane < K_ONE_LANE + 2 * N_PIECES)
    one_q = k_takes_c.astype(F32)
    one_k = q_takes_c.astype(F32)

    def head_pair_norm(x2, gain2):
        sq = x2 * x2
        ss_lo = jnp.sum(jnp.where(low, sq, 0.0), axis=-1, keepdims=True)
        ss_hi = jnp.sum(jnp.where(low, 0.0, sq), axis=-1, keepdims=True)
        inv = lax.rsqrt(jnp.where(low, ss_lo, ss_hi) * (1.0 / HEAD_DIM) + EPS)
        return x2 * inv * gain2

    for pair in range(FOX_HEADS // 2):
        cols = slice(pair * LANES, (pair + 1) * LANES)
        qn = head_pair_norm(hq[:, cols], gq_ref[...])
        kn = head_pair_norm(hk[:, cols], gk_ref[...])
        for half in range(2):
            h = 2 * pair + half
            extra = placed[:, h * LANES:(h + 1) * LANES]
            q_h = qn if half == 0 else pltpu.roll(qn, HEAD_DIM, axis=1)
            k_h = kn if half == 0 else pltpu.roll(kn, HEAD_DIM, axis=1)
            qa_ref[0, h] = jnp.where(low, q_h, jnp.where(q_takes_c, extra, one_q)).astype(BF16)
            ka_ref[0, h] = jnp.where(low, k_h, jnp.where(k_takes_c, extra, one_k)).astype(BF16)

    tail = (lax.broadcasted_iota(jnp.int32, (LANES - HEAD_DIM, KV_TILE), 0) == 0).astype(BF16)
    for h in range(FOX_HEADS):
        for n in range(tm // KV_TILE):
            vat_ref[0, h, n] = jnp.concatenate(
                [vt[h * HEAD_DIM:(h + 1) * HEAD_DIM, n * KV_TILE:(n + 1) * KV_TILE], tail], axis=0)

    hc = _dot(xn, wc_ref[...])
    u = hc[:, 2 * CONV_CH:3 * CONV_CH] * hc[:, 0:CONV_CH]
    ubuf_ref[SUBLANES:SUBLANES + tm, :] = u
    u1 = ubuf_ref[SUBLANES - 1:SUBLANES - 1 + tm, :]
    u2 = ubuf_ref[SUBLANES - 2:SUBLANES - 2 + tm, :]
    cw = cw_ref[...]
    conv = cw[0:1, :] * u2 + cw[1:2, :] * u1 + cw[2:3, :] * u
    yc_ref[0] = (hc[:, CONV_CH:2 * CONV_CH] * conv).astype(BF16)
    ubuf_ref[0:SUBLANES, :] = u[tm - SUBLANES:tm, :]

    hh_ref[0] = _dot(xn, wh_ref[...])


def _mix_in(x, p):
    b, l, _ = x.shape
    tm = min(ROW_TILE, l)
    nt = l // tm
    consts = [p["mix_norm"], p["wc"], p["conv_w"], p["wh"], p["wq"], p["wk"], p["wvt"], p["wf"],
              p["fbias"], p["gq"], p["gk"], p["tri"], p["place"]]
    out_shape = [
        jax.ShapeDtypeStruct((b, l, CONV_CH), BF16),
        jax.ShapeDtypeStruct((b, l, 4 * HGRN_W), F32),
        jax.ShapeDtypeStruct((b, FOX_HEADS, l, LANES), BF16),
        jax.ShapeDtypeStruct((b, FOX_HEADS, l, LANES), BF16),
        jax.ShapeDtypeStruct((b, FOX_HEADS, l // KV_TILE, LANES, KV_TILE), BF16),
        jax.ShapeDtypeStruct((b, nt, SUBLANES, LANES), F32),
    ]
    out_specs = [
        pl.BlockSpec((1, tm, CONV_CH), lambda bi, i: (bi, i, 0)),
        pl.BlockSpec((1, tm, 4 * HGRN_W), lambda bi, i: (bi, i, 0)),
        pl.BlockSpec((1, FOX_HEADS, tm, LANES), lambda bi, i: (bi, 0, i, 0)),
        pl.BlockSpec((1, FOX_HEADS, tm, LANES), lambda bi, i: (bi, 0, i, 0)),
        pl.BlockSpec((1, FOX_HEADS, tm // KV_TILE, LANES, KV_TILE), lambda bi, i: (bi, 0, i, 0, 0)),
        pl.BlockSpec((1, 1, SUBLANES, LANES), lambda bi, i: (bi, i, 0, 0)),
    ]
    return pl.pallas_call(
        functools.partial(_mix_in_body, tm=tm),
        grid=(b, nt),
        in_specs=[pl.BlockSpec((1, tm, D_MODEL), lambda bi, i: (bi, i, 0))]
        + [_const_spec(c.shape) for c in consts],
        out_specs=out_specs,
        out_shape=out_shape,
        scratch_shapes=[pltpu.VMEM((tm + SUBLANES, CONV_CH), F32), pltpu.VMEM((1, LANES), F32)],
        compiler_params=_params(("arbitrary", "arbitrary")),
        name="mix_in",
    )(x, *consts)


def _hgrn_body(q_ref, z_ref, v_ref, g_ref, lbl_ref, gain_ref, cum_ref, rep_ref, bd_ref, bmask_ref,
               o_ref, st_ref, vv_ref, lb_ref, *, layer, rows):
    @pl.when(pl.program_id(1) == 0)
    def _():
        st_ref[...] = jnp.zeros_like(st_ref)
        lbl = lbl_ref[...]
        e = jnp.exp(lbl - jnp.max(lbl, axis=0, keepdims=True))
        soft = e / jnp.sum(e, axis=0, keepdims=True)
        lb_ref[...] = jnp.clip(
            jnp.sum(soft[0:layer + 1, :], axis=0, keepdims=True) - soft[0:1, :], 0.0, 1.0)

    lb = lb_ref[...]
    bd = bd_ref[...]
    n_sub = CHUNK // SUB
    sub_t = lax.broadcasted_iota(jnp.int32, (CHUNK, HGRN_W), 0) // SUB
    sub_s = lax.broadcasted_iota(jnp.int32, (CHUNK, HGRN_W), 1) % CHUNK // SUB

    def within_chunk(ch, slot):
        rs = pl.ds(pl.multiple_of(ch * CHUNK, CHUNK), CHUNK)
        q = q_ref[0, rs, :]
        z = z_ref[0, rs, :]
        v = v_ref[0, rs, :]
        lf = (_log_sigmoid(z) + jnp.log(1.0 + lb * jnp.exp(-z))) * LOG2E
        k = (1.0 - lb) * _sigmoid(-z)
        pieces = jnp.concatenate(_split3(lf), axis=0)
        sums = _dot(cum_ref[0:3 * CHUNK, :], pieces)
        vv_ref[slot] = v
        v16 = v.astype(BF16)
        k16 = k.astype(BF16)
        yield

        bc = sums[0:CHUNK]
        bs = sums[CHUNK:2 * CHUNK]
        be = sums[2 * CHUNK:3 * CHUNK]

        q_t = q * jnp.exp2(bc - bs)
        k_t = (k * jnp.exp2(be - bc)).astype(BF16)
        lags = [q_t]
        for lag in range(2, n_sub):
            be_shift = jnp.concatenate([jnp.zeros((lag * SUB, HGRN_W), F32),
                                        be[0:CHUNK - lag * SUB]], axis=0)
            lags.append(q_t * jnp.exp2(bs - be_shift))
        q_lags = jnp.concatenate(lags, axis=0).astype(BF16)
        sc = _dot_nt(q_lags, jnp.concatenate([k_t] * HGRN_HEADS, axis=0) * bd)

        half_rows = CHUNK * SUB // 2
        pair = jnp.concatenate(
            [_dot(cum_ref[3 * CHUNK + n * half_rows:3 * CHUNK + (n + 1) * half_rows, :], pieces)
             for n in range(2)], axis=0)
        k_rep = jnp.concatenate(
            [_dot(rep_ref[n * half_rows:(n + 1) * half_rows, :], k16) for n in range(2)], axis=0)
        yield

        a_lag = jnp.zeros((CHUNK, HGRN_W), F32)
        for n, lag in enumerate(range(1, n_sub)):
            a_lag = jnp.where(sub_t - sub_s == lag, sc[n * CHUNK:(n + 1) * CHUNK], a_lag)
        o_lag = _dot(a_lag.astype(BF16), jnp.concatenate([v16] * HGRN_HEADS, axis=0) * bd)
        a = []
        for i in range(n_sub):
            rows_i = slice(i * SUB * SUB, (i + 1) * SUB * SUB)
            q_rep = jnp.concatenate([q[i * SUB:(i + 1) * SUB]] * SUB, axis=0)
            a.append((q_rep * jnp.exp2(pair[rows_i]) * k_rep[rows_i]).astype(BF16))
        r = jnp.concatenate(
            [_dot(jnp.concatenate(a[n * n_sub // 2:(n + 1) * n_sub // 2], axis=0), bd)
             for n in range(2)], axis=0)
        b_last = bc[CHUNK - 1:CHUNK, :]
        q_hat = (q * jnp.exp2(bc)).astype(BF16)
        k_hat = (k * jnp.exp2(b_last - bc)).astype(BF16)
        yield

        o_sub = []
        for i in range(n_sub):
            acc = None
            for sl in range(SUB):
                s = i * SUB + sl
                term = r[s * SUB:(s + 1) * SUB] * vv_ref[slot, s:s + 1, :]
                acc = term if acc is None else acc + term
            o_sub.append(acc)
        o = o_lag + jnp.concatenate(o_sub, axis=0)
        return rs, o, q_hat, k_hat, v16, jnp.exp2(b_last)

    def finish(parts):
        updates = [_dot_tn(v16, k_hat) for _, _, _, k_hat, v16, _ in parts]
        st = st_ref[...]
        outs = []
        for (_, o, q_hat, _, _, decay_last), update in zip(parts, updates):
            outs.append(o + _dot_nt(q_hat, st.astype(BF16)))
            st = st * decay_last + bmask_ref[...] * update
        st_ref[...] = st

        means = []
        for o in outs:
            oo = o * o
            hi = oo.astype(BF16)
            lo = (oo - hi.astype(F32)).astype(BF16)
            means.append((_dot(hi, bd) + _dot(lo, bd)) * (1.0 / HEAD_DIM))
        for (rs, *_), o, ms in zip(parts, outs, means):
            gate = g_ref[0, rs, :]
            y = o * lax.rsqrt(ms + EPS) * gain_ref[...] * (gate * _sigmoid(gate))
            o_ref[0, rs, :] = y.astype(BF16)

    def chunk_group(cc, carry):
        gens = [within_chunk(CHUNKS_PER_TRIP * cc + slot, slot) for slot in range(CHUNKS_PER_TRIP)]
        parts = [None] * CHUNKS_PER_TRIP
        while any(part is None for part in parts):
            for slot, gen in enumerate(gens):
                if parts[slot] is None:
                    try:
                        next(gen)
                    except StopIteration as done:
                        parts[slot] = done.value
        finish(parts)
        return carry

    lax.fori_loop(0, rows // (CHUNKS_PER_TRIP * CHUNK), chunk_group, 0)


def _hgrn(hh, p, layer):
    b, l, _ = hh.shape
    rows = min(HGRN_ROWS, l)
    consts = [p["lb_logits"], p["hgrn_gain"], p["cum64"], p["rep64"], p["bd"], p["bmask"]]

    def section(k):
        return pl.BlockSpec((1, rows, HGRN_W), lambda bi, i: (bi, i, k))

    return pl.pallas_call(
        functools.partial(_hgrn_body, layer=layer, rows=rows),
        grid=(b, l // rows),
        in_specs=[section(0), section(1), section(2), section(3)]
        + [_const_spec(c.shape) for c in consts],
        out_specs=pl.BlockSpec((1, rows, HGRN_W), lambda bi, i: (bi, i, 0)),
        out_shape=jax.ShapeDtypeStruct((b, l, HGRN_W), BF16),
        scratch_shapes=[pltpu.VMEM((HGRN_W, HGRN_W), F32),
                        pltpu.VMEM((CHUNKS_PER_TRIP, CHUNK, HGRN_W), F32),
                        pltpu.VMEM((1, HGRN_W), F32)],
        compiler_params=_params(("arbitrary", "arbitrary")),
        name="hgrn",
    )(hh, hh, hh, hh, *consts)


def _fox_body(qkb_ref, cpre_ref, qa_ref, ka_ref, vat_ref, o_ref, m_ref, acc_ref, s_ref,
              *, tq, tk, n_pre):
    bi, hp, qi = pl.program_id(0), pl.program_id(1), pl.program_id(2)
    per_q = tq // tk
    row = lax.broadcasted_iota(jnp.int32, (tk, tq), 0)
    col = lax.broadcasted_iota(jnp.int32, (tk, tq), 1)

    def c_before(hh, n):
        return cpre_ref[(bi * FOX_HEADS + 2 * hp + hh) * n_pre + n]

    def keys(hh, j):
        return ka_ref[0, hh, pl.ds(pl.multiple_of(j * tk, tk), tk), :]

    def scores(hh, j):
        return _dot_nt(keys(hh, j), qa_ref[0, hh])

    def update(hh, j, s, mask=None, off=None, cols=slice(None)):
        if mask is not None:
            s = jnp.where(mask, s, MASK_VALUE)
        m_old = m_ref[hh, :, cols]
        tile_max = jnp.max(s, axis=0, keepdims=True)
        m_new = jnp.maximum(m_old, tile_max if off is None else tile_max + off)
        p = jnp.exp2(s - (m_new if off is None else m_new - off)).astype(BF16)
        acc_ref[hh, :, cols] = (acc_ref[hh, :, cols] * jnp.exp2(m_old - m_new)
                                + _dot(vat_ref[0, hh, j], p))
        m_ref[hh, :, cols] = m_new

    for hh in range(2):
        m_ref[hh] = jnp.full((1, tq), MASK_VALUE, F32)
        acc_ref[hh] = jnp.zeros((LANES, tq), F32)

    diag = [(hh, dj) for dj in range(per_q) for hh in range(2)]
    s_diag = [_dot_nt(keys(hh, qi * per_q + dj), qa_ref[0, hh, dj * tk:tq, :]) for hh, dj in diag]
    near = [(hh, qi * per_q - 1 - e) for e in range(N_NEAR) for hh in range(2)]
    s_near = [scores(hh, jnp.maximum(j, 0)) for hh, j in near]
    j_start = qi * per_q - 1 - N_NEAR
    j_first = jnp.maximum(j_start, 0)
    s_first = [scores(hh, j_first) for hh in range(2)]
    for (hh, dj), s in zip(diag, s_diag):
        update(hh, qi * per_q + dj, s, mask=(row <= col)[:, 0:tq - dj * tk], cols=slice(dj * tk, tq))
    for (hh, j), s in zip(near, s_near):
        update(hh, jnp.maximum(j, 0), s, off=jnp.where(j >= 0, 0.0, MASK_VALUE))

    def needed(j):
        out = None
        for hh in range(2):
            gap = (qkb_ref[0] + c_before(hh, qi * per_q) - c_before(hh, j + 1)
                   - jnp.min(m_ref[hh]))
            out = gap >= SKIP_BELOW if out is None else jnp.logical_or(out, gap >= SKIP_BELOW)
        return out.astype(jnp.int32)

    def cond(carry):
        j, go = carry
        return jnp.logical_and(j >= 0, go > 0)

    def stage(src, dst, j):
        j_next = jnp.maximum(j - 1, 0)
        for hh in range(2):
            s_ref[dst, hh] = scores(hh, j_next)
        for hh in range(2):
            update(hh, j, s_ref[src, hh])
        return jnp.where(j >= 1, needed(j_next), 0)

    def body(carry):
        j, _ = carry
        go = stage(0, 1, j)
        go = lax.cond(go > 0, lambda: stage(1, 0, j - 1), lambda: jnp.int32(0))
        return j - 2, go

    for hh in range(2):
        s_ref[0, hh] = s_first[hh]
    lax.while_loop(cond, body, (j_start, needed(j_first)))

    outs = []
    for hh in range(2):
        acc = acc_ref[hh]
        outs.append(acc[0:HEAD_DIM, :] / acc[HEAD_DIM:HEAD_DIM + 1, :])
    o_ref[0] = jnp.concatenate(outs, axis=0).T.astype(BF16)


def _fox(qkb, cpre, qa, ka, vat):
    b, _, l, _ = qa.shape
    tq, tk = min(Q_TILE, l), vat.shape[-1]
    smem = pl.BlockSpec(memory_space=pltpu.SMEM)
    return pl.pallas_call(
        functools.partial(_fox_body, tq=tq, tk=tk, n_pre=l // tk + 1),
        grid=(b, FOX_HEADS // 2, l // tq),
        in_specs=[
            smem, smem,
            pl.BlockSpec((1, 2, tq, LANES), lambda bi, hp, i: (bi, hp, i, 0)),
            pl.BlockSpec((1, 2, l, LANES), lambda bi, hp, i: (bi, hp, 0, 0)),
            pl.BlockSpec((1, 2, l // tk, LANES, tk), lambda bi, hp, i: (bi, hp, 0, 0, 0)),
        ],
        out_specs=pl.BlockSpec((1, tq, LANES), lambda bi, hp, i: (bi, i, hp)),
        out_shape=jax.ShapeDtypeStruct((b, l, FOX_W), BF16),
        scratch_shapes=[pltpu.VMEM((2, 1, tq), F32), pltpu.VMEM((2, LANES, tq), F32),
                        pltpu.VMEM((2, 2, tk, tq), F32)],
        compiler_params=_params(("arbitrary", "arbitrary", "arbitrary")),
        name="fox",
    )(qkb, cpre, qa, ka, vat)


def _pad_lanes(v, width=LANES):
    return jnp.pad(v, (0, width - v.shape[0])).reshape(1, width)


def _placement():
    m = np.zeros((LANES, FOX_HEADS * LANES), np.float32)
    for p in range(N_PIECES):
        for h in range(FOX_HEADS):
            m[p * FOX_HEADS + h, h * LANES + Q_C_LANE + p] = 1.0
            m[p * FOX_HEADS + h, h * LANES + K_ONE_LANE + N_PIECES + p] = -1.0
    return jnp.asarray(m, BF16)


def _tri(n):
    return jnp.asarray(np.tril(np.ones((n, n), np.float32)), BF16)


def _chunk_sums():
    t = np.arange(CHUNK)[:, None]
    r = np.arange(CHUNK)[None, :]
    first = (t // SUB) * SUB
    s_pair, t_pair = _sub_chunk_pairs()
    rows = [r <= t, r < first, r < first + SUB,
            (r > s_pair[:, None]) & (r <= t_pair[:, None])]
    m = np.concatenate(rows, axis=0).astype(np.float32)
    return jnp.asarray(np.tile(m, (1, N_PIECES)), BF16)


def _sub_chunk_pairs():
    s = np.repeat(np.arange(CHUNK), SUB)
    t = (s // SUB) * SUB + np.tile(np.arange(SUB), CHUNK)
    return s, t


def _causal_replicate():
    s, t = _sub_chunk_pairs()
    m = (np.arange(CHUNK)[None, :] == s[:, None]) & (t >= s)[:, None]
    return jnp.asarray(m.astype(np.float32), BF16)


def _layer_operands(l, mix_norm, w_mix_in, conv_w, hgrn_lb_logits, hgrn_out_gain,
                    fox_q_gain, fox_k_gain, fox_f_bias, w_mix_out):
    w = w_mix_in[l]
    o_h = 3 * CONV_CH
    o_q = o_h + 4 * HGRN_W
    o_k, o_v, o_f = o_q + FOX_W, o_q + 2 * FOX_W, o_q + 3 * FOX_W
    head_blocks = np.kron(np.eye(HGRN_HEADS, dtype=np.float32),
                          np.ones((HEAD_DIM, HEAD_DIM), np.float32))
    wo = w_mix_out[l].astype(BF16)
    q_scale = fox_q_gain[l] * (HEAD_DIM ** -0.5 * LOG2E)
    return {
        "mix_norm": mix_norm[l].reshape(1, D_MODEL),
        "wc": w[:, :o_h].astype(BF16),
        "conv_w": jnp.pad(conv_w[l], ((0, SUBLANES - CONV_WIDTH), (0, 0))),
        "wh": w[:, o_h:o_q].astype(BF16),
        "wq": w[:, o_q:o_k].astype(BF16),
        "wk": w[:, o_k:o_v].astype(BF16),
        "wvt": w[:, o_v:o_f].T.astype(BF16),
        "wf": jnp.pad(w[:, o_f:], ((0, 0), (0, LANES - FOX_HEADS))).astype(BF16),
        "fbias": _pad_lanes(fox_f_bias[l]),
        "gq": jnp.tile(q_scale, LANES // HEAD_DIM).reshape(1, LANES),
        "gk": jnp.tile(fox_k_gain[l], LANES // HEAD_DIM).reshape(1, LANES),
        "qkb": (1.01 * HEAD_DIM * jnp.max(jnp.abs(q_scale))
                * jnp.max(jnp.abs(fox_k_gain[l]))).reshape(1),
        "tri": _tri(KV_TILE),
        "place": _placement(),
        "lb_logits": hgrn_lb_logits,
        "hgrn_gain": jnp.tile(hgrn_out_gain[l], HGRN_HEADS).reshape(1, HGRN_W),
        "cum64": _chunk_sums(),
        "rep64": _causal_replicate(),
        "bd": jnp.asarray(head_blocks, BF16),
        "bmask": jnp.asarray(head_blocks),
        "wo_c": wo[:CONV_CH],
        "wo_h": wo[CONV_CH:CONV_CH + HGRN_W],
        "wo_f": wo[CONV_CH + HGRN_W:],
    }


def kernel(x, ffn1_norm, ffn1_w_in, ffn1_w_out, mix_norm, w_mix_in, conv_w, hgrn_lb_logits,
           hgrn_out_gain, fox_q_gain, fox_k_gain, fox_f_bias, w_mix_out, ffn2_norm, ffn2_w_in,
           ffn2_w_out):
    b, l, d = x.shape
    assert d == D_MODEL and l % ROW_TILE == 0 and ROW_TILE % KV_TILE == 0
    assert l % Q_TILE == 0 and Q_TILE % KV_TILE == 0 and (b * l) % min(FFN_ROWS, b * l) == 0
    assert l % HGRN_ROWS == 0 and HGRN_ROWS % (CHUNKS_PER_TRIP * CHUNK) == 0
    depth = ffn1_norm.shape[0]
    tm = min(ROW_TILE, l)
    n = b * l
    w1_in, w1_out = ffn1_w_in.astype(BF16), ffn1_w_out.astype(BF16)
    w2_in, w2_out = ffn2_w_in.astype(BF16), ffn2_w_out.astype(BF16)
    for layer in range(depth):
        p = _layer_operands(layer, mix_norm, w_mix_in, conv_w, hgrn_lb_logits, hgrn_out_gain,
                            fox_q_gain, fox_k_gain, fox_f_bias, w_mix_out)
        x2 = _ffn(x.reshape(n, d), ffn1_norm[layer].reshape(1, d), w1_in, w1_out, layer)
        yc, hh, qa, ka, vat, cend = _mix_in(x2.reshape(b, l, d), p)
        yh = _hgrn(hh, p, layer)
        cend = cend[:, :, :tm // KV_TILE, :FOX_HEADS].reshape(b, l // KV_TILE, FOX_HEADS)
        cpre = jnp.pad(cend.transpose(0, 2, 1), ((0, 0), (0, 0), (1, 0))) * LOG2E
        yf = _fox(p["qkb"], cpre.reshape(-1), qa, ka, vat)
        x2 = _mix_out_ffn(x2, yc.reshape(n, CONV_CH), yh.reshape(n, HGRN_W), yf.reshape(n, FOX_W),
                          p["wo_c"], p["wo_h"], p["wo_f"], ffn2_norm[layer].reshape(1, d),
                          w2_in, w2_out, layer)
        x = x2.reshape(b, l, d)
    return x
```

```python
import functools

import jax
import jax.numpy as jnp
import numpy as np
from jax import lax
from jax.experimental import pallas as pl
from jax.experimental.pallas import tpu as pltpu

F32 = jnp.float32
BF16 = jnp.bfloat16

D_MODEL = 1024
D_FF = 2816
HEAD_DIM = 64
CONV_CH = 256
CONV_WIDTH = 3
HGRN_HEADS = 4
HGRN_W = 256
FOX_HEADS = 8
FOX_W = 512
CHUNK = 64
EPS = 1e-6
MASK_VALUE = -1e30

LANES = 128
SUBLANES = 8
MXU_TILE = 256
FF_TILE = MXU_TILE
N_FF_TILES = D_FF // FF_TILE
ROW_TILE = 1024
FFN_ROWS = 1024
Q_TILE = 512
KV_TILE = 256
N_NEAR = 2
HGRN_ROWS = 512
SUB = 8
CHUNKS_PER_TRIP = 8
VMEM_LIMIT = 56 * 1024 * 1024

Q_C_LANE = HEAD_DIM
K_ONE_LANE = HEAD_DIM
N_PIECES = 3
LOG2E = 1.4426950408889634
SKIP_BELOW = -138.0


def _dot(a, b):
    return jnp.dot(a, b, preferred_element_type=F32)


def _dot_nt(a, b):
    return lax.dot_general(a, b, (((1,), (1,)), ((), ())), preferred_element_type=F32)


def _dot_tn(a, b):
    return lax.dot_general(a, b, (((0,), (0,)), ((), ())), preferred_element_type=F32)


def _rms_norm(x, gain):
    inv = lax.rsqrt(jnp.mean(x * x, axis=-1, keepdims=True) + EPS)
    return x * inv * gain


def _sigmoid(x):
    return 1.0 / (1.0 + jnp.exp(-x))


def _log_sigmoid(x):
    return jnp.minimum(x, 0.0) - jnp.log(1.0 + jnp.exp(-jnp.abs(x)))


def _split3(x):
    hi = x.astype(BF16)
    r = x - hi.astype(F32)
    mid = r.astype(BF16)
    lo = (r - mid.astype(F32)).astype(BF16)
    return hi, mid, lo


def _const_spec(shape):
    nd = len(shape)
    return pl.BlockSpec(shape, lambda *_: (0,) * nd, pipeline_mode=pl.Buffered(1))


def _layer_spec(stacked_shape, layer):
    nd = len(stacked_shape)
    return pl.BlockSpec((None,) + tuple(stacked_shape[1:]), lambda *_: (layer,) + (0,) * (nd - 1),
                        pipeline_mode=pl.Buffered(1))


def _params(sem):
    return pltpu.CompilerParams(dimension_semantics=sem, vmem_limit_bytes=VMEM_LIMIT)


def _swiglu_half_step(x, g_ref, wi_ref, wo_ref, o_ref, acc_ref):
    xn = _rms_norm(x, g_ref[...]).astype(BF16)
    for c in range(N_FF_TILES):
        gate = _dot(xn, wi_ref[:, c * FF_TILE:(c + 1) * FF_TILE])
        up = _dot(xn, wi_ref[:, D_FF + c * FF_TILE:D_FF + (c + 1) * FF_TILE])
        act = (gate * _sigmoid(gate) * up).astype(BF16)
        part = _dot(act, wo_ref[c * FF_TILE:(c + 1) * FF_TILE, :])
        if c == 0:
            acc_ref[...] = part
        else:
            acc_ref[...] += part
    o_ref[...] = x + 0.5 * acc_ref[...]


def _ffn_body(x_ref, g_ref, wi_ref, wo_ref, o_ref, acc_ref):
    _swiglu_half_step(x_ref[...], g_ref, wi_ref, wo_ref, o_ref, acc_ref)


def _mix_out_ffn_body(x_ref, yc_ref, yh_ref, yf_ref, wc_ref, wh_ref, wf_ref,
                      g_ref, wi_ref, wo_ref, o_ref, acc_ref):
    x = (x_ref[...] + _dot(yc_ref[...], wc_ref[...]) + _dot(yh_ref[...], wh_ref[...])
         + _dot(yf_ref[...], wf_ref[...]))
    _swiglu_half_step(x, g_ref, wi_ref, wo_ref, o_ref, acc_ref)


def _mix_out_ffn(x2d, yc, yh, yf, wc, wh, wf, gain, wi, wo, layer):
    n = x2d.shape[0]
    tm = min(FFN_ROWS, n)

    def row(w):
        return pl.BlockSpec((tm, w), lambda i: (i, 0))

    consts = [wc, wh, wf, gain]
    return pl.pallas_call(
        _mix_out_ffn_body,
        grid=(n // tm,),
        in_specs=[row(D_MODEL), row(CONV_CH), row(HGRN_W), row(FOX_W)]
        + [_const_spec(c.shape) for c in consts]
        + [_layer_spec(wi.shape, layer), _layer_spec(wo.shape, layer)],
        out_specs=row(D_MODEL),
        out_shape=jax.ShapeDtypeStruct((n, D_MODEL), F32),
        scratch_shapes=[pltpu.VMEM((tm, D_MODEL), F32)],
        compiler_params=_params(("arbitrary",)),
        name="mix_out_ffn",
    )(x2d, yc, yh, yf, *consts, wi, wo)


def _ffn(x2d, gain, wi, wo, layer):
    n = x2d.shape[0]
    tm = min(FFN_ROWS, n)
    row = pl.BlockSpec((tm, D_MODEL), lambda i: (i, 0))
    return pl.pallas_call(
        _ffn_body,
        grid=(n // tm,),
        in_specs=[row, _const_spec(gain.shape), _layer_spec(wi.shape, layer),
                  _layer_spec(wo.shape, layer)],
        out_specs=row,
        out_shape=jax.ShapeDtypeStruct((n, D_MODEL), F32),
        scratch_shapes=[pltpu.VMEM((tm, D_MODEL), F32)],
        compiler_params=_params(("arbitrary",)),
        name="ffn",
    )(x2d, gain, wi, wo)


def _mix_in_body(x_ref, g_ref, wc_ref, cw_ref, wh_ref, wq_ref, wk_ref, wvt_ref, wf_ref,
                 fb_ref, gq_ref, gk_ref, tri_ref, place_ref,
                 yc_ref, hh_ref, qa_ref, ka_ref, vat_ref, cend_ref, ubuf_ref, carry_ref, *, tm):
    @pl.when(pl.program_id(1) == 0)
    def _():
        ubuf_ref[0:SUBLANES, :] = jnp.zeros((SUBLANES, CONV_CH), F32)
        carry_ref[...] = jnp.zeros_like(carry_ref)

    xn = _rms_norm(x_ref[0], g_ref[...]).astype(BF16)

    lf = _log_sigmoid(_dot(xn, wf_ref[...]) + fb_ref[...])
    hq = _dot(xn, wq_ref[...])
    hk = _dot(xn, wk_ref[...])
    lf3 = jnp.concatenate(_split3(lf), axis=1)
    last = carry_ref[...]
    blocks, ends = [], []
    for n in range(tm // KV_TILE):
        part = _dot(tri_ref[...], lf3[n * KV_TILE:(n + 1) * KV_TILE])
        blocks.append(last + (part[:, 0:LANES] + part[:, LANES:2 * LANES] + part[:, 2 * LANES:]))
        last = blocks[-1][KV_TILE - 1:KV_TILE, :]
        ends.append(last)
    c = jnp.concatenate(blocks, axis=0)
    carry_ref[...] = last
    cend_ref[0, 0] = jnp.concatenate(
        ends + [jnp.zeros((SUBLANES - len(ends), LANES), F32)], axis=0)
    vt = _dot_nt(wvt_ref[...], xn).astype(BF16)

    lane = lax.broadcasted_iota(jnp.int32, (1, LANES), 1)
    c_hi, c_mid, c_lo = [piece.astype(F32) for piece in _split3(c * LOG2E)]
    packed = jnp.where(lane < FOX_HEADS, c_hi,
                       jnp.where(lane < 2 * FOX_HEADS, pltpu.roll(c_mid, FOX_HEADS, axis=1),
                                 pltpu.roll(c_lo, 2 * FOX_HEADS, axis=1)))
    placed = _dot(packed.astype(BF16), place_ref[...])

    low = lane < HEAD_DIM
    q_takes_c = jnp.logical_and(lane >= Q_C_LANE, lane < Q_C_LANE + N_PIECES)
    k_takes_c = jnp.logical_and(lane >= K_ONE_LANE + N_PIECES, lane < K_ONE_LANE + 2 * N_PIECES)
    one_q = k_takes_c.astype(F32)
    one_k = q_takes_c.astype(F32)

    def head_pair_norm(x2, gain2):
        sq = x2 * x2
        ss_lo = jnp.sum(jnp.where(low, sq, 0.0), axis=-1, keepdims=True)
        ss_hi = jnp.sum(jnp.where(low, 0.0, sq), axis=-1, keepdims=True)
        inv = lax.rsqrt(jnp.where(low, ss_lo, ss_hi) * (1.0 / HEAD_DIM) + EPS)
        return x2 * inv * gain2

    for pair in range(FOX_HEADS // 2):
        cols = slice(pair * LANES, (pair + 1) * LANES)
        qn = head_pair_norm(hq[:, cols], gq_ref[...])
        kn = head_pair_norm(hk[:, cols], gk_ref[...])
        for half in range(2):
            h = 2 * pair + half
            extra = placed[:, h * LANES:(h + 1) * LANES]
            q_h = qn if half == 0 else pltpu.roll(qn, HEAD_DIM, axis=1)
            k_h = kn if half == 0 else pltpu.roll(kn, HEAD_DIM, axis=1)
            qa_ref[0, h] = jnp.where(low, q_h, jnp.where(q_takes_c, extra, one_q)).astype(BF16)
            ka_ref[0, h] = jnp.where(low, k_h, jnp.where(k_takes_c, extra, one_k)).astype(BF16)

    tail = (lax.broadcasted_iota(jnp.int32, (LANES - HEAD_DIM, KV_TILE), 0) == 0).astype(BF16)
    for h in range(FOX_HEADS):
        for n in range(tm // KV_TILE):
            vat_ref[0, h, n] = jnp.concatenate(
                [vt[h * HEAD_DIM:(h + 1) * HEAD_DIM, n * KV_TILE:(n + 1) * KV_TILE], tail], axis=0)

    hc = _dot(xn, wc_ref[...])
    u = hc[:, 2 * CONV_CH:3 * CONV_CH] * hc[:, 0:CONV_CH]
    ubuf_ref[SUBLANES:SUBLANES + tm, :] = u
    u1 = ubuf_ref[SUBLANES - 1:SUBLANES - 1 + tm, :]
    u2 = ubuf_ref[SUBLANES - 2:SUBLANES - 2 + tm, :]
    cw = cw_ref[...]
    conv = cw[0:1, :] * u2 + cw[1:2, :] * u1 + cw[2:3, :] * u
    yc_ref[0] = (hc[:, CONV_CH:2 * CONV_CH] * conv).astype(BF16)
    ubuf_ref[0:SUBLANES, :] = u[tm - SUBLANES:tm, :]

    hh_ref[0] = _dot(xn, wh_ref[...])


def _mix_in(x, p):
    b, l, _ = x.shape
    tm = min(ROW_TILE, l)
    nt = l // tm
    consts = [p["mix_norm"], p["wc"], p["conv_w"], p["wh"], p["wq"], p["wk"], p["wvt"], p["wf"],
              p["fbias"], p["gq"], p["gk"], p["tri"], p["place"]]
    out_shape = [
        jax.ShapeDtypeStruct((b, l, CONV_CH), BF16),
        jax.ShapeDtypeStruct((b, l, 4 * HGRN_W), F32),
        jax.ShapeDtypeStruct((b, FOX_HEADS, l, LANES), BF16),
        jax.ShapeDtypeStruct((b, FOX_HEADS, l, LANES), BF16),
        jax.ShapeDtypeStruct((b, FOX_HEADS, l // KV_TILE, LANES, KV_TILE), BF16),
        jax.ShapeDtypeStruct((b, nt, SUBLANES, LANES), F32),
    ]
    out_specs = [
        pl.BlockSpec((1, tm, CONV_CH), lambda bi, i: (bi, i, 0)),
        pl.BlockSpec((1, tm, 4 * HGRN_W), lambda bi, i: (bi, i, 0)),
        pl.BlockSpec((1, FOX_HEADS, tm, LANES), lambda bi, i: (bi, 0, i, 0)),
        pl.BlockSpec((1, FOX_HEADS, tm, LANES), lambda bi, i: (bi, 0, i, 0)),
        pl.BlockSpec((1, FOX_HEADS, tm // KV_TILE, LANES, KV_TILE), lambda bi, i: (bi, 0, i, 0, 0)),
        pl.BlockSpec((1, 1, SUBLANES, LANES), lambda bi, i: (bi, i, 0, 0)),
    ]
    return pl.pallas_call(
        functools.partial(_mix_in_body, tm=tm),
        grid=(b, nt),
        in_specs=[pl.BlockSpec((1, tm, D_MODEL), lambda bi, i: (bi, i, 0))]
        + [_const_spec(c.shape) for c in consts],
        out_specs=out_specs,
        out_shape=out_shape,
        scratch_shapes=[pltpu.VMEM((tm + SUBLANES, CONV_CH), F32), pltpu.VMEM((1, LANES), F32)],
        compiler_params=_params(("arbitrary", "arbitrary")),
        name="mix_in",
    )(x, *consts)


def _hgrn_body(q_ref, z_ref, v_ref, g_ref, lbl_ref, gain_ref, cum_ref, rep_ref, bd_ref, bmask_ref,
               o_ref, st_ref, vv_ref, lb_ref, *, layer, rows):
    @pl.when(pl.program_id(1) == 0)
    def _():
        st_ref[...] = jnp.zeros_like(st_ref)
        lbl = lbl_ref[...]
        e = jnp.exp(lbl - jnp.max(lbl, axis=0, keepdims=True))
        soft = e / jnp.sum(e, axis=0, keepdims=True)
        lb_ref[...] = jnp.clip(
            jnp.sum(soft[0:layer + 1, :], axis=0, keepdims=True) - soft[0:1, :], 0.0, 1.0)

    lb = lb_ref[...]
    bd = bd_ref[...]
    n_sub = CHUNK // SUB
    sub_t = lax.broadcasted_iota(jnp.int32, (CHUNK, HGRN_W), 0) // SUB
    sub_s = lax.broadcasted_iota(jnp.int32, (CHUNK, HGRN_W), 1) % CHUNK // SUB

    def within_chunk(ch, slot):
        rs = pl.ds(pl.multiple_of(ch * CHUNK, CHUNK), CHUNK)
        q = q_ref[0, rs, :]
        z = z_ref[0, rs, :]
        v = v_ref[0, rs, :]
        lf = (_log_sigmoid(z) + jnp.log(1.0 + lb * jnp.exp(-z))) * LOG2E
        k = (1.0 - lb) * _sigmoid(-z)
        pieces = jnp.concatenate(_split3(lf), axis=0)
        sums = _dot(cum_ref[0:3 * CHUNK, :], pieces)
        vv_ref[slot] = v
        v16 = v.astype(BF16)
        k16 = k.astype(BF16)
        yield

        bc = sums[0:CHUNK]
        bs = sums[CHUNK:2 * CHUNK]
        be = sums[2 * CHUNK:3 * CHUNK]

        q_t = q * jnp.exp2(bc - bs)
        k_t = (k * jnp.exp2(be - bc)).astype(BF16)
        lags = [q_t]
        for lag in range(2, n_sub):
            be_shift = jnp.concatenate([jnp.zeros((lag * SUB, HGRN_W), F32),
                                        be[0:CHUNK - lag * SUB]], axis=0)
            lags.append(q_t * jnp.exp2(bs - be_shift))
        q_lags = jnp.concatenate(lags, axis=0).astype(BF16)
        sc = _dot_nt(q_lags, jnp.concatenate([k_t] * HGRN_HEADS, axis=0) * bd)

        half_rows = CHUNK * SUB // 2
        pair = jnp.concatenate(
            [_dot(cum_ref[3 * CHUNK + n * half_rows:3 * CHUNK + (n + 1) * half_rows, :], pieces)
             for n in range(2)], axis=0)
        k_rep = jnp.concatenate(
            [_dot(rep_ref[n * half_rows:(n + 1) * half_rows, :], k16) for n in range(2)], axis=0)
        yield

        a_lag = jnp.zeros((CHUNK, HGRN_W), F32)
        for n, lag in enumerate(range(1, n_sub)):
            a_lag = jnp.where(sub_t - sub_s == lag, sc[n * CHUNK:(n + 1) * CHUNK], a_lag)
        o_lag = _dot(a_lag.astype(BF16), jnp.concatenate([v16] * HGRN_HEADS, axis=0) * bd)
        a = []
        for i in range(n_sub):
            rows_i = slice(i * SUB * SUB, (i + 1) * SUB * SUB)
            q_rep = jnp.concatenate([q[i * SUB:(i + 1) * SUB]] * SUB, axis=0)
            a.append((q_rep * jnp.exp2(pair[rows_i]) * k_rep[rows_i]).astype(BF16))
        r = jnp.concatenate(
            [_dot(jnp.concatenate(a[n * n_sub // 2:(n + 1) * n_sub // 2], axis=0), bd)
             for n in range(2)], axis=0)
        b_last = bc[CHUNK - 1:CHUNK, :]
        q_hat = (q * jnp.exp2(bc)).astype(BF16)
        k_hat = (k * jnp.exp2(b_last - bc)).astype(BF16)
        yield

        o_sub = []
        for i in range(n_sub):
            acc = None
            for sl in range(SUB):
                s = i * SUB + sl
                term = r[s * SUB:(s + 1) * SUB] * vv_ref[slot, s:s + 1, :]
                acc = term if acc is None else acc + term
            o_sub.append(acc)
        o = o_lag + jnp.concatenate(o_sub, axis=0)
        return rs, o, q_hat, k_hat, v16, jnp.exp2(b_last)

    def finish(parts):
        updates = [_dot_tn(v16, k_hat) for _, _, _, k_hat, v16, _ in parts]
        st = st_ref[...]
        outs = []
        for (_, o, q_hat, _, _, decay_last), update in zip(parts, updates):
            outs.append(o + _dot_nt(q_hat, st.astype(BF16)))
            st = st * decay_last + bmask_ref[...] * update
        st_ref[...] = st

        means = []
        for o in outs:
            oo = o * o
            hi = oo.astype(BF16)
            lo = (oo - hi.astype(F32)).astype(BF16)
            means.append((_dot(hi, bd) + _dot(lo, bd)) * (1.0 / HEAD_DIM))
        for (rs, *_), o, ms in zip(parts, outs, means):
            gate = g_ref[0, rs, :]
            y = o * lax.rsqrt(ms + EPS) * gain_ref[...] * (gate * _sigmoid(gate))
            o_ref[0, rs, :] = y.astype(BF16)

    def chunk_group(cc, carry):
        gens = [within_chunk(CHUNKS_PER_TRIP * cc + slot, slot) for slot in range(CHUNKS_PER_TRIP)]
        parts = [None] * CHUNKS_PER_TRIP
        while any(part is None for part in parts):
            for slot, gen in enumerate(gens):
                if parts[slot] is None:
                    try:
                        next(gen)
                    except StopIteration as done:
                        parts[slot] = done.value
        finish(parts)
        return carry

    lax.fori_loop(0, rows // (CHUNKS_PER_TRIP * CHUNK), chunk_group, 0)


def _hgrn(hh, p, layer):
    b, l, _ = hh.shape
    rows = min(HGRN_ROWS, l)
    consts = [p["lb_logits"], p["hgrn_gain"], p["cum64"], p["rep64"], p["bd"], p["bmask"]]

    def section(k):
        return pl.BlockSpec((1, rows, HGRN_W), lambda bi, i: (bi, i, k))

    return pl.pallas_call(
        functools.partial(_hgrn_body, layer=layer, rows=rows),
        grid=(b, l // rows),
        in_specs=[section(0), section(1), section(2), section(3)]
        + [_const_spec(c.shape) for c in consts],
        out_specs=pl.BlockSpec((1, rows, HGRN_W), lambda bi, i: (bi, i, 0)),
        out_shape=jax.ShapeDtypeStruct((b, l, HGRN_W), BF16),
        scratch_shapes=[pltpu.VMEM((HGRN_W, HGRN_W), F32),
                        pltpu.VMEM((CHUNKS_PER_TRIP, CHUNK, HGRN_W), F32),
                        pltpu.VMEM((1, HGRN_W), F32)],
        compiler_params=_params(("arbitrary", "arbitrary")),
        name="hgrn",
    )(hh, hh, hh, hh, *consts)


def _fox_body(qkb_ref, cpre_ref, qa_ref, ka_ref, vat_ref, o_ref, m_ref, acc_ref, s_ref,
              *, tq, tk, n_pre):
    bi, hp, qi = pl.program_id(0), pl.program_id(1), pl.program_id(2)
    per_q = tq // tk
    row = lax.broadcasted_iota(jnp.int32, (tk, tq), 0)
    col = lax.broadcasted_iota(jnp.int32, (tk, tq), 1)

    def c_before(hh, n):
        return cpre_ref[(bi * FOX_HEADS + 2 * hp + hh) * n_pre + n]

    def keys(hh, j):
        return ka_ref[0, hh, pl.ds(pl.multiple_of(j * tk, tk), tk), :]

    def scores(hh, j):
        return _dot_nt(keys(hh, j), qa_ref[0, hh])

    def update(hh, j, s, mask=None, off=None, cols=slice(None)):
        if mask is not None:
            s = jnp.where(mask, s, MASK_VALUE)
        m_old = m_ref[hh, :, cols]
        tile_max = jnp.max(s, axis=0, keepdims=True)
        m_new = jnp.maximum(m_old, tile_max if off is None else tile_max + off)
        p = jnp.exp2(s - (m_new if off is None else m_new - off)).astype(BF16)
        acc_ref[hh, :, cols] = (acc_ref[hh, :, cols] * jnp.exp2(m_old - m_new)
                                + _dot(vat_ref[0, hh, j], p))
        m_ref[hh, :, cols] = m_new

    for hh in range(2):
        m_ref[hh] = jnp.full((1, tq), MASK_VALUE, F32)
        acc_ref[hh] = jnp.zeros((LANES, tq), F32)

    diag = [(hh, dj) for dj in range(per_q) for hh in range(2)]
    s_diag = [_dot_nt(keys(hh, qi * per_q + dj), qa_ref[0, hh, dj * tk:tq, :]) for hh, dj in diag]
    near = [(hh, qi * per_q - 1 - e) for e in range(N_NEAR) for hh in range(2)]
    s_near = [scores(hh, jnp.maximum(j, 0)) for hh, j in near]
    j_start = qi * per_q - 1 - N_NEAR
    j_first = jnp.maximum(j_start, 0)
    s_first = [scores(hh, j_first) for hh in range(2)]
    for (hh, dj), s in zip(diag, s_diag):
        update(hh, qi * per_q + dj, s, mask=(row <= col)[:, 0:tq - dj * tk], cols=slice(dj * tk, tq))
    for (hh, j), s in zip(near, s_near):
        update(hh, jnp.maximum(j, 0), s, off=jnp.where(j >= 0, 0.0, MASK_VALUE))

    def needed(j):
        out = None
        for hh in range(2):
            gap = (qkb_ref[0] + c_before(hh, qi * per_q) - c_before(hh, j + 1)
                   - jnp.min(m_ref[hh]))
            out = gap >= SKIP_BELOW if out is None else jnp.logical_or(out, gap >= SKIP_BELOW)
        return out.astype(jnp.int32)

    def cond(carry):
        j, go = carry
        return jnp.logical_and(j >= 0, go > 0)

    def stage(src, dst, j):
        j_next = jnp.maximum(j - 1, 0)
        for hh in range(2):
            s_ref[dst, hh] = scores(hh, j_next)
        for hh in range(2):
            update(hh, j, s_ref[src, hh])
        return jnp.where(j >= 1, needed(j_next), 0)

    def body(carry):
        j, _ = carry
        go = stage(0, 1, j)
        go = lax.cond(go > 0, lambda: stage(1, 0, j - 1), lambda: jnp.int32(0))
        return j - 2, go

    for hh in range(2):
        s_ref[0, hh] = s_first[hh]
    lax.while_loop(cond, body, (j_start, needed(j_first)))

    outs = []
    for hh in range(2):
        acc = acc_ref[hh]
        outs.append(acc[0:HEAD_DIM, :] / acc[HEAD_DIM:HEAD_DIM + 1, :])
    o_ref[0] = jnp.concatenate(outs, axis=0).T.astype(BF16)


def _fox(qkb, cpre, qa, ka, vat):
    b, _, l, _ = qa.shape
    tq, tk = min(Q_TILE, l), vat.shape[-1]
    smem = pl.BlockSpec(memory_space=pltpu.SMEM)
    return pl.pallas_call(
        functools.partial(_fox_body, tq=tq, tk=tk, n_pre=l // tk + 1),
        grid=(b, FOX_HEADS // 2, l // tq),
        in_specs=[
            smem, smem,
            pl.BlockSpec((1, 2, tq, LANES), lambda bi, hp, i: (bi, hp, i, 0)),
            pl.BlockSpec((1, 2, l, LANES), lambda bi, hp, i: (bi, hp, 0, 0)),
            pl.BlockSpec((1, 2, l // tk, LANES, tk), lambda bi, hp, i: (bi, hp, 0, 0, 0)),
        ],
        out_specs=pl.BlockSpec((1, tq, LANES), lambda bi, hp, i: (bi, i, hp)),
        out_shape=jax.ShapeDtypeStruct((b, l, FOX_W), BF16),
        scratch_shapes=[pltpu.VMEM((2, 1, tq), F32), pltpu.VMEM((2, LANES, tq), F32),
                        pltpu.VMEM((2, 2, tk, tq), F32)],
        compiler_params=_params(("arbitrary", "arbitrary", "arbitrary")),
        name="fox",
    )(qkb, cpre, qa, ka, vat)


def _pad_lanes(v, width=LANES):
    return jnp.pad(v, (0, width - v.shape[0])).reshape(1, width)


def _placement():
    m = np.zeros((LANES, FOX_HEADS * LANES), np.float32)
    for p in range(N_PIECES):
        for h in range(FOX_HEADS):
            m[p * FOX_HEADS + h, h * LANES + Q_C_LANE + p] = 1.0
            m[p * FOX_HEADS + h, h * LANES + K_ONE_LANE + N_PIECES + p] = -1.0
    return jnp.asarray(m, BF16)


def _tri(n):
    return jnp.asarray(np.tril(np.ones((n, n), np.float32)), BF16)


def _chunk_sums():
    t = np.arange(CHUNK)[:, None]
    r = np.arange(CHUNK)[None, :]
    first = (t // SUB) * SUB
    s_pair, t_pair = _sub_chunk_pairs()
    rows = [r <= t, r < first, r < first + SUB,
            (r > s_pair[:, None]) & (r <= t_pair[:, None])]
    m = np.concatenate(rows, axis=0).astype(np.float32)
    return jnp.asarray(np.tile(m, (1, N_PIECES)), BF16)


def _sub_chunk_pairs():
    s = np.repeat(np.arange(CHUNK), SUB)
    t = (s // SUB) * SUB + np.tile(np.arange(SUB), CHUNK)
    return s, t


def _causal_replicate():
    s, t = _sub_chunk_pairs()
    m = (np.arange(CHUNK)[None, :] == s[:, None]) & (t >= s)[:, None]
    return jnp.asarray(m.astype(np.float32), BF16)


def _layer_operands(l, mix_norm, w_mix_in, conv_w, hgrn_lb_logits, hgrn_out_gain,
                    fox_q_gain, fox_k_gain, fox_f_bias, w_mix_out):
    w = w_mix_in[l]
    o_h = 3 * CONV_CH
    o_q = o_h + 4 * HGRN_W
    o_k, o_v, o_f = o_q + FOX_W, o_q + 2 * FOX_W, o_q + 3 * FOX_W
    head_blocks = np.kron(np.eye(HGRN_HEADS, dtype=np.float32),
                          np.ones((HEAD_DIM, HEAD_DIM), np.float32))
    wo = w_mix_out[l].astype(BF16)
    q_scale = fox_q_gain[l] * (HEAD_DIM ** -0.5 * LOG2E)
    return {
        "mix_norm": mix_norm[l].reshape(1, D_MODEL),
        "wc": w[:, :o_h].astype(BF16),
        "conv_w": jnp.pad(conv_w[l], ((0, SUBLANES - CONV_WIDTH), (0, 0))),
        "wh": w[:, o_h:o_q].astype(BF16),
        "wq": w[:, o_q:o_k].astype(BF16),
        "wk": w[:, o_k:o_v].astype(BF16),
        "wvt": w[:, o_v:o_f].T.astype(BF16),
        "wf": jnp.pad(w[:, o_f:], ((0, 0), (0, LANES - FOX_HEADS))).astype(BF16),
        "fbias": _pad_lanes(fox_f_bias[l]),
        "gq": jnp.tile(q_scale, LANES // HEAD_DIM).reshape(1, LANES),
        "gk": jnp.tile(fox_k_gain[l], LANES // HEAD_DIM).reshape(1, LANES),
        "qkb": (1.01 * HEAD_DIM * jnp.max(jnp.abs(q_scale))
                * jnp.max(jnp.abs(fox_k_gain[l]))).reshape(1),
        "tri": _tri(KV_TILE),
        "place": _placement(),
        "lb_logits": hgrn_lb_logits,
        "hgrn_gain": jnp.tile(hgrn_out_gain[l], HGRN_HEADS).reshape(1, HGRN_W),
        "cum64": _chunk_sums(),
        "rep64": _causal_replicate(),
        "bd": jnp.asarray(head_blocks, BF16),
        "bmask": jnp.asarray(head_blocks),
        "wo_c": wo[:CONV_CH],
        "wo_h": wo[CONV_CH:CONV_CH + HGRN_W],
        "wo_f": wo[CONV_CH + HGRN_W:],
    }


def kernel(x, ffn1_norm, ffn1_w_in, ffn1_w_out, mix_norm, w_mix_in, conv_w, hgrn_lb_logits,
           hgrn_out_gain, fox_q_gain, fox_k_gain, fox_f_bias, w_mix_out, ffn2_norm, ffn2_w_in,
           ffn2_w_out):
    b, l, d = x.shape
    assert d == D_MODEL and l % ROW_TILE == 0 and ROW_TILE % KV_TILE == 0
    assert l % Q_TILE == 0 and Q_TILE % KV_TILE == 0 and (b * l) % min(FFN_ROWS, b * l) == 0
    assert l % HGRN_ROWS == 0 and HGRN_ROWS % (CHUNKS_PER_TRIP * CHUNK) == 0
    depth = ffn1_norm.shape[0]
    tm = min(ROW_TILE, l)
    n = b * l
    w1_in, w1_out = ffn1_w_in.astype(BF16), ffn1_w_out.astype(BF16)
    w2_in, w2_out = ffn2_w_in.astype(BF16), ffn2_w_out.astype(BF16)
    for layer in range(depth):
        p = _layer_operands(layer, mix_norm, w_mix_in, conv_w, hgrn_lb_logits, hgrn_out_gain,
                            fox_q_gain, fox_k_gain, fox_f_bias, w_mix_out)
        x2 = _ffn(x.reshape(n, d), ffn1_norm[layer].reshape(1, d), w1_in, w1_out, layer)
        yc, hh, qa, ka, vat, cend = _mix_in(x2.reshape(b, l, d), p)
        yh = _hgrn(hh, p, layer)
        cend = cend[:, :, :tm // KV_TILE, :FOX_HEADS].reshape(b, l // KV_TILE, FOX_HEADS)
        cpre = jnp.pad(cend.transpose(0, 2, 1), ((0, 0), (0, 0), (1, 0))) * LOG2E
        yf = _fox(p["qkb"], cpre.reshape(-1), qa, ka, vat)
        x2 = _mix_out_ffn(x2, yc.reshape(n, CONV_CH), yh.reshape(n, HGRN_W), yf.reshape(n, FOX_W),
                          p["wo_c"], p["wo_h"], p["wo_f"], ffn2_norm[layer].reshape(1, d),
                          w2_in, w2_out, layer)
        x = x2.reshape(b, l, d)
    return x
```

```python
import functools

import jax
import jax.numpy as jnp
import numpy as np
from jax import lax
from jax.experimental import pallas as pl
from jax.experimental.pallas import tpu as pltpu

F32 = jnp.float32
BF16 = jnp.bfloat16

D_MODEL = 1024
D_FF = 2816
HEAD_DIM = 64
CONV_CH = 256
CONV_WIDTH = 3
HGRN_HEADS = 4
HGRN_W = 256
FOX_HEADS = 8
FOX_W = 512
CHUNK = 64
EPS = 1e-6
MASK_VALUE = -1e30

LANES = 128
SUBLANES = 8
MXU_TILE = 256
FF_TILE = MXU_TILE
N_FF_TILES = D_FF // FF_TILE
ROW_TILE = 1024
FFN_ROWS = 1024
Q_TILE = 512
KV_TILE = 256
N_NEAR = 3
HGRN_ROWS = 512
SUB = 8
CHUNKS_PER_TRIP = 8
VMEM_LIMIT = 56 * 1024 * 1024

Q_C_LANE = HEAD_DIM
K_ONE_LANE = HEAD_DIM
N_PIECES = 3
LOG2E = 1.4426950408889634
SKIP_BELOW = -138.0


def _dot(a, b):
    return jnp.dot(a, b, preferred_element_type=F32)


def _dot_nt(a, b):
    return lax.dot_general(a, b, (((1,), (1,)), ((), ())), preferred_element_type=F32)


def _dot_tn(a, b):
    return lax.dot_general(a, b, (((0,), (0,)), ((), ())), preferred_element_type=F32)


def _rms_norm(x, gain):
    inv = lax.rsqrt(jnp.mean(x * x, axis=-1, keepdims=True) + EPS)
    return x * inv * gain


def _sigmoid(x):
    return 1.0 / (1.0 + jnp.exp(-x))


def _log_sigmoid(x):
    return jnp.minimum(x, 0.0) - jnp.log(1.0 + jnp.exp(-jnp.abs(x)))


def _split3(x):
    hi = x.astype(BF16)
    r = x - hi.astype(F32)
    mid = r.astype(BF16)
    lo = (r - mid.astype(F32)).astype(BF16)
    return hi, mid, lo


def _const_spec(shape):
    nd = len(shape)
    return pl.BlockSpec(shape, lambda *_: (0,) * nd, pipeline_mode=pl.Buffered(1))


def _layer_spec(stacked_shape, layer):
    nd = len(stacked_shape)
    return pl.BlockSpec((None,) + tuple(stacked_shape[1:]), lambda *_: (layer,) + (0,) * (nd - 1),
                        pipeline_mode=pl.Buffered(1))


def _params(sem):
    return pltpu.CompilerParams(dimension_semantics=sem, vmem_limit_bytes=VMEM_LIMIT)


def _swiglu_half_step(x, g_ref, wi_ref, wo_ref, o_ref, acc_ref):
    xn = _rms_norm(x, g_ref[...]).astype(BF16)
    for c in range(N_FF_TILES):
        gate = _dot(xn, wi_ref[:, c * FF_TILE:(c + 1) * FF_TILE])
        up = _dot(xn, wi_ref[:, D_FF + c * FF_TILE:D_FF + (c + 1) * FF_TILE])
        act = (gate * _sigmoid(gate) * up).astype(BF16)
        part = _dot(act, wo_ref[c * FF_TILE:(c + 1) * FF_TILE, :])
        if c == 0:
            acc_ref[...] = part
        else:
            acc_ref[...] += part
    o_ref[...] = x + 0.5 * acc_ref[...]


def _ffn_body(x_ref, g_ref, wi_ref, wo_ref, o_ref, acc_ref):
    _swiglu_half_step(x_ref[...], g_ref, wi_ref, wo_ref, o_ref, acc_ref)


def _mix_out_ffn_body(x_ref, yc_ref, yh_ref, yf_ref, wc_ref, wh_ref, wf_ref,
                      g_ref, wi_ref, wo_ref, o_ref, acc_ref):
    x = (x_ref[...] + _dot(yc_ref[...], wc_ref[...]) + _dot(yh_ref[...], wh_ref[...])
         + _dot(yf_ref[...], wf_ref[...]))
    _swiglu_half_step(x, g_ref, wi_ref, wo_ref, o_ref, acc_ref)


def _mix_out_ffn(x2d, yc, yh, yf, wc, wh, wf, gain, wi, wo, layer):
    n = x2d.shape[0]
    tm = min(FFN_ROWS, n)

    def row(w):
        return pl.BlockSpec((tm, w), lambda i: (i, 0))

    consts = [wc, wh, wf, gain]
    return pl.pallas_call(
        _mix_out_ffn_body,
        grid=(n // tm,),
        in_specs=[row(D_MODEL), row(CONV_CH), row(HGRN_W), row(FOX_W)]
        + [_const_spec(c.shape) for c in consts]
        + [_layer_spec(wi.shape, layer), _layer_spec(wo.shape, layer)],
        out_specs=row(D_MODEL),
        out_shape=jax.ShapeDtypeStruct((n, D_MODEL), F32),
        scratch_shapes=[pltpu.VMEM((tm, D_MODEL), F32)],
        compiler_params=_params(("arbitrary",)),
        name="mix_out_ffn",
    )(x2d, yc, yh, yf, *consts, wi, wo)


def _ffn(x2d, gain, wi, wo, layer):
    n = x2d.shape[0]
    tm = min(FFN_ROWS, n)
    row = pl.BlockSpec((tm, D_MODEL), lambda i: (i, 0))
    return pl.pallas_call(
        _ffn_body,
        grid=(n // tm,),
        in_specs=[row, _const_spec(gain.shape), _layer_spec(wi.shape, layer),
                  _layer_spec(wo.shape, layer)],
        out_specs=row,
        out_shape=jax.ShapeDtypeStruct((n, D_MODEL), F32),
        scratch_shapes=[pltpu.VMEM((tm, D_MODEL), F32)],
        compiler_params=_params(("arbitrary",)),
        name="ffn",
    )(x2d, gain, wi, wo)


def _mix_in_body(x_ref, g_ref, wc_ref, cw_ref, wh_ref, wq_ref, wk_ref, wvt_ref, wf_ref,
                 fb_ref, gq_ref, gk_ref, tri_ref, place_ref,
                 yc_ref, hh_ref, qa_ref, ka_ref, vat_ref, cend_ref, ubuf_ref, carry_ref, *, tm):
    @pl.when(pl.program_id(1) == 0)
    def _():
        ubuf_ref[0:SUBLANES, :] = jnp.zeros((SUBLANES, CONV_CH), F32)
        carry_ref[...] = jnp.zeros_like(carry_ref)

    xn = _rms_norm(x_ref[0], g_ref[...]).astype(BF16)

    lf = _log_sigmoid(_dot(xn, wf_ref[...]) + fb_ref[...])
    hq = _dot(xn, wq_ref[...])
    hk = _dot(xn, wk_ref[...])
    lf3 = jnp.concatenate(_split3(lf), axis=1)
    last = carry_ref[...]
    blocks, ends = [], []
    for n in range(tm // KV_TILE):
        part = _dot(tri_ref[...], lf3[n * KV_TILE:(n + 1) * KV_TILE])
        blocks.append(last + (part[:, 0:LANES] + part[:, LANES:2 * LANES] + part[:, 2 * LANES:]))
        last = blocks[-1][KV_TILE - 1:KV_TILE, :]
        ends.append(last)
    c = jnp.concatenate(blocks, axis=0)
    carry_ref[...] = last
    cend_ref[0, 0] = jnp.concatenate(
        ends + [jnp.zeros((SUBLANES - len(ends), LANES), F32)], axis=0)
    vt = _dot_nt(wvt_ref[...], xn).astype(BF16)

    lane = lax.broadcasted_iota(jnp.int32, (1, LANES), 1)
    c_hi, c_mid, c_lo = [piece.astype(F32) for piece in _split3(c * LOG2E)]
    packed = jnp.where(lane < FOX_HEADS, c_hi,
                       jnp.where(lane < 2 * FOX_HEADS, pltpu.roll(c_mid, FOX_HEADS, axis=1),
                                 pltpu.roll(c_lo, 2 * FOX_HEADS, axis=1)))
    placed = _dot(packed.astype(BF16), place_ref[...])

    low = lane < HEAD_DIM
    q_takes_c = jnp.logical_and(lane >= Q_C_LANE, lane < Q_C_LANE + N_PIECES)
    k_takes_c = jnp.logical_and(lane >= K_ONE_LANE + N_PIECES, lane < K_ONE_LANE + 2 * N_PIECES)
    one_q = k_takes_c.astype(F32)
    one_k = q_takes_c.astype(F32)

    def head_pair_norm(x2, gain2):
        sq = x2 * x2
        ss_lo = jnp.sum(jnp.where(low, sq, 0.0), axis=-1, keepdims=True)
        ss_hi = jnp.sum(jnp.where(low, 0.0, sq), axis=-1, keepdims=True)
        inv = lax.rsqrt(jnp.where(low, ss_lo, ss_hi) * (1.0 / HEAD_DIM) + EPS)
        return x2 * inv * gain2

    for pair in range(FOX_HEADS // 2):
        cols = slice(pair * LANES, (pair + 1) * LANES)
        qn = head_pair_norm(hq[:, cols], gq_ref[...])
        kn = head_pair_norm(hk[:, cols], gk_ref[...])
        for half in range(2):
            h = 2 * pair + half
            extra = placed[:, h * LANES:(h + 1) * LANES]
            q_h = qn if half == 0 else pltpu.roll(qn, HEAD_DIM, axis=1)
            k_h = kn if half == 0 else pltpu.roll(kn, HEAD_DIM, axis=1)
            qa_ref[0, h] = jnp.where(low, q_h, jnp.where(q_takes_c, extra, one_q)).astype(BF16)
            ka_ref[0, h] = jnp.where(low, k_h, jnp.where(k_takes_c, extra, one_k)).astype(BF16)

    tail = (lax.broadcasted_iota(jnp.int32, (LANES - HEAD_DIM, KV_TILE), 0) == 0).astype(BF16)
    for h in range(FOX_HEADS):
        for n in range(tm // KV_TILE):
            vat_ref[0, h, n] = jnp.concatenate(
                [vt[h * HEAD_DIM:(h + 1) * HEAD_DIM, n * KV_TILE:(n + 1) * KV_TILE], tail], axis=0)

    hc = _dot(xn, wc_ref[...])
    u = hc[:, 2 * CONV_CH:3 * CONV_CH] * hc[:, 0:CONV_CH]
    ubuf_ref[SUBLANES:SUBLANES + tm, :] = u
    u1 = ubuf_ref[SUBLANES - 1:SUBLANES - 1 + tm, :]
    u2 = ubuf_ref[SUBLANES - 2:SUBLANES - 2 + tm, :]
    cw = cw_ref[...]
    conv = cw[0:1, :] * u2 + cw[1:2, :] * u1 + cw[2:3, :] * u
    yc_ref[0] = (hc[:, CONV_CH:2 * CONV_CH] * conv).astype(BF16)
    ubuf_ref[0:SUBLANES, :] = u[tm - SUBLANES:tm, :]

    hh_ref[0] = _dot(xn, wh_ref[...])


def _mix_in(x, p):
    b, l, _ = x.shape
    tm = min(ROW_TILE, l)
    nt = l // tm
    consts = [p["mix_norm"], p["wc"], p["conv_w"], p["wh"], p["wq"], p["wk"], p["wvt"], p["wf"],
              p["fbias"], p["gq"], p["gk"], p["tri"], p["place"]]
    out_shape = [
        jax.ShapeDtypeStruct((b, l, CONV_CH), BF16),
        jax.ShapeDtypeStruct((b, l, 4 * HGRN_W), F32),
        jax.ShapeDtypeStruct((b, FOX_HEADS, l, LANES), BF16),
        jax.ShapeDtypeStruct((b, FOX_HEADS, l, LANES), BF16),
        jax.ShapeDtypeStruct((b, FOX_HEADS, l // KV_TILE, LANES, KV_TILE), BF16),
        jax.ShapeDtypeStruct((b, nt, SUBLANES, LANES), F32),
    ]
    out_specs = [
        pl.BlockSpec((1, tm, CONV_CH), lambda bi, i: (bi, i, 0)),
        pl.BlockSpec((1, tm, 4 * HGRN_W), lambda bi, i: (bi, i, 0)),
        pl.BlockSpec((1, FOX_HEADS, tm, LANES), lambda bi, i: (bi, 0, i, 0)),
        pl.BlockSpec((1, FOX_HEADS, tm, LANES), lambda bi, i: (bi, 0, i, 0)),
        pl.BlockSpec((1, FOX_HEADS, tm // KV_TILE, LANES, KV_TILE), lambda bi, i: (bi, 0, i, 0, 0)),
        pl.BlockSpec((1, 1, SUBLANES, LANES), lambda bi, i: (bi, i, 0, 0)),
    ]
    return pl.pallas_call(
        functools.partial(_mix_in_body, tm=tm),
        grid=(b, nt),
        in_specs=[pl.BlockSpec((1, tm, D_MODEL), lambda bi, i: (bi, i, 0))]
        + [_const_spec(c.shape) for c in consts],
        out_specs=out_specs,
        out_shape=out_shape,
        scratch_shapes=[pltpu.VMEM((tm + SUBLANES, CONV_CH), F32), pltpu.VMEM((1, LANES), F32)],
        compiler_params=_params(("arbitrary", "arbitrary")),
        name="mix_in",
    )(x, *consts)


def _hgrn_body(q_ref, z_ref, v_ref, g_ref, lbl_ref, gain_ref, cum_ref, rep_ref, bd_ref, bmask_ref,
               o_ref, st_ref, vv_ref, lb_ref, *, layer, rows):
    @pl.when(pl.program_id(1) == 0)
    def _():
        st_ref[...] = jnp.zeros_like(st_ref)
        lbl = lbl_ref[...]
        e = jnp.exp(lbl - jnp.max(lbl, axis=0, keepdims=True))
        soft = e / jnp.sum(e, axis=0, keepdims=True)
        lb_ref[...] = jnp.clip(
            jnp.sum(soft[0:layer + 1, :], axis=0, keepdims=True) - soft[0:1, :], 0.0, 1.0)

    lb = lb_ref[...]
    bd = bd_ref[...]
    n_sub = CHUNK // SUB
    sub_t = lax.broadcasted_iota(jnp.int32, (CHUNK, HGRN_W), 0) // SUB
    sub_s = lax.broadcasted_iota(jnp.int32, (CHUNK, HGRN_W), 1) % CHUNK // SUB

    def within_chunk(ch, slot):
        rs = pl.ds(pl.multiple_of(ch * CHUNK, CHUNK), CHUNK)
        q = q_ref[0, rs, :]
        z = z_ref[0, rs, :]
        v = v_ref[0, rs, :]
        lf = (_log_sigmoid(z) + jnp.log(1.0 + lb * jnp.exp(-z))) * LOG2E
        k = (1.0 - lb) * _sigmoid(-z)
        pieces = jnp.concatenate(_split3(lf), axis=0)
        sums = _dot(cum_ref[0:3 * CHUNK, :], pieces)
        vv_ref[slot] = v
        v16 = v.astype(BF16)
        k16 = k.astype(BF16)
        yield

        bc = sums[0:CHUNK]
        bs = sums[CHUNK:2 * CHUNK]
        be = sums[2 * CHUNK:3 * CHUNK]

        q_t = q * jnp.exp2(bc - bs)
        k_t = (k * jnp.exp2(be - bc)).astype(BF16)
        lags = [q_t]
        for lag in range(2, n_sub):
            be_shift = jnp.concatenate([jnp.zeros((lag * SUB, HGRN_W), F32),
                                        be[0:CHUNK - lag * SUB]], axis=0)
            lags.append(q_t * jnp.exp2(bs - be_shift))
        q_lags = jnp.concatenate(lags, axis=0).astype(BF16)
        sc = _dot_nt(q_lags, jnp.concatenate([k_t] * HGRN_HEADS, axis=0) * bd)

        half_rows = CHUNK * SUB // 2
        pair = jnp.concatenate(
            [_dot(cum_ref[3 * CHUNK + n * half_rows:3 * CHUNK + (n + 1) * half_rows, :], pieces)
             for n in range(2)], axis=0)
        k_rep = jnp.concatenate(
            [_dot(rep_ref[n * half_rows:(n + 1) * half_rows, :], k16) for n in range(2)], axis=0)
        yield

        a_lag = jnp.zeros((CHUNK, HGRN_W), F32)
        for n, lag in enumerate(range(1, n_sub)):
            a_lag = jnp.where(sub_t - sub_s == lag, sc[n * CHUNK:(n + 1) * CHUNK], a_lag)
        o_lag = _dot(a_lag.astype(BF16), jnp.concatenate([v16] * HGRN_HEADS, axis=0) * bd)
        a = []
        for i in range(n_sub):
            rows_i = slice(i * SUB * SUB, (i + 1) * SUB * SUB)
            q_rep = jnp.concatenate([q[i * SUB:(i + 1) * SUB]] * SUB, axis=0)
            a.append((q_rep * jnp.exp2(pair[rows_i]) * k_rep[rows_i]).astype(BF16))
        r = jnp.concatenate(
            [_dot(jnp.concatenate(a[n * n_sub // 2:(n + 1) * n_sub // 2], axis=0), bd)
             for n in range(2)], axis=0)
        b_last = bc[CHUNK - 1:CHUNK, :]
        q_hat = (q * jnp.exp2(bc)).astype(BF16)
        k_hat = (k * jnp.exp2(b_last - bc)).astype(BF16)
        yield

        o_sub = []
        for i in range(n_sub):
            acc = None
            for sl in range(SUB):
                s = i * SUB + sl
                term = r[s * SUB:(s + 1) * SUB] * vv_ref[slot, s:s + 1, :]
                acc = term if acc is None else acc + term
            o_sub.append(acc)
        o = o_lag + jnp.concatenate(o_sub, axis=0)
        return rs, o, q_hat, k_hat, v16, jnp.exp2(b_last)

    def finish(parts):
        updates = [_dot_tn(v16, k_hat) for _, _, _, k_hat, v16, _ in parts]
        st = st_ref[...]
        outs = []
        for (_, o, q_hat, _, _, decay_last), update in zip(parts, updates):
            outs.append(o + _dot_nt(q_hat, st.astype(BF16)))
            st = st * decay_last + bmask_ref[...] * update
        st_ref[...] = st

        means = []
        for o in outs:
            oo = o * o
            hi = oo.astype(BF16)
            lo = (oo - hi.astype(F32)).astype(BF16)
            means.append((_dot(hi, bd) + _dot(lo, bd)) * (1.0 / HEAD_DIM))
        for (rs, *_), o, ms in zip(parts, outs, means):
            gate = g_ref[0, rs, :]
            y = o * lax.rsqrt(ms + EPS) * gain_ref[...] * (gate * _sigmoid(gate))
            o_ref[0, rs, :] = y.astype(BF16)

    def chunk_group(cc, carry):
        gens = [within_chunk(CHUNKS_PER_TRIP * cc + slot, slot) for slot in range(CHUNKS_PER_TRIP)]
        parts = [None] * CHUNKS_PER_TRIP
        while any(part is None for part in parts):
            for slot, gen in enumerate(gens):
                if parts[slot] is None:
                    try:
                        next(gen)
                    except StopIteration as done:
                        parts[slot] = done.value
        finish(parts)
        return carry

    lax.fori_loop(0, rows // (CHUNKS_PER_TRIP * CHUNK), chunk_group, 0)


def _hgrn(hh, p, layer):
    b, l, _ = hh.shape
    rows = min(HGRN_ROWS, l)
    consts = [p["lb_logits"], p["hgrn_gain"], p["cum64"], p["rep64"], p["bd"], p["bmask"]]

    def section(k):
        return pl.BlockSpec((1, rows, HGRN_W), lambda bi, i: (bi, i, k))

    return pl.pallas_call(
        functools.partial(_hgrn_body, layer=layer, rows=rows),
        grid=(b, l // rows),
        in_specs=[section(0), section(1), section(2), section(3)]
        + [_const_spec(c.shape) for c in consts],
        out_specs=pl.BlockSpec((1, rows, HGRN_W), lambda bi, i: (bi, i, 0)),
        out_shape=jax.ShapeDtypeStruct((b, l, HGRN_W), BF16),
        scratch_shapes=[pltpu.VMEM((HGRN_W, HGRN_W), F32),
                        pltpu.VMEM((CHUNKS_PER_TRIP, CHUNK, HGRN_W), F32),
                        pltpu.VMEM((1, HGRN_W), F32)],
        compiler_params=_params(("arbitrary", "arbitrary")),
        name="hgrn",
    )(hh, hh, hh, hh, *consts)


def _fox_body(qkb_ref, cpre_ref, qa_ref, ka_ref, vat_ref, o_ref, m_ref, acc_ref, s_ref,
              *, tq, tk, n_pre):
    bi, hp, qi = pl.program_id(0), pl.program_id(1), pl.program_id(2)
    per_q = tq // tk
    row = lax.broadcasted_iota(jnp.int32, (tk, tq), 0)
    col = lax.broadcasted_iota(jnp.int32, (tk, tq), 1)

    def c_before(hh, n):
        return cpre_ref[(bi * FOX_HEADS + 2 * hp + hh) * n_pre + n]

    def keys(hh, j):
        return ka_ref[0, hh, pl.ds(pl.multiple_of(j * tk, tk), tk), :]

    def scores(hh, j):
        return _dot_nt(keys(hh, j), qa_ref[0, hh])

    def update(hh, j, s, mask=None, off=None, cols=slice(None)):
        if mask is not None:
            s = jnp.where(mask, s, MASK_VALUE)
        m_old = m_ref[hh, :, cols]
        tile_max = jnp.max(s, axis=0, keepdims=True)
        m_new = jnp.maximum(m_old, tile_max if off is None else tile_max + off)
        p = jnp.exp2(s - (m_new if off is None else m_new - off)).astype(BF16)
        acc_ref[hh, :, cols] = (acc_ref[hh, :, cols] * jnp.exp2(m_old - m_new)
                                + _dot(vat_ref[0, hh, j], p))
        m_ref[hh, :, cols] = m_new

    for hh in range(2):
        m_ref[hh] = jnp.full((1, tq), MASK_VALUE, F32)
        acc_ref[hh] = jnp.zeros((LANES, tq), F32)

    diag = [(hh, dj) for dj in range(per_q) for hh in range(2)]
    s_diag = [_dot_nt(keys(hh, qi * per_q + dj), qa_ref[0, hh, dj * tk:tq, :]) for hh, dj in diag]
    near = [(hh, qi * per_q - 1 - e) for e in range(N_NEAR) for hh in range(2)]
    s_near = [scores(hh, jnp.maximum(j, 0)) for hh, j in near]
    j_start = qi * per_q - 1 - N_NEAR
    j_first = jnp.maximum(j_start, 0)
    s_first = [scores(hh, j_first) for hh in range(2)]
    for (hh, dj), s in zip(diag, s_diag):
        update(hh, qi * per_q + dj, s, mask=(row <= col)[:, 0:tq - dj * tk], cols=slice(dj * tk, tq))
    for (hh, j), s in zip(near, s_near):
        update(hh, jnp.maximum(j, 0), s, off=jnp.where(j >= 0, 0.0, MASK_VALUE))

    def needed(j):
        out = None
        for hh in range(2):
            gap = (qkb_ref[0] + c_before(hh, qi * per_q) - c_before(hh, j + 1)
                   - jnp.min(m_ref[hh]))
            out = gap >= SKIP_BELOW if out is None else jnp.logical_or(out, gap >= SKIP_BELOW)
        return out.astype(jnp.int32)

    def cond(carry):
        j, go = carry
        return jnp.logical_and(j >= 0, go > 0)

    def stage(src, dst, j):
        j_next = jnp.maximum(j - 1, 0)
        for hh in range(2):
            s_ref[dst, hh] = scores(hh, j_next)
        for hh in range(2):
            update(hh, j, s_ref[src, hh])
        return jnp.where(j >= 1, needed(j_next), 0)

    def body(carry):
        j, _ = carry
        go = stage(0, 1, j)
        go = lax.cond(go > 0, lambda: stage(1, 0, j - 1), lambda: jnp.int32(0))
        return j - 2, go

    for hh in range(2):
        s_ref[0, hh] = s_first[hh]
    lax.while_loop(cond, body, (j_start, needed(j_first)))

    outs = []
    for hh in range(2):
        acc = acc_ref[hh]
        outs.append(acc[0:HEAD_DIM, :] / acc[HEAD_DIM:HEAD_DIM + 1, :])
    o_ref[0] = jnp.concatenate(outs, axis=0).T.astype(BF16)


def _fox(qkb, cpre, qa, ka, vat):
    b, _, l, _ = qa.shape
    tq, tk = min(Q_TILE, l), vat.shape[-1]
    smem = pl.BlockSpec(memory_space=pltpu.SMEM)
    return pl.pallas_call(
        functools.partial(_fox_body, tq=tq, tk=tk, n_pre=l // tk + 1),
        grid=(b, FOX_HEADS // 2, l // tq),
        in_specs=[
            smem, smem,
            pl.BlockSpec((1, 2, tq, LANES), lambda bi, hp, i: (bi, hp, i, 0)),
            pl.BlockSpec((1, 2, l, LANES), lambda bi, hp, i: (bi, hp, 0, 0)),
            pl.BlockSpec((1, 2, l // tk, LANES, tk), lambda bi, hp, i: (bi, hp, 0, 0, 0)),
        ],
        out_specs=pl.BlockSpec((1, tq, LANES), lambda bi, hp, i: (bi, i, hp)),
        out_shape=jax.ShapeDtypeStruct((b, l, FOX_W), BF16),
        scratch_shapes=[pltpu.VMEM((2, 1, tq), F32), pltpu.VMEM((2, LANES, tq), F32),
                        pltpu.VMEM((2, 2, tk, tq), F32)],
        compiler_params=_params(("arbitrary", "arbitrary", "arbitrary")),
        name="fox",
    )(qkb, cpre, qa, ka, vat)


def _pad_lanes(v, width=LANES):
    return jnp.pad(v, (0, width - v.shape[0])).reshape(1, width)


def _placement():
    m = np.zeros((LANES, FOX_HEADS * LANES), np.float32)
    for p in range(N_PIECES):
        for h in range(FOX_HEADS):
            m[p * FOX_HEADS + h, h * LANES + Q_C_LANE + p] = 1.0
            m[p * FOX_HEADS + h, h * LANES + K_ONE_LANE + N_PIECES + p] = -1.0
    return jnp.asarray(m, BF16)


def _tri(n):
    return jnp.asarray(np.tril(np.ones((n, n), np.float32)), BF16)


def _chunk_sums():
    t = np.arange(CHUNK)[:, None]
    r = np.arange(CHUNK)[None, :]
    first = (t // SUB) * SUB
    s_pair, t_pair = _sub_chunk_pairs()
    rows = [r <= t, r < first, r < first + SUB,
            (r > s_pair[:, None]) & (r <= t_pair[:, None])]
    m = np.concatenate(rows, axis=0).astype(np.float32)
    return jnp.asarray(np.tile(m, (1, N_PIECES)), BF16)


def _sub_chunk_pairs():
    s = np.repeat(np.arange(CHUNK), SUB)
    t = (s // SUB) * SUB + np.tile(np.arange(SUB), CHUNK)
    return s, t


def _causal_replicate():
    s, t = _sub_chunk_pairs()
    m = (np.arange(CHUNK)[None, :] == s[:, None]) & (t >= s)[:, None]
    return jnp.asarray(m.astype(np.float32), BF16)


def _layer_operands(l, mix_norm, w_mix_in, conv_w, hgrn_lb_logits, hgrn_out_gain,
                    fox_q_gain, fox_k_gain, fox_f_bias, w_mix_out):
    w = w_mix_in[l]
    o_h = 3 * CONV_CH
    o_q = o_h + 4 * HGRN_W
    o_k, o_v, o_f = o_q + FOX_W, o_q + 2 * FOX_W, o_q + 3 * FOX_W
    head_blocks = np.kron(np.eye(HGRN_HEADS, dtype=np.float32),
                          np.ones((HEAD_DIM, HEAD_DIM), np.float32))
    wo = w_mix_out[l].astype(BF16)
    q_scale = fox_q_gain[l] * (HEAD_DIM ** -0.5 * LOG2E)
    return {
        "mix_norm": mix_norm[l].reshape(1, D_MODEL),
        "wc": w[:, :o_h].astype(BF16),
        "conv_w": jnp.pad(conv_w[l], ((0, SUBLANES - CONV_WIDTH), (0, 0))),
        "wh": w[:, o_h:o_q].astype(BF16),
        "wq": w[:, o_q:o_k].astype(BF16),
        "wk": w[:, o_k:o_v].astype(BF16),
        "wvt": w[:, o_v:o_f].T.astype(BF16),
        "wf": jnp.pad(w[:, o_f:], ((0, 0), (0, LANES - FOX_HEADS))).astype(BF16),
        "fbias": _pad_lanes(fox_f_bias[l]),
        "gq": jnp.tile(q_scale, LANES // HEAD_DIM).reshape(1, LANES),
        "gk": jnp.tile(fox_k_gain[l], LANES // HEAD_DIM).reshape(1, LANES),
        "qkb": (1.01 * HEAD_DIM * jnp.max(jnp.abs(q_scale))
                * jnp.max(jnp.abs(fox_k_gain[l]))).reshape(1),
        "tri": _tri(KV_TILE),
        "place": _placement(),
        "lb_logits": hgrn_lb_logits,
        "hgrn_gain": jnp.tile(hgrn_out_gain[l], HGRN_HEADS).reshape(1, HGRN_W),
        "cum64": _chunk_sums(),
        "rep64": _causal_replicate(),
        "bd": jnp.asarray(head_blocks, BF16),
        "bmask": jnp.asarray(head_blocks),
        "wo_c": wo[:CONV_CH],
        "wo_h": wo[CONV_CH:CONV_CH + HGRN_W],
        "wo_f": wo[CONV_CH + HGRN_W:],
    }


def kernel(x, ffn1_norm, ffn1_w_in, ffn1_w_out, mix_norm, w_mix_in, conv_w, hgrn_lb_logits,
           hgrn_out_gain, fox_q_gain, fox_k_gain, fox_f_bias, w_mix_out, ffn2_norm, ffn2_w_in,
           ffn2_w_out):
    b, l, d = x.shape
    assert d == D_MODEL and l % ROW_TILE == 0 and ROW_TILE % KV_TILE == 0
    assert l % Q_TILE == 0 and Q_TILE % KV_TILE == 0 and (b * l) % min(FFN_ROWS, b * l) == 0
    assert l % HGRN_ROWS == 0 and HGRN_ROWS % (CHUNKS_PER_TRIP * CHUNK) == 0
    depth = ffn1_norm.shape[0]
    tm = min(ROW_TILE, l)
    n = b * l
    w1_in, w1_out = ffn1_w_in.astype(BF16), ffn1_w_out.astype(BF16)
    w2_in, w2_out = ffn2_w_in.astype(BF16), ffn2_w_out.astype(BF16)
    for layer in range(depth):
        p = _layer_operands(layer, mix_norm, w_mix_in, conv_w, hgrn_lb_logits, hgrn_out_gain,
                            fox_q_gain, fox_k_gain, fox_f_bias, w_mix_out)
        x2 = _ffn(x.reshape(n, d), ffn1_norm[layer].reshape(1, d), w1_in, w1_out, layer)
        yc, hh, qa, ka, vat, cend = _mix_in(x2.reshape(b, l, d), p)
        yh = _hgrn(hh, p, layer)
        cend = cend[:, :, :tm // KV_TILE, :FOX_HEADS].reshape(b, l // KV_TILE, FOX_HEADS)
        cpre = jnp.pad(cend.transpose(0, 2, 1), ((0, 0), (0, 0), (1, 0))) * LOG2E
        yf = _fox(p["qkb"], cpre.reshape(-1), qa, ka, vat)
        x2 = _mix_out_ffn(x2, yc.reshape(n, CONV_CH), yh.reshape(n, HGRN_W), yf.reshape(n, FOX_W),
                          p["wo_c"], p["wo_h"], p["wo_f"], ffn2_norm[layer].reshape(1, d),
                          w2_in, w2_out, layer)
        x = x2.reshape(b, l, d)
    return x
```

```python
import functools

import jax
import jax.numpy as jnp
import numpy as np
from jax import lax
from jax.experimental import pallas as pl
from jax.experimental.pallas import tpu as pltpu

F32 = jnp.float32
BF16 = jnp.bfloat16

D_MODEL = 1024
D_FF = 2816
HEAD_DIM = 64
CONV_CH = 256
CONV_WIDTH = 3
HGRN_HEADS = 4
HGRN_W = 256
FOX_HEADS = 8
FOX_W = 512
CHUNK = 64
EPS = 1e-6
MASK_VALUE = -1e30

LANES = 128
SUBLANES = 8
MXU_TILE = 256
FF_TILE = MXU_TILE
N_FF_TILES = D_FF // FF_TILE
ROW_TILE = 1024
FFN_ROWS = 1024
Q_TILE = 512
KV_TILE = 256
N_NEAR = 3
HGRN_ROWS = 512
SUB = 8
CHUNKS_PER_TRIP = 8
VMEM_LIMIT = 56 * 1024 * 1024

Q_C_LANE = HEAD_DIM
K_ONE_LANE = HEAD_DIM
N_PIECES = 3
LOG2E = 1.4426950408889634
SKIP_BELOW = -138.0


def _dot(a, b):
    return jnp.dot(a, b, preferred_element_type=F32)


def _dot_nt(a, b):
    return lax.dot_general(a, b, (((1,), (1,)), ((), ())), preferred_element_type=F32)


def _dot_tn(a, b):
    return lax.dot_general(a, b, (((0,), (0,)), ((), ())), preferred_element_type=F32)


def _rms_norm(x, gain):
    inv = lax.rsqrt(jnp.mean(x * x, axis=-1, keepdims=True) + EPS)
    return x * inv * gain


def _sigmoid(x):
    return 1.0 / (1.0 + jnp.exp(-x))


def _log_sigmoid(x):
    return jnp.minimum(x, 0.0) - jnp.log(1.0 + jnp.exp(-jnp.abs(x)))


def _split3(x):
    hi = x.astype(BF16)
    r = x - hi.astype(F32)
    mid = r.astype(BF16)
    lo = (r - mid.astype(F32)).astype(BF16)
    return hi, mid, lo


def _const_spec(shape):
    nd = len(shape)
    return pl.BlockSpec(shape, lambda *_: (0,) * nd, pipeline_mode=pl.Buffered(1))


def _layer_spec(stacked_shape, layer):
    nd = len(stacked_shape)
    return pl.BlockSpec((None,) + tuple(stacked_shape[1:]), lambda *_: (layer,) + (0,) * (nd - 1),
                        pipeline_mode=pl.Buffered(1))


def _params(sem):
    return pltpu.CompilerParams(dimension_semantics=sem, vmem_limit_bytes=VMEM_LIMIT)


def _swiglu_half_step(x, g_ref, wi_ref, wo_ref, o_ref, acc_ref):
    xn = _rms_norm(x, g_ref[...]).astype(BF16)
    for c in range(N_FF_TILES):
        gate = _dot(xn, wi_ref[:, c * FF_TILE:(c + 1) * FF_TILE])
        up = _dot(xn, wi_ref[:, D_FF + c * FF_TILE:D_FF + (c + 1) * FF_TILE])
        act = (gate * _sigmoid(gate) * up).astype(BF16)
        part = _dot(act, wo_ref[c * FF_TILE:(c + 1) * FF_TILE, :])
        if c == 0:
            acc_ref[...] = part
        else:
            acc_ref[...] += part
    o_ref[...] = x + 0.5 * acc_ref[...]


def _ffn_body(x_ref, g_ref, wi_ref, wo_ref, o_ref, acc_ref):
    _swiglu_half_step(x_ref[...], g_ref, wi_ref, wo_ref, o_ref, acc_ref)


def _mix_out_ffn_body(x_ref, yc_ref, yh_ref, yft_ref, wc_ref, wh_ref, wf_ref,
                      g_ref, wi_ref, wo_ref, o_ref, acc_ref):
    x = (x_ref[...] + _dot(yc_ref[...], wc_ref[...]) + _dot(yh_ref[...], wh_ref[...])
         + _dot_tn(yft_ref[0], wf_ref[...]))
    _swiglu_half_step(x, g_ref, wi_ref, wo_ref, o_ref, acc_ref)


def _mix_out_ffn(x2d, yc, yh, yft, wc, wh, wf, gain, wi, wo, layer):
    n = x2d.shape[0]
    tm = min(FFN_ROWS, n)
    per_seq = yft.shape[2] // tm

    def row(w):
        return pl.BlockSpec((tm, w), lambda i: (i, 0))

    consts = [wc, wh, wf, gain]
    return pl.pallas_call(
        _mix_out_ffn_body,
        grid=(n // tm,),
        in_specs=[row(D_MODEL), row(CONV_CH), row(HGRN_W),
                  pl.BlockSpec((1, FOX_W, tm), lambda i: (i // per_seq, 0, i % per_seq))]
        + [_const_spec(c.shape) for c in consts]
        + [_layer_spec(wi.shape, layer), _layer_spec(wo.shape, layer)],
        out_specs=row(D_MODEL),
        out_shape=jax.ShapeDtypeStruct((n, D_MODEL), F32),
        scratch_shapes=[pltpu.VMEM((tm, D_MODEL), F32)],
        compiler_params=_params(("arbitrary",)),
        name="mix_out_ffn",
    )(x2d, yc, yh, yft, *consts, wi, wo)


def _ffn(x2d, gain, wi, wo, layer):
    n = x2d.shape[0]
    tm = min(FFN_ROWS, n)
    row = pl.BlockSpec((tm, D_MODEL), lambda i: (i, 0))
    return pl.pallas_call(
        _ffn_body,
        grid=(n // tm,),
        in_specs=[row, _const_spec(gain.shape), _layer_spec(wi.shape, layer),
                  _layer_spec(wo.shape, layer)],
        out_specs=row,
        out_shape=jax.ShapeDtypeStruct((n, D_MODEL), F32),
        scratch_shapes=[pltpu.VMEM((tm, D_MODEL), F32)],
        compiler_params=_params(("arbitrary",)),
        name="ffn",
    )(x2d, gain, wi, wo)


def _mix_in_body(x_ref, g_ref, wc_ref, cw_ref, wh_ref, wq_ref, wk_ref, wvt_ref, wf_ref,
                 fb_ref, gq_ref, gk_ref, tri_ref, place_ref,
                 yc_ref, hh_ref, qa_ref, ka_ref, vat_ref, cend_ref, ubuf_ref, carry_ref, *, tm):
    @pl.when(pl.program_id(1) == 0)
    def _():
        ubuf_ref[0:SUBLANES, :] = jnp.zeros((SUBLANES, CONV_CH), F32)
        carry_ref[...] = jnp.zeros_like(carry_ref)

    xn = _rms_norm(x_ref[0], g_ref[...]).astype(BF16)

    lf = _log_sigmoid(_dot(xn, wf_ref[...]) + fb_ref[...])
    hq = _dot(xn, wq_ref[...])
    hk = _dot(xn, wk_ref[...])
    lf3 = jnp.concatenate(_split3(lf), axis=1)
    last = carry_ref[...]
    blocks, ends = [], []
    for n in range(tm // KV_TILE):
        part = _dot(tri_ref[...], lf3[n * KV_TILE:(n + 1) * KV_TILE])
        blocks.append(last + (part[:, 0:LANES] + part[:, LANES:2 * LANES] + part[:, 2 * LANES:]))
        last = blocks[-1][KV_TILE - 1:KV_TILE, :]
        ends.append(last)
    c = jnp.concatenate(blocks, axis=0)
    carry_ref[...] = last
    cend_ref[0, 0] = jnp.concatenate(
        ends + [jnp.zeros((SUBLANES - len(ends), LANES), F32)], axis=0)
    vt = _dot_nt(wvt_ref[...], xn).astype(BF16)

    lane = lax.broadcasted_iota(jnp.int32, (1, LANES), 1)
    c_hi, c_mid, c_lo = [piece.astype(F32) for piece in _split3(c * LOG2E)]
    packed = jnp.where(lane < FOX_HEADS, c_hi,
                       jnp.where(lane < 2 * FOX_HEADS, pltpu.roll(c_mid, FOX_HEADS, axis=1),
                                 pltpu.roll(c_lo, 2 * FOX_HEADS, axis=1)))
    placed = _dot(packed.astype(BF16), place_ref[...])

    low = lane < HEAD_DIM
    q_takes_c = jnp.logical_and(lane >= Q_C_LANE, lane < Q_C_LANE + N_PIECES)
    k_takes_c = jnp.logical_and(lane >= K_ONE_LANE + N_PIECES, lane < K_ONE_LANE + 2 * N_PIECES)
    one_q = k_takes_c.astype(F32)
    one_k = q_takes_c.astype(F32)

    def head_pair_norm(x2, gain2):
        sq = x2 * x2
        ss_lo = jnp.sum(jnp.where(low, sq, 0.0), axis=-1, keepdims=True)
        ss_hi = jnp.sum(jnp.where(low, 0.0, sq), axis=-1, keepdims=True)
        inv = lax.rsqrt(jnp.where(low, ss_lo, ss_hi) * (1.0 / HEAD_DIM) + EPS)
        return x2 * inv * gain2

    for pair in range(FOX_HEADS // 2):
        cols = slice(pair * LANES, (pair + 1) * LANES)
        qn = head_pair_norm(hq[:, cols], gq_ref[...])
        kn = head_pair_norm(hk[:, cols], gk_ref[...])
        for half in range(2):
            h = 2 * pair + half
            extra = placed[:, h * LANES:(h + 1) * LANES]
            q_h = qn if half == 0 else pltpu.roll(qn, HEAD_DIM, axis=1)
            k_h = kn if half == 0 else pltpu.roll(kn, HEAD_DIM, axis=1)
            qa_ref[0, h] = jnp.where(low, q_h, jnp.where(q_takes_c, extra, one_q)).astype(BF16)
            ka_ref[0, h] = jnp.where(low, k_h, jnp.where(k_takes_c, extra, one_k)).astype(BF16)

    tail = (lax.broadcasted_iota(jnp.int32, (LANES - HEAD_DIM, KV_TILE), 0) == 0).astype(BF16)
    for h in range(FOX_HEADS):
        for n in range(tm // KV_TILE):
            vat_ref[0, h, n] = jnp.concatenate(
                [vt[h * HEAD_DIM:(h + 1) * HEAD_DIM, n * KV_TILE:(n + 1) * KV_TILE], tail], axis=0)

    hc = _dot(xn, wc_ref[...])
    u = hc[:, 2 * CONV_CH:3 * CONV_CH] * hc[:, 0:CONV_CH]
    ubuf_ref[SUBLANES:SUBLANES + tm, :] = u
    u1 = ubuf_ref[SUBLANES - 1:SUBLANES - 1 + tm, :]
    u2 = ubuf_ref[SUBLANES - 2:SUBLANES - 2 + tm, :]
    cw = cw_ref[...]
    conv = cw[0:1, :] * u2 + cw[1:2, :] * u1 + cw[2:3, :] * u
    yc_ref[0] = (hc[:, CONV_CH:2 * CONV_CH] * conv).astype(BF16)
    ubuf_ref[0:SUBLANES, :] = u[tm - SUBLANES:tm, :]

    hh_ref[0] = _dot(xn, wh_ref[...])


def _mix_in(x, p):
    b, l, _ = x.shape
    tm = min(ROW_TILE, l)
    nt = l // tm
    consts = [p["mix_norm"], p["wc"], p["conv_w"], p["wh"], p["wq"], p["wk"], p["wvt"], p["wf"],
              p["fbias"], p["gq"], p["gk"], p["tri"], p["place"]]
    out_shape = [
        jax.ShapeDtypeStruct((b, l, CONV_CH), BF16),
        jax.ShapeDtypeStruct((b, l, 4 * HGRN_W), F32),
        jax.ShapeDtypeStruct((b, FOX_HEADS, l, LANES), BF16),
        jax.ShapeDtypeStruct((b, FOX_HEADS, l, LANES), BF16),
        jax.ShapeDtypeStruct((b, FOX_HEADS, l // KV_TILE, LANES, KV_TILE), BF16),
        jax.ShapeDtypeStruct((b, nt, SUBLANES, LANES), F32),
    ]
    out_specs = [
        pl.BlockSpec((1, tm, CONV_CH), lambda bi, i: (bi, i, 0)),
        pl.BlockSpec((1, tm, 4 * HGRN_W), lambda bi, i: (bi, i, 0)),
        pl.BlockSpec((1, FOX_HEADS, tm, LANES), lambda bi, i: (bi, 0, i, 0)),
        pl.BlockSpec((1, FOX_HEADS, tm, LANES), lambda bi, i: (bi, 0, i, 0)),
        pl.BlockSpec((1, FOX_HEADS, tm // KV_TILE, LANES, KV_TILE), lambda bi, i: (bi, 0, i, 0, 0)),
        pl.BlockSpec((1, 1, SUBLANES, LANES), lambda bi, i: (bi, i, 0, 0)),
    ]
    return pl.pallas_call(
        functools.partial(_mix_in_body, tm=tm),
        grid=(b, nt),
        in_specs=[pl.BlockSpec((1, tm, D_MODEL), lambda bi, i: (bi, i, 0))]
        + [_const_spec(c.shape) for c in consts],
        out_specs=out_specs,
        out_shape=out_shape,
        scratch_shapes=[pltpu.VMEM((tm + SUBLANES, CONV_CH), F32), pltpu.VMEM((1, LANES), F32)],
        compiler_params=_params(("arbitrary", "arbitrary")),
        name="mix_in",
    )(x, *consts)


def _hgrn_body(q_ref, z_ref, v_ref, g_ref, lbl_ref, gain_ref, cum_ref, rep_ref, bd_ref, bmask_ref,
               o_ref, st_ref, vv_ref, lb_ref, *, layer, rows):
    @pl.when(pl.program_id(1) == 0)
    def _():
        st_ref[...] = jnp.zeros_like(st_ref)
        lbl = lbl_ref[...]
        e = jnp.exp(lbl - jnp.max(lbl, axis=0, keepdims=True))
        soft = e / jnp.sum(e, axis=0, keepdims=True)
        lb_ref[...] = jnp.clip(
            jnp.sum(soft[0:layer + 1, :], axis=0, keepdims=True) - soft[0:1, :], 0.0, 1.0)

    lb = lb_ref[...]
    bd = bd_ref[...]
    n_sub = CHUNK // SUB
    sub_t = lax.broadcasted_iota(jnp.int32, (CHUNK, HGRN_W), 0) // SUB
    sub_s = lax.broadcasted_iota(jnp.int32, (CHUNK, HGRN_W), 1) % CHUNK // SUB

    def within_chunk(ch, slot):
        rs = pl.ds(pl.multiple_of(ch * CHUNK, CHUNK), CHUNK)
        q = q_ref[0, rs, :]
        z = z_ref[0, rs, :]
        v = v_ref[0, rs, :]
        lf = (_log_sigmoid(z) + jnp.log(1.0 + lb * jnp.exp(-z))) * LOG2E
        k = (1.0 - lb) * _sigmoid(-z)
        pieces = jnp.concatenate(_split3(lf), axis=0)
        sums = _dot(cum_ref[0:3 * CHUNK, :], pieces)
        vv_ref[slot] = v
        v16 = v.astype(BF16)
        k16 = k.astype(BF16)
        yield

        bc = sums[0:CHUNK]
        bs = sums[CHUNK:2 * CHUNK]
        be = sums[2 * CHUNK:3 * CHUNK]

        q_t = q * jnp.exp2(bc - bs)
        k_t = (k * jnp.exp2(be - bc)).astype(BF16)
        lags = [q_t]
        for lag in range(2, n_sub):
            be_shift = jnp.concatenate([jnp.zeros((lag * SUB, HGRN_W), F32),
                                        be[0:CHUNK - lag * SUB]], axis=0)
            lags.append(q_t * jnp.exp2(bs - be_shift))
        q_lags = jnp.concatenate(lags, axis=0).astype(BF16)
        sc = _dot_nt(q_lags, jnp.concatenate([k_t] * HGRN_HEADS, axis=0) * bd)

        half_rows = CHUNK * SUB // 2
        pair = jnp.concatenate(
            [_dot(cum_ref[3 * CHUNK + n * half_rows:3 * CHUNK + (n + 1) * half_rows, :], pieces)
             for n in range(2)], axis=0)
        k_rep = jnp.concatenate(
            [_dot(rep_ref[n * half_rows:(n + 1) * half_rows, :], k16) for n in range(2)], axis=0)
        yield

        a_lag = jnp.zeros((CHUNK, HGRN_W), F32)
        for n, lag in enumerate(range(1, n_sub)):
            a_lag = jnp.where(sub_t - sub_s == lag, sc[n * CHUNK:(n + 1) * CHUNK], a_lag)
        o_lag = _dot(a_lag.astype(BF16), jnp.concatenate([v16] * HGRN_HEADS, axis=0) * bd)
        a = []
        for i in range(n_sub):
            rows_i = slice(i * SUB * SUB, (i + 1) * SUB * SUB)
            q_rep = jnp.concatenate([q[i * SUB:(i + 1) * SUB]] * SUB, axis=0)
            a.append((q_rep * jnp.exp2(pair[rows_i]) * k_rep[rows_i]).astype(BF16))
        r = jnp.concatenate(
            [_dot(jnp.concatenate(a[n * n_sub // 2:(n + 1) * n_sub // 2], axis=0), bd)
             for n in range(2)], axis=0)
        b_last = bc[CHUNK - 1:CHUNK, :]
        q_hat = (q * jnp.exp2(bc)).astype(BF16)
        k_hat = (k * jnp.exp2(b_last - bc)).astype(BF16)
        yield

        o_sub = []
        for i in range(n_sub):
            acc = None
            for sl in range(SUB):
                s = i * SUB + sl
                term = r[s * SUB:(s + 1) * SUB] * vv_ref[slot, s:s + 1, :]
                acc = term if acc is None else acc + term
            o_sub.append(acc)
        o = o_lag + jnp.concatenate(o_sub, axis=0)
        return rs, o, q_hat, k_hat, v16, jnp.exp2(b_last)

    def finish(parts):
        updates = [_dot_tn(v16, k_hat) for _, _, _, k_hat, v16, _ in parts]
        st = st_ref[...]
        outs = []
        for (_, o, q_hat, _, _, decay_last), update in zip(parts, updates):
            outs.append(o + _dot_nt(q_hat, st.astype(BF16)))
            st = st * decay_last + bmask_ref[...] * update
        st_ref[...] = st

        means = []
        for o in outs:
            oo = o * o
            hi = oo.astype(BF16)
            lo = (oo - hi.astype(F32)).astype(BF16)
            means.append((_dot(hi, bd) + _dot(lo, bd)) * (1.0 / HEAD_DIM))
        for (rs, *_), o, ms in zip(parts, outs, means):
            gate = g_ref[0, rs, :]
            y = o * lax.rsqrt(ms + EPS) * gain_ref[...] * (gate * _sigmoid(gate))
            o_ref[0, rs, :] = y.astype(BF16)

    def chunk_group(cc, carry):
        gens = [within_chunk(CHUNKS_PER_TRIP * cc + slot, slot) for slot in range(CHUNKS_PER_TRIP)]
        parts = [None] * CHUNKS_PER_TRIP
        while any(part is None for part in parts):
            for slot, gen in enumerate(gens):
                if parts[slot] is None:
                    try:
                        next(gen)
                    except StopIteration as done:
                        parts[slot] = done.value
        finish(parts)
        return carry

    lax.fori_loop(0, rows // (CHUNKS_PER_TRIP * CHUNK), chunk_group, 0)


def _hgrn(hh, p, layer):
    b, l, _ = hh.shape
    rows = min(HGRN_ROWS, l)
    consts = [p["lb_logits"], p["hgrn_gain"], p["cum64"], p["rep64"], p["bd"], p["bmask"]]

    def section(k):
        return pl.BlockSpec((1, rows, HGRN_W), lambda bi, i: (bi, i, k))

    return pl.pallas_call(
        functools.partial(_hgrn_body, layer=layer, rows=rows),
        grid=(b, l // rows),
        in_specs=[section(0), section(1), section(2), section(3)]
        + [_const_spec(c.shape) for c in consts],
        out_specs=pl.BlockSpec((1, rows, HGRN_W), lambda bi, i: (bi, i, 0)),
        out_shape=jax.ShapeDtypeStruct((b, l, HGRN_W), BF16),
        scratch_shapes=[pltpu.VMEM((HGRN_W, HGRN_W), F32),
                        pltpu.VMEM((CHUNKS_PER_TRIP, CHUNK, HGRN_W), F32),
                        pltpu.VMEM((1, HGRN_W), F32)],
        compiler_params=_params(("arbitrary", "arbitrary")),
        name="hgrn",
    )(hh, hh, hh, hh, *consts)


def _fox_body(qkb_ref, cpre_ref, qa_ref, ka_ref, vat_ref, o_ref, m_ref, acc_ref, s_ref,
              *, tq, tk, n_pre):
    bi, hp, qi = pl.program_id(0), pl.program_id(1), pl.program_id(2)
    per_q = tq // tk
    row = lax.broadcasted_iota(jnp.int32, (tk, tq), 0)
    col = lax.broadcasted_iota(jnp.int32, (tk, tq), 1)

    def c_before(hh, n):
        return cpre_ref[(bi * FOX_HEADS + 2 * hp + hh) * n_pre + n]

    def keys(hh, j):
        return ka_ref[0, hh, pl.ds(pl.multiple_of(j * tk, tk), tk), :]

    def scores(hh, j):
        return _dot_nt(keys(hh, j), qa_ref[0, hh])

    def update(hh, j, s, mask=None, off=None, cols=slice(None)):
        if mask is not None:
            s = jnp.where(mask, s, MASK_VALUE)
        m_old = m_ref[hh, :, cols]
        tile_max = jnp.max(s, axis=0, keepdims=True)
        m_new = jnp.maximum(m_old, tile_max if off is None else tile_max + off)
        p = jnp.exp2(s - (m_new if off is None else m_new - off)).astype(BF16)
        acc_ref[hh, :, cols] = (acc_ref[hh, :, cols] * jnp.exp2(m_old - m_new)
                                + _dot(vat_ref[0, hh, j], p))
        m_ref[hh, :, cols] = m_new

    for hh in range(2):
        m_ref[hh] = jnp.full((1, tq), MASK_VALUE, F32)
        acc_ref[hh] = jnp.zeros((LANES, tq), F32)

    diag = [(hh, dj) for dj in range(per_q) for hh in range(2)]
    s_diag = [_dot_nt(keys(hh, qi * per_q + dj), qa_ref[0, hh, dj * tk:tq, :]) for hh, dj in diag]
    near = [(hh, qi * per_q - 1 - e) for e in range(N_NEAR) for hh in range(2)]
    s_near = [scores(hh, jnp.maximum(j, 0)) for hh, j in near]
    j_start = qi * per_q - 1 - N_NEAR
    j_first = jnp.maximum(j_start, 0)
    s_first = [scores(hh, j_first) for hh in range(2)]
    for (hh, dj), s in zip(diag, s_diag):
        update(hh, qi * per_q + dj, s, mask=(row <= col)[:, 0:tq - dj * tk], cols=slice(dj * tk, tq))
    for (hh, j), s in zip(near, s_near):
        update(hh, jnp.maximum(j, 0), s, off=jnp.where(j >= 0, 0.0, MASK_VALUE))

    def needed(j):
        out = None
        for hh in range(2):
            gap = (qkb_ref[0] + c_before(hh, qi * per_q) - c_before(hh, j + 1)
                   - jnp.min(m_ref[hh]))
            out = gap >= SKIP_BELOW if out is None else jnp.logical_or(out, gap >= SKIP_BELOW)
        return out.astype(jnp.int32)

    def cond(carry):
        j, go = carry
        return jnp.logical_and(j >= 0, go > 0)

    def stage(src, dst, j):
        j_next = jnp.maximum(j - 1, 0)
        for hh in range(2):
            s_ref[dst, hh] = scores(hh, j_next)
        for hh in range(2):
            update(hh, j, s_ref[src, hh])
        return jnp.where(j >= 1, needed(j_next), 0)

    def body(carry):
        j, _ = carry
        go = stage(0, 1, j)
        go = lax.cond(go > 0, lambda: stage(1, 0, j - 1), lambda: jnp.int32(0))
        return j - 2, go

    for hh in range(2):
        s_ref[0, hh] = s_first[hh]
    lax.while_loop(cond, body, (j_start, needed(j_first)))

    outs = []
    for hh in range(2):
        acc = acc_ref[hh]
        outs.append(acc[0:HEAD_DIM, :] / acc[HEAD_DIM:HEAD_DIM + 1, :])
    o_ref[0] = jnp.concatenate(outs, axis=0).astype(BF16)


def _fox(qkb, cpre, qa, ka, vat):
    b, _, l, _ = qa.shape
    tq, tk = min(Q_TILE, l), vat.shape[-1]
    smem = pl.BlockSpec(memory_space=pltpu.SMEM)
    return pl.pallas_call(
        functools.partial(_fox_body, tq=tq, tk=tk, n_pre=l // tk + 1),
        grid=(b, FOX_HEADS // 2, l // tq),
        in_specs=[
            smem, smem,
            pl.BlockSpec((1, 2, tq, LANES), lambda bi, hp, i: (bi, hp, i, 0)),
            pl.BlockSpec((1, 2, l, LANES), lambda bi, hp, i: (bi, hp, 0, 0)),
            pl.BlockSpec((1, 2, l // tk, LANES, tk), lambda bi, hp, i: (bi, hp, 0, 0, 0)),
        ],
        out_specs=pl.BlockSpec((1, LANES, tq), lambda bi, hp, i: (bi, hp, i)),
        out_shape=jax.ShapeDtypeStruct((b, FOX_W, l), BF16),
        scratch_shapes=[pltpu.VMEM((2, 1, tq), F32), pltpu.VMEM((2, LANES, tq), F32),
                        pltpu.VMEM((2, 2, tk, tq), F32)],
        compiler_params=_params(("arbitrary", "arbitrary", "arbitrary")),
        name="fox",
    )(qkb, cpre, qa, ka, vat)


def _pad_lanes(v, width=LANES):
    return jnp.pad(v, (0, width - v.shape[0])).reshape(1, width)


def _placement():
    m = np.zeros((LANES, FOX_HEADS * LANES), np.float32)
    for p in range(N_PIECES):
        for h in range(FOX_HEADS):
            m[p * FOX_HEADS + h, h * LANES + Q_C_LANE + p] = 1.0
            m[p * FOX_HEADS + h, h * LANES + K_ONE_LANE + N_PIECES + p] = -1.0
    return jnp.asarray(m, BF16)


def _tri(n):
    return jnp.asarray(np.tril(np.ones((n, n), np.float32)), BF16)


def _chunk_sums():
    t = np.arange(CHUNK)[:, None]
    r = np.arange(CHUNK)[None, :]
    first = (t // SUB) * SUB
    s_pair, t_pair = _sub_chunk_pairs()
    rows = [r <= t, r < first, r < first + SUB,
            (r > s_pair[:, None]) & (r <= t_pair[:, None])]
    m = np.concatenate(rows, axis=0).astype(np.float32)
    return jnp.asarray(np.tile(m, (1, N_PIECES)), BF16)


def _sub_chunk_pairs():
    s = np.repeat(np.arange(CHUNK), SUB)
    t = (s // SUB) * SUB + np.tile(np.arange(SUB), CHUNK)
    return s, t


def _causal_replicate():
    s, t = _sub_chunk_pairs()
    m = (np.arange(CHUNK)[None, :] == s[:, None]) & (t >= s)[:, None]
    return jnp.asarray(m.astype(np.float32), BF16)


def _layer_operands(l, mix_norm, w_mix_in, conv_w, hgrn_lb_logits, hgrn_out_gain,
                    fox_q_gain, fox_k_gain, fox_f_bias, w_mix_out):
    w = w_mix_in[l]
    o_h = 3 * CONV_CH
    o_q = o_h + 4 * HGRN_W
    o_k, o_v, o_f = o_q + FOX_W, o_q + 2 * FOX_W, o_q + 3 * FOX_W
    head_blocks = np.kron(np.eye(HGRN_HEADS, dtype=np.float32),
                          np.ones((HEAD_DIM, HEAD_DIM), np.float32))
    wo = w_mix_out[l].astype(BF16)
    q_scale = fox_q_gain[l] * (HEAD_DIM ** -0.5 * LOG2E)
    return {
        "mix_norm": mix_norm[l].reshape(1, D_MODEL),
        "wc": w[:, :o_h].astype(BF16),
        "conv_w": jnp.pad(conv_w[l], ((0, SUBLANES - CONV_WIDTH), (0, 0))),
        "wh": w[:, o_h:o_q].astype(BF16),
        "wq": w[:, o_q:o_k].astype(BF16),
        "wk": w[:, o_k:o_v].astype(BF16),
        "wvt": w[:, o_v:o_f].T.astype(BF16),
        "wf": jnp.pad(w[:, o_f:], ((0, 0), (0, LANES - FOX_HEADS))).astype(BF16),
        "fbias": _pad_lanes(fox_f_bias[l]),
        "gq": jnp.tile(q_scale, LANES // HEAD_DIM).reshape(1, LANES),
        "gk": jnp.tile(fox_k_gain[l], LANES // HEAD_DIM).reshape(1, LANES),
        "qkb": (1.01 * HEAD_DIM * jnp.max(jnp.abs(q_scale))
                * jnp.max(jnp.abs(fox_k_gain[l]))).reshape(1),
        "tri": _tri(KV_TILE),
        "place": _placement(),
        "lb_logits": hgrn_lb_logits,
        "hgrn_gain": jnp.tile(hgrn_out_gain[l], HGRN_HEADS).reshape(1, HGRN_W),
        "cum64": _chunk_sums(),
        "rep64": _causal_replicate(),
        "bd": jnp.asarray(head_blocks, BF16),
        "bmask": jnp.asarray(head_blocks),
        "wo_c": wo[:CONV_CH],
        "wo_h": wo[CONV_CH:CONV_CH + HGRN_W],
        "wo_f": wo[CONV_CH + HGRN_W:],
    }


def kernel(x, ffn1_norm, ffn1_w_in, ffn1_w_out, mix_norm, w_mix_in, conv_w, hgrn_lb_logits,
           hgrn_out_gain, fox_q_gain, fox_k_gain, fox_f_bias, w_mix_out, ffn2_norm, ffn2_w_in,
           ffn2_w_out):
    b, l, d = x.shape
    assert d == D_MODEL and l % ROW_TILE == 0 and ROW_TILE % KV_TILE == 0
    assert l % Q_TILE == 0 and Q_TILE % KV_TILE == 0 and l % FFN_ROWS == 0
    assert l % HGRN_ROWS == 0 and HGRN_ROWS % (CHUNKS_PER_TRIP * CHUNK) == 0
    depth = ffn1_norm.shape[0]
    tm = min(ROW_TILE, l)
    n = b * l
    w1_in, w1_out = ffn1_w_in.astype(BF16), ffn1_w_out.astype(BF16)
    w2_in, w2_out = ffn2_w_in.astype(BF16), ffn2_w_out.astype(BF16)
    for layer in range(depth):
        p = _layer_operands(layer, mix_norm, w_mix_in, conv_w, hgrn_lb_logits, hgrn_out_gain,
                            fox_q_gain, fox_k_gain, fox_f_bias, w_mix_out)
        x2 = _ffn(x.reshape(n, d), ffn1_norm[layer].reshape(1, d), w1_in, w1_out, layer)
        yc, hh, qa, ka, vat, cend = _mix_in(x2.reshape(b, l, d), p)
        yh = _hgrn(hh, p, layer)
        cend = cend[:, :, :tm // KV_TILE, :FOX_HEADS].reshape(b, l // KV_TILE, FOX_HEADS)
        cpre = jnp.pad(cend.transpose(0, 2, 1), ((0, 0), (0, 0), (1, 0))) * LOG2E
        yf = _fox(p["qkb"], cpre.reshape(-1), qa, ka, vat)
        x2 = _mix_out_ffn(x2, yc.reshape(n, CONV_CH), yh.reshape(n, HGRN_W), yf,
                          p["wo_c"], p["wo_h"], p["wo_f"], ffn2_norm[layer].reshape(1, d),
                          w2_in, w2_out, layer)
        x = x2.reshape(b, l, d)
    return x
```

```python
import functools

import jax
import jax.numpy as jnp
import numpy as np
from jax import lax
from jax.experimental import pallas as pl
from jax.experimental.pallas import tpu as pltpu

F32 = jnp.float32
BF16 = jnp.bfloat16

D_MODEL = 1024
D_FF = 2816
HEAD_DIM = 64
CONV_CH = 256
CONV_WIDTH = 3
HGRN_HEADS = 4
HGRN_W = 256
FOX_HEADS = 8
FOX_W = 512
CHUNK = 64
EPS = 1e-6
MASK_VALUE = -1e30

LANES = 128
SUBLANES = 8
MXU_TILE = 256
FF_TILE = MXU_TILE
N_FF_TILES = D_FF // FF_TILE
ROW_TILE = 1024
HALF_ROWS = 512
FFN_ROWS = 1024
Q_TILE = 512
KV_TILE = 256
N_NEAR = 3
HGRN_ROWS = 512
SUB = 8
CHUNKS_PER_TRIP = 8
VMEM_LIMIT = 56 * 1024 * 1024

Q_C_LANE = HEAD_DIM
K_ONE_LANE = HEAD_DIM
N_PIECES = 3
LOG2E = 1.4426950408889634
SKIP_BELOW = -138.0


def _dot(a, b):
    return jnp.dot(a, b, preferred_element_type=F32)


def _dot_nt(a, b):
    return lax.dot_general(a, b, (((1,), (1,)), ((), ())), preferred_element_type=F32)


def _dot_tn(a, b):
    return lax.dot_general(a, b, (((0,), (0,)), ((), ())), preferred_element_type=F32)


def _rms_norm(x, gain):
    inv = lax.rsqrt(jnp.mean(x * x, axis=-1, keepdims=True) + EPS)
    return x * inv * gain


def _sigmoid(x):
    return 1.0 / (1.0 + jnp.exp(-x))


def _log_sigmoid(x):
    return jnp.minimum(x, 0.0) - jnp.log(1.0 + jnp.exp(-jnp.abs(x)))


def _split3(x):
    hi = x.astype(BF16)
    r = x - hi.astype(F32)
    mid = r.astype(BF16)
    lo = (r - mid.astype(F32)).astype(BF16)
    return hi, mid, lo


def _const_spec(shape):
    nd = len(shape)
    return pl.BlockSpec(shape, lambda *_: (0,) * nd, pipeline_mode=pl.Buffered(1))


def _layer_spec(stacked_shape, layer):
    nd = len(stacked_shape)
    return pl.BlockSpec((None,) + tuple(stacked_shape[1:]), lambda *_: (layer,) + (0,) * (nd - 1),
                        pipeline_mode=pl.Buffered(1))


def _params(sem):
    return pltpu.CompilerParams(dimension_semantics=sem, vmem_limit_bytes=VMEM_LIMIT)


def _swiglu_half_step(x, g_ref, wi_ref, wo_ref, o_ref, acc_ref):
    xn = _rms_norm(x, g_ref[...]).astype(BF16)
    for c in range(N_FF_TILES):
        gate = _dot(xn, wi_ref[:, c * FF_TILE:(c + 1) * FF_TILE])
        up = _dot(xn, wi_ref[:, D_FF + c * FF_TILE:D_FF + (c + 1) * FF_TILE])
        act = (gate * _sigmoid(gate) * up).astype(BF16)
        part = _dot(act, wo_ref[c * FF_TILE:(c + 1) * FF_TILE, :])
        if c == 0:
            acc_ref[...] = part
        else:
            acc_ref[...] += part
    o_ref[...] = x + 0.5 * acc_ref[...]


def _ffn_body(x_ref, g_ref, wi_ref, wo_ref, o_ref, acc_ref):
    _swiglu_half_step(x_ref[...], g_ref, wi_ref, wo_ref, o_ref, acc_ref)


def _mix_out_ffn_body(x_ref, yc_ref, yh_ref, yft_ref, wc_ref, wh_ref, wf_ref,
                      g_ref, wi_ref, wo_ref, o_ref, acc_ref):
    x = (x_ref[...] + _dot(yc_ref[...], wc_ref[...]) + _dot(yh_ref[...], wh_ref[...])
         + _dot_tn(yft_ref[0], wf_ref[...]))
    _swiglu_half_step(x, g_ref, wi_ref, wo_ref, o_ref, acc_ref)


def _mix_out_ffn(x2d, yc, yh, yft, wc, wh, wf, gain, wi, wo, layer):
    n = x2d.shape[0]
    tm = min(FFN_ROWS, n)
    per_seq = yft.shape[2] // tm

    def row(w):
        return pl.BlockSpec((tm, w), lambda i: (i, 0))

    consts = [wc, wh, wf, gain]
    return pl.pallas_call(
        _mix_out_ffn_body,
        grid=(n // tm,),
        in_specs=[row(D_MODEL), row(CONV_CH), row(HGRN_W),
                  pl.BlockSpec((1, FOX_W, tm), lambda i: (i // per_seq, 0, i % per_seq))]
        + [_const_spec(c.shape) for c in consts]
        + [_layer_spec(wi.shape, layer), _layer_spec(wo.shape, layer)],
        out_specs=row(D_MODEL),
        out_shape=jax.ShapeDtypeStruct((n, D_MODEL), F32),
        scratch_shapes=[pltpu.VMEM((tm, D_MODEL), F32)],
        compiler_params=_params(("arbitrary",)),
        name="mix_out_ffn",
    )(x2d, yc, yh, yft, *consts, wi, wo)


def _ffn(x2d, gain, wi, wo, layer):
    n = x2d.shape[0]
    tm = min(FFN_ROWS, n)
    row = pl.BlockSpec((tm, D_MODEL), lambda i: (i, 0))
    return pl.pallas_call(
        _ffn_body,
        grid=(n // tm,),
        in_specs=[row, _const_spec(gain.shape), _layer_spec(wi.shape, layer),
                  _layer_spec(wo.shape, layer)],
        out_specs=row,
        out_shape=jax.ShapeDtypeStruct((n, D_MODEL), F32),
        scratch_shapes=[pltpu.VMEM((tm, D_MODEL), F32)],
        compiler_params=_params(("arbitrary",)),
        name="ffn",
    )(x2d, gain, wi, wo)


def _mix_in_body(x_ref, g_ref, wc_ref, cw_ref, wh_ref, wq_ref, wk_ref, wvt_ref, wf_ref,
                 fb_ref, gq_ref, gk_ref, tri_ref, place_ref,
                 yc_ref, hh_ref, qa_ref, ka_ref, vat_ref, cend_ref, ubuf_ref, carry_ref, *, tm):
    @pl.when(pl.program_id(1) == 0)
    def _():
        ubuf_ref[0:SUBLANES, :] = jnp.zeros((SUBLANES, CONV_CH), F32)
        carry_ref[...] = jnp.zeros_like(carry_ref)

    ends = []
    for r0 in range(0, tm, HALF_ROWS):
        ends += _mix_in_rows(r0, min(HALF_ROWS, tm), x_ref, g_ref, wc_ref, cw_ref, wh_ref, wq_ref,
                             wk_ref, wvt_ref, wf_ref, fb_ref, gq_ref, gk_ref, tri_ref, place_ref,
                             yc_ref, hh_ref, qa_ref, ka_ref, vat_ref, ubuf_ref, carry_ref)
    cend_ref[0, 0] = jnp.concatenate(
        ends + [jnp.zeros((SUBLANES - len(ends), LANES), F32)], axis=0)


def _mix_in_rows(r0, tm, x_ref, g_ref, wc_ref, cw_ref, wh_ref, wq_ref, wk_ref, wvt_ref, wf_ref,
                 fb_ref, gq_ref, gk_ref, tri_ref, place_ref,
                 yc_ref, hh_ref, qa_ref, ka_ref, vat_ref, ubuf_ref, carry_ref):
    rows = slice(r0, r0 + tm)
    xn = _rms_norm(x_ref[0, rows, :], g_ref[...]).astype(BF16)

    lf = _log_sigmoid(_dot(xn, wf_ref[...]) + fb_ref[...])
    hq = _dot(xn, wq_ref[...])
    hk = _dot(xn, wk_ref[...])
    lf3 = jnp.concatenate(_split3(lf), axis=1)
    last = carry_ref[...]
    blocks, ends = [], []
    for n in range(tm // KV_TILE):
        part = _dot(tri_ref[...], lf3[n * KV_TILE:(n + 1) * KV_TILE])
        blocks.append(last + (part[:, 0:LANES] + part[:, LANES:2 * LANES] + part[:, 2 * LANES:]))
        last = blocks[-1][KV_TILE - 1:KV_TILE, :]
        ends.append(last)
    c = jnp.concatenate(blocks, axis=0)
    carry_ref[...] = last
    vt = _dot_nt(wvt_ref[...], xn).astype(BF16)

    lane = lax.broadcasted_iota(jnp.int32, (1, LANES), 1)
    c_hi, c_mid, c_lo = [piece.astype(F32) for piece in _split3(c * LOG2E)]
    packed = jnp.where(lane < FOX_HEADS, c_hi,
                       jnp.where(lane < 2 * FOX_HEADS, pltpu.roll(c_mid, FOX_HEADS, axis=1),
                                 pltpu.roll(c_lo, 2 * FOX_HEADS, axis=1)))
    placed = _dot(packed.astype(BF16), place_ref[...])

    low = lane < HEAD_DIM
    q_takes_c = jnp.logical_and(lane >= Q_C_LANE, lane < Q_C_LANE + N_PIECES)
    k_takes_c = jnp.logical_and(lane >= K_ONE_LANE + N_PIECES, lane < K_ONE_LANE + 2 * N_PIECES)
    one_q = k_takes_c.astype(F32)
    one_k = q_takes_c.astype(F32)

    def head_pair_norm(x2, gain2):
        sq = x2 * x2
        ss_lo = jnp.sum(jnp.where(low, sq, 0.0), axis=-1, keepdims=True)
        ss_hi = jnp.sum(jnp.where(low, 0.0, sq), axis=-1, keepdims=True)
        inv = lax.rsqrt(jnp.where(low, ss_lo, ss_hi) * (1.0 / HEAD_DIM) + EPS)
        return x2 * inv * gain2

    for pair in range(FOX_HEADS // 2):
        cols = slice(pair * LANES, (pair + 1) * LANES)
        qn = head_pair_norm(hq[:, cols], gq_ref[...])
        kn = head_pair_norm(hk[:, cols], gk_ref[...])
        for half in range(2):
            h = 2 * pair + half
            extra = placed[:, h * LANES:(h + 1) * LANES]
            q_h = qn if half == 0 else pltpu.roll(qn, HEAD_DIM, axis=1)
            k_h = kn if half == 0 else pltpu.roll(kn, HEAD_DIM, axis=1)
            qa_ref[0, h, rows, :] = jnp.where(low, q_h, jnp.where(q_takes_c, extra, one_q)).astype(BF16)
            ka_ref[0, h, rows, :] = jnp.where(low, k_h, jnp.where(k_takes_c, extra, one_k)).astype(BF16)

    tail = (lax.broadcasted_iota(jnp.int32, (LANES - HEAD_DIM, KV_TILE), 0) == 0).astype(BF16)
    for h in range(FOX_HEADS):
        for n in range(tm // KV_TILE):
            vat_ref[0, h, r0 // KV_TILE + n] = jnp.concatenate(
                [vt[h * HEAD_DIM:(h + 1) * HEAD_DIM, n * KV_TILE:(n + 1) * KV_TILE], tail], axis=0)

    hc = _dot(xn, wc_ref[...])
    u = hc[:, 2 * CONV_CH:3 * CONV_CH] * hc[:, 0:CONV_CH]
    ubuf_ref[SUBLANES:SUBLANES + tm, :] = u
    u1 = ubuf_ref[SUBLANES - 1:SUBLANES - 1 + tm, :]
    u2 = ubuf_ref[SUBLANES - 2:SUBLANES - 2 + tm, :]
    cw = cw_ref[...]
    conv = cw[0:1, :] * u2 + cw[1:2, :] * u1 + cw[2:3, :] * u
    yc_ref[0, rows, :] = (hc[:, CONV_CH:2 * CONV_CH] * conv).astype(BF16)
    ubuf_ref[0:SUBLANES, :] = u[tm - SUBLANES:tm, :]

    hh_ref[0, rows, :] = _dot(xn, wh_ref[...])
    return ends


def _mix_in(x, p):
    b, l, _ = x.shape
    tm = min(ROW_TILE, l)
    nt = l // tm
    consts = [p["mix_norm"], p["wc"], p["conv_w"], p["wh"], p["wq"], p["wk"], p["wvt"], p["wf"],
              p["fbias"], p["gq"], p["gk"], p["tri"], p["place"]]
    out_shape = [
        jax.ShapeDtypeStruct((b, l, CONV_CH), BF16),
        jax.ShapeDtypeStruct((b, l, 4 * HGRN_W), F32),
        jax.ShapeDtypeStruct((b, FOX_HEADS, l, LANES), BF16),
        jax.ShapeDtypeStruct((b, FOX_HEADS, l, LANES), BF16),
        jax.ShapeDtypeStruct((b, FOX_HEADS, l // KV_TILE, LANES, KV_TILE), BF16),
        jax.ShapeDtypeStruct((b, nt, SUBLANES, LANES), F32),
    ]
    out_specs = [
        pl.BlockSpec((1, tm, CONV_CH), lambda bi, i: (bi, i, 0)),
        pl.BlockSpec((1, tm, 4 * HGRN_W), lambda bi, i: (bi, i, 0)),
        pl.BlockSpec((1, FOX_HEADS, tm, LANES), lambda bi, i: (bi, 0, i, 0)),
        pl.BlockSpec((1, FOX_HEADS, tm, LANES), lambda bi, i: (bi, 0, i, 0)),
        pl.BlockSpec((1, FOX_HEADS, tm // KV_TILE, LANES, KV_TILE), lambda bi, i: (bi, 0, i, 0, 0)),
        pl.BlockSpec((1, 1, SUBLANES, LANES), lambda bi, i: (bi, i, 0, 0)),
    ]
    return pl.pallas_call(
        functools.partial(_mix_in_body, tm=tm),
        grid=(b, nt),
        in_specs=[pl.BlockSpec((1, tm, D_MODEL), lambda bi, i: (bi, i, 0))]
        + [_const_spec(c.shape) for c in consts],
        out_specs=out_specs,
        out_shape=out_shape,
        scratch_shapes=[pltpu.VMEM((min(HALF_ROWS, tm) + SUBLANES, CONV_CH), F32),
                        pltpu.VMEM((1, LANES), F32)],
        compiler_params=_params(("arbitrary", "arbitrary")),
        name="mix_in",
    )(x, *consts)


def _hgrn_body(q_ref, z_ref, v_ref, g_ref, lbl_ref, gain_ref, cum_ref, rep_ref, bd_ref, bmask_ref,
               o_ref, st_ref, vv_ref, lb_ref, *, layer, rows):
    @pl.when(pl.program_id(1) == 0)
    def _():
        st_ref[...] = jnp.zeros_like(st_ref)
        lbl = lbl_ref[...]
        e = jnp.exp(lbl - jnp.max(lbl, axis=0, keepdims=True))
        soft = e / jnp.sum(e, axis=0, keepdims=True)
        lb_ref[...] = jnp.clip(
            jnp.sum(soft[0:layer + 1, :], axis=0, keepdims=True) - soft[0:1, :], 0.0, 1.0)

    lb = lb_ref[...]
    bd = bd_ref[...]
    n_sub = CHUNK // SUB
    sub_t = lax.broadcasted_iota(jnp.int32, (CHUNK, HGRN_W), 0) // SUB
    sub_s = lax.broadcasted_iota(jnp.int32, (CHUNK, HGRN_W), 1) % CHUNK // SUB

    def within_chunk(ch, slot):
        rs = pl.ds(pl.multiple_of(ch * CHUNK, CHUNK), CHUNK)
        q = q_ref[0, rs, :]
        z = z_ref[0, rs, :]
        v = v_ref[0, rs, :]
        lf = (_log_sigmoid(z) + jnp.log(1.0 + lb * jnp.exp(-z))) * LOG2E
        k = (1.0 - lb) * _sigmoid(-z)
        pieces = jnp.concatenate(_split3(lf), axis=0)
        sums = _dot(cum_ref[0:3 * CHUNK, :], pieces)
        vv_ref[slot] = v
        v16 = v.astype(BF16)
        k16 = k.astype(BF16)
        yield

        bc = sums[0:CHUNK]
        bs = sums[CHUNK:2 * CHUNK]
        be = sums[2 * CHUNK:3 * CHUNK]

        q_t = q * jnp.exp2(bc - bs)
        k_t = (k * jnp.exp2(be - bc)).astype(BF16)
        lags = [q_t]
        for lag in range(2, n_sub):
            be_shift = jnp.concatenate([jnp.zeros((lag * SUB, HGRN_W), F32),
                                        be[0:CHUNK - lag * SUB]], axis=0)
            lags.append(q_t * jnp.exp2(bs - be_shift))
        q_lags = jnp.concatenate(lags, axis=0).astype(BF16)
        sc = _dot_nt(q_lags, jnp.concatenate([k_t] * HGRN_HEADS, axis=0) * bd)

        half_rows = CHUNK * SUB // 2
        pair = jnp.concatenate(
            [_dot(cum_ref[3 * CHUNK + n * half_rows:3 * CHUNK + (n + 1) * half_rows, :], pieces)
             for n in range(2)], axis=0)
        k_rep = jnp.concatenate(
            [_dot(rep_ref[n * half_rows:(n + 1) * half_rows, :], k16) for n in range(2)], axis=0)
        yield

        a_lag = jnp.zeros((CHUNK, HGRN_W), F32)
        for n, lag in enumerate(range(1, n_sub)):
            a_lag = jnp.where(sub_t - sub_s == lag, sc[n * CHUNK:(n + 1) * CHUNK], a_lag)
        o_lag = _dot(a_lag.astype(BF16), jnp.concatenate([v16] * HGRN_HEADS, axis=0) * bd)
        a = []
        for i in range(n_sub):
            rows_i = slice(i * SUB * SUB, (i + 1) * SUB * SUB)
            q_rep = jnp.concatenate([q[i * SUB:(i + 1) * SUB]] * SUB, axis=0)
            a.append((q_rep * jnp.exp2(pair[rows_i]) * k_rep[rows_i]).astype(BF16))
        r = jnp.concatenate(
            [_dot(jnp.concatenate(a[n * n_sub // 2:(n + 1) * n_sub // 2], axis=0), bd)
             for n in range(2)], axis=0)
        b_last = bc[CHUNK - 1:CHUNK, :]
        q_hat = (q * jnp.exp2(bc)).astype(BF16)
        k_hat = (k * jnp.exp2(b_last - bc)).astype(BF16)
        yield

        o_sub = []
        for i in range(n_sub):
            acc = None
            for sl in range(SUB):
                s = i * SUB + sl
                term = r[s * SUB:(s + 1) * SUB] * vv_ref[slot, s:s + 1, :]
                acc = term if acc is None else acc + term
            o_sub.append(acc)
        o = o_lag + jnp.concatenate(o_sub, axis=0)
        return rs, o, q_hat, k_hat, v16, jnp.exp2(b_last)

    def finish(parts):
        updates = [_dot_tn(v16, k_hat) for _, _, _, k_hat, v16, _ in parts]
        st = st_ref[...]
        outs = []
        for (_, o, q_hat, _, _, decay_last), update in zip(parts, updates):
            outs.append(o + _dot_nt(q_hat, st.astype(BF16)))
            st = st * decay_last + bmask_ref[...] * update
        st_ref[...] = st

        means = []
        for o in outs:
            oo = o * o
            hi = oo.astype(BF16)
            lo = (oo - hi.astype(F32)).astype(BF16)
            means.append((_dot(hi, bd) + _dot(lo, bd)) * (1.0 / HEAD_DIM))
        for (rs, *_), o, ms in zip(parts, outs, means):
            gate = g_ref[0, rs, :]
            y = o * lax.rsqrt(ms + EPS) * gain_ref[...] * (gate * _sigmoid(gate))
            o_ref[0, rs, :] = y.astype(BF16)

    def chunk_group(cc, carry):
        gens = [within_chunk(CHUNKS_PER_TRIP * cc + slot, slot) for slot in range(CHUNKS_PER_TRIP)]
        parts = [None] * CHUNKS_PER_TRIP
        while any(part is None for part in parts):
            for slot, gen in enumerate(gens):
                if parts[slot] is None:
                    try:
                        next(gen)
                    except StopIteration as done:
                        parts[slot] = done.value
        finish(parts)
        return carry

    lax.fori_loop(0, rows // (CHUNKS_PER_TRIP * CHUNK), chunk_group, 0)


def _hgrn(hh, p, layer):
    b, l, _ = hh.shape
    rows = min(HGRN_ROWS, l)
    consts = [p["lb_logits"], p["hgrn_gain"], p["cum64"], p["rep64"], p["bd"], p["bmask"]]

    def section(k):
        return pl.BlockSpec((1, rows, HGRN_W), lambda bi, i: (bi, i, k))

    return pl.pallas_call(
        functools.partial(_hgrn_body, layer=layer, rows=rows),
        grid=(b, l // rows),
        in_specs=[section(0), section(1), section(2), section(3)]
        + [_const_spec(c.shape) for c in consts],
        out_specs=pl.BlockSpec((1, rows, HGRN_W), lambda bi, i: (bi, i, 0)),
        out_shape=jax.ShapeDtypeStruct((b, l, HGRN_W), BF16),
        scratch_shapes=[pltpu.VMEM((HGRN_W, HGRN_W), F32),
                        pltpu.VMEM((CHUNKS_PER_TRIP, CHUNK, HGRN_W), F32),
                        pltpu.VMEM((1, HGRN_W), F32)],
        compiler_params=_params(("arbitrary", "arbitrary")),
        name="hgrn",
    )(hh, hh, hh, hh, *consts)


def _fox_body(qkb_ref, cpre_ref, qa_ref, ka_ref, vat_ref, o_ref, m_ref, acc_ref, s_ref,
              *, tq, tk, n_pre):
    bi, hp, qi = pl.program_id(0), pl.program_id(1), pl.program_id(2)
    per_q = tq // tk
    row = lax.broadcasted_iota(jnp.int32, (tk, tq), 0)
    col = lax.broadcasted_iota(jnp.int32, (tk, tq), 1)

    def c_before(hh, n):
        return cpre_ref[(bi * FOX_HEADS + 2 * hp + hh) * n_pre + n]

    def keys(hh, j):
        return ka_ref[0, hh, pl.ds(pl.multiple_of(j * tk, tk), tk), :]

    def scores(hh, j):
        return _dot_nt(keys(hh, j), qa_ref[0, hh])

    def update(hh, j, s, mask=None, off=None, cols=slice(None)):
        if mask is not None:
            s = jnp.where(mask, s, MASK_VALUE)
        m_old = m_ref[hh, :, cols]
        tile_max = jnp.max(s, axis=0, keepdims=True)
        m_new = jnp.maximum(m_old, tile_max if off is None else tile_max + off)
        p = jnp.exp2(s - (m_new if off is None else m_new - off)).astype(BF16)
        acc_ref[hh, :, cols] = (acc_ref[hh, :, cols] * jnp.exp2(m_old - m_new)
                                + _dot(vat_ref[0, hh, j], p))
        m_ref[hh, :, cols] = m_new

    for hh in range(2):
        m_ref[hh] = jnp.full((1, tq), MASK_VALUE, F32)
        acc_ref[hh] = jnp.zeros((LANES, tq), F32)

    diag = [(hh, dj) for dj in range(per_q) for hh in range(2)]
    s_diag = [_dot_nt(keys(hh, qi * per_q + dj), qa_ref[0, hh, dj * tk:tq, :]) for hh, dj in diag]
    near = [(hh, qi * per_q - 1 - e) for e in range(N_NEAR) for hh in range(2)]
    s_near = [scores(hh, jnp.maximum(j, 0)) for hh, j in near]
    j_start = qi * per_q - 1 - N_NEAR
    j_first = jnp.maximum(j_start, 0)
    s_first = [scores(hh, j_first) for hh in range(2)]
    for (hh, dj), s in zip(diag, s_diag):
        update(hh, qi * per_q + dj, s, mask=(row <= col)[:, 0:tq - dj * tk], cols=slice(dj * tk, tq))
    for (hh, j), s in zip(near, s_near):
        update(hh, jnp.maximum(j, 0), s, off=jnp.where(j >= 0, 0.0, MASK_VALUE))

    def needed(j):
        out = None
        for hh in range(2):
            gap = (qkb_ref[0] + c_before(hh, qi * per_q) - c_before(hh, j + 1)
                   - jnp.min(m_ref[hh]))
            out = gap >= SKIP_BELOW if out is None else jnp.logical_or(out, gap >= SKIP_BELOW)
        return out.astype(jnp.int32)

    def cond(carry):
        j, go = carry
        return jnp.logical_and(j >= 0, go > 0)

    def stage(src, dst, j):
        j_next = jnp.maximum(j - 1, 0)
        for hh in range(2):
            s_ref[dst, hh] = scores(hh, j_next)
        for hh in range(2):
            update(hh, j, s_ref[src, hh])
        return jnp.where(j >= 1, needed(j_next), 0)

    def body(carry):
        j, _ = carry
        go = stage(0, 1, j)
        go = lax.cond(go > 0, lambda: stage(1, 0, j - 1), lambda: jnp.int32(0))
        return j - 2, go

    for hh in range(2):
        s_ref[0, hh] = s_first[hh]
    lax.while_loop(cond, body, (j_start, needed(j_first)))

    outs = []
    for hh in range(2):
        acc = acc_ref[hh]
        outs.append(acc[0:HEAD_DIM, :] / acc[HEAD_DIM:HEAD_DIM + 1, :])
    o_ref[0] = jnp.concatenate(outs, axis=0).astype(BF16)


def _fox(qkb, cpre, qa, ka, vat):
    b, _, l, _ = qa.shape
    tq, tk = min(Q_TILE, l), vat.shape[-1]
    smem = pl.BlockSpec(memory_space=pltpu.SMEM)
    return pl.pallas_call(
        functools.partial(_fox_body, tq=tq, tk=tk, n_pre=l // tk + 1),
        grid=(b, FOX_HEADS // 2, l // tq),
        in_specs=[
            smem, smem,
            pl.BlockSpec((1, 2, tq, LANES), lambda bi, hp, i: (bi, hp, i, 0)),
            pl.BlockSpec((1, 2, l, LANES), lambda bi, hp, i: (bi, hp, 0, 0)),
            pl.BlockSpec((1, 2, l // tk, LANES, tk), lambda bi, hp, i: (bi, hp, 0, 0, 0)),
        ],
        out_specs=pl.BlockSpec((1, LANES, tq), lambda bi, hp, i: (bi, hp, i)),
        out_shape=jax.ShapeDtypeStruct((b, FOX_W, l), BF16),
        scratch_shapes=[pltpu.VMEM((2, 1, tq), F32), pltpu.VMEM((2, LANES, tq), F32),
                        pltpu.VMEM((2, 2, tk, tq), F32)],
        compiler_params=_params(("arbitrary", "arbitrary", "arbitrary")),
        name="fox",
    )(qkb, cpre, qa, ka, vat)


def _pad_lanes(v, width=LANES):
    return jnp.pad(v, (0, width - v.shape[0])).reshape(1, width)


def _placement():
    m = np.zeros((LANES, FOX_HEADS * LANES), np.float32)
    for p in range(N_PIECES):
        for h in range(FOX_HEADS):
            m[p * FOX_HEADS + h, h * LANES + Q_C_LANE + p] = 1.0
            m[p * FOX_HEADS + h, h * LANES + K_ONE_LANE + N_PIECES + p] = -1.0
    return jnp.asarray(m, BF16)


def _tri(n):
    return jnp.asarray(np.tril(np.ones((n, n), np.float32)), BF16)


def _chunk_sums():
    t = np.arange(CHUNK)[:, None]
    r = np.arange(CHUNK)[None, :]
    first = (t // SUB) * SUB
    s_pair, t_pair = _sub_chunk_pairs()
    rows = [r <= t, r < first, r < first + SUB,
            (r > s_pair[:, None]) & (r <= t_pair[:, None])]
    m = np.concatenate(rows, axis=0).astype(np.float32)
    return jnp.asarray(np.tile(m, (1, N_PIECES)), BF16)


def _sub_chunk_pairs():
    s = np.repeat(np.arange(CHUNK), SUB)
    t = (s // SUB) * SUB + np.tile(np.arange(SUB), CHUNK)
    return s, t


def _causal_replicate():
    s, t = _sub_chunk_pairs()
    m = (np.arange(CHUNK)[None, :] == s[:, None]) & (t >= s)[:, None]
    return jnp.asarray(m.astype(np.float32), BF16)


def _layer_operands(l, mix_norm, w_mix_in, conv_w, hgrn_lb_logits, hgrn_out_gain,
                    fox_q_gain, fox_k_gain, fox_f_bias, w_mix_out):
    w = w_mix_in[l]
    o_h = 3 * CONV_CH
    o_q = o_h + 4 * HGRN_W
    o_k, o_v, o_f = o_q + FOX_W, o_q + 2 * FOX_W, o_q + 3 * FOX_W
    head_blocks = np.kron(np.eye(HGRN_HEADS, dtype=np.float32),
                          np.ones((HEAD_DIM, HEAD_DIM), np.float32))
    wo = w_mix_out[l].astype(BF16)
    q_scale = fox_q_gain[l] * (HEAD_DIM ** -0.5 * LOG2E)
    return {
        "mix_norm": mix_norm[l].reshape(1, D_MODEL),
        "wc": w[:, :o_h].astype(BF16),
        "conv_w": jnp.pad(conv_w[l], ((0, SUBLANES - CONV_WIDTH), (0, 0))),
        "wh": w[:, o_h:o_q].astype(BF16),
        "wq": w[:, o_q:o_k].astype(BF16),
        "wk": w[:, o_k:o_v].astype(BF16),
        "wvt": w[:, o_v:o_f].T.astype(BF16),
        "wf": jnp.pad(w[:, o_f:], ((0, 0), (0, LANES - FOX_HEADS))).astype(BF16),
        "fbias": _pad_lanes(fox_f_bias[l]),
        "gq": jnp.tile(q_scale, LANES // HEAD_DIM).reshape(1, LANES),
        "gk": jnp.tile(fox_k_gain[l], LANES // HEAD_DIM).reshape(1, LANES),
        "qkb": (1.01 * HEAD_DIM * jnp.max(jnp.abs(q_scale))
                * jnp.max(jnp.abs(fox_k_gain[l]))).reshape(1),
        "tri": _tri(KV_TILE),
        "place": _placement(),
        "lb_logits": hgrn_lb_logits,
        "hgrn_gain": jnp.tile(hgrn_out_gain[l], HGRN_HEADS).reshape(1, HGRN_W),
        "cum64": _chunk_sums(),
        "rep64": _causal_replicate(),
        "bd": jnp.asarray(head_blocks, BF16),
        "bmask": jnp.asarray(head_blocks),
        "wo_c": wo[:CONV_CH],
        "wo_h": wo[CONV_CH:CONV_CH + HGRN_W],
        "wo_f": wo[CONV_CH + HGRN_W:],
    }


def kernel(x, ffn1_norm, ffn1_w_in, ffn1_w_out, mix_norm, w_mix_in, conv_w, hgrn_lb_logits,
           hgrn_out_gain, fox_q_gain, fox_k_gain, fox_f_bias, w_mix_out, ffn2_norm, ffn2_w_in,
           ffn2_w_out):
    b, l, d = x.shape
    assert d == D_MODEL and l % ROW_TILE == 0 and ROW_TILE % KV_TILE == 0
    assert l % Q_TILE == 0 and Q_TILE % KV_TILE == 0 and l % FFN_ROWS == 0
    assert l % HGRN_ROWS == 0 and HGRN_ROWS % (CHUNKS_PER_TRIP * CHUNK) == 0
    depth = ffn1_norm.shape[0]
    tm = min(ROW_TILE, l)
    n = b * l
    w1_in, w1_out = ffn1_w_in.astype(BF16), ffn1_w_out.astype(BF16)
    w2_in, w2_out = ffn2_w_in.astype(BF16), ffn2_w_out.astype(BF16)
    for layer in range(depth):
        p = _layer_operands(layer, mix_norm, w_mix_in, conv_w, hgrn_lb_logits, hgrn_out_gain,
                            fox_q_gain, fox_k_gain, fox_f_bias, w_mix_out)
        x2 = _ffn(x.reshape(n, d), ffn1_norm[layer].reshape(1, d), w1_in, w1_out, layer)
        yc, hh, qa, ka, vat, cend = _mix_in(x2.reshape(b, l, d), p)
        yh = _hgrn(hh, p, layer)
        cend = cend[:, :, :tm // KV_TILE, :FOX_HEADS].reshape(b, l // KV_TILE, FOX_HEADS)
        cpre = jnp.pad(cend.transpose(0, 2, 1), ((0, 0), (0, 0), (1, 0))) * LOG2E
        yf = _fox(p["qkb"], cpre.reshape(-1), qa, ka, vat)
        x2 = _mix_out_ffn(x2, yc.reshape(n, CONV_CH), yh.reshape(n, HGRN_W), yf,
                          p["wo_c"], p["wo_h"], p["wo_f"], ffn2_norm[layer].reshape(1, d),
                          w2_in, w2_out, layer)
        x = x2.reshape(b, l, d)
    return x
```

```python
import functools

import jax
import jax.numpy as jnp
import numpy as np
from jax import lax
from jax.experimental import pallas as pl
from jax.experimental.pallas import tpu as pltpu

F32 = jnp.float32
BF16 = jnp.bfloat16

D_MODEL = 1024
D_FF = 2816
HEAD_DIM = 64
CONV_CH = 256
CONV_WIDTH = 3
HGRN_HEADS = 4
HGRN_W = 256
FOX_HEADS = 8
FOX_W = 512
CHUNK = 64
EPS = 1e-6
MASK_VALUE = -1e30

LANES = 128
SUBLANES = 8
MXU_TILE = 256
FF_TILE = MXU_TILE
N_FF_TILES = D_FF // FF_TILE
ROW_TILE = 1024
HALF_ROWS = 512
FFN_ROWS = 1024
Q_TILE = 512
KV_TILE = 256
N_NEAR = 3
HGRN_ROWS = 512
SUB = 8
CHUNKS_PER_TRIP = 8
VMEM_LIMIT = 56 * 1024 * 1024

Q_C_LANE = HEAD_DIM
K_ONE_LANE = HEAD_DIM
N_PIECES = 3
LOG2E = 1.4426950408889634
SKIP_BELOW = -138.0


def _dot(a, b):
    return jnp.dot(a, b, preferred_element_type=F32)


def _dot_nt(a, b):
    return lax.dot_general(a, b, (((1,), (1,)), ((), ())), preferred_element_type=F32)


def _dot_tn(a, b):
    return lax.dot_general(a, b, (((0,), (0,)), ((), ())), preferred_element_type=F32)


def _rms_norm(x, gain):
    inv = lax.rsqrt(jnp.mean(x * x, axis=-1, keepdims=True) + EPS)
    return x * inv * gain


def _sigmoid(x):
    return 1.0 / (1.0 + jnp.exp(-x))


def _log_sigmoid(x):
    return jnp.minimum(x, 0.0) - jnp.log(1.0 + jnp.exp(-jnp.abs(x)))


def _split3(x):
    hi = x.astype(BF16)
    r = x - hi.astype(F32)
    mid = r.astype(BF16)
    lo = (r - mid.astype(F32)).astype(BF16)
    return hi, mid, lo


def _const_spec(shape):
    nd = len(shape)
    return pl.BlockSpec(shape, lambda *_: (0,) * nd, pipeline_mode=pl.Buffered(1))


def _layer_spec(stacked_shape, layer):
    nd = len(stacked_shape)
    return pl.BlockSpec((None,) + tuple(stacked_shape[1:]), lambda *_: (layer,) + (0,) * (nd - 1),
                        pipeline_mode=pl.Buffered(1))


def _params(sem):
    return pltpu.CompilerParams(dimension_semantics=sem, vmem_limit_bytes=VMEM_LIMIT)


def _swiglu_half_step(x, g_ref, wi_ref, wo_ref, o_ref, acc_ref):
    xn = _rms_norm(x, g_ref[...]).astype(BF16)
    for c in range(N_FF_TILES):
        gate = _dot(xn, wi_ref[:, c * FF_TILE:(c + 1) * FF_TILE])
        up = _dot(xn, wi_ref[:, D_FF + c * FF_TILE:D_FF + (c + 1) * FF_TILE])
        act = (gate * _sigmoid(gate) * up).astype(BF16)
        part = _dot(act, wo_ref[c * FF_TILE:(c + 1) * FF_TILE, :])
        if c == 0:
            acc_ref[...] = part
        else:
            acc_ref[...] += part
    o_ref[...] = x + 0.5 * acc_ref[...]


def _ffn_body(x_ref, g_ref, wi_ref, wo_ref, o_ref, acc_ref):
    _swiglu_half_step(x_ref[...], g_ref, wi_ref, wo_ref, o_ref, acc_ref)


def _mix_out_ffn_body(x_ref, yc_ref, yh_ref, yft_ref, wc_ref, wh_ref, wf_ref,
                      g_ref, wi_ref, wo_ref, o_ref, acc_ref):
    x = (x_ref[...] + _dot(yc_ref[...], wc_ref[...]) + _dot(yh_ref[...], wh_ref[...])
         + _dot_tn(yft_ref[0], wf_ref[...]))
    _swiglu_half_step(x, g_ref, wi_ref, wo_ref, o_ref, acc_ref)


def _mix_out_ffn(x2d, yc, yh, yft, wc, wh, wf, gain, wi, wo, layer):
    n = x2d.shape[0]
    tm = min(FFN_ROWS, n)
    per_seq = yft.shape[2] // tm

    def row(w):
        return pl.BlockSpec((tm, w), lambda i: (i, 0))

    consts = [wc, wh, wf, gain]
    return pl.pallas_call(
        _mix_out_ffn_body,
        grid=(n // tm,),
        in_specs=[row(D_MODEL), row(CONV_CH), row(HGRN_W),
                  pl.BlockSpec((1, FOX_W, tm), lambda i: (i // per_seq, 0, i % per_seq))]
        + [_const_spec(c.shape) for c in consts]
        + [_layer_spec(wi.shape, layer), _layer_spec(wo.shape, layer)],
        out_specs=row(D_MODEL),
        out_shape=jax.ShapeDtypeStruct((n, D_MODEL), F32),
        scratch_shapes=[pltpu.VMEM((tm, D_MODEL), F32)],
        compiler_params=_params(("arbitrary",)),
        name="mix_out_ffn",
    )(x2d, yc, yh, yft, *consts, wi, wo)


def _ffn(x2d, gain, wi, wo, layer):
    n = x2d.shape[0]
    tm = min(FFN_ROWS, n)
    row = pl.BlockSpec((tm, D_MODEL), lambda i: (i, 0))
    return pl.pallas_call(
        _ffn_body,
        grid=(n // tm,),
        in_specs=[row, _const_spec(gain.shape), _layer_spec(wi.shape, layer),
                  _layer_spec(wo.shape, layer)],
        out_specs=row,
        out_shape=jax.ShapeDtypeStruct((n, D_MODEL), F32),
        scratch_shapes=[pltpu.VMEM((tm, D_MODEL), F32)],
        compiler_params=_params(("arbitrary",)),
        name="ffn",
    )(x2d, gain, wi, wo)


def _mix_in_body(x_ref, g_ref, wc_ref, cw_ref, wh_ref, wq_ref, wk_ref, wvt_ref, wf_ref,
                 fb_ref, gq_ref, gk_ref, tri_ref, place_ref,
                 yc_ref, hh_ref, qa_ref, ka_ref, vat_ref, cend_ref, ubuf_ref, carry_ref, *, tm):
    @pl.when(pl.program_id(1) == 0)
    def _():
        ubuf_ref[0:SUBLANES, :] = jnp.zeros((SUBLANES, CONV_CH), F32)
        carry_ref[...] = jnp.zeros_like(carry_ref)

    ends = []
    for r0 in range(0, tm, HALF_ROWS):
        ends += _mix_in_rows(r0, min(HALF_ROWS, tm), x_ref, g_ref, wc_ref, cw_ref, wh_ref, wq_ref,
                             wk_ref, wvt_ref, wf_ref, fb_ref, gq_ref, gk_ref, tri_ref, place_ref,
                             yc_ref, hh_ref, qa_ref, ka_ref, vat_ref, ubuf_ref, carry_ref)
    cend_ref[0, 0] = jnp.concatenate(
        ends + [jnp.zeros((SUBLANES - len(ends), LANES), F32)], axis=0)


def _mix_in_rows(r0, tm, x_ref, g_ref, wc_ref, cw_ref, wh_ref, wq_ref, wk_ref, wvt_ref, wf_ref,
                 fb_ref, gq_ref, gk_ref, tri_ref, place_ref,
                 yc_ref, hh_ref, qa_ref, ka_ref, vat_ref, ubuf_ref, carry_ref):
    rows = slice(r0, r0 + tm)
    xn = _rms_norm(x_ref[0, rows, :], g_ref[...]).astype(BF16)

    lf = _log_sigmoid(_dot(xn, wf_ref[...]) + fb_ref[...])
    hq = _dot(xn, wq_ref[...])
    hk = _dot(xn, wk_ref[...])
    lf3 = jnp.concatenate(_split3(lf), axis=1)
    last = carry_ref[...]
    blocks, ends = [], []
    for n in range(tm // KV_TILE):
        part = _dot(tri_ref[...], lf3[n * KV_TILE:(n + 1) * KV_TILE])
        blocks.append(last + (part[:, 0:LANES] + part[:, LANES:2 * LANES] + part[:, 2 * LANES:]))
        last = blocks[-1][KV_TILE - 1:KV_TILE, :]
        ends.append(last)
    c = jnp.concatenate(blocks, axis=0)
    carry_ref[...] = last
    vt = _dot_nt(wvt_ref[...], xn).astype(BF16)

    lane = lax.broadcasted_iota(jnp.int32, (1, LANES), 1)
    c_hi, c_mid, c_lo = [piece.astype(F32) for piece in _split3(c * LOG2E)]
    packed = jnp.where(lane < FOX_HEADS, c_hi,
                       jnp.where(lane < 2 * FOX_HEADS, pltpu.roll(c_mid, FOX_HEADS, axis=1),
                                 pltpu.roll(c_lo, 2 * FOX_HEADS, axis=1)))
    placed = _dot(packed.astype(BF16), place_ref[...])

    low = lane < HEAD_DIM
    q_takes_c = jnp.logical_and(lane >= Q_C_LANE, lane < Q_C_LANE + N_PIECES)
    k_takes_c = jnp.logical_and(lane >= K_ONE_LANE + N_PIECES, lane < K_ONE_LANE + 2 * N_PIECES)
    one_q = k_takes_c.astype(F32)
    one_k = q_takes_c.astype(F32)

    def head_pair_norm(x2, gain2):
        sq = x2 * x2
        ss_lo = jnp.sum(jnp.where(low, sq, 0.0), axis=-1, keepdims=True)
        ss_hi = jnp.sum(jnp.where(low, 0.0, sq), axis=-1, keepdims=True)
        inv = lax.rsqrt(jnp.where(low, ss_lo, ss_hi) * (1.0 / HEAD_DIM) + EPS)
        return x2 * inv * gain2

    for pair in range(FOX_HEADS // 2):
        cols = slice(pair * LANES, (pair + 1) * LANES)
        qn = head_pair_norm(hq[:, cols], gq_ref[...])
        kn = head_pair_norm(hk[:, cols], gk_ref[...])
        for half in range(2):
            h = 2 * pair + half
            extra = placed[:, h * LANES:(h + 1) * LANES]
            q_h = qn if half == 0 else pltpu.roll(qn, HEAD_DIM, axis=1)
            k_h = kn if half == 0 else pltpu.roll(kn, HEAD_DIM, axis=1)
            qa_ref[0, h, rows, :] = jnp.where(low, q_h, jnp.where(q_takes_c, extra, one_q)).astype(BF16)
            ka_ref[0, h, rows, :] = jnp.where(low, k_h, jnp.where(k_takes_c, extra, one_k)).astype(BF16)

    tail = (lax.broadcasted_iota(jnp.int32, (LANES - HEAD_DIM, KV_TILE), 0) == 0).astype(BF16)
    for h in range(FOX_HEADS):
        for n in range(tm // KV_TILE):
            vat_ref[0, h, r0 // KV_TILE + n] = jnp.concatenate(
                [vt[h * HEAD_DIM:(h + 1) * HEAD_DIM, n * KV_TILE:(n + 1) * KV_TILE], tail], axis=0)

    hc = _dot(xn, wc_ref[...])
    u = hc[:, 2 * CONV_CH:3 * CONV_CH] * hc[:, 0:CONV_CH]
    ubuf_ref[SUBLANES:SUBLANES + tm, :] = u
    u1 = ubuf_ref[SUBLANES - 1:SUBLANES - 1 + tm, :]
    u2 = ubuf_ref[SUBLANES - 2:SUBLANES - 2 + tm, :]
    cw = cw_ref[...]
    conv = cw[0:1, :] * u2 + cw[1:2, :] * u1 + cw[2:3, :] * u
    yc_ref[0, rows, :] = (hc[:, CONV_CH:2 * CONV_CH] * conv).astype(BF16)
    ubuf_ref[0:SUBLANES, :] = u[tm - SUBLANES:tm, :]

    hh_ref[0, rows, :] = _dot(xn, wh_ref[...])
    return ends


def _mix_in(x, p):
    b, l, _ = x.shape
    tm = min(ROW_TILE, l)
    nt = l // tm
    consts = [p["mix_norm"], p["wc"], p["conv_w"], p["wh"], p["wq"], p["wk"], p["wvt"], p["wf"],
              p["fbias"], p["gq"], p["gk"], p["tri"], p["place"]]
    out_shape = [
        jax.ShapeDtypeStruct((b, l, CONV_CH), BF16),
        jax.ShapeDtypeStruct((b, l, 4 * HGRN_W), F32),
        jax.ShapeDtypeStruct((b, FOX_HEADS, l, LANES), BF16),
        jax.ShapeDtypeStruct((b, FOX_HEADS, l, LANES), BF16),
        jax.ShapeDtypeStruct((b, FOX_HEADS, l // KV_TILE, LANES, KV_TILE), BF16),
        jax.ShapeDtypeStruct((b, nt, SUBLANES, LANES), F32),
    ]
    out_specs = [
        pl.BlockSpec((1, tm, CONV_CH), lambda bi, i: (bi, i, 0)),
        pl.BlockSpec((1, tm, 4 * HGRN_W), lambda bi, i: (bi, i, 0)),
        pl.BlockSpec((1, FOX_HEADS, tm, LANES), lambda bi, i: (bi, 0, i, 0)),
        pl.BlockSpec((1, FOX_HEADS, tm, LANES), lambda bi, i: (bi, 0, i, 0)),
        pl.BlockSpec((1, FOX_HEADS, tm // KV_TILE, LANES, KV_TILE), lambda bi, i: (bi, 0, i, 0, 0)),
        pl.BlockSpec((1, 1, SUBLANES, LANES), lambda bi, i: (bi, i, 0, 0)),
    ]
    return pl.pallas_call(
        functools.partial(_mix_in_body, tm=tm),
        grid=(b, nt),
        in_specs=[pl.BlockSpec((1, tm, D_MODEL), lambda bi, i: (bi, i, 0))]
        + [_const_spec(c.shape) for c in consts],
        out_specs=out_specs,
        out_shape=out_shape,
        scratch_shapes=[pltpu.VMEM((min(HALF_ROWS, tm) + SUBLANES, CONV_CH), F32),
                        pltpu.VMEM((1, LANES), F32)],
        compiler_params=_params(("arbitrary", "arbitrary")),
        name="mix_in",
    )(x, *consts)


def _hgrn_body(q_ref, z_ref, v_ref, g_ref, lbl_ref, gain_ref, cum_ref, rep_ref, bd_ref, bmask_ref,
               o_ref, st_ref, vv_ref, lb_ref, *, layer, rows):
    @pl.when(pl.program_id(1) == 0)
    def _():
        st_ref[...] = jnp.zeros_like(st_ref)
        lbl = lbl_ref[...]
        e = jnp.exp(lbl - jnp.max(lbl, axis=0, keepdims=True))
        soft = e / jnp.sum(e, axis=0, keepdims=True)
        lb_ref[...] = jnp.clip(
            jnp.sum(soft[0:layer + 1, :], axis=0, keepdims=True) - soft[0:1, :], 0.0, 1.0)

    lb = lb_ref[...]
    bd = bd_ref[...]
    n_sub = CHUNK // SUB
    sub_t = lax.broadcasted_iota(jnp.int32, (CHUNK, HGRN_W), 0) // SUB
    sub_s = lax.broadcasted_iota(jnp.int32, (CHUNK, HGRN_W), 1) % CHUNK // SUB

    def within_chunk(ch, slot):
        rs = pl.ds(pl.multiple_of(ch * CHUNK, CHUNK), CHUNK)
        q = q_ref[0, rs, :]
        z = z_ref[0, rs, :]
        v = v_ref[0, rs, :]
        lf = (_log_sigmoid(z) + jnp.log(1.0 + lb * jnp.exp(-z))) * LOG2E
        k = (1.0 - lb) * _sigmoid(-z)
        pieces = jnp.concatenate(_split3(lf), axis=0)
        sums = _dot(cum_ref[0:3 * CHUNK, :], pieces)
        vv_ref[slot] = v
        v16 = v.astype(BF16)
        k16 = k.astype(BF16)
        yield

        bc = sums[0:CHUNK]
        bs = sums[CHUNK:2 * CHUNK]
        be = sums[2 * CHUNK:3 * CHUNK]

        q_t = q * jnp.exp2(bc - bs)
        k_t = (k * jnp.exp2(be - bc)).astype(BF16)
        lags = [q_t]
        for lag in range(2, n_sub):
            be_shift = jnp.concatenate([jnp.zeros((lag * SUB, HGRN_W), F32),
                                        be[0:CHUNK - lag * SUB]], axis=0)
            lags.append(q_t * jnp.exp2(bs - be_shift))
        q_lags = jnp.concatenate(lags, axis=0).astype(BF16)
        sc = _dot_nt(q_lags, jnp.concatenate([k_t] * HGRN_HEADS, axis=0) * bd)

        half_rows = CHUNK * SUB // 2
        pair = jnp.concatenate(
            [_dot(cum_ref[3 * CHUNK + n * half_rows:3 * CHUNK + (n + 1) * half_rows, :], pieces)
             for n in range(2)], axis=0)
        k_rep = jnp.concatenate(
            [_dot(rep_ref[n * half_rows:(n + 1) * half_rows, :], k16) for n in range(2)], axis=0)
        yield

        a_lag = jnp.zeros((CHUNK, HGRN_W), F32)
        for n, lag in enumerate(range(1, n_sub)):
            a_lag = jnp.where(sub_t - sub_s == lag, sc[n * CHUNK:(n + 1) * CHUNK], a_lag)
        o_lag = _dot(a_lag.astype(BF16), jnp.concatenate([v16] * HGRN_HEADS, axis=0) * bd)
        a = []
        for i in range(n_sub):
            rows_i = slice(i * SUB * SUB, (i + 1) * SUB * SUB)
            q_rep = jnp.concatenate([q[i * SUB:(i + 1) * SUB]] * SUB, axis=0)
            a.append((q_rep * jnp.exp2(pair[rows_i]) * k_rep[rows_i]).astype(BF16))
        r = jnp.concatenate(
            [_dot(jnp.concatenate(a[n * n_sub // 2:(n + 1) * n_sub // 2], axis=0), bd)
             for n in range(2)], axis=0)
        b_last = bc[CHUNK - 1:CHUNK, :]
        q_hat = (q * jnp.exp2(bc)).astype(BF16)
        k_hat = (k * jnp.exp2(b_last - bc)).astype(BF16)
        yield

        o_sub = []
        for i in range(n_sub):
            acc = None
            for sl in range(SUB):
                s = i * SUB + sl
                term = r[s * SUB:(s + 1) * SUB] * vv_ref[slot, s:s + 1, :]
                acc = term if acc is None else acc + term
            o_sub.append(acc)
        o = o_lag + jnp.concatenate(o_sub, axis=0)
        return rs, o, q_hat, k_hat, v16, jnp.exp2(b_last)

    def finish(parts):
        updates = [_dot_tn(v16, k_hat) for _, _, _, k_hat, v16, _ in parts]
        st = st_ref[...]
        outs = []
        for (_, o, q_hat, _, _, decay_last), update in zip(parts, updates):
            outs.append(o + _dot_nt(q_hat, st.astype(BF16)))
            st = st * decay_last + bmask_ref[...] * update
        st_ref[...] = st

        means = []
        for o in outs:
            oo = o * o
            hi = oo.astype(BF16)
            lo = (oo - hi.astype(F32)).astype(BF16)
            means.append((_dot(hi, bd) + _dot(lo, bd)) * (1.0 / HEAD_DIM))
        for (rs, *_), o, ms in zip(parts, outs, means):
            gate = g_ref[0, rs, :]
            y = o * lax.rsqrt(ms + EPS) * gain_ref[...] * (gate * _sigmoid(gate))
            o_ref[0, rs, :] = y.astype(BF16)

    def chunk_group(cc, carry):
        gens = [within_chunk(CHUNKS_PER_TRIP * cc + slot, slot) for slot in range(CHUNKS_PER_TRIP)]
        parts = [None] * CHUNKS_PER_TRIP
        while any(part is None for part in parts):
            for slot, gen in enumerate(gens):
                if parts[slot] is None:
                    try:
                        next(gen)
                    except StopIteration as done:
                        parts[slot] = done.value
        finish(parts)
        return carry

    lax.fori_loop(0, rows // (CHUNKS_PER_TRIP * CHUNK), chunk_group, 0)


def _hgrn(hh, p, layer):
    b, l, _ = hh.shape
    rows = min(HGRN_ROWS, l)
    consts = [p["lb_logits"], p["hgrn_gain"], p["cum64"], p["rep64"], p["bd"], p["bmask"]]

    def section(k):
        return pl.BlockSpec((1, rows, HGRN_W), lambda bi, i: (bi, i, k))

    return pl.pallas_call(
        functools.partial(_hgrn_body, layer=layer, rows=rows),
        grid=(b, l // rows),
        in_specs=[section(0), section(1), section(2), section(3)]
        + [_const_spec(c.shape) for c in consts],
        out_specs=pl.BlockSpec((1, rows, HGRN_W), lambda bi, i: (bi, i, 0)),
        out_shape=jax.ShapeDtypeStruct((b, l, HGRN_W), BF16),
        scratch_shapes=[pltpu.VMEM((HGRN_W, HGRN_W), F32),
                        pltpu.VMEM((CHUNKS_PER_TRIP, CHUNK, HGRN_W), F32),
                        pltpu.VMEM((1, HGRN_W), F32)],
        compiler_params=_params(("arbitrary", "arbitrary")),
        name="hgrn",
    )(hh, hh, hh, hh, *consts)


def _fox_body(qkb_ref, cpre_ref, qa_ref, ka_ref, vat_ref, o_ref, m_ref, acc_ref, s_ref,
              *, tq, tk, n_pre):
    bi, hp, qi = pl.program_id(0), pl.program_id(1), pl.program_id(2)
    per_q = tq // tk
    row = lax.broadcasted_iota(jnp.int32, (tk, tq), 0)
    col = lax.broadcasted_iota(jnp.int32, (tk, tq), 1)

    def c_before(hh, n):
        return cpre_ref[(bi * FOX_HEADS + 2 * hp + hh) * n_pre + n]

    def keys(hh, j):
        return ka_ref[0, hh, pl.ds(pl.multiple_of(j * tk, tk), tk), :]

    def scores(hh, j):
        return _dot_nt(keys(hh, j), qa_ref[0, hh])

    def update(hh, j, s, mask=None, off=None, cols=slice(None)):
        if mask is not None:
            s = jnp.where(mask, s, MASK_VALUE)
        m_old = m_ref[hh, :, cols]
        tile_max = jnp.max(s, axis=0, keepdims=True)
        m_new = jnp.maximum(m_old, tile_max if off is None else tile_max + off)
        p = jnp.exp2(s - (m_new if off is None else m_new - off)).astype(BF16)
        acc_ref[hh, :, cols] = (acc_ref[hh, :, cols] * jnp.exp2(m_old - m_new)
                                + _dot(vat_ref[0, hh, j], p))
        m_ref[hh, :, cols] = m_new

    for hh in range(2):
        m_ref[hh] = jnp.full((1, tq), MASK_VALUE, F32)
        acc_ref[hh] = jnp.zeros((LANES, tq), F32)

    diag = [(hh, dj) for dj in range(per_q) for hh in range(2)]
    s_diag = [_dot_nt(keys(hh, qi * per_q + dj), qa_ref[0, hh, dj * tk:tq, :]) for hh, dj in diag]
    near = [(hh, qi * per_q - 1 - e) for e in range(N_NEAR) for hh in range(2)]
    s_near = [scores(hh, jnp.maximum(j, 0)) for hh, j in near]
    j_start = qi * per_q - 1 - N_NEAR
    j_first = jnp.maximum(j_start, 0)
    s_first = [scores(hh, j_first) for hh in range(2)]
    for (hh, dj), s in zip(diag, s_diag):
        update(hh, qi * per_q + dj, s, mask=(row <= col)[:, 0:tq - dj * tk], cols=slice(dj * tk, tq))
    for (hh, j), s in zip(near, s_near):
        update(hh, jnp.maximum(j, 0), s, off=jnp.where(j >= 0, 0.0, MASK_VALUE))

    def needed(j):
        out = None
        for hh in range(2):
            gap = (qkb_ref[0] + c_before(hh, qi * per_q) - c_before(hh, j + 1)
                   - jnp.min(m_ref[hh]))
            out = gap >= SKIP_BELOW if out is None else jnp.logical_or(out, gap >= SKIP_BELOW)
        return out.astype(jnp.int32)

    def cond(carry):
        j, go = carry
        return jnp.logical_and(j >= 0, go > 0)

    def stage(src, dst, j):
        j_next = jnp.maximum(j - 1, 0)
        for hh in range(2):
            s_ref[dst, hh] = scores(hh, j_next)
        for hh in range(2):
            update(hh, j, s_ref[src, hh])
        return jnp.where(j >= 1, needed(j_next), 0)

    def body(carry):
        j, _ = carry
        go = stage(0, 1, j)
        go = lax.cond(go > 0, lambda: stage(1, 0, j - 1), lambda: jnp.int32(0))
        return j - 2, go

    for hh in range(2):
        s_ref[0, hh] = s_first[hh]
    lax.while_loop(cond, body, (j_start, needed(j_first)))

    outs = []
    for hh in range(2):
        acc = acc_ref[hh]
        outs.append(acc[0:HEAD_DIM, :] / acc[HEAD_DIM:HEAD_DIM + 1, :])
    o_ref[0] = jnp.concatenate(outs, axis=0).astype(BF16)


def _fox(qkb, cpre, qa, ka, vat):
    b, _, l, _ = qa.shape
    tq, tk = min(Q_TILE, l), vat.shape[-1]
    smem = pl.BlockSpec(memory_space=pltpu.SMEM)
    return pl.pallas_call(
        functools.partial(_fox_body, tq=tq, tk=tk, n_pre=l // tk + 1),
        grid=(b, FOX_HEADS // 2, l // tq),
        in_specs=[
            smem, smem,
            pl.BlockSpec((1, 2, tq, LANES), lambda bi, hp, i: (bi, hp, i, 0)),
            pl.BlockSpec((1, 2, l, LANES), lambda bi, hp, i: (bi, hp, 0, 0)),
            pl.BlockSpec((1, 2, l // tk, LANES, tk), lambda bi, hp, i: (bi, hp, 0, 0, 0)),
        ],
        out_specs=pl.BlockSpec((1, LANES, tq), lambda bi, hp, i: (bi, hp, i)),
        out_shape=jax.ShapeDtypeStruct((b, FOX_W, l), BF16),
        scratch_shapes=[pltpu.VMEM((2, 1, tq), F32), pltpu.VMEM((2, LANES, tq), F32),
                        pltpu.VMEM((2, 2, tk, tq), F32)],
        compiler_params=_params(("arbitrary", "arbitrary", "arbitrary")),
        name="fox",
    )(qkb, cpre, qa, ka, vat)


def _pad_lanes(v, width=LANES):
    return jnp.pad(v, (0, width - v.shape[0])).reshape(1, width)


def _placement():
    m = np.zeros((LANES, FOX_HEADS * LANES), np.float32)
    for p in range(N_PIECES):
        for h in range(FOX_HEADS):
            m[p * FOX_HEADS + h, h * LANES + Q_C_LANE + p] = 1.0
            m[p * FOX_HEADS + h, h * LANES + K_ONE_LANE + N_PIECES + p] = -1.0
    return jnp.asarray(m, BF16)


def _tri(n):
    return jnp.asarray(np.tril(np.ones((n, n), np.float32)), BF16)


def _chunk_sums():
    t = np.arange(CHUNK)[:, None]
    r = np.arange(CHUNK)[None, :]
    first = (t // SUB) * SUB
    s_pair, t_pair = _sub_chunk_pairs()
    rows = [r <= t, r < first, r < first + SUB,
            (r > s_pair[:, None]) & (r <= t_pair[:, None])]
    m = np.concatenate(rows, axis=0).astype(np.float32)
    return jnp.asarray(np.tile(m, (1, N_PIECES)), BF16)


def _sub_chunk_pairs():
    s = np.repeat(np.arange(CHUNK), SUB)
    t = (s // SUB) * SUB + np.tile(np.arange(SUB), CHUNK)
    return s, t


def _causal_replicate():
    s, t = _sub_chunk_pairs()
    m = (np.arange(CHUNK)[None, :] == s[:, None]) & (t >= s)[:, None]
    return jnp.asarray(m.astype(np.float32), BF16)


def _layer_operands(l, mix_norm, w_mix_in, conv_w, hgrn_lb_logits, hgrn_out_gain,
                    fox_q_gain, fox_k_gain, fox_f_bias, w_mix_out):
    w = w_mix_in[l]
    o_h = 3 * CONV_CH
    o_q = o_h + 4 * HGRN_W
    o_k, o_v, o_f = o_q + FOX_W, o_q + 2 * FOX_W, o_q + 3 * FOX_W
    head_blocks = np.kron(np.eye(HGRN_HEADS, dtype=np.float32),
                          np.ones((HEAD_DIM, HEAD_DIM), np.float32))
    wo = w_mix_out[l].astype(BF16)
    q_scale = fox_q_gain[l] * (HEAD_DIM ** -0.5 * LOG2E)
    order = jnp.argsort(fox_f_bias[l])
    cols = (order[:, None] * HEAD_DIM + jnp.arange(HEAD_DIM)[None, :]).reshape(-1)
    return {
        "mix_norm": mix_norm[l].reshape(1, D_MODEL),
        "wc": w[:, :o_h].astype(BF16),
        "conv_w": jnp.pad(conv_w[l], ((0, SUBLANES - CONV_WIDTH), (0, 0))),
        "wh": w[:, o_h:o_q].astype(BF16),
        "wq": jnp.take(w[:, o_q:o_k], cols, axis=1).astype(BF16),
        "wk": jnp.take(w[:, o_k:o_v], cols, axis=1).astype(BF16),
        "wvt": jnp.take(w[:, o_v:o_f], cols, axis=1).T.astype(BF16),
        "wf": jnp.pad(jnp.take(w[:, o_f:], order, axis=1),
                      ((0, 0), (0, LANES - FOX_HEADS))).astype(BF16),
        "fbias": _pad_lanes(jnp.take(fox_f_bias[l], order)),
        "gq": jnp.tile(q_scale, LANES // HEAD_DIM).reshape(1, LANES),
        "gk": jnp.tile(fox_k_gain[l], LANES // HEAD_DIM).reshape(1, LANES),
        "qkb": (1.01 * HEAD_DIM * jnp.max(jnp.abs(q_scale))
                * jnp.max(jnp.abs(fox_k_gain[l]))).reshape(1),
        "tri": _tri(KV_TILE),
        "place": _placement(),
        "lb_logits": hgrn_lb_logits,
        "hgrn_gain": jnp.tile(hgrn_out_gain[l], HGRN_HEADS).reshape(1, HGRN_W),
        "cum64": _chunk_sums(),
        "rep64": _causal_replicate(),
        "bd": jnp.asarray(head_blocks, BF16),
        "bmask": jnp.asarray(head_blocks),
        "wo_c": wo[:CONV_CH],
        "wo_h": wo[CONV_CH:CONV_CH + HGRN_W],
        "wo_f": jnp.take(wo[CONV_CH + HGRN_W:], cols, axis=0),
    }


def kernel(x, ffn1_norm, ffn1_w_in, ffn1_w_out, mix_norm, w_mix_in, conv_w, hgrn_lb_logits,
           hgrn_out_gain, fox_q_gain, fox_k_gain, fox_f_bias, w_mix_out, ffn2_norm, ffn2_w_in,
           ffn2_w_out):
    b, l, d = x.shape
    assert d == D_MODEL and l % ROW_TILE == 0 and ROW_TILE % KV_TILE == 0
    assert l % Q_TILE == 0 and Q_TILE % KV_TILE == 0 and l % FFN_ROWS == 0
    assert l % HGRN_ROWS == 0 and HGRN_ROWS % (CHUNKS_PER_TRIP * CHUNK) == 0
    depth = ffn1_norm.shape[0]
    tm = min(ROW_TILE, l)
    n = b * l
    w1_in, w1_out = ffn1_w_in.astype(BF16), ffn1_w_out.astype(BF16)
    w2_in, w2_out = ffn2_w_in.astype(BF16), ffn2_w_out.astype(BF16)
    for layer in range(depth):
        p = _layer_operands(layer, mix_norm, w_mix_in, conv_w, hgrn_lb_logits, hgrn_out_gain,
                            fox_q_gain, fox_k_gain, fox_f_bias, w_mix_out)
        x2 = _ffn(x.reshape(n, d), ffn1_norm[layer].reshape(1, d), w1_in, w1_out, layer)
        yc, hh, qa, ka, vat, cend = _mix_in(x2.reshape(b, l, d), p)
        yh = _hgrn(hh, p, layer)
        cend = cend[:, :, :tm // KV_TILE, :FOX_HEADS].reshape(b, l // KV_TILE, FOX_HEADS)
        cpre = jnp.pad(cend.transpose(0, 2, 1), ((0, 0), (0, 0), (1, 0))) * LOG2E
        yf = _fox(p["qkb"], cpre.reshape(-1), qa, ka, vat)
        x2 = _mix_out_ffn(x2, yc.reshape(n, CONV_CH), yh.reshape(n, HGRN_W), yf,
                          p["wo_c"], p["wo_h"], p["wo_f"], ffn2_norm[layer].reshape(1, d),
                          w2_in, w2_out, layer)
        x = x2.reshape(b, l, d)
    return x
```

```python
import functools

import jax
import jax.numpy as jnp
import numpy as np
from jax import lax
from jax.experimental import pallas as pl
from jax.experimental.pallas import tpu as pltpu

F32 = jnp.float32
BF16 = jnp.bfloat16

D_MODEL = 1024
D_FF = 2816
HEAD_DIM = 64
CONV_CH = 256
CONV_WIDTH = 3
HGRN_HEADS = 4
HGRN_W = 256
FOX_HEADS = 8
FOX_W = 512
CHUNK = 64
EPS = 1e-6
MASK_VALUE = -1e30

LANES = 128
SUBLANES = 8
MXU_TILE = 256
FF_TILE = MXU_TILE
N_FF_TILES = D_FF // FF_TILE
ROW_TILE = 1024
HALF_ROWS = 512
FFN_ROWS = 1024
Q_TILE = 512
KV_TILE = 256
N_NEAR = 3
HGRN_ROWS = 512
SUB = 8
CHUNKS_PER_TRIP = 8
VMEM_LIMIT = 56 * 1024 * 1024

Q_C_LANE = HEAD_DIM
K_ONE_LANE = HEAD_DIM
N_PIECES = 3
LOG2E = 1.4426950408889634
SKIP_BELOW = -138.0


def _dot(a, b):
    return jnp.dot(a, b, preferred_element_type=F32)


def _dot_nt(a, b):
    return lax.dot_general(a, b, (((1,), (1,)), ((), ())), preferred_element_type=F32)


def _dot_tn(a, b):
    return lax.dot_general(a, b, (((0,), (0,)), ((), ())), preferred_element_type=F32)


def _rms_norm(x, gain):
    inv = lax.rsqrt(jnp.mean(x * x, axis=-1, keepdims=True) + EPS)
    return x * inv * gain


def _sigmoid(x):
    return 1.0 / (1.0 + jnp.exp(-x))


def _log_sigmoid(x):
    return jnp.minimum(x, 0.0) - jnp.log(1.0 + jnp.exp(-jnp.abs(x)))


def _split3(x):
    hi = x.astype(BF16)
    r = x - hi.astype(F32)
    mid = r.astype(BF16)
    lo = (r - mid.astype(F32)).astype(BF16)
    return hi, mid, lo


def _const_spec(shape):
    nd = len(shape)
    return pl.BlockSpec(shape, lambda *_: (0,) * nd, pipeline_mode=pl.Buffered(1))


def _layer_spec(stacked_shape, layer):
    nd = len(stacked_shape)
    return pl.BlockSpec((None,) + tuple(stacked_shape[1:]), lambda *_: (layer,) + (0,) * (nd - 1),
                        pipeline_mode=pl.Buffered(1))


def _params(sem):
    return pltpu.CompilerParams(dimension_semantics=sem, vmem_limit_bytes=VMEM_LIMIT)


def _swiglu_half_step(x, g_ref, wi_ref, wo_ref, o_ref, acc_ref):
    xn = _rms_norm(x, g_ref[...]).astype(BF16)
    for c in range(N_FF_TILES):
        gate = _dot(xn, wi_ref[:, c * FF_TILE:(c + 1) * FF_TILE])
        up = _dot(xn, wi_ref[:, D_FF + c * FF_TILE:D_FF + (c + 1) * FF_TILE])
        act = (gate * _sigmoid(gate) * up).astype(BF16)
        part = _dot(act, wo_ref[c * FF_TILE:(c + 1) * FF_TILE, :])
        if c == 0:
            acc_ref[...] = part
        else:
            acc_ref[...] += part
    o_ref[...] = x + 0.5 * acc_ref[...]


def _ffn_body(x_ref, g_ref, wi_ref, wo_ref, o_ref, acc_ref):
    _swiglu_half_step(x_ref[...], g_ref, wi_ref, wo_ref, o_ref, acc_ref)


def _mix_out_ffn_body(x_ref, yc_ref, yh_ref, yft_ref, wc_ref, wh_ref, wf_ref,
                      g_ref, wi_ref, wo_ref, o_ref, acc_ref):
    x = (x_ref[...] + _dot(yc_ref[...], wc_ref[...]) + _dot(yh_ref[...], wh_ref[...])
         + _dot_tn(yft_ref[0], wf_ref[...]))
    _swiglu_half_step(x, g_ref, wi_ref, wo_ref, o_ref, acc_ref)


def _mix_out_ffn(x2d, yc, yh, yft, wc, wh, wf, gain, wi, wo, layer):
    n = x2d.shape[0]
    tm = min(FFN_ROWS, n)
    per_seq = yft.shape[2] // tm

    def row(w):
        return pl.BlockSpec((tm, w), lambda i: (i, 0))

    consts = [wc, wh, wf, gain]
    return pl.pallas_call(
        _mix_out_ffn_body,
        grid=(n // tm,),
        in_specs=[row(D_MODEL), row(CONV_CH), row(HGRN_W),
                  pl.BlockSpec((1, FOX_W, tm), lambda i: (i // per_seq, 0, i % per_seq))]
        + [_const_spec(c.shape) for c in consts]
        + [_layer_spec(wi.shape, layer), _layer_spec(wo.shape, layer)],
        out_specs=row(D_MODEL),
        out_shape=jax.ShapeDtypeStruct((n, D_MODEL), F32),
        scratch_shapes=[pltpu.VMEM((tm, D_MODEL), F32)],
        compiler_params=_params(("arbitrary",)),
        name="mix_out_ffn",
    )(x2d, yc, yh, yft, *consts, wi, wo)


def _ffn(x2d, gain, wi, wo, layer):
    n = x2d.shape[0]
    tm = min(FFN_ROWS, n)
    row = pl.BlockSpec((tm, D_MODEL), lambda i: (i, 0))
    return pl.pallas_call(
        _ffn_body,
        grid=(n // tm,),
        in_specs=[row, _const_spec(gain.shape), _layer_spec(wi.shape, layer),
                  _layer_spec(wo.shape, layer)],
        out_specs=row,
        out_shape=jax.ShapeDtypeStruct((n, D_MODEL), F32),
        scratch_shapes=[pltpu.VMEM((tm, D_MODEL), F32)],
        compiler_params=_params(("arbitrary",)),
        name="ffn",
    )(x2d, gain, wi, wo)


def _mix_in_body(x_ref, g_ref, wc_ref, cw_ref, wh_ref, wq_ref, wk_ref, wvt_ref, wf_ref,
                 fb_ref, gq_ref, gk_ref, tri_ref, place_ref,
                 yc_ref, hh_ref, qa_ref, ka_ref, vat_ref, cend_ref, ubuf_ref, carry_ref, *, tm):
    @pl.when(pl.program_id(1) == 0)
    def _():
        ubuf_ref[0:SUBLANES, :] = jnp.zeros((SUBLANES, CONV_CH), F32)
        carry_ref[...] = jnp.zeros_like(carry_ref)

    ends = []
    for r0 in range(0, tm, HALF_ROWS):
        ends += _mix_in_rows(r0, min(HALF_ROWS, tm), x_ref, g_ref, wc_ref, cw_ref, wh_ref, wq_ref,
                             wk_ref, wvt_ref, wf_ref, fb_ref, gq_ref, gk_ref, tri_ref, place_ref,
                             yc_ref, hh_ref, qa_ref, ka_ref, vat_ref, ubuf_ref, carry_ref)
    cend_ref[0, 0] = jnp.concatenate(
        ends + [jnp.zeros((SUBLANES - len(ends), LANES), F32)], axis=0)


def _mix_in_rows(r0, tm, x_ref, g_ref, wc_ref, cw_ref, wh_ref, wq_ref, wk_ref, wvt_ref, wf_ref,
                 fb_ref, gq_ref, gk_ref, tri_ref, place_ref,
                 yc_ref, hh_ref, qa_ref, ka_ref, vat_ref, ubuf_ref, carry_ref):
    rows = slice(r0, r0 + tm)
    xn = _rms_norm(x_ref[0, rows, :], g_ref[...]).astype(BF16)

    lf = _log_sigmoid(_dot(xn, wf_ref[...]) + fb_ref[...])
    hq = _dot(xn, wq_ref[...])
    hk = _dot(xn, wk_ref[...])
    lf3 = jnp.concatenate(_split3(lf), axis=1)
    last = carry_ref[...]
    blocks, ends = [], []
    for n in range(tm // KV_TILE):
        part = _dot(tri_ref[...], lf3[n * KV_TILE:(n + 1) * KV_TILE])
        blocks.append(last + (part[:, 0:LANES] + part[:, LANES:2 * LANES] + part[:, 2 * LANES:]))
        last = blocks[-1][KV_TILE - 1:KV_TILE, :]
        ends.append(last)
    c = jnp.concatenate(blocks, axis=0)
    carry_ref[...] = last
    vt = _dot_nt(wvt_ref[...], xn).astype(BF16)

    lane = lax.broadcasted_iota(jnp.int32, (1, LANES), 1)
    c_hi, c_mid, c_lo = [piece.astype(F32) for piece in _split3(c * LOG2E)]
    packed = jnp.where(lane < FOX_HEADS, c_hi,
                       jnp.where(lane < 2 * FOX_HEADS, pltpu.roll(c_mid, FOX_HEADS, axis=1),
                                 pltpu.roll(c_lo, 2 * FOX_HEADS, axis=1)))
    placed = _dot(packed.astype(BF16), place_ref[...])

    low = lane < HEAD_DIM
    q_takes_c = jnp.logical_and(lane >= Q_C_LANE, lane < Q_C_LANE + N_PIECES)
    k_takes_c = jnp.logical_and(lane >= K_ONE_LANE + N_PIECES, lane < K_ONE_LANE + 2 * N_PIECES)
    one_q = k_takes_c.astype(F32)
    one_k = q_takes_c.astype(F32)

    def head_pair_norm(x2, gain2):
        sq = x2 * x2
        ss_lo = jnp.sum(jnp.where(low, sq, 0.0), axis=-1, keepdims=True)
        ss_hi = jnp.sum(jnp.where(low, 0.0, sq), axis=-1, keepdims=True)
        inv = lax.rsqrt(jnp.where(low, ss_lo, ss_hi) * (1.0 / HEAD_DIM) + EPS)
        return x2 * inv * gain2

    for pair in range(FOX_HEADS // 2):
        cols = slice(pair * LANES, (pair + 1) * LANES)
        qn = head_pair_norm(hq[:, cols], gq_ref[...])
        kn = head_pair_norm(hk[:, cols], gk_ref[...])
        for half in range(2):
            h = 2 * pair + half
            extra = placed[:, h * LANES:(h + 1) * LANES]
            q_h = qn if half == 0 else pltpu.roll(qn, HEAD_DIM, axis=1)
            k_h = kn if half == 0 else pltpu.roll(kn, HEAD_DIM, axis=1)
            qa_ref[0, h, rows, :] = jnp.where(low, q_h, jnp.where(q_takes_c, extra, one_q)).astype(BF16)
            ka_ref[0, h, rows, :] = jnp.where(low, k_h, jnp.where(k_takes_c, extra, one_k)).astype(BF16)

    tail = (lax.broadcasted_iota(jnp.int32, (LANES - HEAD_DIM, KV_TILE), 0) == 0).astype(BF16)
    for h in range(FOX_HEADS):
        for n in range(tm // KV_TILE):
            vat_ref[0, h, r0 // KV_TILE + n] = jnp.concatenate(
                [vt[h * HEAD_DIM:(h + 1) * HEAD_DIM, n * KV_TILE:(n + 1) * KV_TILE], tail], axis=0)

    hc = _dot(xn, wc_ref[...])
    u = hc[:, 2 * CONV_CH:3 * CONV_CH] * hc[:, 0:CONV_CH]
    ubuf_ref[SUBLANES:SUBLANES + tm, :] = u
    u1 = ubuf_ref[SUBLANES - 1:SUBLANES - 1 + tm, :]
    u2 = ubuf_ref[SUBLANES - 2:SUBLANES - 2 + tm, :]
    cw = cw_ref[...]
    conv = cw[0:1, :] * u2 + cw[1:2, :] * u1 + cw[2:3, :] * u
    yc_ref[0, rows, :] = (hc[:, CONV_CH:2 * CONV_CH] * conv).astype(BF16)
    ubuf_ref[0:SUBLANES, :] = u[tm - SUBLANES:tm, :]

    hh_ref[0, rows, :] = _dot(xn, wh_ref[...])
    return ends


def _mix_in(x, p):
    b, l, _ = x.shape
    tm = min(ROW_TILE, l)
    nt = l // tm
    consts = [p["mix_norm"], p["wc"], p["conv_w"], p["wh"], p["wq"], p["wk"], p["wvt"], p["wf"],
              p["fbias"], p["gq"], p["gk"], p["tri"], p["place"]]
    out_shape = [
        jax.ShapeDtypeStruct((b, l, CONV_CH), BF16),
        jax.ShapeDtypeStruct((b, l, 4 * HGRN_W), F32),
        jax.ShapeDtypeStruct((b, FOX_HEADS, l, LANES), BF16),
        jax.ShapeDtypeStruct((b, FOX_HEADS, l, LANES), BF16),
        jax.ShapeDtypeStruct((b, FOX_HEADS, l // KV_TILE, LANES, KV_TILE), BF16),
        jax.ShapeDtypeStruct((b, nt, SUBLANES, LANES), F32),
    ]
    out_specs = [
        pl.BlockSpec((1, tm, CONV_CH), lambda bi, i: (bi, i, 0)),
        pl.BlockSpec((1, tm, 4 * HGRN_W), lambda bi, i: (bi, i, 0)),
        pl.BlockSpec((1, FOX_HEADS, tm, LANES), lambda bi, i: (bi, 0, i, 0)),
        pl.BlockSpec((1, FOX_HEADS, tm, LANES), lambda bi, i: (bi, 0, i, 0)),
        pl.BlockSpec((1, FOX_HEADS, tm // KV_TILE, LANES, KV_TILE), lambda bi, i: (bi, 0, i, 0, 0)),
        pl.BlockSpec((1, 1, SUBLANES, LANES), lambda bi, i: (bi, i, 0, 0)),
    ]
    return pl.pallas_call(
        functools.partial(_mix_in_body, tm=tm),
        grid=(b, nt),
        in_specs=[pl.BlockSpec((1, tm, D_MODEL), lambda bi, i: (bi, i, 0))]
        + [_const_spec(c.shape) for c in consts],
        out_specs=out_specs,
        out_shape=out_shape,
        scratch_shapes=[pltpu.VMEM((min(HALF_ROWS, tm) + SUBLANES, CONV_CH), F32),
                        pltpu.VMEM((1, LANES), F32)],
        compiler_params=_params(("arbitrary", "arbitrary")),
        name="mix_in",
    )(x, *consts)


def _hgrn_body(q_ref, z_ref, v_ref, g_ref, lbl_ref, gain_ref, cum_ref, rep_ref, bd_ref, bmask_ref,
               o_ref, st_ref, vv_ref, lb_ref, *, layer, rows):
    @pl.when(pl.program_id(1) == 0)
    def _():
        st_ref[...] = jnp.zeros_like(st_ref)
        lbl = lbl_ref[...]
        e = jnp.exp(lbl - jnp.max(lbl, axis=0, keepdims=True))
        soft = e / jnp.sum(e, axis=0, keepdims=True)
        lb_ref[...] = jnp.clip(
            jnp.sum(soft[0:layer + 1, :], axis=0, keepdims=True) - soft[0:1, :], 0.0, 1.0)

    lb = lb_ref[...]
    bd = bd_ref[...]
    n_sub = CHUNK // SUB
    sub_t = lax.broadcasted_iota(jnp.int32, (CHUNK, HGRN_W), 0) // SUB
    sub_s = lax.broadcasted_iota(jnp.int32, (CHUNK, HGRN_W), 1) % CHUNK // SUB

    def within_chunk(ch, slot):
        rs = pl.ds(pl.multiple_of(ch * CHUNK, CHUNK), CHUNK)
        q = q_ref[0, rs, :]
        z = z_ref[0, rs, :]
        v = v_ref[0, rs, :]
        lf = (_log_sigmoid(z) + jnp.log(1.0 + lb * jnp.exp(-z))) * LOG2E
        k = (1.0 - lb) * _sigmoid(-z)
        pieces = jnp.concatenate(_split3(lf), axis=0)
        sums = _dot(cum_ref[0:3 * CHUNK, :], pieces)
        vv_ref[slot] = v
        v16 = v.astype(BF16)
        k16 = k.astype(BF16)
        yield

        bc = sums[0:CHUNK]
        bs = sums[CHUNK:2 * CHUNK]
        be = sums[2 * CHUNK:3 * CHUNK]

        q_t = q * jnp.exp2(bc - bs)
        k_t = (k * jnp.exp2(be - bc)).astype(BF16)
        lags = [q_t]
        for lag in range(2, n_sub):
            be_shift = jnp.concatenate([jnp.zeros((lag * SUB, HGRN_W), F32),
                                        be[0:CHUNK - lag * SUB]], axis=0)
            lags.append(q_t * jnp.exp2(bs - be_shift))
        q_lags = jnp.concatenate(lags, axis=0).astype(BF16)
        sc = _dot_nt(q_lags, jnp.concatenate([k_t] * HGRN_HEADS, axis=0) * bd)

        half_rows = CHUNK * SUB // 2
        pair = jnp.concatenate(
            [_dot(cum_ref[3 * CHUNK + n * half_rows:3 * CHUNK + (n + 1) * half_rows, :], pieces)
             for n in range(2)], axis=0)
        k_rep = jnp.concatenate(
            [_dot(rep_ref[n * half_rows:(n + 1) * half_rows, :], k16) for n in range(2)], axis=0)
        yield

        a_lag = jnp.zeros((CHUNK, HGRN_W), F32)
        for n, lag in enumerate(range(1, n_sub)):
            a_lag = jnp.where(sub_t - sub_s == lag, sc[n * CHUNK:(n + 1) * CHUNK], a_lag)
        o_lag = _dot(a_lag.astype(BF16), jnp.concatenate([v16] * HGRN_HEADS, axis=0) * bd)
        a = []
        for i in range(n_sub):
            rows_i = slice(i * SUB * SUB, (i + 1) * SUB * SUB)
            q_rep = jnp.concatenate([q[i * SUB:(i + 1) * SUB]] * SUB, axis=0)
            a.append((q_rep * jnp.exp2(pair[rows_i]) * k_rep[rows_i]).astype(BF16))
        r = jnp.concatenate(
            [_dot(jnp.concatenate(a[n * n_sub // 2:(n + 1) * n_sub // 2], axis=0), bd)
             for n in range(2)], axis=0)
        b_last = bc[CHUNK - 1:CHUNK, :]
        q_hat = (q * jnp.exp2(bc)).astype(BF16)
        k_hat = (k * jnp.exp2(b_last - bc)).astype(BF16)
        yield

        o_sub = []
        for i in range(n_sub):
            acc = None
            for sl in range(SUB):
                s = i * SUB + sl
                term = r[s * SUB:(s + 1) * SUB] * vv_ref[slot, s:s + 1, :]
                acc = term if acc is None else acc + term
            o_sub.append(acc)
        o = o_lag + jnp.concatenate(o_sub, axis=0)
        return rs, o, q_hat, k_hat, v16, jnp.exp2(b_last)

    def finish(parts):
        updates = [_dot_tn(v16, k_hat) for _, _, _, k_hat, v16, _ in parts]
        st = st_ref[...]
        outs = []
        for (_, o, q_hat, _, _, decay_last), update in zip(parts, updates):
            outs.append(o + _dot_nt(q_hat, st.astype(BF16)))
            st = st * decay_last + bmask_ref[...] * update
        st_ref[...] = st

        means = []
        for o in outs:
            oo = o * o
            hi = oo.astype(BF16)
            lo = (oo - hi.astype(F32)).astype(BF16)
            means.append((_dot(hi, bd) + _dot(lo, bd)) * (1.0 / HEAD_DIM))
        for (rs, *_), o, ms in zip(parts, outs, means):
            gate = g_ref[0, rs, :]
            y = o * lax.rsqrt(ms + EPS) * gain_ref[...] * (gate * _sigmoid(gate))
            o_ref[0, rs, :] = y.astype(BF16)

    def chunk_group(cc, carry):
        gens = [within_chunk(CHUNKS_PER_TRIP * cc + slot, slot) for slot in range(CHUNKS_PER_TRIP)]
        parts = [None] * CHUNKS_PER_TRIP
        while any(part is None for part in parts):
            for slot, gen in enumerate(gens):
                if parts[slot] is None:
                    try:
                        next(gen)
                    except StopIteration as done:
                        parts[slot] = done.value
        finish(parts)
        return carry

    lax.fori_loop(0, rows // (CHUNKS_PER_TRIP * CHUNK), chunk_group, 0)


def _hgrn(hh, p, layer):
    b, l, _ = hh.shape
    rows = min(HGRN_ROWS, l)
    consts = [p["lb_logits"], p["hgrn_gain"], p["cum64"], p["rep64"], p["bd"], p["bmask"]]

    def section(k):
        return pl.BlockSpec((1, rows, HGRN_W), lambda bi, i: (bi, i, k))

    return pl.pallas_call(
        functools.partial(_hgrn_body, layer=layer, rows=rows),
        grid=(b, l // rows),
        in_specs=[section(0), section(1), section(2), section(3)]
        + [_const_spec(c.shape) for c in consts],
        out_specs=pl.BlockSpec((1, rows, HGRN_W), lambda bi, i: (bi, i, 0)),
        out_shape=jax.ShapeDtypeStruct((b, l, HGRN_W), BF16),
        scratch_shapes=[pltpu.VMEM((HGRN_W, HGRN_W), F32),
                        pltpu.VMEM((CHUNKS_PER_TRIP, CHUNK, HGRN_W), F32),
                        pltpu.VMEM((1, HGRN_W), F32)],
        compiler_params=_params(("arbitrary", "arbitrary")),
        name="hgrn",
    )(hh, hh, hh, hh, *consts)


def _fox_body(qkb_ref, cpre_ref, qa_ref, ka_ref, vat_ref, o_ref, m_ref, acc_ref, s_ref,
              *, tq, tk, n_pre):
    bi, hp, qi = pl.program_id(0), pl.program_id(1), pl.program_id(2)
    per_q = tq // tk
    row = lax.broadcasted_iota(jnp.int32, (tk, tq), 0)
    col = lax.broadcasted_iota(jnp.int32, (tk, tq), 1)

    def c_before(hh, n):
        return cpre_ref[(bi * FOX_HEADS + 2 * hp + hh) * n_pre + n]

    def keys(hh, j):
        return ka_ref[0, hh, pl.ds(pl.multiple_of(j * tk, tk), tk), :]

    def scores(hh, j):
        return _dot_nt(keys(hh, j), qa_ref[0, hh])

    def update(hh, j, s, mask=None, off=None, cols=slice(None)):
        if mask is not None:
            s = jnp.where(mask, s, MASK_VALUE)
        m_old = m_ref[hh, :, cols]
        tile_max = jnp.max(s, axis=0, keepdims=True)
        m_new = jnp.maximum(m_old, tile_max if off is None else tile_max + off)
        p = jnp.exp2(s - (m_new if off is None else m_new - off)).astype(BF16)
        acc_ref[hh, :, cols] = (acc_ref[hh, :, cols] * jnp.exp2(m_old - m_new)
                                + _dot(vat_ref[0, hh, j], p))
        m_ref[hh, :, cols] = m_new

    for hh in range(2):
        m_ref[hh] = jnp.full((1, tq), MASK_VALUE, F32)
        acc_ref[hh] = jnp.zeros((LANES, tq), F32)

    diag = [(hh, dj) for dj in range(per_q) for hh in range(2)]
    s_diag = [_dot_nt(keys(hh, qi * per_q + dj), qa_ref[0, hh, dj * tk:tq, :]) for hh, dj in diag]
    near = [(hh, qi * per_q - 1 - e) for e in range(N_NEAR) for hh in range(2)]
    s_near = [scores(hh, jnp.maximum(j, 0)) for hh, j in near]
    j_start = qi * per_q - 1 - N_NEAR
    j_first = jnp.maximum(j_start, 0)
    s_first = [scores(hh, j_first) for hh in range(2)]
    for (hh, dj), s in zip(diag, s_diag):
        update(hh, qi * per_q + dj, s, mask=(row <= col)[:, 0:tq - dj * tk], cols=slice(dj * tk, tq))
    for (hh, j), s in zip(near, s_near):
        update(hh, jnp.maximum(j, 0), s, off=jnp.where(j >= 0, 0.0, MASK_VALUE))

    def needed(j):
        out = None
        for hh in range(2):
            gap = (qkb_ref[0] + c_before(hh, qi * per_q) - c_before(hh, j + 1)
                   - jnp.min(m_ref[hh]))
            out = gap >= SKIP_BELOW if out is None else jnp.logical_or(out, gap >= SKIP_BELOW)
        return out.astype(jnp.int32)

    def cond(carry):
        j, go = carry
        return jnp.logical_and(j >= 0, go > 0)

    def stage(src, dst, j):
        j_next = jnp.maximum(j - 1, 0)
        for hh in range(2):
            s_ref[dst, hh] = scores(hh, j_next)
        for hh in range(2):
            update(hh, j, s_ref[src, hh])
        return jnp.where(j >= 1, needed(j_next), 0)

    def body(carry):
        j, _ = carry
        go = stage(0, 1, j)
        go = lax.cond(go > 0, lambda: stage(1, 0, j - 1), lambda: jnp.int32(0))
        return j - 2, go

    for hh in range(2):
        s_ref[0, hh] = s_first[hh]
    lax.while_loop(cond, body, (j_start, needed(j_first)))

    outs = []
    for hh in range(2):
        acc = acc_ref[hh]
        outs.append(acc[0:HEAD_DIM, :] / acc[HEAD_DIM:HEAD_DIM + 1, :])
    o_ref[0] = jnp.concatenate(outs, axis=0).astype(BF16)


def _fox(qkb, cpre, qa, ka, vat):
    b, _, l, _ = qa.shape
    tq, tk = min(Q_TILE, l), vat.shape[-1]
    smem = pl.BlockSpec(memory_space=pltpu.SMEM)
    return pl.pallas_call(
        functools.partial(_fox_body, tq=tq, tk=tk, n_pre=l // tk + 1),
        grid=(b, FOX_HEADS // 2, l // tq),
        in_specs=[
            smem, smem,
            pl.BlockSpec((1, 2, tq, LANES), lambda bi, hp, i: (bi, hp, i, 0)),
            pl.BlockSpec((1, 2, l, LANES), lambda bi, hp, i: (bi, hp, 0, 0)),
            pl.BlockSpec((1, 2, l // tk, LANES, tk), lambda bi, hp, i: (bi, hp, 0, 0, 0)),
        ],
        out_specs=pl.BlockSpec((1, LANES, tq), lambda bi, hp, i: (bi, hp, i)),
        out_shape=jax.ShapeDtypeStruct((b, FOX_W, l), BF16),
        scratch_shapes=[pltpu.VMEM((2, 1, tq), F32), pltpu.VMEM((2, LANES, tq), F32),
                        pltpu.VMEM((2, 2, tk, tq), F32)],
        compiler_params=_params(("arbitrary", "arbitrary", "arbitrary")),
        name="fox",
    )(qkb, cpre, qa, ka, vat)


def _pad_lanes(v, width=LANES):
    return jnp.pad(v, (0, width - v.shape[0])).reshape(1, width)


def _placement():
    m = np.zeros((LANES, FOX_HEADS * LANES), np.float32)
    for p in range(N_PIECES):
        for h in range(FOX_HEADS):
            m[p * FOX_HEADS + h, h * LANES + Q_C_LANE + p] = 1.0
            m[p * FOX_HEADS + h, h * LANES + K_ONE_LANE + N_PIECES + p] = -1.0
    return jnp.asarray(m, BF16)


def _tri(n):
    return jnp.asarray(np.tril(np.ones((n, n), np.float32)), BF16)


def _chunk_sums():
    t = np.arange(CHUNK)[:, None]
    r = np.arange(CHUNK)[None, :]
    first = (t // SUB) * SUB
    s_pair, t_pair = _sub_chunk_pairs()
    rows = [r <= t, r < first, r < first + SUB,
            (r > s_pair[:, None]) & (r <= t_pair[:, None])]
    m = np.concatenate(rows, axis=0).astype(np.float32)
    return jnp.asarray(np.tile(m, (1, N_PIECES)), BF16)


def _sub_chunk_pairs():
    s = np.repeat(np.arange(CHUNK), SUB)
    t = (s // SUB) * SUB + np.tile(np.arange(SUB), CHUNK)
    return s, t


def _causal_replicate():
    s, t = _sub_chunk_pairs()
    m = (np.arange(CHUNK)[None, :] == s[:, None]) & (t >= s)[:, None]
    return jnp.asarray(m.astype(np.float32), BF16)


def _layer_operands(l, mix_norm, w_mix_in, conv_w, hgrn_lb_logits, hgrn_out_gain,
                    fox_q_gain, fox_k_gain, fox_f_bias, w_mix_out):
    w = w_mix_in[l]
    o_h = 3 * CONV_CH
    o_q = o_h + 4 * HGRN_W
    o_k, o_v, o_f = o_q + FOX_W, o_q + 2 * FOX_W, o_q + 3 * FOX_W
    head_blocks = np.kron(np.eye(HGRN_HEADS, dtype=np.float32),
                          np.ones((HEAD_DIM, HEAD_DIM), np.float32))
    wo = w_mix_out[l].astype(BF16)
    q_scale = fox_q_gain[l] * (HEAD_DIM ** -0.5 * LOG2E)
    order = jnp.argsort(fox_f_bias[l])
    cols = (order[:, None] * HEAD_DIM + jnp.arange(HEAD_DIM)[None, :]).reshape(-1)
    qkv = jnp.take(w[:, o_q:o_f], jnp.concatenate([cols, FOX_W + cols, 2 * FOX_W + cols]),
                   axis=1).astype(BF16)
    return {
        "mix_norm": mix_norm[l].reshape(1, D_MODEL),
        "wc": w[:, :o_h].astype(BF16),
        "conv_w": jnp.pad(conv_w[l], ((0, SUBLANES - CONV_WIDTH), (0, 0))),
        "wh": w[:, o_h:o_q].astype(BF16),
        "wq": qkv[:, 0:FOX_W],
        "wk": qkv[:, FOX_W:2 * FOX_W],
        "wvt": qkv[:, 2 * FOX_W:].T,
        "wf": jnp.pad(jnp.take(w[:, o_f:], order, axis=1),
                      ((0, 0), (0, LANES - FOX_HEADS))).astype(BF16),
        "fbias": _pad_lanes(jnp.take(fox_f_bias[l], order)),
        "gq": jnp.tile(q_scale, LANES // HEAD_DIM).reshape(1, LANES),
        "gk": jnp.tile(fox_k_gain[l], LANES // HEAD_DIM).reshape(1, LANES),
        "qkb": (1.01 * HEAD_DIM * jnp.max(jnp.abs(q_scale))
                * jnp.max(jnp.abs(fox_k_gain[l]))).reshape(1),
        "tri": _tri(KV_TILE),
        "place": _placement(),
        "lb_logits": hgrn_lb_logits,
        "hgrn_gain": jnp.tile(hgrn_out_gain[l], HGRN_HEADS).reshape(1, HGRN_W),
        "cum64": _chunk_sums(),
        "rep64": _causal_replicate(),
        "bd": jnp.asarray(head_blocks, BF16),
        "bmask": jnp.asarray(head_blocks),
        "wo_c": wo[:CONV_CH],
        "wo_h": wo[CONV_CH:CONV_CH + HGRN_W],
        "wo_f": jnp.take(wo[CONV_CH + HGRN_W:], cols, axis=0),
    }


def kernel(x, ffn1_norm, ffn1_w_in, ffn1_w_out, mix_norm, w_mix_in, conv_w, hgrn_lb_logits,
           hgrn_out_gain, fox_q_gain, fox_k_gain, fox_f_bias, w_mix_out, ffn2_norm, ffn2_w_in,
           ffn2_w_out):
    b, l, d = x.shape
    assert d == D_MODEL and l % ROW_TILE == 0 and ROW_TILE % KV_TILE == 0
    assert l % Q_TILE == 0 and Q_TILE % KV_TILE == 0 and l % FFN_ROWS == 0
    assert l % HGRN_ROWS == 0 and HGRN_ROWS % (CHUNKS_PER_TRIP * CHUNK) == 0
    depth = ffn1_norm.shape[0]
    tm = min(ROW_TILE, l)
    n = b * l
    w1_in, w1_out = ffn1_w_in.astype(BF16), ffn1_w_out.astype(BF16)
    w2_in, w2_out = ffn2_w_in.astype(BF16), ffn2_w_out.astype(BF16)
    for layer in range(depth):
        p = _layer_operands(layer, mix_norm, w_mix_in, conv_w, hgrn_lb_logits, hgrn_out_gain,
                            fox_q_gain, fox_k_gain, fox_f_bias, w_mix_out)
        x2 = _ffn(x.reshape(n, d), ffn1_norm[layer].reshape(1, d), w1_in, w1_out, layer)
        yc, hh, qa, ka, vat, cend = _mix_in(x2.reshape(b, l, d), p)
        yh = _hgrn(hh, p, layer)
        cend = cend[:, :, :tm // KV_TILE, :FOX_HEADS].reshape(b, l // KV_TILE, FOX_HEADS)
        cpre = jnp.pad(cend.transpose(0, 2, 1), ((0, 0), (0, 0), (1, 0))) * LOG2E
        yf = _fox(p["qkb"], cpre.reshape(-1), qa, ka, vat)
        x2 = _mix_out_ffn(x2, yc.reshape(n, CONV_CH), yh.reshape(n, HGRN_W), yf,
                          p["wo_c"], p["wo_h"], p["wo_f"], ffn2_norm[layer].reshape(1, d),
                          w2_in, w2_out, layer)
        x = x2.reshape(b, l, d)
    return x
```
